```python
import jax
import jax.numpy as jnp
from jax import lax
import numpy as np

D_MODEL = 2048
BATCH = 4
SEQ = 2048
DEPTH = 2
DEC_BATCH = 128
DEC_SEQ = 1
PAST_LEN = 16384
PAGE_SIZE = 128

N_GROUPS = 4
G_WIDTH = D_MODEL // N_GROUPS
A_CONV = 3
RET_HEADS = 4
RET_DK = G_WIDTH // RET_HEADS
RET_DV = G_WIDTH // RET_HEADS
RET_CHUNK = 128
ROPE_BASE = 10000.0
RWKV_HEAD = 64
RWKV_HEADS = G_WIDTH // RWKV_HEAD
W_LORA = 64
A_LORA = 64
G_LORA = 128
SHIFT_W = 3 * G_WIDTH + W_LORA + A_LORA + G_LORA
RWKV_GN_EPS = 64e-5
D_CONV = 31
N_MEM = 256
XA_HEADS = 4
XA_HD = D_MODEL // XA_HEADS
D_FF = 4 * D_MODEL
A_COLS = 3 * G_WIDTH
B_COLS = 4 * G_WIDTH
C_COLS = SHIFT_W
D_COLS = 2 * G_WIDTH
P_IN = A_COLS + B_COLS + C_COLS + D_COLS
EPS = 1e-6

kernel_name = 'hybrid_parallel_groups_decoder_step'


def rmsnorm(x, g):
    xf = x.astype(jnp.float32)
    y = xf * lax.rsqrt(jnp.mean(xf * xf, axis=-1, keepdims=True) + EPS)
    return (y * g.astype(jnp.float32)).astype(x.dtype)


def layernorm_lastdim(x, eps):
    xf = x.astype(jnp.float32)
    mu = jnp.mean(xf, axis=-1, keepdims=True)
    var = jnp.mean(jnp.square(xf - mu), axis=-1, keepdims=True)
    return (xf - mu) * lax.rsqrt(var + eps)


def causal_dwconv(x_ext, w):
    return lax.conv_general_dilated(x_ext, w[:, None, :].astype(x_ext.dtype), window_strides=(1,), padding='VALID', dimension_numbers=('NWC', 'WIO', 'NWC'), feature_group_count=x_ext.shape[-1])


def rotary(x, pos):
    d = x.shape[-1]
    inv = ROPE_BASE ** (-jnp.arange(0, d, 2, dtype=jnp.float32) / d)
    ang = pos.astype(jnp.float32)[:, None] * inv[None, :]
    cos = jnp.cos(ang)[None, :, None, :]
    sin = jnp.sin(ang)[None, :, None, :]
    x1, x2 = jnp.split(x.astype(jnp.float32), 2, axis=-1)
    return jnp.concatenate([x1 * cos - x2 * sin, x2 * cos + x1 * sin], axis=-1)


def retention_log_decay():
    return jnp.log1p(-jnp.exp2(-5.0 - jnp.arange(RET_HEADS, dtype=jnp.float32)))


def retention_chunk(S, qkv, lg):
    q, k, v = qkv
    L = q.shape[1]
    idx = jnp.arange(L, dtype=jnp.float32)
    diff = idx[:, None] - idx[None, :]
    dmat = jnp.where(diff[None] >= 0, jnp.exp(lg[:, None, None] * jnp.maximum(diff, 0.0)[None]), 0.0)
    scores = jnp.einsum('bihd,bjhd->bhij', q, k) * dmat[None]
    inner = jnp.einsum('bhij,bjhe->bihe', scores, v)
    q_dec = jnp.exp(lg[None, :] * (idx[:, None] + 1.0))
    cross = jnp.einsum('bihd,bhde->bihe', q, S) * q_dec[None, :, :, None]
    k_dec = jnp.exp(lg[None, :] * (L - 1.0 - idx[:, None]))
    S_new = jnp.exp(lg * L)[None, :, None, None] * S + jnp.einsum('bjhd,bjhe->bhde', k * k_dec[None, :, :, None], v)
    return S_new, inner + cross


def retention(q, k, v, S0):
    B, L, H, _ = q.shape
    C = RET_CHUNK if L % RET_CHUNK == 0 else L
    n = L // C
    def to_chunks(t):
        return t.reshape(B, n, C, H, t.shape[-1]).swapaxes(0, 1)
    lg = retention_log_decay()
    S, out = lax.scan(lambda S, c: retention_chunk(S, c, lg), S0, (to_chunks(q), to_chunks(k), to_chunks(v)))
    return out.swapaxes(0, 1).reshape(B, L, H, -1), S


def wkv7_scan(S0, r, w, k, v, kk, a):
    def step(S, inp):
        r_t, w_t, k_t, v_t, kk_t, a_t = inp
        sa = jnp.einsum('bhij,bhj->bhi', S, -kk_t)
        S = S * w_t[:, :, None, :] + sa[..., None] * (kk_t * a_t)[:, :, None, :] + v_t[..., None] * k_t[:, :, None, :]
        return S, jnp.einsum('bhij,bhj->bhi', S, r_t)
    def tm(t):
        return jnp.moveaxis(t, 1, 0)
    S, out = lax.scan(step, S0, (tm(r), tm(w), tm(k), tm(v), tm(kk), tm(a)))
    return jnp.moveaxis(out, 0, 1), S


def token_mixers(h, st, p, pos0):
    st_a, st_ret, st_shift, st_wkv, st_d = st
    B, L, _ = h.shape
    f32 = jnp.float32
    dt = h.dtype
    proj = h @ p['w_in']
    pa, pb, pc, pd = jnp.split(proj, [A_COLS, A_COLS + B_COLS, A_COLS + B_COLS + C_COLS], axis=-1)

    a_b, a_c, a_h = jnp.split(pa, 3, axis=-1)
    u = a_c * a_h
    u_ext = jnp.concatenate([st_a.astype(u.dtype), u], axis=1)
    y_a = a_b * causal_dwconv(u_ext, p['conv_a_w'])
    new_a = u_ext[:, -(A_CONV - 1):]

    q, k, v, g = jnp.split(pb, 4, axis=-1)
    pos = pos0 + jnp.arange(L)
    q = rotary(q.reshape(B, L, RET_HEADS, RET_DK), pos)
    k = rotary(k.reshape(B, L, RET_HEADS, RET_DK), pos) * (RET_DK ** -0.5)
    v = v.reshape(B, L, RET_HEADS, RET_DV).astype(f32)
    o_b, new_ret = retention(q, k, v, st_ret.astype(f32))
    y_b = jax.nn.silu(g.astype(f32)) * layernorm_lastdim(o_b, EPS).reshape(B, L, G_WIDTH)

    prev = jnp.concatenate([st_shift[:, None, :].astype(pc.dtype), pc[:, :-1]], axis=1)
    xs = pc + (prev - pc) * p['mu_c']
    new_shift = pc[:, -1]
    r, kc, vc, wl, al, gl = jnp.split(xs, [G_WIDTH, 2 * G_WIDTH, 3 * G_WIDTH, 3 * G_WIDTH + W_LORA, 3 * G_WIDTH + W_LORA + A_LORA], axis=-1)
    w = -jax.nn.softplus(-(p['w0'] + jnp.tanh(wl) @ p['w2'])) - 0.5
    decay = jnp.exp(-jnp.exp(w.astype(f32)))
    a = jax.nn.sigmoid((p['a0'] + al @ p['a2']).astype(f32))
    gate = (jax.nn.sigmoid(gl) @ p['g2']).astype(f32)
    def hs(t):
        return t.astype(f32).reshape(B, L, RWKV_HEADS, RWKV_HEAD)
    kk = hs(kc * p['k_k'])
    kk = kk / jnp.maximum(jnp.sqrt(jnp.sum(kk * kk, axis=-1, keepdims=True)), 1e-12)
    a4 = hs(a)
    k4 = hs(kc.astype(f32) * (1.0 + (a - 1.0) * p['k_a'].astype(f32)))
    r4 = hs(r)
    v4 = hs(vc)
    o_c, new_wkv = wkv7_scan(st_wkv.astype(f32), r4, hs(decay), k4, v4, kk, a4)
    o_c = layernorm_lastdim(o_c, RWKV_GN_EPS).reshape(B, L, G_WIDTH) * p['lnx_g'].astype(f32) + p['lnx_b'].astype(f32)
    bonus = jnp.sum(r4 * k4 * p['r_k'].astype(f32).reshape(RWKV_HEADS, RWKV_HEAD), axis=-1, keepdims=True) * v4
    y_c = (o_c + bonus.reshape(B, L, G_WIDTH)) * gate

    d1, d2 = jnp.split(pd, 2, axis=-1)
    u_d = d1 * jax.nn.sigmoid(d2)
    ud_ext = jnp.concatenate([st_d.astype(u_d.dtype), u_d], axis=1)
    c = causal_dwconv(ud_ext, p['conv_d_w']) + p['conv_d_b']
    c = layernorm_lastdim(c, EPS) * p['ln_d_g'].astype(f32) + p['ln_d_b'].astype(f32)
    y_d = jax.nn.silu(c)
    new_d = ud_ext[:, -(D_CONV - 1):]

    y = jnp.concatenate([y_a.astype(dt), y_b.astype(dt), y_c.astype(dt), y_d.astype(dt)], axis=-1) @ p['w_out']
    new_st = (new_a.astype(st_a.dtype), new_ret.astype(st_ret.dtype), new_shift.astype(st_shift.dtype), new_wkv.astype(st_wkv.dtype), new_d.astype(st_d.dtype))
    return y, new_st


def memory_kv(mem, p):
    B, M, _ = mem.shape
    mn = rmsnorm(mem, p['g_mem'])
    mk = (mn @ p['wk_x']).reshape(B, M, XA_HEADS, XA_HD)
    mv = (mn @ p['wv_x']).reshape(B, M, XA_HEADS, XA_HD)
    return mk, mv


def cross_attention(h, mk, mv, p):
    B, L, _ = h.shape
    q = (h @ p['wq_x']).reshape(B, L, XA_HEADS, XA_HD)
    s = jnp.einsum('blhd,bmhd->bhlm', q.astype(jnp.float32), mk.astype(jnp.float32)) * (XA_HD ** -0.5)
    pr = jax.nn.softmax(s, axis=-1)
    o = jnp.einsum('bhlm,bmhd->blhd', pr, mv.astype(jnp.float32)).reshape(B, L, D_MODEL)
    return o.astype(h.dtype) @ p['wo_x']


def trunk_layer(x, mk, mv, st, p, pos0):
    y, new_st = token_mixers(rmsnorm(x, p['g_mix']), st, p, pos0)
    x = x + y
    x = x + cross_attention(rmsnorm(x, p['g_xa']), mk, mv, p)
    hm = rmsnorm(x, p['g_mlp'])
    x = x + jnp.square(jax.nn.relu(hm @ p['w_up'])) @ p['w_down']
    return x, new_st


def setup_inputs(seed: int = 0) -> dict:
    key = jax.random.key(seed)
    ks = iter(jax.random.split(key, 48))
    def nrm(shape, scale):
        return jax.random.normal(next(ks), shape, jnp.float32) * scale
    def unif(shape, lo, hi):
        return lo + (hi - lo) * jax.random.uniform(next(ks), shape, jnp.float32)
    G = G_WIDTH
    return {
        'x_prompt': nrm((BATCH, SEQ, D_MODEL), 1.0),
        'x_sample': nrm((DEC_BATCH, DEC_SEQ, D_MODEL), 1.0),
        'mem_prompt': nrm((BATCH, N_MEM, D_MODEL), 1.0),
        'state_conv_a': nrm((DEPTH, DEC_BATCH, A_CONV - 1, G), 1.0),
        'state_ret': nrm((DEPTH, DEC_BATCH, RET_HEADS, RET_DK, RET_DV), 1.0),
        'state_shift': nrm((DEPTH, DEC_BATCH, SHIFT_W), 1.0),
        'state_wkv': nrm((DEPTH, DEC_BATCH, RWKV_HEADS, RWKV_HEAD, RWKV_HEAD), 0.5),
        'state_conv_d': nrm((DEPTH, DEC_BATCH, D_CONV - 1, G), 1.0),
        'cache_mem_k': nrm((DEPTH, DEC_BATCH, N_MEM, XA_HEADS, XA_HD), 1.0),
        'cache_mem_v': nrm((DEPTH, DEC_BATCH, N_MEM, XA_HEADS, XA_HD), 1.0),
        'g_mix': 1.0 + nrm((DEPTH, D_MODEL), 0.02),
        'w_in': nrm((DEPTH, D_MODEL, P_IN), D_MODEL ** -0.5),
        'conv_a_w': nrm((DEPTH, A_CONV, G), A_CONV ** -0.5),
        'mu_c': unif((DEPTH, SHIFT_W), 0.0, 1.0),
        'w0': unif((DEPTH, G), -6.0, -1.0),
        'w2': nrm((DEPTH, W_LORA, G), 0.5 * W_LORA ** -0.5),
        'a0': nrm((DEPTH, G), 0.1),
        'a2': nrm((DEPTH, A_LORA, G), A_LORA ** -0.5),
        'g2': nrm((DEPTH, G_LORA, G), G_LORA ** -0.5),
        'k_k': 1.0 + nrm((DEPTH, G), 0.1),
        'k_a': 1.0 + nrm((DEPTH, G), 0.1),
        'r_k': nrm((DEPTH, G), 0.1),
        'lnx_g': 1.0 + nrm((DEPTH, G), 0.02),
        'lnx_b': nrm((DEPTH, G), 0.02),
        'conv_d_w': nrm((DEPTH, D_CONV, G), D_CONV ** -0.5),
        'conv_d_b': nrm((DEPTH, G), 0.02),
        'ln_d_g': 1.0 + nrm((DEPTH, G), 0.02),
        'ln_d_b': nrm((DEPTH, G), 0.02),
        'w_out': nrm((DEPTH, D_MODEL, D_MODEL), D_MODEL ** -0.5),
        'g_xa': 1.0 + nrm((DEPTH, D_MODEL), 0.02),
        'g_mem': 1.0 + nrm((DEPTH, D_MODEL), 0.02),
        'wq_x': nrm((DEPTH, D_MODEL, D_MODEL), D_MODEL ** -0.5),
        'wk_x': nrm((DEPTH, D_MODEL, D_MODEL), D_MODEL ** -0.5),
        'wv_x': nrm((DEPTH, D_MODEL, D_MODEL), D_MODEL ** -0.5),
        'wo_x': nrm((DEPTH, D_MODEL, D_MODEL), D_MODEL ** -0.5),
        'g_mlp': 1.0 + nrm((DEPTH, D_MODEL), 0.02),
        'w_up': nrm((DEPTH, D_MODEL, D_FF), D_MODEL ** -0.5),
        'w_down': nrm((DEPTH, D_FF, D_MODEL), D_FF ** -0.5),
        'g_final': 1.0 + nrm((D_MODEL,), 0.02),
    }


def reference(x_prompt, x_sample, mem_prompt, state_conv_a, state_ret, state_shift, state_wkv, state_conv_d, cache_mem_k, cache_mem_v, g_mix, w_in, conv_a_w, mu_c, w0, w2, a0, a2, g2, k_k, k_a, r_k, lnx_g, lnx_b, conv_d_w, conv_d_b, ln_d_g, ln_d_b, w_out, g_xa, g_mem, wq_x, wk_x, wv_x, wo_x, g_mlp, w_up, w_down, g_final):
    xp = x_prompt
    xs = x_sample
    Bp = x_prompt.shape[0]
    dt = x_prompt.dtype
    conv_a_p, conv_a_s, ret_p, ret_s, shift_p, shift_s = [], [], [], [], [], []
    wkv_p, wkv_s, conv_d_p, conv_d_s, mem_k_p, mem_v_p = [], [], [], [], [], []
    for l in range(DEPTH):
        p = {'g_mix': g_mix[l], 'w_in': w_in[l], 'conv_a_w': conv_a_w[l], 'mu_c': mu_c[l], 'w0': w0[l], 'w2': w2[l], 'a0': a0[l], 'a2': a2[l], 'g2': g2[l], 'k_k': k_k[l], 'k_a': k_a[l], 'r_k': r_k[l], 'lnx_g': lnx_g[l], 'lnx_b': lnx_b[l], 'conv_d_w': conv_d_w[l], 'conv_d_b': conv_d_b[l], 'ln_d_g': ln_d_g[l], 'ln_d_b': ln_d_b[l], 'w_out': w_out[l], 'g_xa': g_xa[l], 'g_mem': g_mem[l], 'wq_x': wq_x[l], 'wk_x': wk_x[l], 'wv_x': wv_x[l], 'wo_x': wo_x[l], 'g_mlp': g_mlp[l], 'w_up': w_up[l], 'w_down': w_down[l]}
        zero_st = (jnp.zeros((Bp, A_CONV - 1, G_WIDTH), dt), jnp.zeros((Bp, RET_HEADS, RET_DK, RET_DV), dt), jnp.zeros((Bp, SHIFT_W), dt), jnp.zeros((Bp, RWKV_HEADS, RWKV_HEAD, RWKV_HEAD), dt), jnp.zeros((Bp, D_CONV - 1, G_WIDTH), dt))
        mk_p, mv_p = memory_kv(mem_prompt, p)
        xp, st_p = trunk_layer(xp, mk_p, mv_p, zero_st, p, 0)
        st_in = (state_conv_a[l], state_ret[l], state_shift[l], state_wkv[l], state_conv_d[l])
        xs, st_s = trunk_layer(xs, cache_mem_k[l], cache_mem_v[l], st_in, p, PAST_LEN)
        conv_a_p.append(st_p[0]); conv_a_s.append(st_s[0])
        ret_p.append(st_p[1]); ret_s.append(st_s[1])
        shift_p.append(st_p[2]); shift_s.append(st_s[2])
        wkv_p.append(st_p[3]); wkv_s.append(st_s[3])
        conv_d_p.append(st_p[4]); conv_d_s.append(st_s[4])
        mem_k_p.append(mk_p); mem_v_p.append(mv_p)
    y_prompt = rmsnorm(xp, g_final)
    y_sample = rmsnorm(xs, g_final)
    return (y_prompt, y_sample, jnp.stack(conv_a_p), jnp.stack(conv_a_s), jnp.stack(ret_p), jnp.stack(ret_s), jnp.stack(shift_p), jnp.stack(shift_s), jnp.stack(wkv_p), jnp.stack(wkv_s), jnp.stack(conv_d_p), jnp.stack(conv_d_s), jnp.stack(mem_k_p), jnp.stack(mem_v_p))
```

```python
import functools
import math

import jax
import jax.numpy as jnp
from jax import lax
from jax.experimental import pallas as pl
from jax.experimental.pallas import tpu as pltpu

F32 = jnp.float32
BF16 = jnp.bfloat16
HI = lax.Precision.HIGHEST

D_MODEL = 2048
G_WIDTH = 512
RET_HEADS = 4
RET_D = 128
RET_CHUNK = 128
ROPE_BASE = 10000.0
RWKV_HEAD = 64
RWKV_HEADS = 8
WKV_CHUNK = 64
RWKV_GN_EPS = 64e-5
A_CONV = 3
D_CONV = 31
D_HIST = 32
N_MEM = 256
XA_HEADS = 4
XA_HD = 512
SHIFT_W = 1792
PAST_LEN = 16384
EPS = 1e-6
RET_LOG_DECAY = tuple(math.log1p(-(2.0 ** (-5.0 - h))) for h in range(RET_HEADS))
VMEM_LIMIT = 56 * 1024 * 1024


def _cparams(*sem):
    return pltpu.CompilerParams(dimension_semantics=sem, vmem_limit_bytes=VMEM_LIMIT)


def _pick(n, cands):
    for c in cands:
        if n % c == 0:
            return c
    raise ValueError(f"no tile for {n}")


def _sigmoid(x):
    return 1.0 / (1.0 + jnp.exp(-x))


def _dot(a, b, precision=None):
    return jnp.dot(a, b, preferred_element_type=F32, precision=precision)


def _dot_nt(a, b, precision=None):
    return lax.dot_general(a, b, (((1,), (1,)), ((), ())), preferred_element_type=F32, precision=precision)


def _dot_tn(a, b, precision=None):
    return lax.dot_general(a, b, (((0,), (0,)), ((), ())), preferred_element_type=F32, precision=precision)


def _layernorm(x, eps):
    mu = jnp.mean(x, axis=-1, keepdims=True)
    xc = x - mu
    var = jnp.mean(xc * xc, axis=-1, keepdims=True)
    return xc * lax.rsqrt(var + eps)


def _rmsnorm_body(x_ref, g_ref, o_ref):
    x = x_ref[...]
    ms = jnp.mean(x * x, axis=-1, keepdims=True)
    o_ref[...] = ((x * lax.rsqrt(ms + EPS)) * g_ref[...]).astype(o_ref.dtype)


def _rmsnorm(x, g, out_dtype):
    m, d = x.shape
    tm = _pick(m, (1040, 1024, 640, 512, 256, 128))
    return pl.pallas_call(
        _rmsnorm_body,
        grid=(m // tm,),
        in_specs=[pl.BlockSpec((tm, d), lambda i: (i, 0)), pl.BlockSpec((1, d), lambda i: (0, 0))],
        out_specs=pl.BlockSpec((tm, d), lambda i: (i, 0)),
        out_shape=jax.ShapeDtypeStruct((m, d), out_dtype),
        compiler_params=_cparams("parallel"),
        name="rmsnorm",
    )(x, g.reshape(1, d))


def _mm_body(*refs, act, has_res, nk):
    a_ref, w_ref = refs[0], refs[1]
    res_ref = refs[2] if has_res else None
    o_ref = refs[3] if has_res else refs[2]

    def finish(r):
        if act == "relu2":
            r = jnp.square(jnp.maximum(r, 0.0))
        if has_res:
            r = res_ref[...] + r
        o_ref[...] = r.astype(o_ref.dtype)

    if nk == 1:
        finish(_dot(a_ref[...], w_ref[...]))
        return
    acc_ref = refs[-1]
    k = pl.program_id(2)

    @pl.when(k == 0)
    def _():
        acc_ref[...] = jnp.zeros_like(acc_ref)

    acc_ref[...] += _dot(a_ref[...], w_ref[...])

    @pl.when(k == nk - 1)
    def _():
        finish(acc_ref[...])


def _matmul(a, w, *, residual=None, act=None, out_dtype=F32, tm=None, tn=None, tk=None):
    m, kdim = a.shape
    n = w.shape[1]
    tm = tm or _pick(m, (1040, 1024, 640, 512, 256, 128))
    tn = tn or _pick(n, (1024, 896, 768, 512, 256))
    tk = tk or _pick(kdim, (2048, 1024, 512))
    nk = kdim // tk
    has_res = residual is not None
    in_specs = [pl.BlockSpec((tm, tk), lambda i, j, k: (i, k)), pl.BlockSpec((tk, tn), lambda i, j, k: (k, j))]
    args = [a, w]
    if has_res:
        in_specs.append(pl.BlockSpec((tm, tn), lambda i, j, k: (i, j)))
        args.append(residual)
    return pl.pallas_call(
        functools.partial(_mm_body, act=act, has_res=has_res, nk=nk),
        grid=(m // tm, n // tn, nk),
        in_specs=in_specs,
        out_specs=pl.BlockSpec((tm, tn), lambda i, j, k: (i, j)),
        out_shape=jax.ShapeDtypeStruct((m, n), out_dtype),
        scratch_shapes=[pltpu.VMEM((tm, tn), F32)] if nk > 1 else [],
        compiler_params=_cparams("parallel", "parallel", "arbitrary"),
        name="matmul",
    )(*args)


def _conv_a_body(b_ref, c_ref, h_ref, w_ref, y_ref, st_ref, ext_ref, *, tl):
    t = pl.program_id(1)

    @pl.when(t == 0)
    def _():
        ext_ref[0:8, :] = jnp.zeros((8, G_WIDTH), F32)

    u = c_ref[...] * h_ref[...]
    ext_ref[8:, :] = u
    u1 = ext_ref[7:7 + tl, :]
    u2 = ext_ref[6:6 + tl, :]
    y = b_ref[...] * (w_ref[0:1, :] * u2 + w_ref[1:2, :] * u1 + w_ref[2:3, :] * u)
    y_ref[...] = y.astype(y_ref.dtype)
    st_ref[0] = ext_ref[tl + 6:tl + 8, :]
    ext_ref[0:8, :] = ext_ref[tl:tl + 8, :]


def _conv_a_prompt(pa, conv_w, bp, seq):
    tl = _pick(seq, (512, 256, 128))
    nt = seq // tl
    col = lambda j: pl.BlockSpec((tl, G_WIDTH), lambda b, t, j=j: (b * nt + t, j))
    return pl.pallas_call(
        functools.partial(_conv_a_body, tl=tl),
        grid=(bp, nt),
        in_specs=[col(0), col(1), col(2), pl.BlockSpec((A_CONV, G_WIDTH), lambda b, t: (0, 0))],
        out_specs=[pl.BlockSpec((tl, G_WIDTH), lambda b, t: (b * nt + t, 0)),
                   pl.BlockSpec((1, A_CONV - 1, G_WIDTH), lambda b, t: (b, 0, 0))],
        out_shape=[jax.ShapeDtypeStruct((bp * seq, G_WIDTH), BF16),
                   jax.ShapeDtypeStruct((bp, A_CONV - 1, G_WIDTH), F32)],
        scratch_shapes=[pltpu.VMEM((tl + 8, G_WIDTH), F32)],
        compiler_params=_cparams("parallel", "arbitrary"),
        name="conv_a_prompt",
    )(pa, pa, pa, conv_w)


CONV_D_ROWS = 32


def _conv_d_body(d1_ref, d2_ref, w_ref, bias_ref, g_ref, beta_ref, y_ref, st_ref, ext_ref, *, tl):
    t = pl.program_id(1)

    @pl.when(t == 0)
    def _():
        ext_ref[0:D_HIST, :] = jnp.zeros((D_HIST, G_WIDTH), F32)

    ext_ref[D_HIST:, :] = d1_ref[...] * _sigmoid(d2_ref[...])
    first = D_HIST - (D_CONV - 1)

    def blk(i, carry):
        base = pl.multiple_of(i * CONV_D_ROWS, CONV_D_ROWS)
        win = ext_ref[pl.ds(base, CONV_D_ROWS + D_HIST), :]
        acc = jnp.broadcast_to(bias_ref[...], (CONV_D_ROWS, G_WIDTH))
        for k in range(D_CONV):
            acc = acc + win[first + k:first + k + CONV_D_ROWS, :] * w_ref[k:k + 1, :]
        c = _layernorm(acc, EPS) * g_ref[...] + beta_ref[...]
        y_ref[pl.ds(base, CONV_D_ROWS), :] = (c * _sigmoid(c)).astype(y_ref.dtype)
        return carry

    lax.fori_loop(0, tl // CONV_D_ROWS, blk, 0)
    st_ref[0] = ext_ref[tl + first:tl + D_HIST, :]
    ext_ref[0:D_HIST, :] = ext_ref[tl:tl + D_HIST, :]


def _conv_d_prompt(pd, w, bias, g, beta, bp, seq):
    tl = _pick(seq, (512, 256, 128))
    nt = seq // tl
    col = lambda j: pl.BlockSpec((tl, G_WIDTH), lambda b, t, j=j: (b * nt + t, j))
    vec = pl.BlockSpec((1, G_WIDTH), lambda b, t: (0, 0))
    return pl.pallas_call(
        functools.partial(_conv_d_body, tl=tl),
        grid=(bp, nt),
        in_specs=[col(0), col(1), pl.BlockSpec((D_CONV, G_WIDTH), lambda b, t: (0, 0)), vec, vec, vec],
        out_specs=[pl.BlockSpec((tl, G_WIDTH), lambda b, t: (b * nt + t, 0)),
                   pl.BlockSpec((1, D_CONV - 1, G_WIDTH), lambda b, t: (b, 0, 0))],
        out_shape=[jax.ShapeDtypeStruct((bp * seq, G_WIDTH), BF16),
                   jax.ShapeDtypeStruct((bp, D_CONV - 1, G_WIDTH), F32)],
        scratch_shapes=[pltpu.VMEM((tl + D_HIST, G_WIDTH), F32)],
        compiler_params=_cparams("parallel", "arbitrary"),
        name="conv_d_prompt",
    )(pd, pd, w, bias.reshape(1, -1), g.reshape(1, -1), beta.reshape(1, -1))


def _rope_tables(pos):
    inv = ROPE_BASE ** (-jnp.arange(0, RET_D, 2, dtype=F32) / RET_D)
    ang = pos.astype(F32)[:, None] * inv[None, :]
    cos = jnp.cos(ang)
    sin = jnp.sin(ang)
    return jnp.concatenate([cos, cos], axis=-1), jnp.concatenate([-sin, sin], axis=-1)


def _rope(x, cos2, sin2):
    return x * cos2 + pltpu.roll(x, RET_D // 2, 1) * sin2


def _ret_body(q_ref, k_ref, v_ref, g_ref, cos_ref, sin_ref, y_ref, st_ref, s_ref, *, nchunks):
    c = pl.program_id(1)
    C = RET_CHUNK

    @pl.when(c == 0)
    def _():
        s_ref[...] = jnp.zeros_like(s_ref)

    ii = lax.broadcasted_iota(jnp.int32, (C, C), 0).astype(F32)
    jj = lax.broadcasted_iota(jnp.int32, (C, C), 1).astype(F32)
    diff = ii - jj
    cos2 = cos_ref[...]
    sin2 = sin_ref[...]
    for h in range(RET_HEADS):
        lg = RET_LOG_DECAY[h]
        sl = slice(h * RET_D, (h + 1) * RET_D)
        q = _rope(q_ref[:, sl], cos2, sin2)
        k = _rope(k_ref[:, sl], cos2, sin2) * (RET_D ** -0.5)
        v = v_ref[:, sl]
        dmat = jnp.where(diff >= 0.0, jnp.exp(lg * jnp.maximum(diff, 0.0)), 0.0)
        scores = _dot_nt(q.astype(BF16), k.astype(BF16)) * dmat
        inner = _dot(scores.astype(BF16), v.astype(BF16))
        s_old = s_ref[h]
        cross = _dot(q.astype(BF16), s_old.astype(BF16)) * jnp.exp(lg * (ii + 1.0))
        k_dec = k * jnp.exp(lg * (C - 1.0 - ii))
        s_ref[h] = math.exp(lg * C) * s_old + _dot_tn(k_dec.astype(BF16), v.astype(BF16))
        o = _layernorm(inner + cross, EPS)
        g = g_ref[:, sl]
        y_ref[:, sl] = (g * _sigmoid(g) * o).astype(y_ref.dtype)

    @pl.when(c == nchunks - 1)
    def _():
        st_ref[0] = s_ref[...]


def _retention_prompt(pb, cos2, sin2, bp, seq):
    C = RET_CHUNK
    nc = seq // C
    col = lambda j: pl.BlockSpec((C, G_WIDTH), lambda b, c, j=j: (b * nc + c, j))
    tab = pl.BlockSpec((C, RET_D), lambda b, c: (c, 0))
    return pl.pallas_call(
        functools.partial(_ret_body, nchunks=nc),
        grid=(bp, nc),
        in_specs=[col(0), col(1), col(2), col(3), tab, tab],
        out_specs=[pl.BlockSpec((C, G_WIDTH), lambda b, c: (b * nc + c, 0)),
                   pl.BlockSpec((1, RET_HEADS, RET_D, RET_D), lambda b, c: (b, 0, 0, 0))],
        out_shape=[jax.ShapeDtypeStruct((bp * seq, G_WIDTH), BF16),
                   jax.ShapeDtypeStruct((bp, RET_HEADS, RET_D, RET_D), F32)],
        scratch_shapes=[pltpu.VMEM((RET_HEADS, RET_D, RET_D), F32)],
        compiler_params=_cparams("parallel", "arbitrary"),
        name="retention_prompt",
    )(pb, pb, pb, pb, cos2, sin2)


def _softplus(z):
    return jnp.maximum(z, 0.0) + jnp.log1p(jnp.exp(-jnp.abs(z)))


def _rwkv_pre(r, kc, vc, lora_x, w2_w, w2_a, g2, w0, a0, k_k, k_a):
    lin = lora_x[:, 0:128]
    lane = lax.broadcasted_iota(jnp.int32, lin.shape, 1)
    lin = jnp.where(lane < 64, jnp.tanh(lin), lin).astype(BF16)
    gate = _dot(_sigmoid(lora_x[:, 128:256]).astype(BF16), g2)
    w_pre = -_softplus(-(w0 + _dot(lin, w2_w))) - 0.5
    log_decay = -jnp.exp(w_pre)
    a = _sigmoid(a0 + _dot(lin, w2_a))
    kk = kc * k_k
    k4 = kc * (1.0 + (a - 1.0) * k_a)
    return r, log_decay, k4, vc, kk, a, gate


def _l2_normalize(kk):
    nrm = jnp.sqrt(jnp.sum(kk * kk, axis=-1, keepdims=True))
    return kk / jnp.maximum(nrm, 1e-12)


def _wkv_chunk(r, lw, cum, k, v, kk, a, z, mask2, eye):
    T = WKV_CHUNK
    g_inc = jnp.exp(cum)
    g_exc = jnp.exp(cum - lw)
    g_inv = jnp.exp(-cum)
    g_tot = g_inc[T - 1:T, :]
    ah = -(kk * g_exc)
    rg = r * g_inc
    bd = kk * a * g_inv
    kd = k * g_inv
    ar = jnp.concatenate([ah, rg], axis=0)
    xb = jnp.where(mask2, _dot_nt(ar, bd, HI), 0.0)
    xk = jnp.where(mask2, _dot_nt(ar, kd, HI), 0.0)
    acur = xb[0:T]
    arb = xb[T:2 * T]
    p = eye
    for it in range(6):
        if it < 5:
            x = _dot(jnp.concatenate([acur, p], axis=0), acur, HI)
            acur = x[0:T]
            p = p + x[T:2 * T]
        else:
            p = p + _dot(p, acur, HI)
    xkv = _dot(xk, v, HI)
    wu = _dot(p, jnp.concatenate([ah, xkv[0:T]], axis=1), HI)
    qo = _dot(arb, wu, HI)
    qp = rg + qo[:, 0:64]
    o0 = qo[:, 64:128] + xkv[T:2 * T]
    lhs_t = jnp.concatenate([bd * g_tot, kd * g_tot], axis=0)
    rhs = jnp.concatenate([wu, jnp.concatenate([jnp.zeros((T, 64), F32), v], axis=1)], axis=0)
    mn = _dot_tn(lhs_t, rhs, HI)
    mz = mn[:, 0:64] + jnp.where(eye > 0.0, g_tot, 0.0)
    nz = mn[:, 64:128]
    oz = _dot(jnp.concatenate([qp, mz], axis=0), z, HI)
    return oz[0:T] + o0, oz[T:2 * T] + nz


def _rwkv_post(o, r, k4, v, gate, rk, lnx_g, lnx_b):
    gn = _layernorm(o, RWKV_GN_EPS) * lnx_g + lnx_b
    bonus = jnp.sum(r * k4 * rk, axis=-1, keepdims=True) * v
    return (gn + bonus) * gate


def _rwkv_prompt_body(pc_ref, mu_ref, w2a_ref, g2_ref, w0_ref, a0_ref, kk_ref, ka_ref, rk_ref, lg_ref, lb_ref,
                      y_ref, shift_ref, st_ref, carry_ref, z_ref, *, nchunks):
    c = pl.program_id(1)
    T = WKV_CHUNK

    @pl.when(c == 0)
    def _():
        carry_ref[...] = jnp.zeros_like(carry_ref)
        z_ref[...] = jnp.zeros_like(z_ref)

    pc = pc_ref[...]
    row = lax.broadcasted_iota(jnp.int32, pc.shape, 0)
    prev = jnp.where(row == 0, carry_ref[0:1, :], pltpu.roll(pc, 1, 0))
    carry_ref[0:1, :] = pc[T - 1:T, :]
    shift_ref[0] = pc[T - 1:T, :]
    xs = pc + (prev - pc) * mu_ref[...]
    G = G_WIDTH
    r, lw, k4, v, kk, a, gate = _rwkv_pre(xs[:, 0:G], xs[:, G:2 * G], xs[:, 2 * G:3 * G], xs[:, 3 * G:3 * G + 256],
                                          w2a_ref[:, 0:G], w2a_ref[:, G:2 * G], g2_ref[...], w0_ref[...], a0_ref[...],
                                          kk_ref[...], ka_ref[...])
    ti = lax.broadcasted_iota(jnp.int32, (T, T), 0)
    si = lax.broadcasted_iota(jnp.int32, (T, T), 1)
    tril = (ti >= si).astype(F32)
    eye = (ti == si).astype(F32)
    cum = _dot(tril, lw, HI)
    t2 = lax.broadcasted_iota(jnp.int32, (2 * T, T), 0)
    s2 = lax.broadcasted_iota(jnp.int32, (2 * T, T), 1)
    mask2 = jnp.where(t2 < T, t2 - 1, t2 - T) >= s2
    ys = []
    for h in range(RWKV_HEADS):
        sl = slice(h * RWKV_HEAD, (h + 1) * RWKV_HEAD)
        o, z_new = _wkv_chunk(r[:, sl], lw[:, sl], cum[:, sl], k4[:, sl], v[:, sl], _l2_normalize(kk[:, sl]),
                              a[:, sl], z_ref[h], mask2, eye)
        z_ref[h] = z_new
        ys.append(_rwkv_post(o, r[:, sl], k4[:, sl], v[:, sl], gate[:, sl], rk_ref[:, sl], lg_ref[:, sl], lb_ref[:, sl]))
    y_ref[...] = jnp.concatenate(ys, axis=1).astype(y_ref.dtype)

    @pl.when(c == nchunks - 1)
    def _():
        for h in range(RWKV_HEADS):
            st_ref[0, h] = z_ref[h].T


def _rwkv_prompt(pc, mu, w2a, g2, w0, a0, k_k, k_a, r_k, lnx_g, lnx_b, bp, seq):
    T = WKV_CHUNK
    nc = seq // T
    G = G_WIDTH
    vec = lambda n: pl.BlockSpec((1, n), lambda b, c: (0, 0))
    return pl.pallas_call(
        functools.partial(_rwkv_prompt_body, nchunks=nc),
        grid=(bp, nc),
        in_specs=[pl.BlockSpec((T, SHIFT_W), lambda b, c: (b * nc + c, 0)), vec(SHIFT_W),
                  pl.BlockSpec((128, 2 * G), lambda b, c: (0, 0)), pl.BlockSpec((128, G), lambda b, c: (0, 0)),
                  vec(G), vec(G), vec(G), vec(G), vec(G), vec(G), vec(G)],
        out_specs=[pl.BlockSpec((T, G), lambda b, c: (b * nc + c, 0)),
                   pl.BlockSpec((1, 1, SHIFT_W), lambda b, c: (b, 0, 0)),
                   pl.BlockSpec((1, RWKV_HEADS, RWKV_HEAD, RWKV_HEAD), lambda b, c: (b, 0, 0, 0))],
        out_shape=[jax.ShapeDtypeStruct((bp * seq, G), BF16),
                   jax.ShapeDtypeStruct((bp, 1, SHIFT_W), F32),
                   jax.ShapeDtypeStruct((bp, RWKV_HEADS, RWKV_HEAD, RWKV_HEAD), F32)],
        scratch_shapes=[pltpu.VMEM((8, SHIFT_W), F32), pltpu.VMEM((RWKV_HEADS, RWKV_HEAD, RWKV_HEAD), F32)],
        compiler_params=_cparams("parallel", "arbitrary"),
        name="rwkv_prompt",
    )(pc, mu, w2a, g2, w0, a0, k_k, k_a, r_k, lnx_g, lnx_b)


def _xattn_prompt_body(q_ref, k_ref, v_ref, o_ref, kb_ref, vb_ref):
    @pl.when(pl.program_id(1) == 0)
    def _():
        kb_ref[...] = k_ref[...].astype(BF16)
        vb_ref[...] = v_ref[...].astype(BF16)

    for h in range(XA_HEADS):
        sl = slice(h * XA_HD, (h + 1) * XA_HD)
        s = _dot_nt(q_ref[:, sl].astype(BF16), kb_ref[:, sl]) * (XA_HD ** -0.5)
        e = jnp.exp(s - jnp.max(s, axis=-1, keepdims=True))
        p = e / jnp.sum(e, axis=-1, keepdims=True)
        o_ref[:, sl] = _dot(p.astype(BF16), vb_ref[:, sl]).astype(o_ref.dtype)


def _xattn_prompt(q, mk, mv, bp, seq):
    tq = _pick(seq, (512, 256, 128))
    nt = seq // tq
    kv = pl.BlockSpec((N_MEM, D_MODEL), lambda b, t: (b, 0))
    return pl.pallas_call(
        _xattn_prompt_body,
        grid=(bp, nt),
        in_specs=[pl.BlockSpec((tq, D_MODEL), lambda b, t: (b * nt + t, 0)), kv, kv],
        out_specs=pl.BlockSpec((tq, D_MODEL), lambda b, t: (b * nt + t, 0)),
        out_shape=jax.ShapeDtypeStruct((bp * seq, D_MODEL), BF16),
        scratch_shapes=[pltpu.VMEM((N_MEM, D_MODEL), BF16), pltpu.VMEM((N_MEM, D_MODEL), BF16)],
        compiler_params=_cparams("parallel", "arbitrary"),
        name="xattn_prompt",
    )(q, mk, mv)


XA_SAMPLES_PER_STEP = 2


def _xattn_sample_body(q_ref, k_ref, v_ref, o_ref):
    i = pl.program_id(0)
    for j in range(XA_SAMPLES_PER_STEP):
        b = i * XA_SAMPLES_PER_STEP + j
        q = q_ref[pl.ds(b, 1), :]
        rows = slice(j * N_MEM, (j + 1) * N_MEM)
        outs = []
        for h in range(XA_HEADS):
            sl = slice(h * XA_HD, (h + 1) * XA_HD)
            s = jnp.sum(k_ref[rows, sl] * q[:, sl], axis=-1, keepdims=True) * (XA_HD ** -0.5)
            e = jnp.exp(s - jnp.max(s, axis=0, keepdims=True))
            p = e / jnp.sum(e, axis=0, keepdims=True)
            outs.append(jnp.sum(p * v_ref[rows, sl], axis=0, keepdims=True))
        o_ref[pl.ds(b, 1), :] = jnp.concatenate(outs, axis=1).astype(o_ref.dtype)


def _xattn_sample(q, ck, cv):
    bs = q.shape[0]
    nb = XA_SAMPLES_PER_STEP
    kv = pl.BlockSpec((nb * N_MEM, D_MODEL), lambda i: (i, 0))
    full = pl.BlockSpec((bs, D_MODEL), lambda i: (0, 0))
    return pl.pallas_call(
        _xattn_sample_body,
        grid=(bs // nb,),
        in_specs=[full, kv, kv],
        out_specs=full,
        out_shape=jax.ShapeDtypeStruct((bs, D_MODEL), F32),
        compiler_params=_cparams("arbitrary"),
        name="xattn_sample",
    )(q, ck, cv)


def _sample_ad_body(ab_ref, ac_ref, ah_ref, sta_ref, wa_ref, d1_ref, d2_ref, std_ref, wd_ref, bias_ref, g_ref, beta_ref,
                    ya_ref, yd_ref, na_ref, nd_ref):
    G = G_WIDTH
    u = ac_ref[...] * ah_ref[...]
    ya = ab_ref[...] * (wa_ref[0:1, :] * sta_ref[:, 0:G] + wa_ref[1:2, :] * sta_ref[:, G:2 * G] + wa_ref[2:3, :] * u)
    ya_ref[...] = ya.astype(ya_ref.dtype)
    na_ref[:, 0:G] = sta_ref[:, G:2 * G]
    na_ref[:, G:2 * G] = u
    ud = d1_ref[...] * _sigmoid(d2_ref[...])
    acc = jnp.broadcast_to(bias_ref[...], ud.shape)
    nh = D_CONV - 1
    for k in range(nh):
        acc = acc + std_ref[:, k * G:(k + 1) * G] * wd_ref[k:k + 1, :]
    acc = acc + ud * wd_ref[nh:nh + 1, :]
    c = _layernorm(acc, EPS) * g_ref[...] + beta_ref[...]
    yd_ref[...] = (c * _sigmoid(c)).astype(yd_ref.dtype)
    nd_ref[:, 0:(nh - 1) * G] = std_ref[:, G:nh * G]
    nd_ref[:, (nh - 1) * G:nh * G] = ud


def _sample_ad(pa, pd, st_a, st_d, wa, wd, bias, g, beta, row0):
    bs = st_a.shape[0]
    G = G_WIDTH
    tb = 32
    r0 = row0 // tb
    col = lambda j: pl.BlockSpec((tb, G), lambda i, j=j: (r0 + i, j))
    row = lambda n: pl.BlockSpec((tb, n), lambda i: (i, 0))
    vec = pl.BlockSpec((1, G), lambda i: (0, 0))
    nh = D_CONV - 1
    return pl.pallas_call(
        _sample_ad_body,
        grid=(bs // tb,),
        in_specs=[col(0), col(1), col(2), row(2 * G), pl.BlockSpec((A_CONV, G), lambda i: (0, 0)),
                  col(0), col(1), row(nh * G), pl.BlockSpec((D_CONV, G), lambda i: (0, 0)), vec, vec, vec],
        out_specs=[row(G), row(G), row(2 * G), row(nh * G)],
        out_shape=[jax.ShapeDtypeStruct((bs, G), BF16), jax.ShapeDtypeStruct((bs, G), BF16),
                   jax.ShapeDtypeStruct((bs, 2 * G), F32), jax.ShapeDtypeStruct((bs, nh * G), F32)],
        compiler_params=_cparams("parallel"),
        name="sample_conv_ad",
    )(pa, pa, pa, st_a, wa, pd, pd, st_d, wd, bias.reshape(1, -1), g.reshape(1, -1), beta.reshape(1, -1))


RET_D_BLOCK = 16


def _ret_sample_body(q_ref, k_ref, v_ref, g_ref, cos_ref, sin_ref, s_ref, y_ref, ns_ref, qt_ref, kt_ref, acc_ref):
    h = pl.program_id(0)
    j = pl.program_id(1)
    nd = RET_D // RET_D_BLOCK
    gammas = [math.exp(lg) for lg in RET_LOG_DECAY]
    gamma = jnp.where(h == 0, gammas[0], jnp.where(h == 1, gammas[1], jnp.where(h == 2, gammas[2], gammas[3])))
    gamma = gamma.astype(F32)

    @pl.when(j == 0)
    def _():
        q = _rope(q_ref[...], cos_ref[...], sin_ref[...])
        k = _rope(k_ref[...], cos_ref[...], sin_ref[...]) * (RET_D ** -0.5)
        qt_ref[...] = q.T
        kt_ref[...] = k.T
        acc_ref[...] = jnp.zeros_like(acc_ref)

    vt = v_ref[...].T
    st = s_ref[...].T
    acc = acc_ref[...]
    news = []
    for d in range(RET_D_BLOCK):
        s_d = st[d * RET_D:(d + 1) * RET_D, :]
        row = j * RET_D_BLOCK + d
        q_d = qt_ref[pl.ds(row, 1), :]
        k_d = kt_ref[pl.ds(row, 1), :]
        acc = acc + q_d * s_d
        news.append(gamma * s_d + k_d * vt)
    acc_ref[...] = acc
    ns_ref[...] = jnp.concatenate(news, axis=0).T

    @pl.when(j == nd - 1)
    def _():
        qk = jnp.sum(qt_ref[...] * kt_ref[...], axis=0, keepdims=True)
        o = qk * vt + acc * gamma
        mu = jnp.mean(o, axis=0, keepdims=True)
        oc = o - mu
        var = jnp.mean(oc * oc, axis=0, keepdims=True)
        ln = (oc * lax.rsqrt(var + EPS)).T
        g = g_ref[...]
        y_ref[...] = (g * _sigmoid(g) * ln).astype(y_ref.dtype)


def _retention_sample(pb, cos2, sin2, state, row0):
    bs = state.shape[0]
    r0 = row0 // bs
    nd = RET_D // RET_D_BLOCK
    blk = RET_D_BLOCK * RET_D
    col = lambda c: pl.BlockSpec((bs, RET_D), lambda h, j, c=c: (r0, c * RET_HEADS + h))
    tab = pl.BlockSpec((1, RET_D), lambda h, j: (0, 0))
    sblk = pl.BlockSpec((bs, blk), lambda h, j: (0, h * nd + j))
    return pl.pallas_call(
        _ret_sample_body,
        grid=(RET_HEADS, nd),
        in_specs=[col(0), col(1), col(2), col(3), tab, tab, sblk],
        out_specs=[pl.BlockSpec((bs, RET_D), lambda h, j: (0, h)), sblk],
        out_shape=[jax.ShapeDtypeStruct((bs, G_WIDTH), BF16), jax.ShapeDtypeStruct(state.shape, F32)],
        scratch_shapes=[pltpu.VMEM((RET_D, bs), F32), pltpu.VMEM((RET_D, bs), F32), pltpu.VMEM((RET_D, bs), F32)],
        compiler_params=_cparams("parallel", "arbitrary"),
        name="retention_sample",
    )(pb, pb, pb, pb, cos2, sin2, state)


def _rwkv_sample_body(xr_ref, xk_ref, xv_ref, xl_ref, pr_ref, pk_ref, pv_ref, pl_ref, mr_ref, mk_ref, mv_ref, ml_ref,
                      w2w_ref, w2a_ref, g2_ref, w0_ref, a0_ref, kk_ref, ka_ref, rk_ref, lg_ref, lb_ref,
                      s_ref, y_ref, ns_ref, vec_ref, ot_ref):
    N = RWKV_HEAD

    def shifted(x_ref, prev_ref, mu_ref):
        x = x_ref[...]
        return x + (prev_ref[...] - x) * mu_ref[...]

    r, lw, k4, v, kk, a, gate = _rwkv_pre(shifted(xr_ref, pr_ref, mr_ref), shifted(xk_ref, pk_ref, mk_ref),
                                          shifted(xv_ref, pv_ref, mv_ref), shifted(xl_ref, pl_ref, ml_ref),
                                          w2w_ref[...], w2a_ref[...], g2_ref[...], w0_ref[...], a0_ref[...],
                                          kk_ref[...], ka_ref[...])
    kkn = jnp.concatenate([_l2_normalize(kk[:, 0:N]), _l2_normalize(kk[:, N:2 * N])], axis=1)
    w = jnp.exp(lw)
    vec_ref[0] = kkn.T
    vec_ref[1] = w.T
    vec_ref[2] = (kkn * a).T
    vec_ref[3] = k4.T
    vec_ref[4] = r.T
    vec_ref[5] = v.T
    st = s_ref[...].T
    news = []
    for hh in range(2):
        ch = slice(hh * N, (hh + 1) * N)
        kk_t = vec_ref[0, ch, :]
        w_t = vec_ref[1, ch, :]
        kka_t = vec_ref[2, ch, :]
        k_t = vec_ref[3, ch, :]
        r_t = vec_ref[4, ch, :]
        for i in range(N):
            base = (hh * N + i) * N
            s_i = st[base:base + N, :]
            sa = -jnp.sum(s_i * kk_t, axis=0, keepdims=True)
            v_i = vec_ref[5, hh * N + i:hh * N + i + 1, :]
            s_new = s_i * w_t + sa * kka_t + v_i * k_t
            news.append(s_new)
            ot_ref[hh * N + i:hh * N + i + 1, :] = jnp.sum(s_new * r_t, axis=0, keepdims=True)
    ns_ref[...] = jnp.concatenate(news, axis=0).T
    o = ot_ref[...].T
    rk = rk_ref[...]
    lg = lg_ref[...]
    lb = lb_ref[...]
    ys = []
    for hh in range(2):
        ch = slice(hh * N, (hh + 1) * N)
        ys.append(_rwkv_post(o[:, ch], r[:, ch], k4[:, ch], v[:, ch], gate[:, ch], rk[:, ch], lg[:, ch], lb[:, ch]))
    y_ref[...] = jnp.concatenate(ys, axis=1).astype(y_ref.dtype)


def _rwkv_sample(pc, prev, mu, w2a, g2, w0, a0, k_k, k_a, r_k, lnx_g, lnx_b, state, row0):
    bs = state.shape[0]
    r0 = row0 // bs
    G = G_WIDTH
    N = RWKV_HEAD
    W = 2 * N
    npair = G // W
    cols = lambda rows, r_idx: [pl.BlockSpec((rows, W), lambda p, c=c: (r_idx, c * npair + p)) for c in range(3)] + [
        pl.BlockSpec((rows, 2 * W), lambda p: (r_idx, 3 * G // (2 * W)))]
    pairv = pl.BlockSpec((1, W), lambda p: (0, p))
    pairm = pl.BlockSpec((128, W), lambda p: (0, p))
    sblk = pl.BlockSpec((bs, W * N), lambda p: (0, p))
    return pl.pallas_call(
        _rwkv_sample_body,
        grid=(npair,),
        in_specs=cols(bs, r0) + cols(bs, 0) + cols(1, 0) + [
            pairm, pl.BlockSpec((128, W), lambda p: (0, npair + p)), pairm,
            pairv, pairv, pairv, pairv, pairv, pairv, pairv, sblk],
        out_specs=[pl.BlockSpec((bs, W), lambda p: (0, p)), sblk],
        out_shape=[jax.ShapeDtypeStruct((bs, G), BF16), jax.ShapeDtypeStruct(state.shape, F32)],
        scratch_shapes=[pltpu.VMEM((6, W, bs), F32), pltpu.VMEM((W, bs), F32)],
        compiler_params=_cparams("parallel"),
        name="rwkv_sample",
    )(pc, pc, pc, pc, prev, prev, prev, prev, mu, mu, mu, mu, w2a, w2a, g2, w0, a0, k_k, k_a, r_k, lnx_g, lnx_b, state)


def kernel(x_prompt, x_sample, mem_prompt, state_conv_a, state_ret, state_shift, state_wkv, state_conv_d, cache_mem_k, cache_mem_v, g_mix, w_in, conv_a_w, mu_c, w0, w2, a0, a2, g2, k_k, k_a, r_k, lnx_g, lnx_b, conv_d_w, conv_d_b, ln_d_g, ln_d_b, w_out, g_xa, g_mem, wq_x, wk_x, wv_x, wo_x, g_mlp, w_up, w_down, g_final):
    bp, seq, d = x_prompt.shape
    bs = x_sample.shape[0]
    depth = w_in.shape[0]
    np_rows = bp * seq
    G = G_WIDTH
    x = jnp.concatenate([x_prompt.reshape(np_rows, d), x_sample.reshape(bs, d)], axis=0)
    mem = mem_prompt.reshape(bp * N_MEM, d)
    cos_p, sin_p = _rope_tables(jnp.arange(seq))
    cos_s, sin_s = _rope_tables(PAST_LEN + jnp.arange(1))
    row = lambda v: v.reshape(1, -1)
    outs = {k: [] for k in ("a_p", "a_s", "ret_p", "ret_s", "sh_p", "sh_s", "wkv_p", "wkv_s", "d_p", "d_s", "mk", "mv")}
    for l in range(depth):
        w_in_l = w_in[l].astype(BF16)
        w_a, w_b, w_c, w_d = (w_in_l[:, 0:3 * G], w_in_l[:, 3 * G:7 * G], w_in_l[:, 7 * G:7 * G + SHIFT_W],
                              w_in_l[:, 7 * G + SHIFT_W:])
        z64 = jnp.zeros((64, G), F32)
        w2a = jnp.concatenate([jnp.concatenate([w2[l], z64], axis=1), jnp.concatenate([z64, a2[l]], axis=1)], axis=0).astype(BF16)
        g2_l = g2[l].astype(BF16)

        h = _rmsnorm(x, g_mix[l], BF16)
        pa = _matmul(h, w_a)
        pb = _matmul(h, w_b)
        pc = _matmul(h, w_c, tn=SHIFT_W)
        pd = _matmul(h, w_d)
        ya_p, na_p = _conv_a_prompt(pa, conv_a_w[l], bp, seq)
        yd_p, nd_p = _conv_d_prompt(pd, conv_d_w[l], conv_d_b[l], ln_d_g[l], ln_d_b[l], bp, seq)
        yb_p, nret_p = _retention_prompt(pb, cos_p, sin_p, bp, seq)
        rw = (row(mu_c[l]), w2a, g2_l, row(w0[l]), row(a0[l]), row(k_k[l]), row(k_a[l]), row(r_k[l]), row(lnx_g[l]), row(lnx_b[l]))
        yc_p, nsh_p, nwkv_p = _rwkv_prompt(pc, *rw, bp, seq)
        ya_s, yd_s, na_s, nd_s = _sample_ad(pa, pd, state_conv_a[l].reshape(bs, -1), state_conv_d[l].reshape(bs, -1),
                                            conv_a_w[l], conv_d_w[l], conv_d_b[l], ln_d_g[l], ln_d_b[l], np_rows)
        yb_s, nret_s = _retention_sample(pb, cos_s, sin_s, state_ret[l].reshape(bs, -1), np_rows)
        yc_s, nwkv_s = _rwkv_sample(pc, state_shift[l], *rw, state_wkv[l].reshape(bs, -1), np_rows)
        y = jnp.concatenate([jnp.concatenate([ya_p, yb_p, yc_p, yd_p], axis=1),
                             jnp.concatenate([ya_s, yb_s, yc_s, yd_s], axis=1)], axis=0)
        x = _matmul(y, w_out[l].astype(BF16), residual=x)

        mn = _rmsnorm(mem, g_mem[l], BF16)
        mk = _matmul(mn, wk_x[l].astype(BF16))
        mv = _matmul(mn, wv_x[l].astype(BF16))
        hq = _rmsnorm(x, g_xa[l], BF16)
        q = _matmul(hq, wq_x[l].astype(BF16))
        o_p = _xattn_prompt(q, mk, mv, bp, seq)
        o_s = _xattn_sample(q[np_rows:], cache_mem_k[l].reshape(bs * N_MEM, d), cache_mem_v[l].reshape(bs * N_MEM, d))
        o = jnp.concatenate([o_p, o_s.astype(BF16)], axis=0)
        x = _matmul(o, wo_x[l].astype(BF16), residual=x)

        hm = _rmsnorm(x, g_mlp[l], BF16)
        up = _matmul(hm, w_up[l].astype(BF16), act="relu2", out_dtype=BF16)
        x = _matmul(up, w_down[l].astype(BF16), residual=x)

        outs["a_p"].append(na_p)
        outs["a_s"].append(na_s.reshape(bs, A_CONV - 1, G))
        outs["ret_p"].append(nret_p)
        outs["ret_s"].append(nret_s.reshape(bs, RET_HEADS, RET_D, RET_D))
        outs["sh_p"].append(nsh_p.reshape(bp, SHIFT_W))
        outs["sh_s"].append(pc[np_rows:])
        outs["wkv_p"].append(nwkv_p)
        outs["wkv_s"].append(nwkv_s.reshape(bs, RWKV_HEADS, RWKV_HEAD, RWKV_HEAD))
        outs["d_p"].append(nd_p)
        outs["d_s"].append(nd_s.reshape(bs, D_CONV - 1, G))
        outs["mk"].append(mk.reshape(bp, N_MEM, XA_HEADS, XA_HD))
        outs["mv"].append(mv.reshape(bp, N_MEM, XA_HEADS, XA_HD))
    yf = _rmsnorm(x, g_final, F32)
    st = lambda k: jnp.stack(outs[k])
    return (yf[:np_rows].reshape(bp, seq, d), yf[np_rows:].reshape(bs, 1, d), st("a_p"), st("a_s"), st("ret_p"), st("ret_s"),
            st("sh_p"), st("sh_s"), st("wkv_p"), st("wkv_s"), st("d_p"), st("d_s"), st("mk"), st("mv"))
```

```python
import functools
import math

import jax
import jax.numpy as jnp
from jax import lax
from jax.experimental import pallas as pl
from jax.experimental.pallas import tpu as pltpu

F32 = jnp.float32
BF16 = jnp.bfloat16
HI = lax.Precision.HIGHEST

D_MODEL = 2048
G_WIDTH = 512
P_IN = 6400
RET_HEADS = 4
RET_D = 128
RET_CHUNK = 128
ROPE_BASE = 10000.0
RWKV_HEAD = 64
RWKV_HEADS = 8
WKV_CHUNK = 64
RWKV_GN_EPS = 64e-5
A_CONV = 3
D_CONV = 31
D_HIST = 32
N_MEM = 256
XA_HEADS = 4
XA_HD = 512
SHIFT_W = 1792
PAST_LEN = 16384
EPS = 1e-6
RET_LOG_DECAY = tuple(math.log1p(-(2.0 ** (-5.0 - h))) for h in range(RET_HEADS))
VMEM_LIMIT = 56 * 1024 * 1024
COL_A = 0
COL_B = 3 * G_WIDTH
COL_C = 7 * G_WIDTH
COL_D = 7 * G_WIDTH + SHIFT_W

ANY_SPEC = pl.BlockSpec(memory_space=pl.ANY)


def _cparams(*sem):
    return pltpu.CompilerParams(dimension_semantics=sem, vmem_limit_bytes=VMEM_LIMIT)


def _pick(n, cands):
    for c in cands:
        if n % c == 0:
            return c
    raise ValueError(f"no tile for {n}")


def _sigmoid(x):
    return 1.0 / (1.0 + jnp.exp(-x))


def _dot(a, b, precision=None):
    return jnp.dot(a, b, preferred_element_type=F32, precision=precision)


def _dot_nt(a, b, precision=None):
    return lax.dot_general(a, b, (((1,), (1,)), ((), ())), preferred_element_type=F32, precision=precision)


def _dot_tn(a, b, precision=None):
    return lax.dot_general(a, b, (((0,), (0,)), ((), ())), preferred_element_type=F32, precision=precision)


def _layernorm(x, eps):
    mu = jnp.mean(x, axis=-1, keepdims=True)
    xc = x - mu
    var = jnp.mean(xc * xc, axis=-1, keepdims=True)
    return xc * lax.rsqrt(var + eps)


def _layer_vec(a, l):
    n = a.shape[-1]
    return a.reshape(a.shape[0], 1, n), pl.BlockSpec((None, 1, n), lambda *_: (l, 0, 0))


def _layer_mat(a, l):
    return pl.BlockSpec((None,) + a.shape[1:], lambda *_: (l, 0, 0))


def _rmsnorm_body(x_ref, g_ref, o_ref):
    x = x_ref[...]
    ms = jnp.mean(x * x, axis=-1, keepdims=True)
    o_ref[...] = ((x * lax.rsqrt(ms + EPS)) * g_ref[...]).astype(o_ref.dtype)


def _rmsnorm(x, g_spec, out_dtype, row0=0, nrows=None):
    m, d = x.shape
    nrows = nrows or m
    tm = _pick(math.gcd(nrows, row0) if row0 else nrows, (1040, 1024, 640, 512, 256, 128))
    r0 = row0 // tm
    g, gspec = g_spec
    return pl.pallas_call(
        _rmsnorm_body,
        grid=(nrows // tm,),
        in_specs=[pl.BlockSpec((tm, d), lambda i: (r0 + i, 0)), gspec],
        out_specs=pl.BlockSpec((tm, d), lambda i: (i, 0)),
        out_shape=jax.ShapeDtypeStruct((nrows, d), out_dtype),
        compiler_params=_cparams("parallel"),
        name="rmsnorm",
    )(x, g)


def _mm_body(*refs, act, has_res, nk):
    a_ref, w_ref = refs[0], refs[1]
    res_ref = refs[2] if has_res else None
    o_ref = refs[3] if has_res else refs[2]

    def finish(r):
        if act == "relu2":
            r = jnp.square(jnp.maximum(r, 0.0))
        if has_res:
            r = res_ref[...] + r
        o_ref[...] = r.astype(o_ref.dtype)

    if nk == 1:
        finish(_dot(a_ref[...], w_ref[...]))
        return
    acc_ref = refs[-1]
    k = pl.program_id(2)

    @pl.when(k == 0)
    def _():
        acc_ref[...] = jnp.zeros_like(acc_ref)

    acc_ref[...] += _dot(a_ref[...], w_ref[...])

    @pl.when(k == nk - 1)
    def _():
        finish(acc_ref[...])


def _matmul(a, w, l, *, residual=None, act=None, out_dtype=F32, tn=None):
    m, kdim = a.shape
    n = w.shape[2]
    tm = _pick(m, (1040, 1024, 640, 512, 256, 128))
    tn = tn or _pick(n, (1024, 896, 768, 512, 256))
    tk = _pick(kdim, (2048, 1024, 512))
    nk = kdim // tk
    has_res = residual is not None
    in_specs = [pl.BlockSpec((tm, tk), lambda i, j, k: (i, k)), pl.BlockSpec((None, tk, tn), lambda i, j, k: (l, k, j))]
    args = [a, w]
    if has_res:
        in_specs.append(pl.BlockSpec((tm, tn), lambda i, j, k: (i, j)))
        args.append(residual)
    return pl.pallas_call(
        functools.partial(_mm_body, act=act, has_res=has_res, nk=nk),
        grid=(m // tm, n // tn, nk),
        in_specs=in_specs,
        out_specs=pl.BlockSpec((tm, tn), lambda i, j, k: (i, j)),
        out_shape=jax.ShapeDtypeStruct((m, n), out_dtype),
        scratch_shapes=[pltpu.VMEM((tm, tn), F32)] if nk > 1 else [],
        compiler_params=_cparams("parallel", "parallel", "arbitrary"),
        name="matmul",
    )(*args)


def _conv_a_body(b_ref, c_ref, h_ref, w_ref, y_ref, st_ref, ext_ref, *, tl):
    t = pl.program_id(1)

    @pl.when(t == 0)
    def _():
        ext_ref[0:8, :] = jnp.zeros((8, G_WIDTH), F32)

    u = c_ref[...] * h_ref[...]
    ext_ref[8:, :] = u
    u1 = ext_ref[7:7 + tl, :]
    u2 = ext_ref[6:6 + tl, :]
    y = b_ref[...] * (w_ref[0:1, :] * u2 + w_ref[1:2, :] * u1 + w_ref[2:3, :] * u)
    y_ref[...] = y.astype(y_ref.dtype)
    st_ref[0] = ext_ref[tl + 6:tl + 8, :]
    ext_ref[0:8, :] = ext_ref[tl:tl + 8, :]


def _conv_a_prompt(proj, conv_w, l, bp, seq):
    tl = _pick(seq, (512, 256, 128))
    nt = seq // tl
    c0 = COL_A // G_WIDTH
    col = lambda j: pl.BlockSpec((tl, G_WIDTH), lambda b, t: (b * nt + t, c0 + j))
    return pl.pallas_call(
        functools.partial(_conv_a_body, tl=tl),
        grid=(bp, nt),
        in_specs=[col(0), col(1), col(2), _layer_mat(conv_w, l)],
        out_specs=[pl.BlockSpec((tl, G_WIDTH), lambda b, t: (b * nt + t, 0)),
                   pl.BlockSpec((1, A_CONV - 1, G_WIDTH), lambda b, t: (b, 0, 0))],
        out_shape=[jax.ShapeDtypeStruct((proj.shape[0], D_MODEL), BF16),
                   jax.ShapeDtypeStruct((bp, A_CONV - 1, G_WIDTH), F32)],
        scratch_shapes=[pltpu.VMEM((tl + 8, G_WIDTH), F32)],
        compiler_params=_cparams("parallel", "arbitrary"),
        name="conv_a_prompt",
    )(proj, proj, proj, conv_w)


CONV_D_ROWS = 32


def _conv_d_body(d1a_ref, d1b_ref, d2a_ref, d2b_ref, w_ref, bias_ref, g_ref, beta_ref, ymix_ref, y_ref, st_ref, ext_ref, *, tl):
    del ymix_ref
    t = pl.program_id(1)

    @pl.when(t == 0)
    def _():
        ext_ref[0:D_HIST, :] = jnp.zeros((D_HIST, G_WIDTH), F32)

    d1 = jnp.concatenate([d1a_ref[...], d1b_ref[...]], axis=1)
    d2 = jnp.concatenate([d2a_ref[...], d2b_ref[...]], axis=1)
    ext_ref[D_HIST:, :] = d1 * _sigmoid(d2)
    first = D_HIST - (D_CONV - 1)

    def blk(i, carry):
        base = pl.multiple_of(i * CONV_D_ROWS, CONV_D_ROWS)
        win = ext_ref[pl.ds(base, CONV_D_ROWS + D_HIST), :]
        acc = jnp.broadcast_to(bias_ref[...], (CONV_D_ROWS, G_WIDTH))
        for k in range(D_CONV):
            acc = acc + win[first + k:first + k + CONV_D_ROWS, :] * w_ref[k:k + 1, :]
        c = _layernorm(acc, EPS) * g_ref[...] + beta_ref[...]
        y_ref[pl.ds(base, CONV_D_ROWS), :] = (c * _sigmoid(c)).astype(y_ref.dtype)
        return carry

    lax.fori_loop(0, tl // CONV_D_ROWS, blk, 0)
    st_ref[0] = ext_ref[tl + first:tl + D_HIST, :]
    ext_ref[0:D_HIST, :] = ext_ref[tl:tl + D_HIST, :]


def _conv_d_prompt(proj, w, bias, g, beta, ymix, l, bp, seq):
    tl = _pick(seq, (512, 256, 128))
    nt = seq // tl
    half = G_WIDTH // 2
    c0 = COL_D // half
    col = lambda j: pl.BlockSpec((tl, half), lambda b, t: (b * nt + t, c0 + j))
    (bias, bias_s), (g, g_s), (beta, beta_s) = _layer_vec(bias, l), _layer_vec(g, l), _layer_vec(beta, l)
    return pl.pallas_call(
        functools.partial(_conv_d_body, tl=tl),
        grid=(bp, nt),
        in_specs=[col(0), col(1), col(2), col(3), _layer_mat(w, l), bias_s, g_s, beta_s, ANY_SPEC],
        out_specs=[pl.BlockSpec((tl, G_WIDTH), lambda b, t: (b * nt + t, 3)),
                   pl.BlockSpec((1, D_CONV - 1, G_WIDTH), lambda b, t: (b, 0, 0))],
        out_shape=[jax.ShapeDtypeStruct(ymix.shape, ymix.dtype),
                   jax.ShapeDtypeStruct((bp, D_CONV - 1, G_WIDTH), F32)],
        scratch_shapes=[pltpu.VMEM((tl + D_HIST, G_WIDTH), F32)],
        input_output_aliases={8: 0},
        compiler_params=_cparams("parallel", "arbitrary"),
        name="conv_d_prompt",
    )(proj, proj, proj, proj, w, bias, g, beta, ymix)


def _rope_tables(pos):
    inv = ROPE_BASE ** (-jnp.arange(0, RET_D, 2, dtype=F32) / RET_D)
    ang = pos.astype(F32)[:, None] * inv[None, :]
    cos = jnp.cos(ang)
    sin = jnp.sin(ang)
    return jnp.concatenate([cos, cos], axis=-1), jnp.concatenate([-sin, sin], axis=-1)


def _rope(x, cos2, sin2):
    return x * cos2 + pltpu.roll(x, RET_D // 2, 1) * sin2


def _ret_body(q_ref, k_ref, v_ref, g_ref, cos_ref, sin_ref, ymix_ref, y_ref, st_ref, s_ref, *, nchunks):
    del ymix_ref
    c = pl.program_id(1)
    C = RET_CHUNK

    @pl.when(c == 0)
    def _():
        s_ref[...] = jnp.zeros_like(s_ref)

    ii = lax.broadcasted_iota(jnp.int32, (C, C), 0).astype(F32)
    jj = lax.broadcasted_iota(jnp.int32, (C, C), 1).astype(F32)
    diff = ii - jj
    cos2 = cos_ref[...]
    sin2 = sin_ref[...]
    for h in range(RET_HEADS):
        lg = RET_LOG_DECAY[h]
        sl = slice(h * RET_D, (h + 1) * RET_D)
        q = _rope(q_ref[:, sl], cos2, sin2)
        k = _rope(k_ref[:, sl], cos2, sin2) * (RET_D ** -0.5)
        v = v_ref[:, sl]
        dmat = jnp.where(diff >= 0.0, jnp.exp(lg * jnp.maximum(diff, 0.0)), 0.0)
        scores = _dot_nt(q.astype(BF16), k.astype(BF16)) * dmat
        inner = _dot(scores.astype(BF16), v.astype(BF16))
        s_old = s_ref[h]
        cross = _dot(q.astype(BF16), s_old.astype(BF16)) * jnp.exp(lg * (ii + 1.0))
        k_dec = k * jnp.exp(lg * (C - 1.0 - ii))
        s_ref[h] = math.exp(lg * C) * s_old + _dot_tn(k_dec.astype(BF16), v.astype(BF16))
        o = _layernorm(inner + cross, EPS)
        g = g_ref[:, sl]
        y_ref[:, sl] = (g * _sigmoid(g) * o).astype(y_ref.dtype)

    @pl.when(c == nchunks - 1)
    def _():
        st_ref[0] = s_ref[...]


def _retention_prompt(proj, cos2, sin2, ymix, bp, seq):
    C = RET_CHUNK
    nc = seq // C
    c0 = COL_B // G_WIDTH
    col = lambda j: pl.BlockSpec((C, G_WIDTH), lambda b, c: (b * nc + c, c0 + j))
    tab = pl.BlockSpec((C, RET_D), lambda b, c: (c, 0))
    return pl.pallas_call(
        functools.partial(_ret_body, nchunks=nc),
        grid=(bp, nc),
        in_specs=[col(0), col(1), col(2), col(3), tab, tab, ANY_SPEC],
        out_specs=[pl.BlockSpec((C, G_WIDTH), lambda b, c: (b * nc + c, 1)),
                   pl.BlockSpec((1, RET_HEADS, RET_D, RET_D), lambda b, c: (b, 0, 0, 0))],
        out_shape=[jax.ShapeDtypeStruct(ymix.shape, ymix.dtype),
                   jax.ShapeDtypeStruct((bp, RET_HEADS, RET_D, RET_D), F32)],
        scratch_shapes=[pltpu.VMEM((RET_HEADS, RET_D, RET_D), F32)],
        input_output_aliases={6: 0},
        compiler_params=_cparams("parallel", "arbitrary"),
        name="retention_prompt",
    )(proj, proj, proj, proj, cos2, sin2, ymix)


def _softplus(z):
    return jnp.maximum(z, 0.0) + jnp.log1p(jnp.exp(-jnp.abs(z)))


def _rwkv_pre(r, kc, vc, lora_x, w2_w, w2_a, g2, w0, a0, k_k, k_a):
    lin = lora_x[:, 0:128]
    lane = lax.broadcasted_iota(jnp.int32, lin.shape, 1)
    lin = jnp.where(lane < 64, jnp.tanh(lin), lin).astype(BF16)
    gate = _dot(_sigmoid(lora_x[:, 128:256]).astype(BF16), g2)
    w_pre = -_softplus(-(w0 + _dot(lin, w2_w))) - 0.5
    log_decay = -jnp.exp(w_pre)
    a = _sigmoid(a0 + _dot(lin, w2_a))
    kk = kc * k_k
    k4 = kc * (1.0 + (a - 1.0) * k_a)
    return r, log_decay, k4, vc, kk, a, gate


def _l2_normalize(kk):
    nrm = jnp.sqrt(jnp.sum(kk * kk, axis=-1, keepdims=True))
    return kk / jnp.maximum(nrm, 1e-12)


def _bdot(a, b):
    return _dot(a.astype(BF16), b.astype(BF16))


def _wkv_chunk(r, lw, cum, k, v, kk, a, z, mask2, eye):
    T = WKV_CHUNK
    H = range(len(r))
    g_inc = [jnp.exp(cum[h]) for h in H]
    g_tot = [g_inc[h][T - 1:T, :] for h in H]
    ah = [-(kk[h] * jnp.exp(cum[h] - lw[h])) for h in H]
    rg = [r[h] * g_inc[h] for h in H]
    g_inv = [jnp.exp(-cum[h]) for h in H]
    bd = [kk[h] * a[h] * g_inv[h] for h in H]
    kd = [k[h] * g_inv[h] for h in H]
    ar = [jnp.concatenate([ah[h], rg[h]], axis=0).astype(BF16) for h in H]
    xb = [jnp.where(mask2, _dot_nt(ar[h], bd[h].astype(BF16)), 0.0) for h in H]
    xk = [jnp.where(mask2, _dot_nt(ar[h], kd[h].astype(BF16)), 0.0).astype(BF16) for h in H]
    vb = [v[h].astype(BF16) for h in H]
    xkv = [_dot(xk[h], vb[h]) for h in H]
    acur = [xb[h][0:T] for h in H]
    p = [eye for _ in H]
    for it in range(5):
        x = [_bdot(jnp.concatenate([acur[h], p[h]], axis=0), acur[h]) for h in H]
        acur = [x[h][0:T] for h in H]
        p = [p[h] + x[h][T:2 * T] for h in H]
    x = [_bdot(p[h], acur[h]) for h in H]
    p = [p[h] + x[h] for h in H]
    wub = [_bdot(p[h], jnp.concatenate([ah[h], xkv[h][0:T]], axis=1)).astype(BF16) for h in H]
    qo = [_dot(xb[h][T:2 * T].astype(BF16), wub[h]) for h in H]
    lhs_t = [jnp.concatenate([bd[h] * g_tot[h], kd[h] * g_tot[h]], axis=0).astype(BF16) for h in H]
    zero = jnp.zeros((T, 64), BF16)
    mn = [_dot_tn(lhs_t[h], jnp.concatenate([wub[h], jnp.concatenate([zero, vb[h]], axis=1)], axis=0)) for h in H]
    qm = [jnp.concatenate([rg[h] + qo[h][:, 0:64], mn[h][:, 0:64] + jnp.where(eye > 0.0, g_tot[h], 0.0)], axis=0) for h in H]
    oz = [_bdot(qm[h], z[h]) for h in H]
    o = [oz[h][0:T] + qo[h][:, 64:128] + xkv[h][T:2 * T] for h in H]
    z_new = [oz[h][T:2 * T] + mn[h][:, 64:128] for h in H]
    return o, z_new


def _rwkv_post(o, r, k4, v, gate, rk, lnx_g, lnx_b):
    gn = _layernorm(o, RWKV_GN_EPS) * lnx_g + lnx_b
    bonus = jnp.sum(r * k4 * rk, axis=-1, keepdims=True) * v
    return (gn + bonus) * gate


def _rwkv_prompt_body(pc_ref, mu_ref, w2a_ref, g2_ref, w0_ref, a0_ref, kk_ref, ka_ref, rk_ref, lg_ref, lb_ref, ymix_ref,
                      y_ref, shift_ref, st_ref, carry_ref, z_ref, *, nchunks):
    del ymix_ref
    c = pl.program_id(1)
    T = WKV_CHUNK

    @pl.when(c == 0)
    def _():
        carry_ref[...] = jnp.zeros_like(carry_ref)
        z_ref[...] = jnp.zeros_like(z_ref)

    pc = pc_ref[...]
    row = lax.broadcasted_iota(jnp.int32, pc.shape, 0)
    prev = jnp.where(row == 0, carry_ref[0:1, :], pltpu.roll(pc, 1, 0))
    carry_ref[0:1, :] = pc[T - 1:T, :]
    shift_ref[0] = pc[T - 1:T, :]
    xs = pc + (prev - pc) * mu_ref[...]
    G = G_WIDTH
    r, lw, k4, v, kk, a, gate = _rwkv_pre(xs[:, 0:G], xs[:, G:2 * G], xs[:, 2 * G:3 * G], xs[:, 3 * G:3 * G + 256],
                                          w2a_ref[:, 0:G], w2a_ref[:, G:2 * G], g2_ref[...], w0_ref[...], a0_ref[...],
                                          kk_ref[...], ka_ref[...])
    ti = lax.broadcasted_iota(jnp.int32, (T, T), 0)
    si = lax.broadcasted_iota(jnp.int32, (T, T), 1)
    tril = (ti >= si).astype(F32)
    eye = (ti == si).astype(F32)
    cum = _dot(tril, lw, HI)
    t2 = lax.broadcasted_iota(jnp.int32, (2 * T, T), 0)
    s2 = lax.broadcasted_iota(jnp.int32, (2 * T, T), 1)
    mask2 = jnp.where(t2 < T, t2 - 1, t2 - T) >= s2
    heads = lambda x: [x[:, h * RWKV_HEAD:(h + 1) * RWKV_HEAD] for h in range(RWKV_HEADS)]
    r_h, k_h, v_h = heads(r), heads(k4), heads(v)
    o, z_new = _wkv_chunk(r_h, heads(lw), heads(cum), k_h, v_h, [_l2_normalize(x) for x in heads(kk)], heads(a),
                          [z_ref[h] for h in range(RWKV_HEADS)], mask2, eye)
    for h in range(RWKV_HEADS):
        z_ref[h] = z_new[h]
    gate_h, rk_h, lg_h, lb_h = heads(gate), heads(rk_ref[...]), heads(lg_ref[...]), heads(lb_ref[...])
    ys = [_rwkv_post(o[h], r_h[h], k_h[h], v_h[h], gate_h[h], rk_h[h], lg_h[h], lb_h[h]) for h in range(RWKV_HEADS)]
    y_ref[...] = jnp.concatenate(ys, axis=1).astype(y_ref.dtype)

    @pl.when(c == nchunks - 1)
    def _():
        for h in range(RWKV_HEADS):
            st_ref[0, h] = z_ref[h].T


def _rwkv_prompt(proj, rw, ymix, l, bp, seq):
    T = WKV_CHUNK
    nc = seq // T
    G = G_WIDTH
    arrs, specs = zip(*rw)
    return pl.pallas_call(
        functools.partial(_rwkv_prompt_body, nchunks=nc),
        grid=(bp, nc),
        in_specs=[pl.BlockSpec((T, SHIFT_W), lambda b, c: (b * nc + c, COL_C // SHIFT_W))] + list(specs) + [ANY_SPEC],
        out_specs=[pl.BlockSpec((T, G), lambda b, c: (b * nc + c, 2)),
                   pl.BlockSpec((1, 1, SHIFT_W), lambda b, c: (b, 0, 0)),
                   pl.BlockSpec((1, RWKV_HEADS, RWKV_HEAD, RWKV_HEAD), lambda b, c: (b, 0, 0, 0))],
        out_shape=[jax.ShapeDtypeStruct(ymix.shape, ymix.dtype),
                   jax.ShapeDtypeStruct((bp, 1, SHIFT_W), F32),
                   jax.ShapeDtypeStruct((bp, RWKV_HEADS, RWKV_HEAD, RWKV_HEAD), F32)],
        scratch_shapes=[pltpu.VMEM((8, SHIFT_W), F32), pltpu.VMEM((RWKV_HEADS, RWKV_HEAD, RWKV_HEAD), F32)],
        input_output_aliases={1 + len(arrs): 0},
        compiler_params=_cparams("parallel", "arbitrary"),
        name="rwkv_prompt",
    )(proj, *arrs, ymix)


def _xattn_prompt_body(q_ref, k_ref, v_ref, o_ref, kb_ref, vb_ref):
    @pl.when(pl.program_id(1) == 0)
    def _():
        kb_ref[...] = k_ref[...].astype(BF16)
        vb_ref[...] = v_ref[...].astype(BF16)

    for h in range(XA_HEADS):
        sl = slice(h * XA_HD, (h + 1) * XA_HD)
        s = _dot_nt(q_ref[:, sl].astype(BF16), kb_ref[:, sl]) * (XA_HD ** -0.5)
        e = jnp.exp(s - jnp.max(s, axis=-1, keepdims=True))
        p = e / jnp.sum(e, axis=-1, keepdims=True)
        o_ref[:, sl] = _dot(p.astype(BF16), vb_ref[:, sl]).astype(o_ref.dtype)


def _xattn_prompt(q, mk, mv, bp, seq):
    tq = _pick(seq, (512, 256, 128))
    nt = seq // tq
    kv = pl.BlockSpec((N_MEM, D_MODEL), lambda b, t: (b, 0))
    return pl.pallas_call(
        _xattn_prompt_body,
        grid=(bp, nt),
        in_specs=[pl.BlockSpec((tq, D_MODEL), lambda b, t: (b * nt + t, 0)), kv, kv],
        out_specs=pl.BlockSpec((tq, D_MODEL), lambda b, t: (b * nt + t, 0)),
        out_shape=jax.ShapeDtypeStruct((q.shape[0], D_MODEL), BF16),
        scratch_shapes=[pltpu.VMEM((N_MEM, D_MODEL), BF16), pltpu.VMEM((N_MEM, D_MODEL), BF16)],
        compiler_params=_cparams("parallel", "arbitrary"),
        name="xattn_prompt",
    )(q, mk, mv)


XA_SAMPLES_PER_STEP = 2


def _xattn_sample_body(q_ref, k_ref, v_ref, obuf_ref, o_ref, acc_ref, *, nsteps):
    del obuf_ref
    i = pl.program_id(0)
    for j in range(XA_SAMPLES_PER_STEP):
        b = i * XA_SAMPLES_PER_STEP + j
        s = jnp.sum(k_ref[j] * q_ref[b], axis=-1, keepdims=True) * (XA_HD ** -0.5)
        e = jnp.exp(s - jnp.max(s, axis=0, keepdims=True))
        p = e / jnp.sum(e, axis=0, keepdims=True)
        o = jnp.sum(p * v_ref[j], axis=0)
        acc_ref[pl.ds(b, 1), :] = jnp.concatenate([o[h:h + 1, :] for h in range(XA_HEADS)], axis=1)

    @pl.when(i == nsteps - 1)
    def _():
        o_ref[...] = acc_ref[...].astype(o_ref.dtype)


def _xattn_sample(q, ck, cv, obuf, l, row0):
    bs = q.shape[0]
    nb = XA_SAMPLES_PER_STEP
    kv = pl.BlockSpec((None, nb, N_MEM, XA_HEADS, XA_HD), lambda i: (l, i, 0, 0, 0))
    return pl.pallas_call(
        functools.partial(_xattn_sample_body, nsteps=bs // nb),
        grid=(bs // nb,),
        in_specs=[pl.BlockSpec((bs, XA_HEADS, XA_HD), lambda i: (0, 0, 0)), kv, kv, ANY_SPEC],
        out_specs=pl.BlockSpec((bs, D_MODEL), lambda i: (row0 // bs, 0)),
        out_shape=jax.ShapeDtypeStruct(obuf.shape, obuf.dtype),
        scratch_shapes=[pltpu.VMEM((bs, D_MODEL), F32)],
        input_output_aliases={3: 0},
        compiler_params=_cparams("arbitrary"),
        name="xattn_sample",
    )(q, ck, cv, obuf)


def _alias_args(prev):
    return ([], []) if prev is None else ([prev], [ANY_SPEC])


def _sample_ad_body(*refs, has_prev):
    (ab_ref, ac_ref, ah_ref, sta_ref, wa_ref, d1a_ref, d1b_ref, d2a_ref, d2b_ref, std_ref, wd_ref, bias_ref, g_ref,
     beta_ref) = refs[:14]
    ya_ref, yd_ref, na_ref, nd_ref = refs[14 + 2 * has_prev:]
    u = ac_ref[...] * ah_ref[...]
    s0 = sta_ref[:, 0, :]
    s1 = sta_ref[:, 1, :]
    ya = ab_ref[...] * (wa_ref[0:1, :] * s0 + wa_ref[1:2, :] * s1 + wa_ref[2:3, :] * u)
    ya_ref[...] = ya.astype(ya_ref.dtype)
    na_ref[:, 0, :] = s1
    na_ref[:, 1, :] = u
    d1 = jnp.concatenate([d1a_ref[...], d1b_ref[...]], axis=1)
    d2 = jnp.concatenate([d2a_ref[...], d2b_ref[...]], axis=1)
    ud = d1 * _sigmoid(d2)
    acc = jnp.broadcast_to(bias_ref[...], ud.shape)
    nh = D_CONV - 1
    for k in range(nh):
        s_k = std_ref[:, k, :]
        acc = acc + s_k * wd_ref[k:k + 1, :]
        if k > 0:
            nd_ref[:, k - 1, :] = s_k
    acc = acc + ud * wd_ref[nh:nh + 1, :]
    nd_ref[:, nh - 1, :] = ud
    c = _layernorm(acc, EPS) * g_ref[...] + beta_ref[...]
    yd_ref[...] = (c * _sigmoid(c)).astype(yd_ref.dtype)


def _sample_ad(proj, st_a, st_d, wa, wd, bias, g, beta, l, row0, prev_a, prev_d):
    bs = st_a.shape[1]
    G = G_WIDTH
    tb = 32
    r0 = row0 // tb
    half = G // 2
    cola = lambda j: pl.BlockSpec((tb, G), lambda i: (r0 + i, COL_A // G + j))
    cold = lambda j: pl.BlockSpec((tb, half), lambda i: (r0 + i, COL_D // half + j))
    st = lambda n: pl.BlockSpec((None, tb, n, G), lambda i: (l, i, 0, 0))
    yrow = pl.BlockSpec((tb, G), lambda i: (i, 0))
    (bias, bias_s), (g, g_s), (beta, beta_s) = _layer_vec(bias, l), _layer_vec(g, l), _layer_vec(beta, l)
    nh = D_CONV - 1
    has_prev = prev_a is not None
    extra = [prev_a, prev_d] if has_prev else []
    return pl.pallas_call(
        functools.partial(_sample_ad_body, has_prev=has_prev),
        grid=(bs // tb,),
        in_specs=[cola(0), cola(1), cola(2), st(A_CONV - 1), _layer_mat(wa, l), cold(0), cold(1), cold(2), cold(3), st(nh),
                  _layer_mat(wd, l), bias_s, g_s, beta_s] + [ANY_SPEC] * len(extra),
        out_specs=[yrow, yrow, st(A_CONV - 1), st(nh)],
        out_shape=[jax.ShapeDtypeStruct((bs, G), BF16), jax.ShapeDtypeStruct((bs, G), BF16),
                   jax.ShapeDtypeStruct(st_a.shape, F32), jax.ShapeDtypeStruct(st_d.shape, F32)],
        input_output_aliases={14: 2, 15: 3} if has_prev else {},
        compiler_params=_cparams("parallel"),
        name="sample_conv_ad",
    )(proj, proj, proj, st_a, wa, proj, proj, proj, proj, st_d, wd, bias, g, beta, *extra)


RET_D_BLOCK = 16


def _ret_sample_body(*refs, has_prev):
    q_ref, k_ref, v_ref, g_ref, cos_ref, sin_ref, s_ref = refs[:7]
    y_ref, ns_ref, qt_ref, kt_ref, acc_ref = refs[7 + has_prev:]
    h = pl.program_id(0)
    j = pl.program_id(1)
    nd = RET_D // RET_D_BLOCK
    gammas = [math.exp(lg) for lg in RET_LOG_DECAY]
    gamma = jnp.where(h == 0, gammas[0], jnp.where(h == 1, gammas[1], jnp.where(h == 2, gammas[2], gammas[3])))
    gamma = gamma.astype(F32)

    @pl.when(j == 0)
    def _():
        q = _rope(q_ref[...], cos_ref[...], sin_ref[...])
        k = _rope(k_ref[...], cos_ref[...], sin_ref[...]) * (RET_D ** -0.5)
        qt_ref[...] = q.T
        kt_ref[...] = k.T
        acc_ref[...] = jnp.zeros_like(acc_ref)

    vt = v_ref[...].T
    acc = acc_ref[...]
    for d in range(RET_D_BLOCK):
        s_d = s_ref[:, d, :].T
        row = j * RET_D_BLOCK + d
        q_d = qt_ref[pl.ds(row, 1), :]
        k_d = kt_ref[pl.ds(row, 1), :]
        acc = acc + q_d * s_d
        ns_ref[:, d, :] = (gamma * s_d + k_d * vt).T
    acc_ref[...] = acc

    @pl.when(j == nd - 1)
    def _():
        qk = jnp.sum(qt_ref[...] * kt_ref[...], axis=0, keepdims=True)
        o = qk * vt + acc * gamma
        mu = jnp.mean(o, axis=0, keepdims=True)
        oc = o - mu
        var = jnp.mean(oc * oc, axis=0, keepdims=True)
        ln = (oc * lax.rsqrt(var + EPS)).T
        g = g_ref[...]
        y_ref[...] = (g * _sigmoid(g) * ln).astype(y_ref.dtype)


def _retention_sample(proj, cos2, sin2, state, l, row0, prev):
    bs = state.shape[1]
    r0 = row0 // bs
    nd = RET_D // RET_D_BLOCK
    col = lambda c: pl.BlockSpec((bs, RET_D), lambda h, j: (r0, COL_B // RET_D + c * RET_HEADS + h))
    tab = pl.BlockSpec((1, RET_D), lambda h, j: (0, 0))
    sblk = pl.BlockSpec((None, bs, None, RET_D_BLOCK, RET_D), lambda h, j: (l, 0, h, j, 0))
    extra, extra_specs = _alias_args(prev)
    return pl.pallas_call(
        functools.partial(_ret_sample_body, has_prev=prev is not None),
        grid=(RET_HEADS, nd),
        in_specs=[col(0), col(1), col(2), col(3), tab, tab, sblk] + extra_specs,
        out_specs=[pl.BlockSpec((bs, RET_D), lambda h, j: (0, h)), sblk],
        out_shape=[jax.ShapeDtypeStruct((bs, G_WIDTH), BF16), jax.ShapeDtypeStruct(state.shape, F32)],
        scratch_shapes=[pltpu.VMEM((RET_D, bs), F32), pltpu.VMEM((RET_D, bs), F32), pltpu.VMEM((RET_D, bs), F32)],
        input_output_aliases={7: 1} if prev is not None else {},
        compiler_params=_cparams("parallel", "arbitrary"),
        name="retention_sample",
    )(proj, proj, proj, proj, cos2, sin2, state, *extra)


def _rwkv_sample_body(*refs, has_prev):
    (xr_ref, xk_ref, xv_ref, xl_ref, pr_ref, pk_ref, pv_ref, pl_ref, mr_ref, mk_ref, mv_ref, ml_ref,
     w2w_ref, w2a_ref, g2_ref, w0_ref, a0_ref, kk_ref, ka_ref, rk_ref, lg_ref, lb_ref, s_ref) = refs[:23]
    y_ref, ns_ref, vec_ref, ot_ref = refs[23 + has_prev:]
    N = RWKV_HEAD

    def shifted(x_ref, prev_ref, mu_ref):
        x = x_ref[...]
        return x + (prev_ref[...] - x) * mu_ref[...]

    r, lw, k4, v, kk, a, gate = _rwkv_pre(shifted(xr_ref, pr_ref, mr_ref), shifted(xk_ref, pk_ref, mk_ref),
                                          shifted(xv_ref, pv_ref, mv_ref), shifted(xl_ref, pl_ref, ml_ref),
                                          w2w_ref[...], w2a_ref[...], g2_ref[...], w0_ref[...], a0_ref[...],
                                          kk_ref[...], ka_ref[...])
    kkn = jnp.concatenate([_l2_normalize(kk[:, 0:N]), _l2_normalize(kk[:, N:2 * N])], axis=1)
    w = jnp.exp(lw)
    vec_ref[0] = kkn.T
    vec_ref[1] = w.T
    vec_ref[2] = (kkn * a).T
    vec_ref[3] = k4.T
    vec_ref[4] = r.T
    vec_ref[5] = v.T
    st = s_ref[...].T
    news = []
    for hh in range(2):
        ch = slice(hh * N, (hh + 1) * N)
        kk_t = vec_ref[0, ch, :]
        w_t = vec_ref[1, ch, :]
        kka_t = vec_ref[2, ch, :]
        k_t = vec_ref[3, ch, :]
        r_t = vec_ref[4, ch, :]
        for i in range(N):
            base = (hh * N + i) * N
            s_i = st[base:base + N, :]
            sa = -jnp.sum(s_i * kk_t, axis=0, keepdims=True)
            v_i = vec_ref[5, hh * N + i:hh * N + i + 1, :]
            s_new = s_i * w_t + sa * kka_t + v_i * k_t
            news.append(s_new)
            ot_ref[hh * N + i:hh * N + i + 1, :] = jnp.sum(s_new * r_t, axis=0, keepdims=True)
    ns_ref[...] = jnp.concatenate(news, axis=0).T
    o = ot_ref[...].T
    rk = rk_ref[...]
    lg = lg_ref[...]
    lb = lb_ref[...]
    ys = []
    for hh in range(2):
        ch = slice(hh * N, (hh + 1) * N)
        ys.append(_rwkv_post(o[:, ch], r[:, ch], k4[:, ch], v[:, ch], gate[:, ch], rk[:, ch], lg[:, ch], lb[:, ch]))
    y_ref[...] = jnp.concatenate(ys, axis=1).astype(y_ref.dtype)


def _rwkv_sample(proj, prev_shift, rw_arrs, state, l, row0, prev):
    mu, w2a, g2, w0, a0, k_k, k_a, r_k, lnx_g, lnx_b = rw_arrs
    bs = state.shape[1]
    r0 = row0 // bs
    G = G_WIDTH
    N = RWKV_HEAD
    W = 2 * N
    npair = G // W
    def cols(shape_lead, lead_idx, base):
        mk = lambda width, cidx: pl.BlockSpec(shape_lead + (width,), lambda p: lead_idx + (cidx(p),))
        return [mk(W, lambda p, c=c: base // W + c * npair + p) for c in range(3)] + [
            mk(2 * W, lambda p: (base + 3 * G) // (2 * W))]
    pairv = pl.BlockSpec((None, 1, W), lambda p: (l, 0, p))
    pairm = pl.BlockSpec((None, 128, W), lambda p: (l, 0, p))
    sblk = pl.BlockSpec((None, bs, W * N), lambda p: (l, 0, p))
    extra, extra_specs = _alias_args(prev)
    return pl.pallas_call(
        functools.partial(_rwkv_sample_body, has_prev=prev is not None),
        grid=(npair,),
        in_specs=cols((bs,), (r0,), COL_C) + cols((None, bs), (l, 0), 0) + cols((None, 1), (l, 0), 0) + [
            pairm, pl.BlockSpec((None, 128, W), lambda p: (l, 0, npair + p)), pairm,
            pairv, pairv, pairv, pairv, pairv, pairv, pairv, sblk] + extra_specs,
        out_specs=[pl.BlockSpec((bs, W), lambda p: (0, p)), sblk],
        out_shape=[jax.ShapeDtypeStruct((bs, G), BF16), jax.ShapeDtypeStruct(state.shape, F32)],
        scratch_shapes=[pltpu.VMEM((6, W, bs), F32), pltpu.VMEM((W, bs), F32)],
        input_output_aliases={23: 1} if prev is not None else {},
        compiler_params=_cparams("parallel"),
        name="rwkv_sample",
    )(proj, proj, proj, proj, prev_shift, prev_shift, prev_shift, prev_shift, mu, mu, mu, mu, w2a, w2a, g2,
      w0, a0, k_k, k_a, r_k, lnx_g, lnx_b, state, *extra)


def _assemble_body(a_ref, b_ref, c_ref, d_ref, buf_ref, o_ref):
    del buf_ref
    o_ref[...] = jnp.concatenate([a_ref[...], b_ref[...], c_ref[...], d_ref[...]], axis=1)


def _assemble_rows(parts, buf, row0):
    bs = parts[0].shape[0]
    part = pl.BlockSpec((bs, G_WIDTH), lambda i: (0, 0))
    return pl.pallas_call(
        _assemble_body,
        grid=(1,),
        in_specs=[part, part, part, part, ANY_SPEC],
        out_specs=pl.BlockSpec((bs, D_MODEL), lambda i: (row0 // bs, 0)),
        out_shape=jax.ShapeDtypeStruct(buf.shape, buf.dtype),
        input_output_aliases={4: 0},
        compiler_params=_cparams("arbitrary"),
        name="assemble_sample_rows",
    )(*parts, buf)


def kernel(x_prompt, x_sample, mem_prompt, state_conv_a, state_ret, state_shift, state_wkv, state_conv_d, cache_mem_k, cache_mem_v, g_mix, w_in, conv_a_w, mu_c, w0, w2, a0, a2, g2, k_k, k_a, r_k, lnx_g, lnx_b, conv_d_w, conv_d_b, ln_d_g, ln_d_b, w_out, g_xa, g_mem, wq_x, wk_x, wv_x, wo_x, g_mlp, w_up, w_down, g_final):
    bp, seq, d = x_prompt.shape
    bs = x_sample.shape[0]
    depth = w_in.shape[0]
    np_rows = bp * seq
    G = G_WIDTH
    x = jnp.concatenate([x_prompt.reshape(np_rows, d), x_sample.reshape(bs, d)], axis=0)
    mem = mem_prompt.reshape(bp * N_MEM, d)
    cos_p, sin_p = _rope_tables(jnp.arange(seq))
    cos_s, sin_s = _rope_tables(PAST_LEN + jnp.arange(1))

    w_in_b, w_out_b, wq_b, wk_b, wv_b, wo_b, w_up_b, w_down_b = (
        w.astype(BF16) for w in (w_in, w_out, wq_x, wk_x, wv_x, wo_x, w_up, w_down))
    z64 = jnp.zeros((depth, 64, G), F32)
    w2a = jnp.concatenate([jnp.concatenate([w2, z64], axis=2), jnp.concatenate([z64, a2], axis=2)], axis=1).astype(BF16)
    g2_b = g2.astype(BF16)
    wkv_flat = state_wkv.reshape(depth, bs, -1)
    vec3 = lambda a: a.reshape(depth, 1, -1)
    rw_arrs = (vec3(mu_c), w2a, g2_b, vec3(w0), vec3(a0), vec3(k_k), vec3(k_a), vec3(r_k), vec3(lnx_g), vec3(lnx_b))

    small = {k: [] for k in ("a_p", "ret_p", "sh_p", "sh_s", "wkv_p", "d_p", "mk", "mv")}
    na_s = nd_s = nret_s = nwkv_s = None
    for l in range(depth):
        h = _rmsnorm(x, _layer_vec(g_mix, l), BF16)
        proj = _matmul(h, w_in_b, l, tn=1280)
        ymix, na_p = _conv_a_prompt(proj, conv_a_w, l, bp, seq)
        ymix, nd_p = _conv_d_prompt(proj, conv_d_w, conv_d_b, ln_d_g, ln_d_b, ymix, l, bp, seq)
        ymix, nret_p = _retention_prompt(proj, cos_p, sin_p, ymix, bp, seq)
        rw = [(rw_arrs[0], pl.BlockSpec((None, 1, SHIFT_W), lambda *_: (l, 0, 0))),
              (w2a, _layer_mat(w2a, l)), (g2_b, _layer_mat(g2_b, l))] + [
              (a, pl.BlockSpec((None, 1, G), lambda *_: (l, 0, 0))) for a in rw_arrs[3:]]
        ymix, nsh_p, nwkv_p = _rwkv_prompt(proj, rw, ymix, l, bp, seq)
        ya_s, yd_s, na_s, nd_s = _sample_ad(proj, state_conv_a, state_conv_d, conv_a_w, conv_d_w, conv_d_b, ln_d_g, ln_d_b,
                                            l, np_rows, na_s, nd_s)
        yb_s, nret_s = _retention_sample(proj, cos_s, sin_s, state_ret, l, np_rows, nret_s)
        yc_s, nwkv_s = _rwkv_sample(proj, state_shift, rw_arrs, wkv_flat, l, np_rows, nwkv_s)
        ymix = _assemble_rows([ya_s, yb_s, yc_s, yd_s], ymix, np_rows)
        x = _matmul(ymix, w_out_b, l, residual=x)

        mn = _rmsnorm(mem, _layer_vec(g_mem, l), BF16)
        mk = _matmul(mn, wk_b, l)
        mv = _matmul(mn, wv_b, l)
        hq = _rmsnorm(x, _layer_vec(g_xa, l), BF16)
        q = _matmul(hq, wq_b, l)
        obuf = _xattn_prompt(q, mk, mv, bp, seq)
        obuf = _xattn_sample(q[np_rows:].reshape(bs, XA_HEADS, XA_HD), cache_mem_k, cache_mem_v, obuf, l, np_rows)
        x = _matmul(obuf, wo_b, l, residual=x)

        hm = _rmsnorm(x, _layer_vec(g_mlp, l), BF16)
        up = _matmul(hm, w_up_b, l, act="relu2", out_dtype=BF16)
        x = _matmul(up, w_down_b, l, residual=x)

        small["a_p"].append(na_p)
        small["ret_p"].append(nret_p)
        small["sh_p"].append(nsh_p.reshape(bp, SHIFT_W))
        small["sh_s"].append(proj[np_rows:, COL_C:COL_C + SHIFT_W])
        small["wkv_p"].append(nwkv_p)
        small["d_p"].append(nd_p)
        small["mk"].append(mk.reshape(bp, N_MEM, XA_HEADS, XA_HD))
        small["mv"].append(mv.reshape(bp, N_MEM, XA_HEADS, XA_HD))
    gf = (g_final.reshape(1, d), pl.BlockSpec((1, d), lambda *_: (0, 0)))
    y_p = _rmsnorm(x, gf, F32, 0, np_rows)
    y_s = _rmsnorm(x, gf, F32, np_rows, bs)
    st = lambda k: jnp.stack(small[k])
    return (y_p.reshape(bp, seq, d), y_s.reshape(bs, 1, d), st("a_p"), na_s, st("ret_p"), nret_s,
            st("sh_p"), st("sh_s"), st("wkv_p"), nwkv_s.reshape(depth, bs, RWKV_HEADS, RWKV_HEAD, RWKV_HEAD),
            st("d_p"), nd_s, st("mk"), st("mv"))
```

```python
import functools
import math

import jax
import jax.numpy as jnp
from jax import lax
from jax.experimental import pallas as pl
from jax.experimental.pallas import tpu as pltpu

F32 = jnp.float32
BF16 = jnp.bfloat16
HI = lax.Precision.HIGHEST

D_MODEL = 2048
G_WIDTH = 512
P_IN = 6400
RET_HEADS = 4
RET_D = 128
RET_CHUNK = 128
ROPE_BASE = 10000.0
RWKV_HEAD = 64
RWKV_HEADS = 8
WKV_CHUNK = 64
RWKV_GN_EPS = 64e-5
A_CONV = 3
D_CONV = 31
D_HIST = 32
N_MEM = 256
XA_HEADS = 4
XA_HD = 512
SHIFT_W = 1792
PAST_LEN = 16384
EPS = 1e-6
RET_LOG_DECAY = tuple(math.log1p(-(2.0 ** (-5.0 - h))) for h in range(RET_HEADS))
VMEM_LIMIT = 56 * 1024 * 1024
COL_A = 0
COL_B = 3 * G_WIDTH
COL_C = 7 * G_WIDTH
COL_D = 7 * G_WIDTH + SHIFT_W

ANY_SPEC = pl.BlockSpec(memory_space=pl.ANY)


def _cparams(*sem):
    return pltpu.CompilerParams(dimension_semantics=sem, vmem_limit_bytes=VMEM_LIMIT)


def _pick(n, cands):
    for c in cands:
        if n % c == 0:
            return c
    raise ValueError(f"no tile for {n}")


def _sigmoid(x):
    return 1.0 / (1.0 + jnp.exp(-x))


def _dot(a, b, precision=None):
    return jnp.dot(a, b, preferred_element_type=F32, precision=precision)


def _dot_nt(a, b, precision=None):
    return lax.dot_general(a, b, (((1,), (1,)), ((), ())), preferred_element_type=F32, precision=precision)


def _dot_tn(a, b, precision=None):
    return lax.dot_general(a, b, (((0,), (0,)), ((), ())), preferred_element_type=F32, precision=precision)


def _layernorm(x, eps):
    mu = jnp.mean(x, axis=-1, keepdims=True)
    xc = x - mu
    var = jnp.mean(xc * xc, axis=-1, keepdims=True)
    return xc * lax.rsqrt(var + eps)


def _layer_vec(a, l):
    n = a.shape[-1]
    return a.reshape(a.shape[0], 1, n), pl.BlockSpec((None, 1, n), lambda *_: (l, 0, 0))


def _layer_mat(a, l):
    return pl.BlockSpec((None,) + a.shape[1:], lambda *_: (l, 0, 0))


def _rmsnorm_body(x_ref, g_ref, o_ref):
    x = x_ref[...]
    ms = jnp.mean(x * x, axis=-1, keepdims=True)
    o_ref[...] = ((x * lax.rsqrt(ms + EPS)) * g_ref[...]).astype(o_ref.dtype)


def _rmsnorm(x, g_spec, out_dtype, row0=0, nrows=None):
    m, d = x.shape
    nrows = nrows or m
    tm = _pick(math.gcd(nrows, row0) if row0 else nrows, (1040, 1024, 640, 512, 256, 128))
    r0 = row0 // tm
    g, gspec = g_spec
    return pl.pallas_call(
        _rmsnorm_body,
        grid=(nrows // tm,),
        in_specs=[pl.BlockSpec((tm, d), lambda i: (r0 + i, 0)), gspec],
        out_specs=pl.BlockSpec((tm, d), lambda i: (i, 0)),
        out_shape=jax.ShapeDtypeStruct((nrows, d), out_dtype),
        compiler_params=_cparams("parallel"),
        name="rmsnorm",
    )(x, g)


def _mm_body(*refs, act, has_res, nk):
    a_ref, w_ref = refs[0], refs[1]
    res_ref = refs[2] if has_res else None
    o_ref = refs[3] if has_res else refs[2]

    def finish(r):
        if act == "relu2":
            r = jnp.square(jnp.maximum(r, 0.0))
        if has_res:
            r = res_ref[...] + r
        o_ref[...] = r.astype(o_ref.dtype)

    if nk == 1:
        finish(_dot(a_ref[...], w_ref[...]))
        return
    acc_ref = refs[-1]
    k = pl.program_id(2)

    @pl.when(k == 0)
    def _():
        acc_ref[...] = jnp.zeros_like(acc_ref)

    acc_ref[...] += _dot(a_ref[...], w_ref[...])

    @pl.when(k == nk - 1)
    def _():
        finish(acc_ref[...])


def _matmul(a, w, l, *, residual=None, act=None, out_dtype=F32, tn=None):
    m, kdim = a.shape
    n = w.shape[2]
    tm = _pick(m, (1040, 1024, 640, 512, 256, 128))
    tn = tn or _pick(n, (1024, 896, 768, 512, 256))
    tk = _pick(kdim, (2048, 1024, 512))
    nk = kdim // tk
    has_res = residual is not None
    in_specs = [pl.BlockSpec((tm, tk), lambda i, j, k: (i, k)), pl.BlockSpec((None, tk, tn), lambda i, j, k: (l, k, j))]
    args = [a, w]
    if has_res:
        in_specs.append(pl.BlockSpec((tm, tn), lambda i, j, k: (i, j)))
        args.append(residual)
    return pl.pallas_call(
        functools.partial(_mm_body, act=act, has_res=has_res, nk=nk),
        grid=(m // tm, n // tn, nk),
        in_specs=in_specs,
        out_specs=pl.BlockSpec((tm, tn), lambda i, j, k: (i, j)),
        out_shape=jax.ShapeDtypeStruct((m, n), out_dtype),
        scratch_shapes=[pltpu.VMEM((tm, tn), F32)] if nk > 1 else [],
        compiler_params=_cparams("parallel", "parallel", "arbitrary"),
        name="matmul",
    )(*args)


def _mm_ws_body(*refs, act, has_res):
    a_ref, w_ref = refs[0], refs[1]
    res_ref = refs[2] if has_res else None
    o_ref = refs[3] if has_res else refs[2]
    wb_ref = refs[-1]

    @pl.when(pl.program_id(1) == 0)
    def _():
        wb_ref[...] = w_ref[...].astype(BF16)

    r = _dot(a_ref[...], wb_ref[...])
    if act == "relu2":
        r = jnp.square(jnp.maximum(r, 0.0))
    if has_res:
        r = res_ref[...] + r
    o_ref[...] = r.astype(o_ref.dtype)


def _matmul_ws(a, w, l, *, residual=None, act=None, out_dtype=F32, tn=1024):
    m, kdim = a.shape
    n = w.shape[2]
    tm = _pick(m, (1040, 1024, 640, 512, 256, 128))
    has_res = residual is not None
    in_specs = [pl.BlockSpec((tm, kdim), lambda j, i: (i, 0)), pl.BlockSpec((None, kdim, tn), lambda j, i: (l, 0, j))]
    args = [a, w]
    if has_res:
        in_specs.append(pl.BlockSpec((tm, tn), lambda j, i: (i, j)))
        args.append(residual)
    return pl.pallas_call(
        functools.partial(_mm_ws_body, act=act, has_res=has_res),
        grid=(n // tn, m // tm),
        in_specs=in_specs,
        out_specs=pl.BlockSpec((tm, tn), lambda j, i: (i, j)),
        out_shape=jax.ShapeDtypeStruct((m, n), out_dtype),
        scratch_shapes=[pltpu.VMEM((kdim, tn), BF16)],
        compiler_params=_cparams("parallel", "arbitrary"),
        name="matmul_ws",
    )(*args)


def _conv_a_body(b_ref, c_ref, h_ref, w_ref, y_ref, st_ref, ext_ref, *, tl):
    t = pl.program_id(1)

    @pl.when(t == 0)
    def _():
        ext_ref[0:8, :] = jnp.zeros((8, G_WIDTH), F32)

    u = c_ref[...] * h_ref[...]
    ext_ref[8:, :] = u
    u1 = ext_ref[7:7 + tl, :]
    u2 = ext_ref[6:6 + tl, :]
    y = b_ref[...] * (w_ref[0:1, :] * u2 + w_ref[1:2, :] * u1 + w_ref[2:3, :] * u)
    y_ref[...] = y.astype(y_ref.dtype)
    st_ref[0] = ext_ref[tl + 6:tl + 8, :]
    ext_ref[0:8, :] = ext_ref[tl:tl + 8, :]


def _conv_a_prompt(proj, conv_w, l, bp, seq):
    tl = _pick(seq, (512, 256, 128))
    nt = seq // tl
    c0 = COL_A // G_WIDTH
    col = lambda j: pl.BlockSpec((tl, G_WIDTH), lambda b, t: (b * nt + t, c0 + j))
    return pl.pallas_call(
        functools.partial(_conv_a_body, tl=tl),
        grid=(bp, nt),
        in_specs=[col(0), col(1), col(2), _layer_mat(conv_w, l)],
        out_specs=[pl.BlockSpec((tl, G_WIDTH), lambda b, t: (b * nt + t, 0)),
                   pl.BlockSpec((1, A_CONV - 1, G_WIDTH), lambda b, t: (b, 0, 0))],
        out_shape=[jax.ShapeDtypeStruct((proj.shape[0], D_MODEL), BF16),
                   jax.ShapeDtypeStruct((bp, A_CONV - 1, G_WIDTH), F32)],
        scratch_shapes=[pltpu.VMEM((tl + 8, G_WIDTH), F32)],
        compiler_params=_cparams("parallel", "arbitrary"),
        name="conv_a_prompt",
    )(proj, proj, proj, conv_w)


CONV_D_ROWS = 32


def _conv_d_body(d1a_ref, d1b_ref, d2a_ref, d2b_ref, w_ref, bias_ref, g_ref, beta_ref, ymix_ref, y_ref, st_ref, ext_ref, *, tl):
    del ymix_ref
    t = pl.program_id(1)

    @pl.when(t == 0)
    def _():
        ext_ref[0:D_HIST, :] = jnp.zeros((D_HIST, G_WIDTH), F32)

    d1 = jnp.concatenate([d1a_ref[...], d1b_ref[...]], axis=1)
    d2 = jnp.concatenate([d2a_ref[...], d2b_ref[...]], axis=1)
    ext_ref[D_HIST:, :] = d1 * _sigmoid(d2)
    first = D_HIST - (D_CONV - 1)

    def blk(i, carry):
        base = pl.multiple_of(i * CONV_D_ROWS, CONV_D_ROWS)
        nwin = CONV_D_ROWS + D_HIST
        win = ext_ref[pl.ds(base, nwin), :]
        acc = jnp.broadcast_to(bias_ref[...], (CONV_D_ROWS, G_WIDTH))
        for s in range(8):
            ws = win if s == 0 else pltpu.roll(win, nwin - s, 0)
            for k in range(D_CONV):
                if (first + k) % 8 == s:
                    a0 = first + k - s
                    acc = acc + ws[a0:a0 + CONV_D_ROWS, :] * w_ref[k:k + 1, :]
        c = _layernorm(acc, EPS) * g_ref[...] + beta_ref[...]
        y_ref[pl.ds(base, CONV_D_ROWS), :] = (c * _sigmoid(c)).astype(y_ref.dtype)
        return carry

    lax.fori_loop(0, tl // CONV_D_ROWS, blk, 0)
    st_ref[0] = ext_ref[tl + first:tl + D_HIST, :]
    ext_ref[0:D_HIST, :] = ext_ref[tl:tl + D_HIST, :]


def _conv_d_prompt(proj, w, bias, g, beta, ymix, l, bp, seq):
    tl = _pick(seq, (512, 256, 128))
    nt = seq // tl
    half = G_WIDTH // 2
    c0 = COL_D // half
    col = lambda j: pl.BlockSpec((tl, half), lambda b, t: (b * nt + t, c0 + j))
    (bias, bias_s), (g, g_s), (beta, beta_s) = _layer_vec(bias, l), _layer_vec(g, l), _layer_vec(beta, l)
    return pl.pallas_call(
        functools.partial(_conv_d_body, tl=tl),
        grid=(bp, nt),
        in_specs=[col(0), col(1), col(2), col(3), _layer_mat(w, l), bias_s, g_s, beta_s, ANY_SPEC],
        out_specs=[pl.BlockSpec((tl, G_WIDTH), lambda b, t: (b * nt + t, 3)),
                   pl.BlockSpec((1, D_CONV - 1, G_WIDTH), lambda b, t: (b, 0, 0))],
        out_shape=[jax.ShapeDtypeStruct(ymix.shape, ymix.dtype),
                   jax.ShapeDtypeStruct((bp, D_CONV - 1, G_WIDTH), F32)],
        scratch_shapes=[pltpu.VMEM((tl + D_HIST, G_WIDTH), F32)],
        input_output_aliases={8: 0},
        compiler_params=_cparams("parallel", "arbitrary"),
        name="conv_d_prompt",
    )(proj, proj, proj, proj, w, bias, g, beta, ymix)


def _rope_tables(pos):
    inv = ROPE_BASE ** (-jnp.arange(0, RET_D, 2, dtype=F32) / RET_D)
    ang = pos.astype(F32)[:, None] * inv[None, :]
    cos = jnp.cos(ang)
    sin = jnp.sin(ang)
    return jnp.concatenate([cos, cos], axis=-1), jnp.concatenate([-sin, sin], axis=-1)


def _rope(x, cos2, sin2):
    return x * cos2 + pltpu.roll(x, RET_D // 2, 1) * sin2


def _ret_body(q_ref, k_ref, v_ref, g_ref, cos_ref, sin_ref, ymix_ref, y_ref, st_ref, s_ref, *, nchunks):
    del ymix_ref
    c = pl.program_id(1)
    C = RET_CHUNK

    @pl.when(c == 0)
    def _():
        s_ref[...] = jnp.zeros_like(s_ref)

    ii = lax.broadcasted_iota(jnp.int32, (C, C), 0).astype(F32)
    jj = lax.broadcasted_iota(jnp.int32, (C, C), 1).astype(F32)
    diff = ii - jj
    cos2 = cos_ref[...]
    sin2 = sin_ref[...]
    H = range(RET_HEADS)
    sl = [slice(h * RET_D, (h + 1) * RET_D) for h in H]
    lg = RET_LOG_DECAY
    q = [_rope(q_ref[:, sl[h]], cos2, sin2) for h in H]
    k = [_rope(k_ref[:, sl[h]], cos2, sin2) * (RET_D ** -0.5) for h in H]
    qb = [q[h].astype(BF16) for h in H]
    vb = [v_ref[:, sl[h]].astype(BF16) for h in H]
    s_old = [s_ref[h] for h in H]
    scores = [_dot_nt(qb[h], k[h].astype(BF16)) for h in H]
    cross = [_dot(qb[h], s_old[h].astype(BF16)) for h in H]
    kv = [_dot_tn((k[h] * jnp.exp(lg[h] * (C - 1.0 - ii))).astype(BF16), vb[h]) for h in H]
    dmat = [jnp.where(diff >= 0.0, jnp.exp(lg[h] * jnp.maximum(diff, 0.0)), 0.0) for h in H]
    inner = [_dot((scores[h] * dmat[h]).astype(BF16), vb[h]) for h in H]
    for h in H:
        s_ref[h] = math.exp(lg[h] * C) * s_old[h] + kv[h]
        o = _layernorm(inner[h] + cross[h] * jnp.exp(lg[h] * (ii + 1.0)), EPS)
        g = g_ref[:, sl[h]]
        y_ref[:, sl[h]] = (g * _sigmoid(g) * o).astype(y_ref.dtype)

    @pl.when(c == nchunks - 1)
    def _():
        st_ref[0] = s_ref[...]


def _retention_prompt(proj, cos2, sin2, ymix, bp, seq):
    C = RET_CHUNK
    nc = seq // C
    c0 = COL_B // G_WIDTH
    col = lambda j: pl.BlockSpec((C, G_WIDTH), lambda b, c: (b * nc + c, c0 + j))
    tab = pl.BlockSpec((C, RET_D), lambda b, c: (c, 0))
    return pl.pallas_call(
        functools.partial(_ret_body, nchunks=nc),
        grid=(bp, nc),
        in_specs=[col(0), col(1), col(2), col(3), tab, tab, ANY_SPEC],
        out_specs=[pl.BlockSpec((C, G_WIDTH), lambda b, c: (b * nc + c, 1)),
                   pl.BlockSpec((1, RET_HEADS, RET_D, RET_D), lambda b, c: (b, 0, 0, 0))],
        out_shape=[jax.ShapeDtypeStruct(ymix.shape, ymix.dtype),
                   jax.ShapeDtypeStruct((bp, RET_HEADS, RET_D, RET_D), F32)],
        scratch_shapes=[pltpu.VMEM((RET_HEADS, RET_D, RET_D), F32)],
        input_output_aliases={6: 0},
        compiler_params=_cparams("parallel", "arbitrary"),
        name="retention_prompt",
    )(proj, proj, proj, proj, cos2, sin2, ymix)


def _softplus(z):
    return jnp.maximum(z, 0.0) + jnp.log1p(jnp.exp(-jnp.abs(z)))


def _rwkv_pre(r, kc, vc, lora_x, w2_w, w2_a, g2, w0, a0, k_k, k_a):
    lin = lora_x[:, 0:128]
    lane = lax.broadcasted_iota(jnp.int32, lin.shape, 1)
    lin = jnp.where(lane < 64, jnp.tanh(lin), lin).astype(BF16)
    gate = _dot(_sigmoid(lora_x[:, 128:256]).astype(BF16), g2)
    w_pre = -_softplus(-(w0 + _dot(lin, w2_w))) - 0.5
    log_decay = -jnp.exp(w_pre)
    a = _sigmoid(a0 + _dot(lin, w2_a))
    kk = kc * k_k
    k4 = kc * (1.0 + (a - 1.0) * k_a)
    return r, log_decay, k4, vc, kk, a, gate


def _l2_normalize(kk):
    nrm = jnp.sqrt(jnp.sum(kk * kk, axis=-1, keepdims=True))
    return kk / jnp.maximum(nrm, 1e-12)


def _bdot(a, b):
    return _dot(a.astype(BF16), b.astype(BF16))


def _wkv_chunk(r, lw, cum, k, v, kk, a, z, mask2, eye):
    T = WKV_CHUNK
    H = range(len(r))
    g_inc = [jnp.exp(cum[h]) for h in H]
    g_tot = [g_inc[h][T - 1:T, :] for h in H]
    ah = [-(kk[h] * jnp.exp(cum[h] - lw[h])) for h in H]
    rg = [r[h] * g_inc[h] for h in H]
    g_inv = [jnp.exp(-cum[h]) for h in H]
    bd = [kk[h] * a[h] * g_inv[h] for h in H]
    kd = [k[h] * g_inv[h] for h in H]
    ar = [jnp.concatenate([ah[h], rg[h]], axis=0).astype(BF16) for h in H]
    xb = [jnp.where(mask2, _dot_nt(ar[h], bd[h].astype(BF16)), 0.0) for h in H]
    xk = [jnp.where(mask2, _dot_nt(ar[h], kd[h].astype(BF16)), 0.0).astype(BF16) for h in H]
    vb = [v[h].astype(BF16) for h in H]
    xkv = [_dot(xk[h], vb[h]) for h in H]
    acur = [xb[h][0:T] for h in H]
    p = [eye for _ in H]
    for it in range(5):
        x = [_bdot(jnp.concatenate([acur[h], p[h]], axis=0), acur[h]) for h in H]
        acur = [x[h][0:T] for h in H]
        p = [p[h] + x[h][T:2 * T] for h in H]
    x = [_bdot(p[h], acur[h]) for h in H]
    p = [p[h] + x[h] for h in H]
    wub = [_bdot(p[h], jnp.concatenate([ah[h], xkv[h][0:T]], axis=1)).astype(BF16) for h in H]
    qo = [_dot(xb[h][T:2 * T].astype(BF16), wub[h]) for h in H]
    lhs_t = [jnp.concatenate([bd[h] * g_tot[h], kd[h] * g_tot[h]], axis=0).astype(BF16) for h in H]
    zero = jnp.zeros((T, 64), BF16)
    mn = [_dot_tn(lhs_t[h], jnp.concatenate([wub[h], jnp.concatenate([zero, vb[h]], axis=1)], axis=0)) for h in H]
    qm = [jnp.concatenate([rg[h] + qo[h][:, 0:64], mn[h][:, 0:64] + jnp.where(eye > 0.0, g_tot[h], 0.0)], axis=0) for h in H]
    oz = [_bdot(qm[h], z[h]) for h in H]
    o = [oz[h][0:T] + qo[h][:, 64:128] + xkv[h][T:2 * T] for h in H]
    z_new = [oz[h][T:2 * T] + mn[h][:, 64:128] for h in H]
    return o, z_new


def _rwkv_post(o, r, k4, v, gate, rk, lnx_g, lnx_b):
    gn = _layernorm(o, RWKV_GN_EPS) * lnx_g + lnx_b
    bonus = jnp.sum(r * k4 * rk, axis=-1, keepdims=True) * v
    return (gn + bonus) * gate


def _rwkv_prompt_body(pc_ref, mu_ref, w2a_ref, g2_ref, w0_ref, a0_ref, kk_ref, ka_ref, rk_ref, lg_ref, lb_ref, ymix_ref,
                      y_ref, shift_ref, st_ref, carry_ref, z_ref, *, nchunks):
    del ymix_ref
    c = pl.program_id(1)
    T = WKV_CHUNK

    @pl.when(c == 0)
    def _():
        carry_ref[...] = jnp.zeros_like(carry_ref)
        z_ref[...] = jnp.zeros_like(z_ref)

    pc = pc_ref[...]
    row = lax.broadcasted_iota(jnp.int32, pc.shape, 0)
    prev = jnp.where(row == 0, carry_ref[0:1, :], pltpu.roll(pc, 1, 0))
    carry_ref[0:1, :] = pc[T - 1:T, :]
    shift_ref[0] = pc[T - 1:T, :]
    xs = pc + (prev - pc) * mu_ref[...]
    G = G_WIDTH
    r, lw, k4, v, kk, a, gate = _rwkv_pre(xs[:, 0:G], xs[:, G:2 * G], xs[:, 2 * G:3 * G], xs[:, 3 * G:3 * G + 256],
                                          w2a_ref[:, 0:G], w2a_ref[:, G:2 * G], g2_ref[...], w0_ref[...], a0_ref[...],
                                          kk_ref[...], ka_ref[...])
    ti = lax.broadcasted_iota(jnp.int32, (T, T), 0)
    si = lax.broadcasted_iota(jnp.int32, (T, T), 1)
    tril = (ti >= si).astype(F32)
    eye = (ti == si).astype(F32)
    cum = _dot(tril, lw, HI)
    t2 = lax.broadcasted_iota(jnp.int32, (2 * T, T), 0)
    s2 = lax.broadcasted_iota(jnp.int32, (2 * T, T), 1)
    mask2 = jnp.where(t2 < T, t2 - 1, t2 - T) >= s2
    heads = lambda x: [x[:, h * RWKV_HEAD:(h + 1) * RWKV_HEAD] for h in range(RWKV_HEADS)]
    r_h, k_h, v_h = heads(r), heads(k4), heads(v)
    o, z_new = _wkv_chunk(r_h, heads(lw), heads(cum), k_h, v_h, [_l2_normalize(x) for x in heads(kk)], heads(a),
                          [z_ref[h] for h in range(RWKV_HEADS)], mask2, eye)
    for h in range(RWKV_HEADS):
        z_ref[h] = z_new[h]
    gate_h, rk_h, lg_h, lb_h = heads(gate), heads(rk_ref[...]), heads(lg_ref[...]), heads(lb_ref[...])
    ys = [_rwkv_post(o[h], r_h[h], k_h[h], v_h[h], gate_h[h], rk_h[h], lg_h[h], lb_h[h]) for h in range(RWKV_HEADS)]
    y_ref[...] = jnp.concatenate(ys, axis=1).astype(y_ref.dtype)

    @pl.when(c == nchunks - 1)
    def _():
        for h in range(RWKV_HEADS):
            st_ref[0, h] = z_ref[h].T


def _rwkv_prompt(proj, rw, ymix, l, bp, seq):
    T = WKV_CHUNK
    nc = seq // T
    G = G_WIDTH
    arrs, specs = zip(*rw)
    return pl.pallas_call(
        functools.partial(_rwkv_prompt_body, nchunks=nc),
        grid=(bp, nc),
        in_specs=[pl.BlockSpec((T, SHIFT_W), lambda b, c: (b * nc + c, COL_C // SHIFT_W))] + list(specs) + [ANY_SPEC],
        out_specs=[pl.BlockSpec((T, G), lambda b, c: (b * nc + c, 2)),
                   pl.BlockSpec((1, 1, SHIFT_W), lambda b, c: (b, 0, 0)),
                   pl.BlockSpec((1, RWKV_HEADS, RWKV_HEAD, RWKV_HEAD), lambda b, c: (b, 0, 0, 0))],
        out_shape=[jax.ShapeDtypeStruct(ymix.shape, ymix.dtype),
                   jax.ShapeDtypeStruct((bp, 1, SHIFT_W), F32),
                   jax.ShapeDtypeStruct((bp, RWKV_HEADS, RWKV_HEAD, RWKV_HEAD), F32)],
        scratch_shapes=[pltpu.VMEM((8, SHIFT_W), F32), pltpu.VMEM((RWKV_HEADS, RWKV_HEAD, RWKV_HEAD), F32)],
        input_output_aliases={1 + len(arrs): 0},
        compiler_params=_cparams("parallel", "arbitrary"),
        name="rwkv_prompt",
    )(proj, *arrs, ymix)


def _xattn_prompt_body(q_ref, k_ref, v_ref, o_ref, kb_ref, vb_ref):
    @pl.when(pl.program_id(1) == 0)
    def _():
        kb_ref[...] = k_ref[...].astype(BF16)
        vb_ref[...] = v_ref[...].astype(BF16)

    for h in range(XA_HEADS):
        sl = slice(h * XA_HD, (h + 1) * XA_HD)
        s = _dot_nt(q_ref[:, sl].astype(BF16), kb_ref[:, sl]) * (XA_HD ** -0.5)
        e = jnp.exp(s - jnp.max(s, axis=-1, keepdims=True))
        p = e / jnp.sum(e, axis=-1, keepdims=True)
        o_ref[:, sl] = _dot(p.astype(BF16), vb_ref[:, sl]).astype(o_ref.dtype)


def _xattn_prompt(q, mk, mv, bp, seq):
    tq = _pick(seq, (512, 256, 128))
    nt = seq // tq
    kv = pl.BlockSpec((N_MEM, D_MODEL), lambda b, t: (b, 0))
    return pl.pallas_call(
        _xattn_prompt_body,
        grid=(bp, nt),
        in_specs=[pl.BlockSpec((tq, D_MODEL), lambda b, t: (b * nt + t, 0)), kv, kv],
        out_specs=pl.BlockSpec((tq, D_MODEL), lambda b, t: (b * nt + t, 0)),
        out_shape=jax.ShapeDtypeStruct((q.shape[0], D_MODEL), BF16),
        scratch_shapes=[pltpu.VMEM((N_MEM, D_MODEL), BF16), pltpu.VMEM((N_MEM, D_MODEL), BF16)],
        compiler_params=_cparams("parallel", "arbitrary"),
        name="xattn_prompt",
    )(q, mk, mv)


XA_SAMPLES_PER_STEP = 2


def _xattn_sample_body(q_ref, k_ref, v_ref, obuf_ref, o_ref, acc_ref, *, nsteps):
    del obuf_ref
    i = pl.program_id(0)
    for j in range(XA_SAMPLES_PER_STEP):
        b = i * XA_SAMPLES_PER_STEP + j
        for h in range(XA_HEADS):
            q = q_ref[h, pl.ds(b, 1), :]
            s = jnp.sum(k_ref[j, :, h, :] * q, axis=-1, keepdims=True) * (XA_HD ** -0.5)
            e = jnp.exp(s - jnp.max(s, axis=0, keepdims=True))
            p = e / jnp.sum(e, axis=0, keepdims=True)
            acc_ref[h, pl.ds(b, 1), :] = jnp.sum(p * v_ref[j, :, h, :], axis=0, keepdims=True)

    @pl.when(i == nsteps - 1)
    def _():
        o_ref[...] = jnp.concatenate([acc_ref[h] for h in range(XA_HEADS)], axis=1).astype(o_ref.dtype)


def _xattn_sample(q, ck, cv, obuf, l, row0):
    bs = q.shape[1]
    nb = XA_SAMPLES_PER_STEP
    kv = pl.BlockSpec((None, nb, N_MEM, XA_HEADS, XA_HD), lambda i: (l, i, 0, 0, 0))
    return pl.pallas_call(
        functools.partial(_xattn_sample_body, nsteps=bs // nb),
        grid=(bs // nb,),
        in_specs=[pl.BlockSpec((XA_HEADS, bs, XA_HD), lambda i: (0, 0, 0)), kv, kv, ANY_SPEC],
        out_specs=pl.BlockSpec((bs, D_MODEL), lambda i: (row0 // bs, 0)),
        out_shape=jax.ShapeDtypeStruct(obuf.shape, obuf.dtype),
        scratch_shapes=[pltpu.VMEM((XA_HEADS, bs, XA_HD), F32)],
        input_output_aliases={3: 0},
        compiler_params=_cparams("arbitrary"),
        name="xattn_sample",
    )(q, ck, cv, obuf)


def _alias_args(prev):
    return ([], []) if prev is None else ([prev], [ANY_SPEC])


def _sample_ad_body(*refs, has_prev):
    (ab_ref, ac_ref, ah_ref, sta_ref, wa_ref, d1a_ref, d1b_ref, d2a_ref, d2b_ref, std_ref, wd_ref, bias_ref, g_ref,
     beta_ref) = refs[:14]
    ya_ref, yd_ref, na_ref, nd_ref = refs[14 + 2 * has_prev:]
    u = ac_ref[...] * ah_ref[...]
    s0 = sta_ref[:, 0, :]
    s1 = sta_ref[:, 1, :]
    ya = ab_ref[...] * (wa_ref[0:1, :] * s0 + wa_ref[1:2, :] * s1 + wa_ref[2:3, :] * u)
    ya_ref[...] = ya.astype(ya_ref.dtype)
    na_ref[:, 0, :] = s1
    na_ref[:, 1, :] = u
    d1 = jnp.concatenate([d1a_ref[...], d1b_ref[...]], axis=1)
    d2 = jnp.concatenate([d2a_ref[...], d2b_ref[...]], axis=1)
    ud = d1 * _sigmoid(d2)
    acc = jnp.broadcast_to(bias_ref[...], ud.shape)
    nh = D_CONV - 1
    for k in range(nh):
        s_k = std_ref[:, k, :]
        acc = acc + s_k * wd_ref[k:k + 1, :]
        if k > 0:
            nd_ref[:, k - 1, :] = s_k
    acc = acc + ud * wd_ref[nh:nh + 1, :]
    nd_ref[:, nh - 1, :] = ud
    c = _layernorm(acc, EPS) * g_ref[...] + beta_ref[...]
    yd_ref[...] = (c * _sigmoid(c)).astype(yd_ref.dtype)


def _sample_ad(proj, st_a, st_d, wa, wd, bias, g, beta, l, row0, prev_a, prev_d):
    bs = st_a.shape[1]
    G = G_WIDTH
    tb = 32
    r0 = row0 // tb
    half = G // 2
    cola = lambda j: pl.BlockSpec((tb, G), lambda i: (r0 + i, COL_A // G + j))
    cold = lambda j: pl.BlockSpec((tb, half), lambda i: (r0 + i, COL_D // half + j))
    st = lambda n: pl.BlockSpec((None, tb, n, G), lambda i: (l, i, 0, 0))
    yrow = pl.BlockSpec((tb, G), lambda i: (i, 0))
    (bias, bias_s), (g, g_s), (beta, beta_s) = _layer_vec(bias, l), _layer_vec(g, l), _layer_vec(beta, l)
    nh = D_CONV - 1
    has_prev = prev_a is not None
    extra = [prev_a, prev_d] if has_prev else []
    return pl.pallas_call(
        functools.partial(_sample_ad_body, has_prev=has_prev),
        grid=(bs // tb,),
        in_specs=[cola(0), cola(1), cola(2), st(A_CONV - 1), _layer_mat(wa, l), cold(0), cold(1), cold(2), cold(3), st(nh),
                  _layer_mat(wd, l), bias_s, g_s, beta_s] + [ANY_SPEC] * len(extra),
        out_specs=[yrow, yrow, st(A_CONV - 1), st(nh)],
        out_shape=[jax.ShapeDtypeStruct((bs, G), BF16), jax.ShapeDtypeStruct((bs, G), BF16),
                   jax.ShapeDtypeStruct(st_a.shape, F32), jax.ShapeDtypeStruct(st_d.shape, F32)],
        input_output_aliases={14: 2, 15: 3} if has_prev else {},
        compiler_params=_cparams("parallel"),
        name="sample_conv_ad",
    )(proj, proj, proj, st_a, wa, proj, proj, proj, proj, st_d, wd, bias, g, beta, *extra)


RET_D_BLOCK = 16


def _ret_sample_body(*refs, has_prev):
    q_ref, k_ref, v_ref, g_ref, cos_ref, sin_ref, s_ref = refs[:7]
    y_ref, ns_ref, qt_ref, kt_ref, acc_ref = refs[7 + has_prev:]
    h = pl.program_id(0)
    j = pl.program_id(1)
    nd = RET_D // RET_D_BLOCK
    gammas = [math.exp(lg) for lg in RET_LOG_DECAY]
    gamma = jnp.where(h == 0, gammas[0], jnp.where(h == 1, gammas[1], jnp.where(h == 2, gammas[2], gammas[3])))
    gamma = gamma.astype(F32)

    @pl.when(j == 0)
    def _():
        q = _rope(q_ref[...], cos_ref[...], sin_ref[...])
        k = _rope(k_ref[...], cos_ref[...], sin_ref[...]) * (RET_D ** -0.5)
        qt_ref[...] = q.T
        kt_ref[...] = k.T
        acc_ref[...] = jnp.zeros_like(acc_ref)

    vt = v_ref[...].T
    acc = acc_ref[...]
    for d in range(RET_D_BLOCK):
        s_d = s_ref[:, d, :].T
        row = j * RET_D_BLOCK + d
        q_d = qt_ref[pl.ds(row, 1), :]
        k_d = kt_ref[pl.ds(row, 1), :]
        acc = acc + q_d * s_d
        ns_ref[:, d, :] = (gamma * s_d + k_d * vt).T
    acc_ref[...] = acc

    @pl.when(j == nd - 1)
    def _():
        qk = jnp.sum(qt_ref[...] * kt_ref[...], axis=0, keepdims=True)
        o = qk * vt + acc * gamma
        mu = jnp.mean(o, axis=0, keepdims=True)
        oc = o - mu
        var = jnp.mean(oc * oc, axis=0, keepdims=True)
        ln = (oc * lax.rsqrt(var + EPS)).T
        g = g_ref[...]
        y_ref[...] = (g * _sigmoid(g) * ln).astype(y_ref.dtype)


def _retention_sample(proj, cos2, sin2, state, l, row0, prev):
    bs = state.shape[1]
    r0 = row0 // bs
    nd = RET_D // RET_D_BLOCK
    col = lambda c: pl.BlockSpec((bs, RET_D), lambda h, j: (r0, COL_B // RET_D + c * RET_HEADS + h))
    tab = pl.BlockSpec((1, RET_D), lambda h, j: (0, 0))
    sblk = pl.BlockSpec((None, bs, None, RET_D_BLOCK, RET_D), lambda h, j: (l, 0, h, j, 0))
    extra, extra_specs = _alias_args(prev)
    return pl.pallas_call(
        functools.partial(_ret_sample_body, has_prev=prev is not None),
        grid=(RET_HEADS, nd),
        in_specs=[col(0), col(1), col(2), col(3), tab, tab, sblk] + extra_specs,
        out_specs=[pl.BlockSpec((bs, RET_D), lambda h, j: (0, h)), sblk],
        out_shape=[jax.ShapeDtypeStruct((bs, G_WIDTH), BF16), jax.ShapeDtypeStruct(state.shape, F32)],
        scratch_shapes=[pltpu.VMEM((RET_D, bs), F32), pltpu.VMEM((RET_D, bs), F32), pltpu.VMEM((RET_D, bs), F32)],
        input_output_aliases={7: 1} if prev is not None else {},
        compiler_params=_cparams("parallel", "arbitrary"),
        name="retention_sample",
    )(proj, proj, proj, proj, cos2, sin2, state, *extra)


def _rwkv_sample_body(*refs, has_prev):
    (xr_ref, xk_ref, xv_ref, xl_ref, pr_ref, pk_ref, pv_ref, pl_ref, mr_ref, mk_ref, mv_ref, ml_ref,
     w2w_ref, w2a_ref, g2_ref, w0_ref, a0_ref, kk_ref, ka_ref, rk_ref, lg_ref, lb_ref, s_ref) = refs[:23]
    y_ref, ns_ref, vec_ref, ot_ref = refs[23 + has_prev:]
    N = RWKV_HEAD

    def shifted(x_ref, prev_ref, mu_ref):
        x = x_ref[...]
        return x + (prev_ref[...] - x) * mu_ref[...]

    r, lw, k4, v, kk, a, gate = _rwkv_pre(shifted(xr_ref, pr_ref, mr_ref), shifted(xk_ref, pk_ref, mk_ref),
                                          shifted(xv_ref, pv_ref, mv_ref), shifted(xl_ref, pl_ref, ml_ref),
                                          w2w_ref[...], w2a_ref[...], g2_ref[...], w0_ref[...], a0_ref[...],
                                          kk_ref[...], ka_ref[...])
    kkn = jnp.concatenate([_l2_normalize(kk[:, 0:N]), _l2_normalize(kk[:, N:2 * N])], axis=1)
    w = jnp.exp(lw)
    vec_ref[0] = kkn.T
    vec_ref[1] = w.T
    vec_ref[2] = (kkn * a).T
    vec_ref[3] = k4.T
    vec_ref[4] = r.T
    vec_ref[5] = v.T
    st = s_ref[...].T
    news = []
    for hh in range(2):
        ch = slice(hh * N, (hh + 1) * N)
        kk_t = vec_ref[0, ch, :]
        w_t = vec_ref[1, ch, :]
        kka_t = vec_ref[2, ch, :]
        k_t = vec_ref[3, ch, :]
        r_t = vec_ref[4, ch, :]
        for i in range(N):
            base = (hh * N + i) * N
            s_i = st[base:base + N, :]
            sa = -jnp.sum(s_i * kk_t, axis=0, keepdims=True)
            v_i = vec_ref[5, hh * N + i:hh * N + i + 1, :]
            s_new = s_i * w_t + sa * kka_t + v_i * k_t
            news.append(s_new)
            ot_ref[hh * N + i:hh * N + i + 1, :] = jnp.sum(s_new * r_t, axis=0, keepdims=True)
    ns_ref[...] = jnp.concatenate(news, axis=0).T
    o = ot_ref[...].T
    rk = rk_ref[...]
    lg = lg_ref[...]
    lb = lb_ref[...]
    ys = []
    for hh in range(2):
        ch = slice(hh * N, (hh + 1) * N)
        ys.append(_rwkv_post(o[:, ch], r[:, ch], k4[:, ch], v[:, ch], gate[:, ch], rk[:, ch], lg[:, ch], lb[:, ch]))
    y_ref[...] = jnp.concatenate(ys, axis=1).astype(y_ref.dtype)


def _rwkv_sample(proj, prev_shift, rw_arrs, state, l, row0, prev):
    mu, w2a, g2, w0, a0, k_k, k_a, r_k, lnx_g, lnx_b = rw_arrs
    bs = state.shape[1]
    r0 = row0 // bs
    G = G_WIDTH
    N = RWKV_HEAD
    W = 2 * N
    npair = G // W
    def cols(shape_lead, lead_idx, base):
        mk = lambda width, cidx: pl.BlockSpec(shape_lead + (width,), lambda p: lead_idx + (cidx(p),))
        return [mk(W, lambda p, c=c: base // W + c * npair + p) for c in range(3)] + [
            mk(2 * W, lambda p: (base + 3 * G) // (2 * W))]
    pairv = pl.BlockSpec((None, 1, W), lambda p: (l, 0, p))
    pairm = pl.BlockSpec((None, 128, W), lambda p: (l, 0, p))
    sblk = pl.BlockSpec((None, bs, W * N), lambda p: (l, 0, p))
    extra, extra_specs = _alias_args(prev)
    return pl.pallas_call(
        functools.partial(_rwkv_sample_body, has_prev=prev is not None),
        grid=(npair,),
        in_specs=cols((bs,), (r0,), COL_C) + cols((None, bs), (l, 0), 0) + cols((None, 1), (l, 0), 0) + [
            pairm, pl.BlockSpec((None, 128, W), lambda p: (l, 0, npair + p)), pairm,
            pairv, pairv, pairv, pairv, pairv, pairv, pairv, sblk] + extra_specs,
        out_specs=[pl.BlockSpec((bs, W), lambda p: (0, p)), sblk],
        out_shape=[jax.ShapeDtypeStruct((bs, G), BF16), jax.ShapeDtypeStruct(state.shape, F32)],
        scratch_shapes=[pltpu.VMEM((6, W, bs), F32), pltpu.VMEM((W, bs), F32)],
        input_output_aliases={23: 1} if prev is not None else {},
        compiler_params=_cparams("parallel"),
        name="rwkv_sample",
    )(proj, proj, proj, proj, prev_shift, prev_shift, prev_shift, prev_shift, mu, mu, mu, mu, w2a, w2a, g2,
      w0, a0, k_k, k_a, r_k, lnx_g, lnx_b, state, *extra)


def _assemble_body(a_ref, b_ref, c_ref, d_ref, buf_ref, o_ref):
    del buf_ref
    o_ref[...] = jnp.concatenate([a_ref[...], b_ref[...], c_ref[...], d_ref[...]], axis=1)


def _assemble_rows(parts, buf, row0):
    bs = parts[0].shape[0]
    part = pl.BlockSpec((bs, G_WIDTH), lambda i: (0, 0))
    return pl.pallas_call(
        _assemble_body,
        grid=(1,),
        in_specs=[part, part, part, part, ANY_SPEC],
        out_specs=pl.BlockSpec((bs, D_MODEL), lambda i: (row0 // bs, 0)),
        out_shape=jax.ShapeDtypeStruct(buf.shape, buf.dtype),
        input_output_aliases={4: 0},
        compiler_params=_cparams("arbitrary"),
        name="assemble_sample_rows",
    )(*parts, buf)


def kernel(x_prompt, x_sample, mem_prompt, state_conv_a, state_ret, state_shift, state_wkv, state_conv_d, cache_mem_k, cache_mem_v, g_mix, w_in, conv_a_w, mu_c, w0, w2, a0, a2, g2, k_k, k_a, r_k, lnx_g, lnx_b, conv_d_w, conv_d_b, ln_d_g, ln_d_b, w_out, g_xa, g_mem, wq_x, wk_x, wv_x, wo_x, g_mlp, w_up, w_down, g_final):
    bp, seq, d = x_prompt.shape
    bs = x_sample.shape[0]
    depth = w_in.shape[0]
    np_rows = bp * seq
    G = G_WIDTH
    x = jnp.concatenate([x_prompt.reshape(np_rows, d), x_sample.reshape(bs, d)], axis=0)
    mem = mem_prompt.reshape(bp * N_MEM, d)
    cos_p, sin_p = _rope_tables(jnp.arange(seq))
    cos_s, sin_s = _rope_tables(PAST_LEN + jnp.arange(1))

    w_down_b = w_down.astype(BF16)
    z64 = jnp.zeros((depth, 64, G), F32)
    w2a = jnp.concatenate([jnp.concatenate([w2, z64], axis=2), jnp.concatenate([z64, a2], axis=2)], axis=1).astype(BF16)
    g2_b = g2.astype(BF16)
    wkv_flat = state_wkv.reshape(depth, bs, -1)
    vec3 = lambda a: a.reshape(depth, 1, -1)
    rw_arrs = (vec3(mu_c), w2a, g2_b, vec3(w0), vec3(a0), vec3(k_k), vec3(k_a), vec3(r_k), vec3(lnx_g), vec3(lnx_b))

    small = {k: [] for k in ("a_p", "ret_p", "sh_p", "sh_s", "wkv_p", "d_p", "mk", "mv")}
    na_s = nd_s = nret_s = nwkv_s = None
    for l in range(depth):
        h = _rmsnorm(x, _layer_vec(g_mix, l), BF16)
        proj = _matmul_ws(h, w_in, l, tn=1280)
        ymix, na_p = _conv_a_prompt(proj, conv_a_w, l, bp, seq)
        ymix, nd_p = _conv_d_prompt(proj, conv_d_w, conv_d_b, ln_d_g, ln_d_b, ymix, l, bp, seq)
        ymix, nret_p = _retention_prompt(proj, cos_p, sin_p, ymix, bp, seq)
        rw = [(rw_arrs[0], pl.BlockSpec((None, 1, SHIFT_W), lambda *_: (l, 0, 0))),
              (w2a, _layer_mat(w2a, l)), (g2_b, _layer_mat(g2_b, l))] + [
              (a, pl.BlockSpec((None, 1, G), lambda *_: (l, 0, 0))) for a in rw_arrs[3:]]
        ymix, nsh_p, nwkv_p = _rwkv_prompt(proj, rw, ymix, l, bp, seq)
        ya_s, yd_s, na_s, nd_s = _sample_ad(proj, state_conv_a, state_conv_d, conv_a_w, conv_d_w, conv_d_b, ln_d_g, ln_d_b,
                                            l, np_rows, na_s, nd_s)
        yb_s, nret_s = _retention_sample(proj, cos_s, sin_s, state_ret, l, np_rows, nret_s)
        yc_s, nwkv_s = _rwkv_sample(proj, state_shift, rw_arrs, wkv_flat, l, np_rows, nwkv_s)
        ymix = _assemble_rows([ya_s, yb_s, yc_s, yd_s], ymix, np_rows)
        x = _matmul_ws(ymix, w_out, l, residual=x)

        mn = _rmsnorm(mem, _layer_vec(g_mem, l), BF16)
        mk = _matmul_ws(mn, wk_x, l)
        mv = _matmul_ws(mn, wv_x, l)
        hq = _rmsnorm(x, _layer_vec(g_xa, l), BF16)
        q = _matmul_ws(hq, wq_x, l)
        obuf = _xattn_prompt(q, mk, mv, bp, seq)
        q_s = q[np_rows:].reshape(bs, XA_HEADS, XA_HD).transpose(1, 0, 2)
        obuf = _xattn_sample(q_s, cache_mem_k, cache_mem_v, obuf, l, np_rows)
        x = _matmul_ws(obuf, wo_x, l, residual=x)

        hm = _rmsnorm(x, _layer_vec(g_mlp, l), BF16)
        up = _matmul_ws(hm, w_up, l, act="relu2", out_dtype=BF16)
        x = _matmul(up, w_down_b, l, residual=x)

        small["a_p"].append(na_p)
        small["ret_p"].append(nret_p)
        small["sh_p"].append(nsh_p.reshape(bp, SHIFT_W))
        small["sh_s"].append(proj[np_rows:, COL_C:COL_C + SHIFT_W])
        small["wkv_p"].append(nwkv_p)
        small["d_p"].append(nd_p)
        small["mk"].append(mk.reshape(bp, N_MEM, XA_HEADS, XA_HD))
        small["mv"].append(mv.reshape(bp, N_MEM, XA_HEADS, XA_HD))
    gf = (g_final.reshape(1, d), pl.BlockSpec((1, d), lambda *_: (0, 0)))
    y_p = _rmsnorm(x, gf, F32, 0, np_rows)
    y_s = _rmsnorm(x, gf, F32, np_rows, bs)
    st = lambda k: jnp.stack(small[k])
    return (y_p.reshape(bp, seq, d), y_s.reshape(bs, 1, d), st("a_p"), na_s, st("ret_p"), nret_s,
            st("sh_p"), st("sh_s"), st("wkv_p"), nwkv_s.reshape(depth, bs, RWKV_HEADS, RWKV_HEAD, RWKV_HEAD),
            st("d_p"), nd_s, st("mk"), st("mv"))
```

```python
import functools
import math

import jax
import jax.numpy as jnp
from jax import lax
from jax.experimental import pallas as pl
from jax.experimental.pallas import tpu as pltpu

F32 = jnp.float32
BF16 = jnp.bfloat16
HI = lax.Precision.HIGHEST

D_MODEL = 2048
G_WIDTH = 512
P_IN = 6400
RET_HEADS = 4
RET_D = 128
RET_CHUNK = 128
ROPE_BASE = 10000.0
RWKV_HEAD = 64
RWKV_HEADS = 8
WKV_CHUNK = 64
RWKV_GN_EPS = 64e-5
A_CONV = 3
D_CONV = 31
D_HIST = 32
N_MEM = 256
XA_HEADS = 4
XA_HD = 512
SHIFT_W = 1792
PAST_LEN = 16384
EPS = 1e-6
RET_LOG_DECAY = tuple(math.log1p(-(2.0 ** (-5.0 - h))) for h in range(RET_HEADS))
VMEM_LIMIT = 56 * 1024 * 1024
COL_A = 0
COL_B = 3 * G_WIDTH
COL_C = 7 * G_WIDTH
COL_D = 7 * G_WIDTH + SHIFT_W

ANY_SPEC = pl.BlockSpec(memory_space=pl.ANY)


def _cparams(*sem):
    return pltpu.CompilerParams(dimension_semantics=sem, vmem_limit_bytes=VMEM_LIMIT)


def _pick(n, cands):
    for c in cands:
        if n % c == 0:
            return c
    raise ValueError(f"no tile for {n}")


def _sigmoid(x):
    return 1.0 / (1.0 + jnp.exp(-x))


def _dot(a, b, precision=None):
    return jnp.dot(a, b, preferred_element_type=F32, precision=precision)


def _dot_nt(a, b, precision=None):
    return lax.dot_general(a, b, (((1,), (1,)), ((), ())), preferred_element_type=F32, precision=precision)


def _dot_tn(a, b, precision=None):
    return lax.dot_general(a, b, (((0,), (0,)), ((), ())), preferred_element_type=F32, precision=precision)


def _layernorm(x, eps):
    mu = jnp.mean(x, axis=-1, keepdims=True)
    xc = x - mu
    var = jnp.mean(xc * xc, axis=-1, keepdims=True)
    return xc * lax.rsqrt(var + eps)


def _layer_vec(a, l):
    n = a.shape[-1]
    return a.reshape(a.shape[0], 1, n), pl.BlockSpec((None, 1, n), lambda *_: (l, 0, 0))


def _layer_mat(a, l):
    return pl.BlockSpec((None,) + a.shape[1:], lambda *_: (l, 0, 0))


def _rmsnorm_body(x_ref, g_ref, o_ref):
    x = x_ref[...]
    ms = jnp.mean(x * x, axis=-1, keepdims=True)
    o_ref[...] = ((x * lax.rsqrt(ms + EPS)) * g_ref[...]).astype(o_ref.dtype)


def _rmsnorm(x, g_spec, out_dtype, row0=0, nrows=None):
    m, d = x.shape
    nrows = nrows or m
    tm = _pick(math.gcd(nrows, row0) if row0 else nrows, (1040, 1024, 640, 512, 256, 128))
    r0 = row0 // tm
    g, gspec = g_spec
    return pl.pallas_call(
        _rmsnorm_body,
        grid=(nrows // tm,),
        in_specs=[pl.BlockSpec((tm, d), lambda i: (r0 + i, 0)), gspec],
        out_specs=pl.BlockSpec((tm, d), lambda i: (i, 0)),
        out_shape=jax.ShapeDtypeStruct((nrows, d), out_dtype),
        compiler_params=_cparams("parallel"),
        name="rmsnorm",
    )(x, g)


def _mm_body(*refs, act, has_res, nk):
    a_ref, w_ref = refs[0], refs[1]
    res_ref = refs[2] if has_res else None
    o_ref = refs[3] if has_res else refs[2]

    def finish(r):
        if act == "relu2":
            r = jnp.square(jnp.maximum(r, 0.0))
        if has_res:
            r = res_ref[...] + r
        o_ref[...] = r.astype(o_ref.dtype)

    if nk == 1:
        finish(_dot(a_ref[...], w_ref[...]))
        return
    acc_ref = refs[-1]
    k = pl.program_id(2)

    @pl.when(k == 0)
    def _():
        acc_ref[...] = jnp.zeros_like(acc_ref)

    acc_ref[...] += _dot(a_ref[...], w_ref[...])

    @pl.when(k == nk - 1)
    def _():
        finish(acc_ref[...])


def _matmul(a, w, l, *, residual=None, act=None, out_dtype=F32, tn=None):
    m, kdim = a.shape
    n = w.shape[2]
    tm = _pick(m, (1040, 1024, 640, 512, 256, 128))
    tn = tn or _pick(n, (1024, 896, 768, 512, 256))
    tk = _pick(kdim, (2048, 1024, 512))
    nk = kdim // tk
    has_res = residual is not None
    in_specs = [pl.BlockSpec((tm, tk), lambda i, j, k: (i, k)), pl.BlockSpec((None, tk, tn), lambda i, j, k: (l, k, j))]
    args = [a, w]
    if has_res:
        in_specs.append(pl.BlockSpec((tm, tn), lambda i, j, k: (i, j)))
        args.append(residual)
    return pl.pallas_call(
        functools.partial(_mm_body, act=act, has_res=has_res, nk=nk),
        grid=(m // tm, n // tn, nk),
        in_specs=in_specs,
        out_specs=pl.BlockSpec((tm, tn), lambda i, j, k: (i, j)),
        out_shape=jax.ShapeDtypeStruct((m, n), out_dtype),
        scratch_shapes=[pltpu.VMEM((tm, tn), F32)] if nk > 1 else [],
        compiler_params=_cparams("parallel", "parallel", "arbitrary"),
        name="matmul",
    )(*args)


def _mm_ws_body(*refs, act, has_res):
    a_ref, w_ref = refs[0], refs[1]
    res_ref = refs[2] if has_res else None
    o_ref = refs[3] if has_res else refs[2]
    wb_ref = refs[-1]

    @pl.when(pl.program_id(1) == 0)
    def _():
        wb_ref[...] = w_ref[...].astype(BF16)

    r = _dot(a_ref[...], wb_ref[...])
    if act == "relu2":
        r = jnp.square(jnp.maximum(r, 0.0))
    if has_res:
        r = res_ref[...] + r
    o_ref[...] = r.astype(o_ref.dtype)


def _matmul_ws(a, w, l, *, residual=None, act=None, out_dtype=F32, tn=1024):
    m, kdim = a.shape
    n = w.shape[2]
    tm = _pick(m, (1040, 1024, 640, 512, 256, 128))
    has_res = residual is not None
    in_specs = [pl.BlockSpec((tm, kdim), lambda j, i: (i, 0)), pl.BlockSpec((None, kdim, tn), lambda j, i: (l, 0, j))]
    args = [a, w]
    if has_res:
        in_specs.append(pl.BlockSpec((tm, tn), lambda j, i: (i, j)))
        args.append(residual)
    return pl.pallas_call(
        functools.partial(_mm_ws_body, act=act, has_res=has_res),
        grid=(n // tn, m // tm),
        in_specs=in_specs,
        out_specs=pl.BlockSpec((tm, tn), lambda j, i: (i, j)),
        out_shape=jax.ShapeDtypeStruct((m, n), out_dtype),
        scratch_shapes=[pltpu.VMEM((kdim, tn), BF16)],
        compiler_params=_cparams("parallel", "arbitrary"),
        name="matmul_ws",
    )(*args)


def _conv_a_body(b_ref, c_ref, h_ref, w_ref, y_ref, st_ref, ext_ref, *, tl):
    t = pl.program_id(1)

    @pl.when(t == 0)
    def _():
        ext_ref[0:8, :] = jnp.zeros((8, G_WIDTH), F32)

    u = c_ref[...] * h_ref[...]
    ext_ref[8:, :] = u
    u1 = ext_ref[7:7 + tl, :]
    u2 = ext_ref[6:6 + tl, :]
    y = b_ref[...] * (w_ref[0:1, :] * u2 + w_ref[1:2, :] * u1 + w_ref[2:3, :] * u)
    y_ref[...] = y.astype(y_ref.dtype)
    st_ref[0] = ext_ref[tl + 6:tl + 8, :]
    ext_ref[0:8, :] = ext_ref[tl:tl + 8, :]


def _conv_a_prompt(proj, conv_w, l, bp, seq):
    tl = _pick(seq, (512, 256, 128))
    nt = seq // tl
    c0 = COL_A // G_WIDTH
    col = lambda j: pl.BlockSpec((tl, G_WIDTH), lambda b, t: (b * nt + t, c0 + j))
    return pl.pallas_call(
        functools.partial(_conv_a_body, tl=tl),
        grid=(bp, nt),
        in_specs=[col(0), col(1), col(2), _layer_mat(conv_w, l)],
        out_specs=[pl.BlockSpec((tl, G_WIDTH), lambda b, t: (b * nt + t, 0)),
                   pl.BlockSpec((1, A_CONV - 1, G_WIDTH), lambda b, t: (b, 0, 0))],
        out_shape=[jax.ShapeDtypeStruct((proj.shape[0], D_MODEL), BF16),
                   jax.ShapeDtypeStruct((bp, A_CONV - 1, G_WIDTH), F32)],
        scratch_shapes=[pltpu.VMEM((tl + 8, G_WIDTH), F32)],
        compiler_params=_cparams("parallel", "arbitrary"),
        name="conv_a_prompt",
    )(proj, proj, proj, conv_w)


CONV_D_ROWS = 32


def _conv_d_body(d1a_ref, d1b_ref, d2a_ref, d2b_ref, w_ref, bias_ref, g_ref, beta_ref, ymix_ref, y_ref, st_ref, ext_ref,
                 acc_ref, *, tl):
    del ymix_ref
    t = pl.program_id(1)
    next_ = D_HIST + tl

    @pl.when(t == 0)
    def _():
        ext_ref[0, 0:D_HIST, :] = jnp.zeros((D_HIST, G_WIDTH), F32)

    d1 = jnp.concatenate([d1a_ref[...], d1b_ref[...]], axis=1)
    d2 = jnp.concatenate([d2a_ref[...], d2b_ref[...]], axis=1)
    ext_ref[0, D_HIST:, :] = d1 * _sigmoid(d2)
    ext = ext_ref[0]
    for s in range(1, 8):
        ext_ref[s] = pltpu.roll(ext, next_ - s, 0)
    first = D_HIST - (D_CONV - 1)

    def blk(i, carry):
        base = pl.multiple_of(i * CONV_D_ROWS, CONV_D_ROWS)
        acc = jnp.broadcast_to(bias_ref[...], (CONV_D_ROWS, G_WIDTH))
        for k in range(D_CONV):
            s = (first + k) % 8
            a0 = first + k - s
            acc = acc + ext_ref[s, pl.ds(base + a0, CONV_D_ROWS), :] * w_ref[k:k + 1, :]
        acc_ref[pl.ds(base, CONV_D_ROWS), :] = acc
        return carry

    lax.fori_loop(0, tl // CONV_D_ROWS, blk, 0)
    c = _layernorm(acc_ref[...], EPS) * g_ref[...] + beta_ref[...]
    y_ref[...] = (c * _sigmoid(c)).astype(y_ref.dtype)
    st_ref[0] = ext_ref[0, tl + first:tl + D_HIST, :]
    ext_ref[0, 0:D_HIST, :] = ext_ref[0, tl:tl + D_HIST, :]


def _conv_d_prompt(proj, w, bias, g, beta, ymix, l, bp, seq):
    tl = _pick(seq, (512, 256, 128))
    nt = seq // tl
    half = G_WIDTH // 2
    c0 = COL_D // half
    col = lambda j: pl.BlockSpec((tl, half), lambda b, t: (b * nt + t, c0 + j))
    (bias, bias_s), (g, g_s), (beta, beta_s) = _layer_vec(bias, l), _layer_vec(g, l), _layer_vec(beta, l)
    return pl.pallas_call(
        functools.partial(_conv_d_body, tl=tl),
        grid=(bp, nt),
        in_specs=[col(0), col(1), col(2), col(3), _layer_mat(w, l), bias_s, g_s, beta_s, ANY_SPEC],
        out_specs=[pl.BlockSpec((tl, G_WIDTH), lambda b, t: (b * nt + t, 3)),
                   pl.BlockSpec((1, D_CONV - 1, G_WIDTH), lambda b, t: (b, 0, 0))],
        out_shape=[jax.ShapeDtypeStruct(ymix.shape, ymix.dtype),
                   jax.ShapeDtypeStruct((bp, D_CONV - 1, G_WIDTH), F32)],
        scratch_shapes=[pltpu.VMEM((8, tl + D_HIST, G_WIDTH), F32), pltpu.VMEM((tl, G_WIDTH), F32)],
        input_output_aliases={8: 0},
        compiler_params=_cparams("parallel", "arbitrary"),
        name="conv_d_prompt",
    )(proj, proj, proj, proj, w, bias, g, beta, ymix)


def _rope_tables(pos):
    inv = ROPE_BASE ** (-jnp.arange(0, RET_D, 2, dtype=F32) / RET_D)
    ang = pos.astype(F32)[:, None] * inv[None, :]
    cos = jnp.cos(ang)
    sin = jnp.sin(ang)
    return jnp.concatenate([cos, cos], axis=-1), jnp.concatenate([-sin, sin], axis=-1)


def _rope(x, cos2, sin2):
    return x * cos2 + pltpu.roll(x, RET_D // 2, 1) * sin2


def _ret_body(q_ref, k_ref, v_ref, g_ref, cos_ref, sin_ref, ymix_ref, y_ref, st_ref, s_ref, *, nchunks):
    del ymix_ref
    c = pl.program_id(1)
    C = RET_CHUNK

    @pl.when(c == 0)
    def _():
        s_ref[...] = jnp.zeros_like(s_ref)

    ii = lax.broadcasted_iota(jnp.int32, (C, C), 0).astype(F32)
    jj = lax.broadcasted_iota(jnp.int32, (C, C), 1).astype(F32)
    diff = ii - jj
    cos2 = cos_ref[...]
    sin2 = sin_ref[...]
    H = range(RET_HEADS)
    sl = [slice(h * RET_D, (h + 1) * RET_D) for h in H]
    lg = RET_LOG_DECAY
    q = [_rope(q_ref[:, sl[h]], cos2, sin2) for h in H]
    k = [_rope(k_ref[:, sl[h]], cos2, sin2) * (RET_D ** -0.5) for h in H]
    qb = [q[h].astype(BF16) for h in H]
    vb = [v_ref[:, sl[h]].astype(BF16) for h in H]
    s_old = [s_ref[h] for h in H]
    scores = [_dot_nt(qb[h], k[h].astype(BF16)) for h in H]
    cross = [_dot(qb[h], s_old[h].astype(BF16)) for h in H]
    kv = [_dot_tn((k[h] * jnp.exp(lg[h] * (C - 1.0 - ii))).astype(BF16), vb[h]) for h in H]
    dmat = [jnp.where(diff >= 0.0, jnp.exp(lg[h] * jnp.maximum(diff, 0.0)), 0.0) for h in H]
    inner = [_dot((scores[h] * dmat[h]).astype(BF16), vb[h]) for h in H]
    for h in H:
        s_ref[h] = math.exp(lg[h] * C) * s_old[h] + kv[h]
        o = _layernorm(inner[h] + cross[h] * jnp.exp(lg[h] * (ii + 1.0)), EPS)
        g = g_ref[:, sl[h]]
        y_ref[:, sl[h]] = (g * _sigmoid(g) * o).astype(y_ref.dtype)

    @pl.when(c == nchunks - 1)
    def _():
        st_ref[0] = s_ref[...]


def _retention_prompt(proj, cos2, sin2, ymix, bp, seq):
    C = RET_CHUNK
    nc = seq // C
    c0 = COL_B // G_WIDTH
    col = lambda j: pl.BlockSpec((C, G_WIDTH), lambda b, c: (b * nc + c, c0 + j))
    tab = pl.BlockSpec((C, RET_D), lambda b, c: (c, 0))
    return pl.pallas_call(
        functools.partial(_ret_body, nchunks=nc),
        grid=(bp, nc),
        in_specs=[col(0), col(1), col(2), col(3), tab, tab, ANY_SPEC],
        out_specs=[pl.BlockSpec((C, G_WIDTH), lambda b, c: (b * nc + c, 1)),
                   pl.BlockSpec((1, RET_HEADS, RET_D, RET_D), lambda b, c: (b, 0, 0, 0))],
        out_shape=[jax.ShapeDtypeStruct(ymix.shape, ymix.dtype),
                   jax.ShapeDtypeStruct((bp, RET_HEADS, RET_D, RET_D), F32)],
        scratch_shapes=[pltpu.VMEM((RET_HEADS, RET_D, RET_D), F32)],
        input_output_aliases={6: 0},
        compiler_params=_cparams("parallel", "arbitrary"),
        name="retention_prompt",
    )(proj, proj, proj, proj, cos2, sin2, ymix)


def _softplus(z):
    return jnp.maximum(z, 0.0) + jnp.log1p(jnp.exp(-jnp.abs(z)))


def _rwkv_pre(r, kc, vc, lora_x, w2_w, w2_a, g2, w0, a0, k_k, k_a):
    lin = lora_x[:, 0:128]
    lane = lax.broadcasted_iota(jnp.int32, lin.shape, 1)
    lin = jnp.where(lane < 64, jnp.tanh(lin), lin).astype(BF16)
    gate = _dot(_sigmoid(lora_x[:, 128:256]).astype(BF16), g2)
    w_pre = -_softplus(-(w0 + _dot(lin, w2_w))) - 0.5
    log_decay = -jnp.exp(w_pre)
    a = _sigmoid(a0 + _dot(lin, w2_a))
    kk = kc * k_k
    k4 = kc * (1.0 + (a - 1.0) * k_a)
    return r, log_decay, k4, vc, kk, a, gate


def _l2_normalize(kk):
    nrm = jnp.sqrt(jnp.sum(kk * kk, axis=-1, keepdims=True))
    return kk / jnp.maximum(nrm, 1e-12)


def _bdot(a, b):
    return _dot(a.astype(BF16), b.astype(BF16))


def _wkv_chunk(r, lw, cum, k, v, kk, a, z, mask2, eye):
    T = WKV_CHUNK
    H = range(len(r))
    g_inc = [jnp.exp(cum[h]) for h in H]
    g_tot = [g_inc[h][T - 1:T, :] for h in H]
    ah = [-(kk[h] * jnp.exp(cum[h] - lw[h])) for h in H]
    rg = [r[h] * g_inc[h] for h in H]
    g_inv = [jnp.exp(-cum[h]) for h in H]
    bd = [kk[h] * a[h] * g_inv[h] for h in H]
    kd = [k[h] * g_inv[h] for h in H]
    ar = [jnp.concatenate([ah[h], rg[h]], axis=0).astype(BF16) for h in H]
    xb = [jnp.where(mask2, _dot_nt(ar[h], bd[h].astype(BF16)), 0.0) for h in H]
    xk = [jnp.where(mask2, _dot_nt(ar[h], kd[h].astype(BF16)), 0.0).astype(BF16) for h in H]
    vb = [v[h].astype(BF16) for h in H]
    xkv = [_dot(xk[h], vb[h]) for h in H]
    acur = [xb[h][0:T] for h in H]
    p = [eye for _ in H]
    for it in range(5):
        x = [_bdot(jnp.concatenate([acur[h], p[h]], axis=0), acur[h]) for h in H]
        acur = [x[h][0:T] for h in H]
        p = [p[h] + x[h][T:2 * T] for h in H]
    x = [_bdot(p[h], acur[h]) for h in H]
    p = [p[h] + x[h] for h in H]
    wub = [_bdot(p[h], jnp.concatenate([ah[h], xkv[h][0:T]], axis=1)).astype(BF16) for h in H]
    qo = [_dot(xb[h][T:2 * T].astype(BF16), wub[h]) for h in H]
    lhs_t = [jnp.concatenate([bd[h] * g_tot[h], kd[h] * g_tot[h]], axis=0).astype(BF16) for h in H]
    zero = jnp.zeros((T, 64), BF16)
    mn = [_dot_tn(lhs_t[h], jnp.concatenate([wub[h], jnp.concatenate([zero, vb[h]], axis=1)], axis=0)) for h in H]
    qm = [jnp.concatenate([rg[h] + qo[h][:, 0:64], mn[h][:, 0:64] + jnp.where(eye > 0.0, g_tot[h], 0.0)], axis=0) for h in H]
    oz = [_bdot(qm[h], z[h]) for h in H]
    o = [oz[h][0:T] + qo[h][:, 64:128] + xkv[h][T:2 * T] for h in H]
    z_new = [oz[h][T:2 * T] + mn[h][:, 64:128] for h in H]
    return o, z_new


def _rwkv_post(o, r, k4, v, gate, rk, lnx_g, lnx_b):
    gn = _layernorm(o, RWKV_GN_EPS) * lnx_g + lnx_b
    bonus = jnp.sum(r * k4 * rk, axis=-1, keepdims=True) * v
    return (gn + bonus) * gate


def _rwkv_prompt_body(*refs, nb, nchunks):
    pc_refs = refs[:nb]
    mu_ref, w2a_ref, g2_ref, w0_ref, a0_ref, kk_ref, ka_ref, rk_ref, lg_ref, lb_ref = refs[nb:nb + 10]
    y_ref, shift_ref, st_ref, carry_ref, z_ref = refs[nb + 10:]
    c = pl.program_id(0)
    T = WKV_CHUNK
    NH = RWKV_HEADS

    @pl.when(c == 0)
    def _():
        carry_ref[...] = jnp.zeros_like(carry_ref)
        z_ref[...] = jnp.zeros_like(z_ref)

    pcs, prevs = [], []
    for b in range(nb):
        pc = pc_refs[b][...]
        row = lax.broadcasted_iota(jnp.int32, pc.shape, 0)
        prevs.append(jnp.where(row == 0, carry_ref[b, 0:1, :], pltpu.roll(pc, 1, 0)))
        carry_ref[b, 0:1, :] = pc[T - 1:T, :]
        shift_ref[b] = pc[T - 1:T, :]
        pcs.append(pc)
    pc = jnp.concatenate(pcs, axis=0)
    xs = pc + (jnp.concatenate(prevs, axis=0) - pc) * mu_ref[...]
    G = G_WIDTH
    r, lw, k4, v, kk, a, gate = _rwkv_pre(xs[:, 0:G], xs[:, G:2 * G], xs[:, 2 * G:3 * G], xs[:, 3 * G:3 * G + 256],
                                          w2a_ref[:, 0:G], w2a_ref[:, G:2 * G], g2_ref[...], w0_ref[...], a0_ref[...],
                                          kk_ref[...], ka_ref[...])
    ti = lax.broadcasted_iota(jnp.int32, (T, T), 0)
    si = lax.broadcasted_iota(jnp.int32, (T, T), 1)
    tril = (ti >= si).astype(F32)
    eye = (ti == si).astype(F32)
    cum = jnp.concatenate([_dot(tril, lw[b * T:(b + 1) * T, :], HI) for b in range(nb)], axis=0)
    t2 = lax.broadcasted_iota(jnp.int32, (2 * T, T), 0)
    s2 = lax.broadcasted_iota(jnp.int32, (2 * T, T), 1)
    mask2 = jnp.where(t2 < T, t2 - 1, t2 - T) >= s2
    heads = lambda x: [x[b * T:(b + 1) * T, h * RWKV_HEAD:(h + 1) * RWKV_HEAD] for b in range(nb) for h in range(NH)]
    r_h, k_h, v_h = heads(r), heads(k4), heads(v)
    o, z_new = _wkv_chunk(r_h, heads(lw), heads(cum), k_h, v_h, [_l2_normalize(x) for x in heads(kk)], heads(a),
                          [z_ref[i] for i in range(nb * NH)], mask2, eye)
    for i in range(nb * NH):
        z_ref[i] = z_new[i]
    gate_h = heads(gate)
    vec_heads = lambda ref: [ref[:, h * RWKV_HEAD:(h + 1) * RWKV_HEAD] for h in range(NH)]
    rk_h, lg_h, lb_h = vec_heads(rk_ref), vec_heads(lg_ref), vec_heads(lb_ref)
    for b in range(nb):
        ys = [_rwkv_post(o[b * NH + h], r_h[b * NH + h], k_h[b * NH + h], v_h[b * NH + h], gate_h[b * NH + h],
                         rk_h[h], lg_h[h], lb_h[h]) for h in range(NH)]
        y_ref[b] = jnp.concatenate(ys, axis=1).astype(y_ref.dtype)

    @pl.when(c == nchunks - 1)
    def _():
        for b in range(nb):
            for h in range(NH):
                st_ref[b, h] = z_ref[b * NH + h].T


def _rwkv_prompt(proj, rw, bp, seq):
    T = WKV_CHUNK
    nc = seq // T
    G = G_WIDTH
    arrs, specs = zip(*rw)
    pc_specs = [pl.BlockSpec((T, SHIFT_W), lambda c, b=b: (b * nc + c, COL_C // SHIFT_W)) for b in range(bp)]
    return pl.pallas_call(
        functools.partial(_rwkv_prompt_body, nb=bp, nchunks=nc),
        grid=(nc,),
        in_specs=pc_specs + list(specs),
        out_specs=[pl.BlockSpec((bp, T, G), lambda c: (0, c, 0)),
                   pl.BlockSpec((bp, 1, SHIFT_W), lambda c: (0, 0, 0)),
                   pl.BlockSpec((bp, RWKV_HEADS, RWKV_HEAD, RWKV_HEAD), lambda c: (0, 0, 0, 0))],
        out_shape=[jax.ShapeDtypeStruct((bp, seq, G), BF16),
                   jax.ShapeDtypeStruct((bp, 1, SHIFT_W), F32),
                   jax.ShapeDtypeStruct((bp, RWKV_HEADS, RWKV_HEAD, RWKV_HEAD), F32)],
        scratch_shapes=[pltpu.VMEM((bp, 8, SHIFT_W), F32), pltpu.VMEM((bp * RWKV_HEADS, RWKV_HEAD, RWKV_HEAD), F32)],
        compiler_params=_cparams("arbitrary"),
        name="rwkv_prompt",
    )(*([proj] * bp), *arrs)


def _xattn_prompt_body(q_ref, k_ref, v_ref, o_ref, kb_ref, vb_ref):
    @pl.when(pl.program_id(1) == 0)
    def _():
        kb_ref[...] = k_ref[...].astype(BF16)
        vb_ref[...] = v_ref[...].astype(BF16)

    for h in range(XA_HEADS):
        sl = slice(h * XA_HD, (h + 1) * XA_HD)
        s = _dot_nt(q_ref[:, sl].astype(BF16), kb_ref[:, sl]) * (XA_HD ** -0.5)
        e = jnp.exp(s - jnp.max(s, axis=-1, keepdims=True))
        p = e / jnp.sum(e, axis=-1, keepdims=True)
        o_ref[:, sl] = _dot(p.astype(BF16), vb_ref[:, sl]).astype(o_ref.dtype)


def _xattn_prompt(q, mk, mv, bp, seq):
    tq = _pick(seq, (512, 256, 128))
    nt = seq // tq
    kv = pl.BlockSpec((N_MEM, D_MODEL), lambda b, t: (b, 0))
    return pl.pallas_call(
        _xattn_prompt_body,
        grid=(bp, nt),
        in_specs=[pl.BlockSpec((tq, D_MODEL), lambda b, t: (b * nt + t, 0)), kv, kv],
        out_specs=pl.BlockSpec((tq, D_MODEL), lambda b, t: (b * nt + t, 0)),
        out_shape=jax.ShapeDtypeStruct((q.shape[0], D_MODEL), BF16),
        scratch_shapes=[pltpu.VMEM((N_MEM, D_MODEL), BF16), pltpu.VMEM((N_MEM, D_MODEL), BF16)],
        compiler_params=_cparams("parallel", "arbitrary"),
        name="xattn_prompt",
    )(q, mk, mv)


XA_SAMPLES_PER_STEP = 2


def _xattn_sample_body(q_ref, k_ref, v_ref, obuf_ref, o_ref, acc_ref, *, nsteps):
    del obuf_ref
    i = pl.program_id(0)
    for j in range(XA_SAMPLES_PER_STEP):
        b = i * XA_SAMPLES_PER_STEP + j
        for h in range(XA_HEADS):
            q = q_ref[h, pl.ds(b, 1), :]
            s = jnp.sum(k_ref[j, :, h, :] * q, axis=-1, keepdims=True) * (XA_HD ** -0.5)
            e = jnp.exp(s - jnp.max(s, axis=0, keepdims=True))
            p = e / jnp.sum(e, axis=0, keepdims=True)
            acc_ref[h, pl.ds(b, 1), :] = jnp.sum(p * v_ref[j, :, h, :], axis=0, keepdims=True)

    @pl.when(i == nsteps - 1)
    def _():
        o_ref[...] = jnp.concatenate([acc_ref[h] for h in range(XA_HEADS)], axis=1).astype(o_ref.dtype)


def _xattn_sample(q, ck, cv, obuf, l, row0):
    bs = q.shape[1]
    nb = XA_SAMPLES_PER_STEP
    kv = pl.BlockSpec((None, nb, N_MEM, XA_HEADS, XA_HD), lambda i: (l, i, 0, 0, 0))
    return pl.pallas_call(
        functools.partial(_xattn_sample_body, nsteps=bs // nb),
        grid=(bs // nb,),
        in_specs=[pl.BlockSpec((XA_HEADS, bs, XA_HD), lambda i: (0, 0, 0)), kv, kv, ANY_SPEC],
        out_specs=pl.BlockSpec((bs, D_MODEL), lambda i: (row0 // bs, 0)),
        out_shape=jax.ShapeDtypeStruct(obuf.shape, obuf.dtype),
        scratch_shapes=[pltpu.VMEM((XA_HEADS, bs, XA_HD), F32)],
        input_output_aliases={3: 0},
        compiler_params=_cparams("arbitrary"),
        name="xattn_sample",
    )(q, ck, cv, obuf)


def _alias_args(prev):
    return ([], []) if prev is None else ([prev], [ANY_SPEC])


def _sample_ad_body(*refs, has_prev):
    (ab_ref, ac_ref, ah_ref, sta_ref, wa_ref, d1a_ref, d1b_ref, d2a_ref, d2b_ref, std_ref, wd_ref, bias_ref, g_ref,
     beta_ref) = refs[:14]
    ya_ref, yd_ref, na_ref, nd_ref = refs[14 + 2 * has_prev:]
    u = ac_ref[...] * ah_ref[...]
    s0 = sta_ref[:, 0, :]
    s1 = sta_ref[:, 1, :]
    ya = ab_ref[...] * (wa_ref[0:1, :] * s0 + wa_ref[1:2, :] * s1 + wa_ref[2:3, :] * u)
    ya_ref[...] = ya.astype(ya_ref.dtype)
    na_ref[:, 0, :] = s1
    na_ref[:, 1, :] = u
    d1 = jnp.concatenate([d1a_ref[...], d1b_ref[...]], axis=1)
    d2 = jnp.concatenate([d2a_ref[...], d2b_ref[...]], axis=1)
    ud = d1 * _sigmoid(d2)
    acc = jnp.broadcast_to(bias_ref[...], ud.shape)
    nh = D_CONV - 1
    for k in range(nh):
        s_k = std_ref[:, k, :]
        acc = acc + s_k * wd_ref[k:k + 1, :]
        if k > 0:
            nd_ref[:, k - 1, :] = s_k
    acc = acc + ud * wd_ref[nh:nh + 1, :]
    nd_ref[:, nh - 1, :] = ud
    c = _layernorm(acc, EPS) * g_ref[...] + beta_ref[...]
    yd_ref[...] = (c * _sigmoid(c)).astype(yd_ref.dtype)


def _sample_ad(proj, st_a, st_d, wa, wd, bias, g, beta, l, row0, prev_a, prev_d):
    bs = st_a.shape[1]
    G = G_WIDTH
    tb = 32
    r0 = row0 // tb
    half = G // 2
    cola = lambda j: pl.BlockSpec((tb, G), lambda i: (r0 + i, COL_A // G + j))
    cold = lambda j: pl.BlockSpec((tb, half), lambda i: (r0 + i, COL_D // half + j))
    st = lambda n: pl.BlockSpec((None, tb, n, G), lambda i: (l, i, 0, 0))
    yrow = pl.BlockSpec((tb, G), lambda i: (i, 0))
    (bias, bias_s), (g, g_s), (beta, beta_s) = _layer_vec(bias, l), _layer_vec(g, l), _layer_vec(beta, l)
    nh = D_CONV - 1
    has_prev = prev_a is not None
    extra = [prev_a, prev_d] if has_prev else []
    return pl.pallas_call(
        functools.partial(_sample_ad_body, has_prev=has_prev),
        grid=(bs // tb,),
        in_specs=[cola(0), cola(1), cola(2), st(A_CONV - 1), _layer_mat(wa, l), cold(0), cold(1), cold(2), cold(3), st(nh),
                  _layer_mat(wd, l), bias_s, g_s, beta_s] + [ANY_SPEC] * len(extra),
        out_specs=[yrow, yrow, st(A_CONV - 1), st(nh)],
        out_shape=[jax.ShapeDtypeStruct((bs, G), BF16), jax.ShapeDtypeStruct((bs, G), BF16),
                   jax.ShapeDtypeStruct(st_a.shape, F32), jax.ShapeDtypeStruct(st_d.shape, F32)],
        input_output_aliases={14: 2, 15: 3} if has_prev else {},
        compiler_params=_cparams("parallel"),
        name="sample_conv_ad",
    )(proj, proj, proj, st_a, wa, proj, proj, proj, proj, st_d, wd, bias, g, beta, *extra)


RET_D_BLOCK = 16


def _ret_sample_body(*refs, has_prev):
    q_ref, k_ref, v_ref, g_ref, cos_ref, sin_ref, s_ref = refs[:7]
    y_ref, ns_ref, qt_ref, kt_ref, acc_ref = refs[7 + has_prev:]
    h = pl.program_id(0)
    j = pl.program_id(1)
    nd = RET_D // RET_D_BLOCK
    gammas = [math.exp(lg) for lg in RET_LOG_DECAY]
    gamma = jnp.where(h == 0, gammas[0], jnp.where(h == 1, gammas[1], jnp.where(h == 2, gammas[2], gammas[3])))
    gamma = gamma.astype(F32)

    @pl.when(j == 0)
    def _():
        q = _rope(q_ref[...], cos_ref[...], sin_ref[...])
        k = _rope(k_ref[...], cos_ref[...], sin_ref[...]) * (RET_D ** -0.5)
        qt_ref[...] = q.T
        kt_ref[...] = k.T
        acc_ref[...] = jnp.zeros_like(acc_ref)

    vt = v_ref[...].T
    acc = acc_ref[...]
    for d in range(RET_D_BLOCK):
        s_d = s_ref[:, d, :].T
        row = j * RET_D_BLOCK + d
        q_d = qt_ref[pl.ds(row, 1), :]
        k_d = kt_ref[pl.ds(row, 1), :]
        acc = acc + q_d * s_d
        ns_ref[:, d, :] = (gamma * s_d + k_d * vt).T
    acc_ref[...] = acc

    @pl.when(j == nd - 1)
    def _():
        qk = jnp.sum(qt_ref[...] * kt_ref[...], axis=0, keepdims=True)
        o = qk * vt + acc * gamma
        mu = jnp.mean(o, axis=0, keepdims=True)
        oc = o - mu
        var = jnp.mean(oc * oc, axis=0, keepdims=True)
        ln = (oc * lax.rsqrt(var + EPS)).T
        g = g_ref[...]
        y_ref[...] = (g * _sigmoid(g) * ln).astype(y_ref.dtype)


def _retention_sample(proj, cos2, sin2, state, l, row0, prev):
    bs = state.shape[1]
    r0 = row0 // bs
    nd = RET_D // RET_D_BLOCK
    col = lambda c: pl.BlockSpec((bs, RET_D), lambda h, j: (r0, COL_B // RET_D + c * RET_HEADS + h))
    tab = pl.BlockSpec((1, RET_D), lambda h, j: (0, 0))
    sblk = pl.BlockSpec((None, bs, None, RET_D_BLOCK, RET_D), lambda h, j: (l, 0, h, j, 0))
    extra, extra_specs = _alias_args(prev)
    return pl.pallas_call(
        functools.partial(_ret_sample_body, has_prev=prev is not None),
        grid=(RET_HEADS, nd),
        in_specs=[col(0), col(1), col(2), col(3), tab, tab, sblk] + extra_specs,
        out_specs=[pl.BlockSpec((bs, RET_D), lambda h, j: (0, h)), sblk],
        out_shape=[jax.ShapeDtypeStruct((bs, G_WIDTH), BF16), jax.ShapeDtypeStruct(state.shape, F32)],
        scratch_shapes=[pltpu.VMEM((RET_D, bs), F32), pltpu.VMEM((RET_D, bs), F32), pltpu.VMEM((RET_D, bs), F32)],
        input_output_aliases={7: 1} if prev is not None else {},
        compiler_params=_cparams("parallel", "arbitrary"),
        name="retention_sample",
    )(proj, proj, proj, proj, cos2, sin2, state, *extra)


def _rwkv_sample_body(*refs, has_prev):
    (xr_ref, xk_ref, xv_ref, xl_ref, pr_ref, pk_ref, pv_ref, pl_ref, mr_ref, mk_ref, mv_ref, ml_ref,
     w2w_ref, w2a_ref, g2_ref, w0_ref, a0_ref, kk_ref, ka_ref, rk_ref, lg_ref, lb_ref, s_ref) = refs[:23]
    y_ref, ns_ref, vec_ref, ot_ref = refs[23 + has_prev:]
    N = RWKV_HEAD

    def shifted(x_ref, prev_ref, mu_ref):
        x = x_ref[...]
        return x + (prev_ref[...] - x) * mu_ref[...]

    r, lw, k4, v, kk, a, gate = _rwkv_pre(shifted(xr_ref, pr_ref, mr_ref), shifted(xk_ref, pk_ref, mk_ref),
                                          shifted(xv_ref, pv_ref, mv_ref), shifted(xl_ref, pl_ref, ml_ref),
                                          w2w_ref[...], w2a_ref[...], g2_ref[...], w0_ref[...], a0_ref[...],
                                          kk_ref[...], ka_ref[...])
    kkn = jnp.concatenate([_l2_normalize(kk[:, 0:N]), _l2_normalize(kk[:, N:2 * N])], axis=1)
    w = jnp.exp(lw)
    vec_ref[0] = kkn.T
    vec_ref[1] = w.T
    vec_ref[2] = (kkn * a).T
    vec_ref[3] = k4.T
    vec_ref[4] = r.T
    vec_ref[5] = v.T
    st = s_ref[...].T
    news = []
    for hh in range(2):
        ch = slice(hh * N, (hh + 1) * N)
        kk_t = vec_ref[0, ch, :]
        w_t = vec_ref[1, ch, :]
        kka_t = vec_ref[2, ch, :]
        k_t = vec_ref[3, ch, :]
        r_t = vec_ref[4, ch, :]
        for i in range(N):
            base = (hh * N + i) * N
            s_i = st[base:base + N, :]
            sa = -jnp.sum(s_i * kk_t, axis=0, keepdims=True)
            v_i = vec_ref[5, hh * N + i:hh * N + i + 1, :]
            s_new = s_i * w_t + sa * kka_t + v_i * k_t
            news.append(s_new)
            ot_ref[hh * N + i:hh * N + i + 1, :] = jnp.sum(s_new * r_t, axis=0, keepdims=True)
    ns_ref[...] = jnp.concatenate(news, axis=0).T
    o = ot_ref[...].T
    rk = rk_ref[...]
    lg = lg_ref[...]
    lb = lb_ref[...]
    ys = []
    for hh in range(2):
        ch = slice(hh * N, (hh + 1) * N)
        ys.append(_rwkv_post(o[:, ch], r[:, ch], k4[:, ch], v[:, ch], gate[:, ch], rk[:, ch], lg[:, ch], lb[:, ch]))
    y_ref[...] = jnp.concatenate(ys, axis=1).astype(y_ref.dtype)


def _rwkv_sample(proj, prev_shift, rw_arrs, state, l, row0, prev):
    mu, w2a, g2, w0, a0, k_k, k_a, r_k, lnx_g, lnx_b = rw_arrs
    bs = state.shape[1]
    r0 = row0 // bs
    G = G_WIDTH
    N = RWKV_HEAD
    W = 2 * N
    npair = G // W
    def cols(shape_lead, lead_idx, base):
        mk = lambda width, cidx: pl.BlockSpec(shape_lead + (width,), lambda p: lead_idx + (cidx(p),))
        return [mk(W, lambda p, c=c: base // W + c * npair + p) for c in range(3)] + [
            mk(2 * W, lambda p: (base + 3 * G) // (2 * W))]
    pairv = pl.BlockSpec((None, 1, W), lambda p: (l, 0, p))
    pairm = pl.BlockSpec((None, 128, W), lambda p: (l, 0, p))
    sblk = pl.BlockSpec((None, bs, W * N), lambda p: (l, 0, p))
    extra, extra_specs = _alias_args(prev)
    return pl.pallas_call(
        functools.partial(_rwkv_sample_body, has_prev=prev is not None),
        grid=(npair,),
        in_specs=cols((bs,), (r0,), COL_C) + cols((None, bs), (l, 0), 0) + cols((None, 1), (l, 0), 0) + [
            pairm, pl.BlockSpec((None, 128, W), lambda p: (l, 0, npair + p)), pairm,
            pairv, pairv, pairv, pairv, pairv, pairv, pairv, sblk] + extra_specs,
        out_specs=[pl.BlockSpec((bs, W), lambda p: (0, p)), sblk],
        out_shape=[jax.ShapeDtypeStruct((bs, G), BF16), jax.ShapeDtypeStruct(state.shape, F32)],
        scratch_shapes=[pltpu.VMEM((6, W, bs), F32), pltpu.VMEM((W, bs), F32)],
        input_output_aliases={23: 1} if prev is not None else {},
        compiler_params=_cparams("parallel"),
        name="rwkv_sample",
    )(proj, proj, proj, proj, prev_shift, prev_shift, prev_shift, prev_shift, mu, mu, mu, mu, w2a, w2a, g2,
      w0, a0, k_k, k_a, r_k, lnx_g, lnx_b, state, *extra)


def _place_cols_body(src_ref, buf_ref, o_ref):
    del buf_ref
    o_ref[...] = src_ref[...]


def _place_cols(src, buf, col):
    rows, w = src.shape
    tr = _pick(rows, (1024, 512, 256, 128))
    return pl.pallas_call(
        _place_cols_body,
        grid=(rows // tr,),
        in_specs=[pl.BlockSpec((tr, w), lambda i: (i, 0)), ANY_SPEC],
        out_specs=pl.BlockSpec((tr, w), lambda i: (i, col)),
        out_shape=jax.ShapeDtypeStruct(buf.shape, buf.dtype),
        input_output_aliases={1: 0},
        compiler_params=_cparams("parallel"),
        name="place_cols",
    )(src, buf)


def _assemble_body(a_ref, b_ref, c_ref, d_ref, buf_ref, o_ref):
    del buf_ref
    o_ref[...] = jnp.concatenate([a_ref[...], b_ref[...], c_ref[...], d_ref[...]], axis=1)


def _assemble_rows(parts, buf, row0):
    bs = parts[0].shape[0]
    part = pl.BlockSpec((bs, G_WIDTH), lambda i: (0, 0))
    return pl.pallas_call(
        _assemble_body,
        grid=(1,),
        in_specs=[part, part, part, part, ANY_SPEC],
        out_specs=pl.BlockSpec((bs, D_MODEL), lambda i: (row0 // bs, 0)),
        out_shape=jax.ShapeDtypeStruct(buf.shape, buf.dtype),
        input_output_aliases={4: 0},
        compiler_params=_cparams("arbitrary"),
        name="assemble_sample_rows",
    )(*parts, buf)


def kernel(x_prompt, x_sample, mem_prompt, state_conv_a, state_ret, state_shift, state_wkv, state_conv_d, cache_mem_k, cache_mem_v, g_mix, w_in, conv_a_w, mu_c, w0, w2, a0, a2, g2, k_k, k_a, r_k, lnx_g, lnx_b, conv_d_w, conv_d_b, ln_d_g, ln_d_b, w_out, g_xa, g_mem, wq_x, wk_x, wv_x, wo_x, g_mlp, w_up, w_down, g_final):
    bp, seq, d = x_prompt.shape
    bs = x_sample.shape[0]
    depth = w_in.shape[0]
    np_rows = bp * seq
    G = G_WIDTH
    x = jnp.concatenate([x_prompt.reshape(np_rows, d), x_sample.reshape(bs, d)], axis=0)
    mem = mem_prompt.reshape(bp * N_MEM, d)
    cos_p, sin_p = _rope_tables(jnp.arange(seq))
    cos_s, sin_s = _rope_tables(PAST_LEN + jnp.arange(1))

    w_down_b = w_down.astype(BF16)
    z64 = jnp.zeros((depth, 64, G), F32)
    w2a = jnp.concatenate([jnp.concatenate([w2, z64], axis=2), jnp.concatenate([z64, a2], axis=2)], axis=1).astype(BF16)
    g2_b = g2.astype(BF16)
    wkv_flat = state_wkv.reshape(depth, bs, -1)
    vec3 = lambda a: a.reshape(depth, 1, -1)
    rw_arrs = (vec3(mu_c), w2a, g2_b, vec3(w0), vec3(a0), vec3(k_k), vec3(k_a), vec3(r_k), vec3(lnx_g), vec3(lnx_b))

    small = {k: [] for k in ("a_p", "ret_p", "sh_p", "sh_s", "wkv_p", "d_p", "mk", "mv")}
    na_s = nd_s = nret_s = nwkv_s = None
    for l in range(depth):
        h = _rmsnorm(x, _layer_vec(g_mix, l), BF16)
        proj = _matmul_ws(h, w_in, l, tn=1280)
        ymix, na_p = _conv_a_prompt(proj, conv_a_w, l, bp, seq)
        ymix, nd_p = _conv_d_prompt(proj, conv_d_w, conv_d_b, ln_d_g, ln_d_b, ymix, l, bp, seq)
        ymix, nret_p = _retention_prompt(proj, cos_p, sin_p, ymix, bp, seq)
        rw = [(rw_arrs[0], pl.BlockSpec((None, 1, SHIFT_W), lambda *_: (l, 0, 0))),
              (w2a, _layer_mat(w2a, l)), (g2_b, _layer_mat(g2_b, l))] + [
              (a, pl.BlockSpec((None, 1, G), lambda *_: (l, 0, 0))) for a in rw_arrs[3:]]
        yc_p, nsh_p, nwkv_p = _rwkv_prompt(proj, rw, bp, seq)
        ymix = _place_cols(yc_p.reshape(np_rows, G), ymix, 2)
        ya_s, yd_s, na_s, nd_s = _sample_ad(proj, state_conv_a, state_conv_d, conv_a_w, conv_d_w, conv_d_b, ln_d_g, ln_d_b,
                                            l, np_rows, na_s, nd_s)
        yb_s, nret_s = _retention_sample(proj, cos_s, sin_s, state_ret, l, np_rows, nret_s)
        yc_s, nwkv_s = _rwkv_sample(proj, state_shift, rw_arrs, wkv_flat, l, np_rows, nwkv_s)
        ymix = _assemble_rows([ya_s, yb_s, yc_s, yd_s], ymix, np_rows)
        x = _matmul_ws(ymix, w_out, l, residual=x)

        mn = _rmsnorm(mem, _layer_vec(g_mem, l), BF16)
        mk = _matmul_ws(mn, wk_x, l)
        mv = _matmul_ws(mn, wv_x, l)
        hq = _rmsnorm(x, _layer_vec(g_xa, l), BF16)
        q = _matmul_ws(hq, wq_x, l)
        obuf = _xattn_prompt(q, mk, mv, bp, seq)
        q_s = q[np_rows:].reshape(bs, XA_HEADS, XA_HD).transpose(1, 0, 2)
        obuf = _xattn_sample(q_s, cache_mem_k, cache_mem_v, obuf, l, np_rows)
        x = _matmul_ws(obuf, wo_x, l, residual=x)

        hm = _rmsnorm(x, _layer_vec(g_mlp, l), BF16)
        up = _matmul_ws(hm, w_up, l, act="relu2", out_dtype=BF16)
        x = _matmul(up, w_down_b, l, residual=x)

        small["a_p"].append(na_p)
        small["ret_p"].append(nret_p)
        small["sh_p"].append(nsh_p.reshape(bp, SHIFT_W))
        small["sh_s"].append(proj[np_rows:, COL_C:COL_C + SHIFT_W])
        small["wkv_p"].append(nwkv_p)
        small["d_p"].append(nd_p)
        small["mk"].append(mk.reshape(bp, N_MEM, XA_HEADS, XA_HD))
        small["mv"].append(mv.reshape(bp, N_MEM, XA_HEADS, XA_HD))
    gf = (g_final.reshape(1, d), pl.BlockSpec((1, d), lambda *_: (0, 0)))
    y_p = _rmsnorm(x, gf, F32, 0, np_rows)
    y_s = _rmsnorm(x, gf, F32, np_rows, bs)
    st = lambda k: jnp.stack(small[k])
    return (y_p.reshape(bp, seq, d), y_s.reshape(bs, 1, d), st("a_p"), na_s, st("ret_p"), nret_s,
            st("sh_p"), st("sh_s"), st("wkv_p"), nwkv_s.reshape(depth, bs, RWKV_HEADS, RWKV_HEAD, RWKV_HEAD),
            st("d_p"), nd_s, st("mk"), st("mv"))
```

```python
import functools
import math

import jax
import jax.numpy as jnp
from jax import lax
from jax.experimental import pallas as pl
from jax.experimental.pallas import tpu as pltpu

F32 = jnp.float32
BF16 = jnp.bfloat16
HI = lax.Precision.HIGHEST

D_MODEL = 2048
G_WIDTH = 512
P_IN = 6400
RET_HEADS = 4
RET_D = 128
RET_CHUNK = 128
ROPE_BASE = 10000.0
RWKV_HEAD = 64
RWKV_HEADS = 8
WKV_CHUNK = 64
RWKV_GN_EPS = 64e-5
A_CONV = 3
D_CONV = 31
D_HIST = 32
N_MEM = 256
XA_HEADS = 4
XA_HD = 512
SHIFT_W = 1792
PAST_LEN = 16384
EPS = 1e-6
RET_LOG_DECAY = tuple(math.log1p(-(2.0 ** (-5.0 - h))) for h in range(RET_HEADS))
VMEM_LIMIT = 56 * 1024 * 1024
COL_A = 0
COL_B = 3 * G_WIDTH
COL_C = 7 * G_WIDTH
COL_D = 7 * G_WIDTH + SHIFT_W

ANY_SPEC = pl.BlockSpec(memory_space=pl.ANY)


def _cparams(*sem):
    return pltpu.CompilerParams(dimension_semantics=sem, vmem_limit_bytes=VMEM_LIMIT)


def _pick(n, cands):
    for c in cands:
        if n % c == 0:
            return c
    raise ValueError(f"no tile for {n}")


def _sigmoid(x):
    return 1.0 / (1.0 + jnp.exp(-x))


def _dot(a, b, precision=None):
    return jnp.dot(a, b, preferred_element_type=F32, precision=precision)


def _dot_nt(a, b, precision=None):
    return lax.dot_general(a, b, (((1,), (1,)), ((), ())), preferred_element_type=F32, precision=precision)


def _dot_tn(a, b, precision=None):
    return lax.dot_general(a, b, (((0,), (0,)), ((), ())), preferred_element_type=F32, precision=precision)


def _layernorm(x, eps):
    mu = jnp.mean(x, axis=-1, keepdims=True)
    xc = x - mu
    var = jnp.mean(xc * xc, axis=-1, keepdims=True)
    return xc * lax.rsqrt(var + eps)


def _layer_vec(a, l):
    n = a.shape[-1]
    return a.reshape(a.shape[0], 1, n), pl.BlockSpec((None, 1, n), lambda *_: (l, 0, 0))


def _layer_mat(a, l):
    return pl.BlockSpec((None,) + a.shape[1:], lambda *_: (l, 0, 0))


def _rmsnorm_body(x_ref, g_ref, o_ref):
    x = x_ref[...]
    ms = jnp.mean(x * x, axis=-1, keepdims=True)
    o_ref[...] = ((x * lax.rsqrt(ms + EPS)) * g_ref[...]).astype(o_ref.dtype)


def _rmsnorm(x, g_spec, out_dtype, row0=0, nrows=None):
    m, d = x.shape
    nrows = nrows or m
    tm = _pick(math.gcd(nrows, row0) if row0 else nrows, (1040, 1024, 640, 512, 256, 128))
    r0 = row0 // tm
    g, gspec = g_spec
    return pl.pallas_call(
        _rmsnorm_body,
        grid=(nrows // tm,),
        in_specs=[pl.BlockSpec((tm, d), lambda i: (r0 + i, 0)), gspec],
        out_specs=pl.BlockSpec((tm, d), lambda i: (i, 0)),
        out_shape=jax.ShapeDtypeStruct((nrows, d), out_dtype),
        compiler_params=_cparams("parallel"),
        name="rmsnorm",
    )(x, g)


def _mm_body(*refs, act, has_res, nk):
    a_ref, w_ref = refs[0], refs[1]
    res_ref = refs[2] if has_res else None
    o_ref = refs[3] if has_res else refs[2]

    def finish(r):
        if act == "relu2":
            r = jnp.square(jnp.maximum(r, 0.0))
        if has_res:
            r = res_ref[...] + r
        o_ref[...] = r.astype(o_ref.dtype)

    if nk == 1:
        finish(_dot(a_ref[...], w_ref[...]))
        return
    acc_ref = refs[-1]
    k = pl.program_id(2)

    @pl.when(k == 0)
    def _():
        acc_ref[...] = jnp.zeros_like(acc_ref)

    acc_ref[...] += _dot(a_ref[...], w_ref[...])

    @pl.when(k == nk - 1)
    def _():
        finish(acc_ref[...])


def _matmul(a, w, l, *, residual=None, act=None, out_dtype=F32, tn=None):
    m, kdim = a.shape
    n = w.shape[2]
    tm = _pick(m, (1040, 1024, 640, 512, 256, 128))
    tn = tn or _pick(n, (1024, 896, 768, 512, 256))
    tk = _pick(kdim, (2048, 1024, 512))
    nk = kdim // tk
    has_res = residual is not None
    in_specs = [pl.BlockSpec((tm, tk), lambda i, j, k: (i, k)), pl.BlockSpec((None, tk, tn), lambda i, j, k: (l, k, j))]
    args = [a, w]
    if has_res:
        in_specs.append(pl.BlockSpec((tm, tn), lambda i, j, k: (i, j)))
        args.append(residual)
    return pl.pallas_call(
        functools.partial(_mm_body, act=act, has_res=has_res, nk=nk),
        grid=(m // tm, n // tn, nk),
        in_specs=in_specs,
        out_specs=pl.BlockSpec((tm, tn), lambda i, j, k: (i, j)),
        out_shape=jax.ShapeDtypeStruct((m, n), out_dtype),
        scratch_shapes=[pltpu.VMEM((tm, tn), F32)] if nk > 1 else [],
        compiler_params=_cparams("parallel", "parallel", "arbitrary"),
        name="matmul",
    )(*args)


def _mm_ws_body(*refs, act, has_res, has_norm):
    a_ref, w_ref = refs[0], refs[1]
    g_ref = refs[2] if has_norm else None
    res_ref = refs[2 + has_norm] if has_res else None
    o_ref = refs[2 + has_norm + has_res]
    wb_ref = refs[-1]

    @pl.when(pl.program_id(1) == 0)
    def _():
        wb_ref[...] = w_ref[...].astype(BF16)

    a = a_ref[...]
    if has_norm:
        ms = jnp.mean(a * a, axis=-1, keepdims=True)
        a = ((a * lax.rsqrt(ms + EPS)) * g_ref[...]).astype(BF16)
    r = _dot(a, wb_ref[...])
    if act == "relu2":
        r = jnp.square(jnp.maximum(r, 0.0))
    if has_res:
        r = res_ref[...] + r
    o_ref[...] = r.astype(o_ref.dtype)


def _matmul_ws(a, w, l, *, norm_g=None, residual=None, act=None, out_dtype=F32, tn=1024, tm=None):
    m, kdim = a.shape
    n = w.shape[2]
    tm = tm or _pick(m, (1040, 1024, 640, 512, 256, 128))
    has_res = residual is not None
    has_norm = norm_g is not None
    in_specs = [pl.BlockSpec((tm, kdim), lambda j, i: (i, 0)), pl.BlockSpec((None, kdim, tn), lambda j, i: (l, 0, j))]
    args = [a, w]
    if has_norm:
        args.append(norm_g[0])
        in_specs.append(norm_g[1])
    if has_res:
        in_specs.append(pl.BlockSpec((tm, tn), lambda j, i: (i, j)))
        args.append(residual)
    return pl.pallas_call(
        functools.partial(_mm_ws_body, act=act, has_res=has_res, has_norm=has_norm),
        grid=(n // tn, m // tm),
        in_specs=in_specs,
        out_specs=pl.BlockSpec((tm, tn), lambda j, i: (i, j)),
        out_shape=jax.ShapeDtypeStruct((m, n), out_dtype),
        scratch_shapes=[pltpu.VMEM((kdim, tn), BF16)],
        compiler_params=_cparams("parallel", "arbitrary"),
        name="matmul_ws",
    )(*args)


def _conv_a_body(b_ref, c_ref, h_ref, w_ref, y_ref, st_ref, ext_ref, *, tl):
    t = pl.program_id(1)

    @pl.when(t == 0)
    def _():
        ext_ref[0:8, :] = jnp.zeros((8, G_WIDTH), F32)

    u = c_ref[...] * h_ref[...]
    ext_ref[8:, :] = u
    u1 = ext_ref[7:7 + tl, :]
    u2 = ext_ref[6:6 + tl, :]
    y = b_ref[...] * (w_ref[0:1, :] * u2 + w_ref[1:2, :] * u1 + w_ref[2:3, :] * u)
    y_ref[...] = y.astype(y_ref.dtype)
    st_ref[0] = ext_ref[tl + 6:tl + 8, :]
    ext_ref[0:8, :] = ext_ref[tl:tl + 8, :]


def _conv_a_prompt(proj, conv_w, l, bp, seq):
    tl = _pick(seq, (512, 256, 128))
    nt = seq // tl
    c0 = COL_A // G_WIDTH
    col = lambda j: pl.BlockSpec((tl, G_WIDTH), lambda b, t: (b * nt + t, c0 + j))
    return pl.pallas_call(
        functools.partial(_conv_a_body, tl=tl),
        grid=(bp, nt),
        in_specs=[col(0), col(1), col(2), _layer_mat(conv_w, l)],
        out_specs=[pl.BlockSpec((tl, G_WIDTH), lambda b, t: (b * nt + t, 0)),
                   pl.BlockSpec((1, A_CONV - 1, G_WIDTH), lambda b, t: (b, 0, 0))],
        out_shape=[jax.ShapeDtypeStruct((proj.shape[0], D_MODEL), BF16),
                   jax.ShapeDtypeStruct((bp, A_CONV - 1, G_WIDTH), F32)],
        scratch_shapes=[pltpu.VMEM((tl + 8, G_WIDTH), F32)],
        compiler_params=_cparams("parallel", "arbitrary"),
        name="conv_a_prompt",
    )(proj, proj, proj, conv_w)


CONV_D_ROWS = 32


def _conv_d_body(d1a_ref, d1b_ref, d2a_ref, d2b_ref, w_ref, bias_ref, g_ref, beta_ref, ymix_ref, y_ref, st_ref, ext_ref,
                 acc_ref, *, tl):
    del ymix_ref
    t = pl.program_id(1)
    next_ = D_HIST + tl

    @pl.when(t == 0)
    def _():
        ext_ref[0, 0:D_HIST, :] = jnp.zeros((D_HIST, G_WIDTH), F32)

    d1 = jnp.concatenate([d1a_ref[...], d1b_ref[...]], axis=1)
    d2 = jnp.concatenate([d2a_ref[...], d2b_ref[...]], axis=1)
    ext_ref[0, D_HIST:, :] = d1 * _sigmoid(d2)
    ext = ext_ref[0]
    for s in range(1, 8):
        ext_ref[s] = pltpu.roll(ext, next_ - s, 0)
    first = D_HIST - (D_CONV - 1)

    def blk(i, carry):
        base = pl.multiple_of(i * CONV_D_ROWS, CONV_D_ROWS)
        acc = jnp.broadcast_to(bias_ref[...], (CONV_D_ROWS, G_WIDTH))
        for k in range(D_CONV):
            s = (first + k) % 8
            a0 = first + k - s
            acc = acc + ext_ref[s, pl.ds(base + a0, CONV_D_ROWS), :] * w_ref[k:k + 1, :]
        acc_ref[pl.ds(base, CONV_D_ROWS), :] = acc
        return carry

    lax.fori_loop(0, tl // CONV_D_ROWS, blk, 0)
    c = _layernorm(acc_ref[...], EPS) * g_ref[...] + beta_ref[...]
    y_ref[...] = (c * _sigmoid(c)).astype(y_ref.dtype)
    st_ref[0] = ext_ref[0, tl + first:tl + D_HIST, :]
    ext_ref[0, 0:D_HIST, :] = ext_ref[0, tl:tl + D_HIST, :]


def _conv_d_prompt(proj, w, bias, g, beta, ymix, l, bp, seq):
    tl = _pick(seq, (512, 256, 128))
    nt = seq // tl
    half = G_WIDTH // 2
    c0 = COL_D // half
    col = lambda j: pl.BlockSpec((tl, half), lambda b, t: (b * nt + t, c0 + j))
    (bias, bias_s), (g, g_s), (beta, beta_s) = _layer_vec(bias, l), _layer_vec(g, l), _layer_vec(beta, l)
    return pl.pallas_call(
        functools.partial(_conv_d_body, tl=tl),
        grid=(bp, nt),
        in_specs=[col(0), col(1), col(2), col(3), _layer_mat(w, l), bias_s, g_s, beta_s, ANY_SPEC],
        out_specs=[pl.BlockSpec((tl, G_WIDTH), lambda b, t: (b * nt + t, 3)),
                   pl.BlockSpec((1, D_CONV - 1, G_WIDTH), lambda b, t: (b, 0, 0))],
        out_shape=[jax.ShapeDtypeStruct(ymix.shape, ymix.dtype),
                   jax.ShapeDtypeStruct((bp, D_CONV - 1, G_WIDTH), F32)],
        scratch_shapes=[pltpu.VMEM((8, tl + D_HIST, G_WIDTH), F32), pltpu.VMEM((tl, G_WIDTH), F32)],
        input_output_aliases={8: 0},
        compiler_params=_cparams("parallel", "arbitrary"),
        name="conv_d_prompt",
    )(proj, proj, proj, proj, w, bias, g, beta, ymix)


def _rope_tables(pos):
    inv = ROPE_BASE ** (-jnp.arange(0, RET_D, 2, dtype=F32) / RET_D)
    ang = pos.astype(F32)[:, None] * inv[None, :]
    cos = jnp.cos(ang)
    sin = jnp.sin(ang)
    return jnp.concatenate([cos, cos], axis=-1), jnp.concatenate([-sin, sin], axis=-1)


def _rope(x, cos2, sin2):
    return x * cos2 + pltpu.roll(x, RET_D // 2, 1) * sin2


def _ret_body(q_ref, k_ref, v_ref, g_ref, cos_ref, sin_ref, ymix_ref, y_ref, st_ref, s_ref, *, nchunks):
    del ymix_ref
    c = pl.program_id(1)
    C = RET_CHUNK

    @pl.when(c == 0)
    def _():
        s_ref[...] = jnp.zeros_like(s_ref)

    ii = lax.broadcasted_iota(jnp.int32, (C, C), 0).astype(F32)
    jj = lax.broadcasted_iota(jnp.int32, (C, C), 1).astype(F32)
    diff = ii - jj
    cos2 = cos_ref[...]
    sin2 = sin_ref[...]
    H = range(RET_HEADS)
    sl = [slice(h * RET_D, (h + 1) * RET_D) for h in H]
    lg = RET_LOG_DECAY
    q = [_rope(q_ref[:, sl[h]], cos2, sin2) for h in H]
    k = [_rope(k_ref[:, sl[h]], cos2, sin2) * (RET_D ** -0.5) for h in H]
    qb = [q[h].astype(BF16) for h in H]
    vb = [v_ref[:, sl[h]].astype(BF16) for h in H]
    s_old = [s_ref[h] for h in H]
    scores = [_dot_nt(qb[h], k[h].astype(BF16)) for h in H]
    cross = [_dot(qb[h], s_old[h].astype(BF16)) for h in H]
    kv = [_dot_tn((k[h] * jnp.exp(lg[h] * (C - 1.0 - ii))).astype(BF16), vb[h]) for h in H]
    dmat = [jnp.where(diff >= 0.0, jnp.exp(lg[h] * jnp.maximum(diff, 0.0)), 0.0) for h in H]
    inner = [_dot((scores[h] * dmat[h]).astype(BF16), vb[h]) for h in H]
    for h in H:
        s_ref[h] = math.exp(lg[h] * C) * s_old[h] + kv[h]
        o = _layernorm(inner[h] + cross[h] * jnp.exp(lg[h] * (ii + 1.0)), EPS)
        g = g_ref[:, sl[h]]
        y_ref[:, sl[h]] = (g * _sigmoid(g) * o).astype(y_ref.dtype)

    @pl.when(c == nchunks - 1)
    def _():
        st_ref[0] = s_ref[...]


def _retention_prompt(proj, cos2, sin2, ymix, bp, seq):
    C = RET_CHUNK
    nc = seq // C
    c0 = COL_B // G_WIDTH
    col = lambda j: pl.BlockSpec((C, G_WIDTH), lambda b, c: (b * nc + c, c0 + j))
    tab = pl.BlockSpec((C, RET_D), lambda b, c: (c, 0))
    return pl.pallas_call(
        functools.partial(_ret_body, nchunks=nc),
        grid=(bp, nc),
        in_specs=[col(0), col(1), col(2), col(3), tab, tab, ANY_SPEC],
        out_specs=[pl.BlockSpec((C, G_WIDTH), lambda b, c: (b * nc + c, 1)),
                   pl.BlockSpec((1, RET_HEADS, RET_D, RET_D), lambda b, c: (b, 0, 0, 0))],
        out_shape=[jax.ShapeDtypeStruct(ymix.shape, ymix.dtype),
                   jax.ShapeDtypeStruct((bp, RET_HEADS, RET_D, RET_D), F32)],
        scratch_shapes=[pltpu.VMEM((RET_HEADS, RET_D, RET_D), F32)],
        input_output_aliases={6: 0},
        compiler_params=_cparams("parallel", "arbitrary"),
        name="retention_prompt",
    )(proj, proj, proj, proj, cos2, sin2, ymix)


def _softplus(z):
    return jnp.maximum(z, 0.0) + jnp.log(1.0 + jnp.exp(-jnp.abs(z)))


def _rwkv_pre(r, kc, vc, lora_x, w2_w, w2_a, g2, w0, a0, k_k, k_a):
    lin = lora_x[:, 0:128]
    lane = lax.broadcasted_iota(jnp.int32, lin.shape, 1)
    lin = jnp.where(lane < 64, jnp.tanh(lin), lin).astype(BF16)
    gate = _dot(_sigmoid(lora_x[:, 128:256]).astype(BF16), g2)
    w_pre = -_softplus(-(w0 + _dot(lin, w2_w))) - 0.5
    log_decay = -jnp.exp(w_pre)
    a = _sigmoid(a0 + _dot(lin, w2_a))
    kk = kc * k_k
    k4 = kc * (1.0 + (a - 1.0) * k_a)
    return r, log_decay, k4, vc, kk, a, gate


def _l2_normalize(kk):
    ss = jnp.sum(kk * kk, axis=-1, keepdims=True)
    return kk * jnp.minimum(lax.rsqrt(ss), 1e12)


def _bdot(a, b):
    return _dot(a.astype(BF16), b.astype(BF16))


def _wkv_chunk(r, lw, cum, k, v, kk, a, z, mask2, eye):
    T = WKV_CHUNK
    H = range(len(r))
    g_inc = [jnp.exp(cum[h]) for h in H]
    g_tot = [g_inc[h][T - 1:T, :] for h in H]
    ah = [-(kk[h] * jnp.exp(cum[h] - lw[h])) for h in H]
    rg = [r[h] * g_inc[h] for h in H]
    g_inv = [jnp.exp(-cum[h]) for h in H]
    bd = [kk[h] * a[h] * g_inv[h] for h in H]
    kd = [k[h] * g_inv[h] for h in H]
    ar = [jnp.concatenate([ah[h], rg[h]], axis=0).astype(BF16) for h in H]
    xb = [jnp.where(mask2, _dot_nt(ar[h], bd[h].astype(BF16)), 0.0) for h in H]
    xk = [jnp.where(mask2, _dot_nt(ar[h], kd[h].astype(BF16)), 0.0).astype(BF16) for h in H]
    vb = [v[h].astype(BF16) for h in H]
    xkv = [_dot(xk[h], vb[h]) for h in H]
    acur = [xb[h][0:T] for h in H]
    p = [eye for _ in H]
    for it in range(5):
        x = [_bdot(jnp.concatenate([acur[h], p[h]], axis=0), acur[h]) for h in H]
        acur = [x[h][0:T] for h in H]
        p = [p[h] + x[h][T:2 * T] for h in H]
    x = [_bdot(p[h], acur[h]) for h in H]
    p = [p[h] + x[h] for h in H]
    wub = [_bdot(p[h], jnp.concatenate([ah[h], xkv[h][0:T]], axis=1)).astype(BF16) for h in H]
    qo = [_dot(xb[h][T:2 * T].astype(BF16), wub[h]) for h in H]
    lhs_t = [jnp.concatenate([bd[h] * g_tot[h], kd[h] * g_tot[h]], axis=0).astype(BF16) for h in H]
    zero = jnp.zeros((T, 64), BF16)
    mn = [_dot_tn(lhs_t[h], jnp.concatenate([wub[h], jnp.concatenate([zero, vb[h]], axis=1)], axis=0)) for h in H]
    qm = [jnp.concatenate([rg[h] + qo[h][:, 0:64], mn[h][:, 0:64] + jnp.where(eye > 0.0, g_tot[h], 0.0)], axis=0) for h in H]
    oz = [_bdot(qm[h], z[h]) for h in H]
    o = [oz[h][0:T] + qo[h][:, 64:128] + xkv[h][T:2 * T] for h in H]
    z_new = [oz[h][T:2 * T] + mn[h][:, 64:128] for h in H]
    return o, z_new


def _rwkv_post(o, r, k4, v, gate, rk, lnx_g, lnx_b):
    gn = _layernorm(o, RWKV_GN_EPS) * lnx_g + lnx_b
    bonus = jnp.sum(r * k4 * rk, axis=-1, keepdims=True) * v
    return (gn + bonus) * gate


def _rwkv_prompt_body(*refs, nb, nchunks):
    pc_refs = refs[:nb]
    mu_ref, w2a_ref, g2_ref, w0_ref, a0_ref, kk_ref, ka_ref, rk_ref, lg_ref, lb_ref = refs[nb:nb + 10]
    y_ref, shift_ref, st_ref, carry_ref, z_ref = refs[nb + 10:]
    c = pl.program_id(0)
    T = WKV_CHUNK
    NH = RWKV_HEADS

    @pl.when(c == 0)
    def _():
        carry_ref[...] = jnp.zeros_like(carry_ref)
        z_ref[...] = jnp.zeros_like(z_ref)

    pcs, prevs = [], []
    for b in range(nb):
        pc = pc_refs[b][...]
        row = lax.broadcasted_iota(jnp.int32, pc.shape, 0)
        prevs.append(jnp.where(row == 0, carry_ref[b, 0:1, :], pltpu.roll(pc, 1, 0)))
        carry_ref[b, 0:1, :] = pc[T - 1:T, :]
        shift_ref[b] = pc[T - 1:T, :]
        pcs.append(pc)
    pc = jnp.concatenate(pcs, axis=0)
    xs = pc + (jnp.concatenate(prevs, axis=0) - pc) * mu_ref[...]
    G = G_WIDTH
    r, lw, k4, v, kk, a, gate = _rwkv_pre(xs[:, 0:G], xs[:, G:2 * G], xs[:, 2 * G:3 * G], xs[:, 3 * G:3 * G + 256],
                                          w2a_ref[:, 0:G], w2a_ref[:, G:2 * G], g2_ref[...], w0_ref[...], a0_ref[...],
                                          kk_ref[...], ka_ref[...])
    ti = lax.broadcasted_iota(jnp.int32, (T, T), 0)
    si = lax.broadcasted_iota(jnp.int32, (T, T), 1)
    tril = (ti >= si).astype(F32)
    eye = (ti == si).astype(F32)
    cum = jnp.concatenate([_dot(tril, lw[b * T:(b + 1) * T, :], HI) for b in range(nb)], axis=0)
    t2 = lax.broadcasted_iota(jnp.int32, (2 * T, T), 0)
    s2 = lax.broadcasted_iota(jnp.int32, (2 * T, T), 1)
    mask2 = jnp.where(t2 < T, t2 - 1, t2 - T) >= s2
    heads = lambda x: [x[b * T:(b + 1) * T, h * RWKV_HEAD:(h + 1) * RWKV_HEAD] for b in range(nb) for h in range(NH)]
    r_h, k_h, v_h = heads(r), heads(k4), heads(v)
    o, z_new = _wkv_chunk(r_h, heads(lw), heads(cum), k_h, v_h, [_l2_normalize(x) for x in heads(kk)], heads(a),
                          [z_ref[i] for i in range(nb * NH)], mask2, eye)
    for i in range(nb * NH):
        z_ref[i] = z_new[i]
    gate_h = heads(gate)
    vec_heads = lambda ref: [ref[:, h * RWKV_HEAD:(h + 1) * RWKV_HEAD] for h in range(NH)]
    rk_h, lg_h, lb_h = vec_heads(rk_ref), vec_heads(lg_ref), vec_heads(lb_ref)
    for b in range(nb):
        ys = [_rwkv_post(o[b * NH + h], r_h[b * NH + h], k_h[b * NH + h], v_h[b * NH + h], gate_h[b * NH + h],
                         rk_h[h], lg_h[h], lb_h[h]) for h in range(NH)]
        y_ref[b] = jnp.concatenate(ys, axis=1).astype(y_ref.dtype)

    @pl.when(c == nchunks - 1)
    def _():
        for b in range(nb):
            for h in range(NH):
                st_ref[b, h] = z_ref[b * NH + h].T


def _rwkv_prompt(proj, rw, bp, seq):
    T = WKV_CHUNK
    nc = seq // T
    G = G_WIDTH
    arrs, specs = zip(*rw)
    pc_specs = [pl.BlockSpec((T, SHIFT_W), lambda c, b=b: (b * nc + c, COL_C // SHIFT_W)) for b in range(bp)]
    return pl.pallas_call(
        functools.partial(_rwkv_prompt_body, nb=bp, nchunks=nc),
        grid=(nc,),
        in_specs=pc_specs + list(specs),
        out_specs=[pl.BlockSpec((bp, T, G), lambda c: (0, c, 0)),
                   pl.BlockSpec((bp, 1, SHIFT_W), lambda c: (0, 0, 0)),
                   pl.BlockSpec((bp, RWKV_HEADS, RWKV_HEAD, RWKV_HEAD), lambda c: (0, 0, 0, 0))],
        out_shape=[jax.ShapeDtypeStruct((bp, seq, G), BF16),
                   jax.ShapeDtypeStruct((bp, 1, SHIFT_W), F32),
                   jax.ShapeDtypeStruct((bp, RWKV_HEADS, RWKV_HEAD, RWKV_HEAD), F32)],
        scratch_shapes=[pltpu.VMEM((bp, 8, SHIFT_W), F32), pltpu.VMEM((bp * RWKV_HEADS, RWKV_HEAD, RWKV_HEAD), F32)],
        compiler_params=_cparams("arbitrary"),
        name="rwkv_prompt",
    )(*([proj] * bp), *arrs)


def _xattn_prompt_body(q_ref, k_ref, v_ref, o_ref, kb_ref, vb_ref):
    @pl.when(pl.program_id(1) == 0)
    def _():
        kb_ref[...] = k_ref[...].astype(BF16)
        vb_ref[...] = v_ref[...].astype(BF16)

    for h in range(XA_HEADS):
        sl = slice(h * XA_HD, (h + 1) * XA_HD)
        s = _dot_nt(q_ref[:, sl].astype(BF16), kb_ref[:, sl]) * (XA_HD ** -0.5)
        e = jnp.exp(s - jnp.max(s, axis=-1, keepdims=True))
        p = e / jnp.sum(e, axis=-1, keepdims=True)
        o_ref[:, sl] = _dot(p.astype(BF16), vb_ref[:, sl]).astype(o_ref.dtype)


def _xattn_prompt(q, mk, mv, bp, seq):
    tq = _pick(seq, (512, 256, 128))
    nt = seq // tq
    kv = pl.BlockSpec((N_MEM, D_MODEL), lambda b, t: (b, 0))
    return pl.pallas_call(
        _xattn_prompt_body,
        grid=(bp, nt),
        in_specs=[pl.BlockSpec((tq, D_MODEL), lambda b, t: (b * nt + t, 0)), kv, kv],
        out_specs=pl.BlockSpec((tq, D_MODEL), lambda b, t: (b * nt + t, 0)),
        out_shape=jax.ShapeDtypeStruct((q.shape[0], D_MODEL), BF16),
        scratch_shapes=[pltpu.VMEM((N_MEM, D_MODEL), BF16), pltpu.VMEM((N_MEM, D_MODEL), BF16)],
        compiler_params=_cparams("parallel", "arbitrary"),
        name="xattn_prompt",
    )(q, mk, mv)


XA_SAMPLES_PER_STEP = 2


def _xattn_sample_body(q_ref, k_ref, v_ref, obuf_ref, o_ref, acc_ref, *, nsteps):
    del obuf_ref
    i = pl.program_id(0)
    H = XA_HEADS
    groups = N_MEM * H // (2 * H)
    for j in range(XA_SAMPLES_PER_STEP):
        b = i * XA_SAMPLES_PER_STEP + j
        k3 = k_ref[j].reshape(N_MEM * H, XA_HD).reshape(groups, 2 * H, XA_HD)
        s = jnp.sum(k3 * q_ref[b], axis=-1, keepdims=True) * (XA_HD ** -0.5)
        mx = jnp.max(s, axis=0, keepdims=True)
        mx = jnp.maximum(mx, pltpu.roll(mx, H, 1))
        e = jnp.exp(s - mx)
        den = jnp.sum(e, axis=0, keepdims=True)
        den = den + pltpu.roll(den, H, 1)
        v3 = v_ref[j].reshape(N_MEM * H, XA_HD).reshape(groups, 2 * H, XA_HD)
        o8 = jnp.sum((e / den) * v3, axis=0)
        o = o8[0:H] + o8[H:2 * H]
        acc_ref[pl.ds(b, 1), :] = jnp.concatenate([o[h:h + 1, :] for h in range(H)], axis=1)

    @pl.when(i == nsteps - 1)
    def _():
        o_ref[...] = acc_ref[...].astype(o_ref.dtype)


def _xattn_sample(q, ck, cv, obuf, l, row0):
    bs = q.shape[0]
    nb = XA_SAMPLES_PER_STEP
    kv = pl.BlockSpec((None, nb, N_MEM, XA_HEADS, XA_HD), lambda i: (l, i, 0, 0, 0))
    return pl.pallas_call(
        functools.partial(_xattn_sample_body, nsteps=bs // nb),
        grid=(bs // nb,),
        in_specs=[pl.BlockSpec((bs, 2 * XA_HEADS, XA_HD), lambda i: (0, 0, 0)), kv, kv, ANY_SPEC],
        out_specs=pl.BlockSpec((bs, D_MODEL), lambda i: (row0 // bs, 0)),
        out_shape=jax.ShapeDtypeStruct(obuf.shape, obuf.dtype),
        scratch_shapes=[pltpu.VMEM((bs, D_MODEL), F32)],
        input_output_aliases={3: 0},
        compiler_params=_cparams("arbitrary"),
        name="xattn_sample",
    )(q, ck, cv, obuf)


def _alias_args(prev):
    return ([], []) if prev is None else ([prev], [ANY_SPEC])


def _sample_ad_body(*refs, has_prev):
    (ab_ref, ac_ref, ah_ref, sta_ref, wa_ref, d1a_ref, d1b_ref, d2a_ref, d2b_ref, std_ref, wd_ref, bias_ref, g_ref,
     beta_ref) = refs[:14]
    ya_ref, yd_ref, na_ref, nd_ref = refs[14 + 2 * has_prev:]
    u = ac_ref[...] * ah_ref[...]
    s0 = sta_ref[:, 0, :]
    s1 = sta_ref[:, 1, :]
    ya = ab_ref[...] * (wa_ref[0:1, :] * s0 + wa_ref[1:2, :] * s1 + wa_ref[2:3, :] * u)
    ya_ref[...] = ya.astype(ya_ref.dtype)
    na_ref[:, 0, :] = s1
    na_ref[:, 1, :] = u
    d1 = jnp.concatenate([d1a_ref[...], d1b_ref[...]], axis=1)
    d2 = jnp.concatenate([d2a_ref[...], d2b_ref[...]], axis=1)
    ud = d1 * _sigmoid(d2)
    acc = jnp.broadcast_to(bias_ref[...], ud.shape)
    nh = D_CONV - 1
    for k in range(nh):
        s_k = std_ref[:, k, :]
        acc = acc + s_k * wd_ref[k:k + 1, :]
        if k > 0:
            nd_ref[:, k - 1, :] = s_k
    acc = acc + ud * wd_ref[nh:nh + 1, :]
    nd_ref[:, nh - 1, :] = ud
    c = _layernorm(acc, EPS) * g_ref[...] + beta_ref[...]
    yd_ref[...] = (c * _sigmoid(c)).astype(yd_ref.dtype)


def _sample_ad(proj, st_a, st_d, wa, wd, bias, g, beta, l, row0, prev_a, prev_d):
    bs = st_a.shape[1]
    G = G_WIDTH
    tb = 32
    r0 = row0 // tb
    half = G // 2
    cola = lambda j: pl.BlockSpec((tb, G), lambda i: (r0 + i, COL_A // G + j))
    cold = lambda j: pl.BlockSpec((tb, half), lambda i: (r0 + i, COL_D // half + j))
    st = lambda n: pl.BlockSpec((None, tb, n, G), lambda i: (l, i, 0, 0))
    yrow = pl.BlockSpec((tb, G), lambda i: (i, 0))
    (bias, bias_s), (g, g_s), (beta, beta_s) = _layer_vec(bias, l), _layer_vec(g, l), _layer_vec(beta, l)
    nh = D_CONV - 1
    has_prev = prev_a is not None
    extra = [prev_a, prev_d] if has_prev else []
    return pl.pallas_call(
        functools.partial(_sample_ad_body, has_prev=has_prev),
        grid=(bs // tb,),
        in_specs=[cola(0), cola(1), cola(2), st(A_CONV - 1), _layer_mat(wa, l), cold(0), cold(1), cold(2), cold(3), st(nh),
                  _layer_mat(wd, l), bias_s, g_s, beta_s] + [ANY_SPEC] * len(extra),
        out_specs=[yrow, yrow, st(A_CONV - 1), st(nh)],
        out_shape=[jax.ShapeDtypeStruct((bs, G), BF16), jax.ShapeDtypeStruct((bs, G), BF16),
                   jax.ShapeDtypeStruct(st_a.shape, F32), jax.ShapeDtypeStruct(st_d.shape, F32)],
        input_output_aliases={14: 2, 15: 3} if has_prev else {},
        compiler_params=_cparams("parallel"),
        name="sample_conv_ad",
    )(proj, proj, proj, st_a, wa, proj, proj, proj, proj, st_d, wd, bias, g, beta, *extra)


RET_D_BLOCK = 16


def _ret_sample_body(*refs, has_prev):
    q_ref, k_ref, v_ref, g_ref, cos_ref, sin_ref, s_ref = refs[:7]
    y_ref, ns_ref, qt_ref, kt_ref, acc_ref = refs[7 + has_prev:]
    h = pl.program_id(0)
    j = pl.program_id(1)
    nd = RET_D // RET_D_BLOCK
    gammas = [math.exp(lg) for lg in RET_LOG_DECAY]
    gamma = jnp.where(h == 0, gammas[0], jnp.where(h == 1, gammas[1], jnp.where(h == 2, gammas[2], gammas[3])))
    gamma = gamma.astype(F32)

    @pl.when(j == 0)
    def _():
        q = _rope(q_ref[...], cos_ref[...], sin_ref[...])
        k = _rope(k_ref[...], cos_ref[...], sin_ref[...]) * (RET_D ** -0.5)
        qt_ref[...] = q.T
        kt_ref[...] = k.T
        acc_ref[...] = jnp.zeros_like(acc_ref)

    vt = v_ref[...].T
    acc = acc_ref[...]
    for d in range(RET_D_BLOCK):
        s_d = s_ref[:, d, :].T
        row = j * RET_D_BLOCK + d
        q_d = qt_ref[pl.ds(row, 1), :]
        k_d = kt_ref[pl.ds(row, 1), :]
        acc = acc + q_d * s_d
        ns_ref[:, d, :] = (gamma * s_d + k_d * vt).T
    acc_ref[...] = acc

    @pl.when(j == nd - 1)
    def _():
        qk = jnp.sum(qt_ref[...] * kt_ref[...], axis=0, keepdims=True)
        o = qk * vt + acc * gamma
        mu = jnp.mean(o, axis=0, keepdims=True)
        oc = o - mu
        var = jnp.mean(oc * oc, axis=0, keepdims=True)
        ln = (oc * lax.rsqrt(var + EPS)).T
        g = g_ref[...]
        y_ref[...] = (g * _sigmoid(g) * ln).astype(y_ref.dtype)


def _retention_sample(proj, cos2, sin2, state, l, row0, prev):
    bs = state.shape[1]
    r0 = row0 // bs
    nd = RET_D // RET_D_BLOCK
    col = lambda c: pl.BlockSpec((bs, RET_D), lambda h, j: (r0, COL_B // RET_D + c * RET_HEADS + h))
    tab = pl.BlockSpec((1, RET_D), lambda h, j: (0, 0))
    sblk = pl.BlockSpec((None, bs, None, RET_D_BLOCK, RET_D), lambda h, j: (l, 0, h, j, 0))
    extra, extra_specs = _alias_args(prev)
    return pl.pallas_call(
        functools.partial(_ret_sample_body, has_prev=prev is not None),
        grid=(RET_HEADS, nd),
        in_specs=[col(0), col(1), col(2), col(3), tab, tab, sblk] + extra_specs,
        out_specs=[pl.BlockSpec((bs, RET_D), lambda h, j: (0, h)), sblk],
        out_shape=[jax.ShapeDtypeStruct((bs, G_WIDTH), BF16), jax.ShapeDtypeStruct(state.shape, F32)],
        scratch_shapes=[pltpu.VMEM((RET_D, bs), F32), pltpu.VMEM((RET_D, bs), F32), pltpu.VMEM((RET_D, bs), F32)],
        input_output_aliases={7: 1} if prev is not None else {},
        compiler_params=_cparams("parallel", "arbitrary"),
        name="retention_sample",
    )(proj, proj, proj, proj, cos2, sin2, state, *extra)


def _rwkv_sample_body(*refs, has_prev):
    (xr_ref, xk_ref, xv_ref, xl_ref, pr_ref, pk_ref, pv_ref, pl_ref, mr_ref, mk_ref, mv_ref, ml_ref,
     w2w_ref, w2a_ref, g2_ref, w0_ref, a0_ref, kk_ref, ka_ref, rk_ref, lg_ref, lb_ref, s_ref) = refs[:23]
    y_ref, ns_ref, vec_ref, ot_ref = refs[23 + has_prev:]
    N = RWKV_HEAD

    def shifted(x_ref, prev_ref, mu_ref):
        x = x_ref[...]
        return x + (prev_ref[...] - x) * mu_ref[...]

    r, lw, k4, v, kk, a, gate = _rwkv_pre(shifted(xr_ref, pr_ref, mr_ref), shifted(xk_ref, pk_ref, mk_ref),
                                          shifted(xv_ref, pv_ref, mv_ref), shifted(xl_ref, pl_ref, ml_ref),
                                          w2w_ref[...], w2a_ref[...], g2_ref[...], w0_ref[...], a0_ref[...],
                                          kk_ref[...], ka_ref[...])
    kkn = jnp.concatenate([_l2_normalize(kk[:, 0:N]), _l2_normalize(kk[:, N:2 * N])], axis=1)
    w = jnp.exp(lw)
    vec_ref[0] = kkn.T
    vec_ref[1] = w.T
    vec_ref[2] = (kkn * a).T
    vec_ref[3] = k4.T
    vec_ref[4] = r.T
    vec_ref[5] = v.T
    st = s_ref[...].T
    news = []
    for hh in range(2):
        ch = slice(hh * N, (hh + 1) * N)
        kk_t = vec_ref[0, ch, :]
        w_t = vec_ref[1, ch, :]
        kka_t = vec_ref[2, ch, :]
        k_t = vec_ref[3, ch, :]
        r_t = vec_ref[4, ch, :]
        for i in range(N):
            base = (hh * N + i) * N
            s_i = st[base:base + N, :]
            sa = -jnp.sum(s_i * kk_t, axis=0, keepdims=True)
            v_i = vec_ref[5, hh * N + i:hh * N + i + 1, :]
            s_new = s_i * w_t + sa * kka_t + v_i * k_t
            news.append(s_new)
            ot_ref[hh * N + i:hh * N + i + 1, :] = jnp.sum(s_new * r_t, axis=0, keepdims=True)
    ns_ref[...] = jnp.concatenate(news, axis=0).T
    o = ot_ref[...].T
    rk = rk_ref[...]
    lg = lg_ref[...]
    lb = lb_ref[...]
    ys = []
    for hh in range(2):
        ch = slice(hh * N, (hh + 1) * N)
        ys.append(_rwkv_post(o[:, ch], r[:, ch], k4[:, ch], v[:, ch], gate[:, ch], rk[:, ch], lg[:, ch], lb[:, ch]))
    y_ref[...] = jnp.concatenate(ys, axis=1).astype(y_ref.dtype)


def _rwkv_sample(proj, prev_shift, rw_arrs, state, l, row0, prev):
    mu, w2a, g2, w0, a0, k_k, k_a, r_k, lnx_g, lnx_b = rw_arrs
    bs = state.shape[1]
    r0 = row0 // bs
    G = G_WIDTH
    N = RWKV_HEAD
    W = 2 * N
    npair = G // W
    def cols(shape_lead, lead_idx, base):
        mk = lambda width, cidx: pl.BlockSpec(shape_lead + (width,), lambda p: lead_idx + (cidx(p),))
        return [mk(W, lambda p, c=c: base // W + c * npair + p) for c in range(3)] + [
            mk(2 * W, lambda p: (base + 3 * G) // (2 * W))]
    pairv = pl.BlockSpec((None, 1, W), lambda p: (l, 0, p))
    pairm = pl.BlockSpec((None, 128, W), lambda p: (l, 0, p))
    sblk = pl.BlockSpec((None, bs, W * N), lambda p: (l, 0, p))
    extra, extra_specs = _alias_args(prev)
    return pl.pallas_call(
        functools.partial(_rwkv_sample_body, has_prev=prev is not None),
        grid=(npair,),
        in_specs=cols((bs,), (r0,), COL_C) + cols((None, bs), (l, 0), 0) + cols((None, 1), (l, 0), 0) + [
            pairm, pl.BlockSpec((None, 128, W), lambda p: (l, 0, npair + p)), pairm,
            pairv, pairv, pairv, pairv, pairv, pairv, pairv, sblk] + extra_specs,
        out_specs=[pl.BlockSpec((bs, W), lambda p: (0, p)), sblk],
        out_shape=[jax.ShapeDtypeStruct((bs, G), BF16), jax.ShapeDtypeStruct(state.shape, F32)],
        scratch_shapes=[pltpu.VMEM((6, W, bs), F32), pltpu.VMEM((W, bs), F32)],
        input_output_aliases={23: 1} if prev is not None else {},
        compiler_params=_cparams("parallel"),
        name="rwkv_sample",
    )(proj, proj, proj, proj, prev_shift, prev_shift, prev_shift, prev_shift, mu, mu, mu, mu, w2a, w2a, g2,
      w0, a0, k_k, k_a, r_k, lnx_g, lnx_b, state, *extra)


def _place_cols_body(src_ref, buf_ref, o_ref):
    del buf_ref
    o_ref[...] = src_ref[...]


def _place_cols(src, buf, col):
    rows, w = src.shape
    tr = _pick(rows, (1024, 512, 256, 128))
    return pl.pallas_call(
        _place_cols_body,
        grid=(rows // tr,),
        in_specs=[pl.BlockSpec((tr, w), lambda i: (i, 0)), ANY_SPEC],
        out_specs=pl.BlockSpec((tr, w), lambda i: (i, col)),
        out_shape=jax.ShapeDtypeStruct(buf.shape, buf.dtype),
        input_output_aliases={1: 0},
        compiler_params=_cparams("parallel"),
        name="place_cols",
    )(src, buf)


def _assemble_body(a_ref, b_ref, c_ref, d_ref, buf_ref, o_ref):
    del buf_ref
    o_ref[...] = jnp.concatenate([a_ref[...], b_ref[...], c_ref[...], d_ref[...]], axis=1)


def _assemble_rows(parts, buf, row0):
    bs = parts[0].shape[0]
    part = pl.BlockSpec((bs, G_WIDTH), lambda i: (0, 0))
    return pl.pallas_call(
        _assemble_body,
        grid=(1,),
        in_specs=[part, part, part, part, ANY_SPEC],
        out_specs=pl.BlockSpec((bs, D_MODEL), lambda i: (row0 // bs, 0)),
        out_shape=jax.ShapeDtypeStruct(buf.shape, buf.dtype),
        input_output_aliases={4: 0},
        compiler_params=_cparams("arbitrary"),
        name="assemble_sample_rows",
    )(*parts, buf)


def kernel(x_prompt, x_sample, mem_prompt, state_conv_a, state_ret, state_shift, state_wkv, state_conv_d, cache_mem_k, cache_mem_v, g_mix, w_in, conv_a_w, mu_c, w0, w2, a0, a2, g2, k_k, k_a, r_k, lnx_g, lnx_b, conv_d_w, conv_d_b, ln_d_g, ln_d_b, w_out, g_xa, g_mem, wq_x, wk_x, wv_x, wo_x, g_mlp, w_up, w_down, g_final):
    bp, seq, d = x_prompt.shape
    bs = x_sample.shape[0]
    depth = w_in.shape[0]
    np_rows = bp * seq
    G = G_WIDTH
    x = jnp.concatenate([x_prompt.reshape(np_rows, d), x_sample.reshape(bs, d)], axis=0)
    mem = mem_prompt.reshape(bp * N_MEM, d)
    cos_p, sin_p = _rope_tables(jnp.arange(seq))
    cos_s, sin_s = _rope_tables(PAST_LEN + jnp.arange(1))

    w_down_b = w_down.astype(BF16)
    z64 = jnp.zeros((depth, 64, G), F32)
    w2a = jnp.concatenate([jnp.concatenate([w2, z64], axis=2), jnp.concatenate([z64, a2], axis=2)], axis=1).astype(BF16)
    g2_b = g2.astype(BF16)
    wkv_flat = state_wkv.reshape(depth, bs, -1)
    vec3 = lambda a: a.reshape(depth, 1, -1)
    rw_arrs = (vec3(mu_c), w2a, g2_b, vec3(w0), vec3(a0), vec3(k_k), vec3(k_a), vec3(r_k), vec3(lnx_g), vec3(lnx_b))

    small = {k: [] for k in ("a_p", "ret_p", "sh_p", "sh_s", "wkv_p", "d_p", "mk", "mv")}
    na_s = nd_s = nret_s = nwkv_s = None
    for l in range(depth):
        proj = _matmul_ws(x, w_in, l, norm_g=_layer_vec(g_mix, l), tn=1280, tm=_pick(x.shape[0], (640, 512, 256, 128)))
        ymix, na_p = _conv_a_prompt(proj, conv_a_w, l, bp, seq)
        ymix, nd_p = _conv_d_prompt(proj, conv_d_w, conv_d_b, ln_d_g, ln_d_b, ymix, l, bp, seq)
        ymix, nret_p = _retention_prompt(proj, cos_p, sin_p, ymix, bp, seq)
        rw = [(rw_arrs[0], pl.BlockSpec((None, 1, SHIFT_W), lambda *_: (l, 0, 0))),
              (w2a, _layer_mat(w2a, l)), (g2_b, _layer_mat(g2_b, l))] + [
              (a, pl.BlockSpec((None, 1, G), lambda *_: (l, 0, 0))) for a in rw_arrs[3:]]
        yc_p, nsh_p, nwkv_p = _rwkv_prompt(proj, rw, bp, seq)
        ymix = _place_cols(yc_p.reshape(np_rows, G), ymix, 2)
        ya_s, yd_s, na_s, nd_s = _sample_ad(proj, state_conv_a, state_conv_d, conv_a_w, conv_d_w, conv_d_b, ln_d_g, ln_d_b,
                                            l, np_rows, na_s, nd_s)
        yb_s, nret_s = _retention_sample(proj, cos_s, sin_s, state_ret, l, np_rows, nret_s)
        yc_s, nwkv_s = _rwkv_sample(proj, state_shift, rw_arrs, wkv_flat, l, np_rows, nwkv_s)
        ymix = _assemble_rows([ya_s, yb_s, yc_s, yd_s], ymix, np_rows)
        x = _matmul_ws(ymix, w_out, l, residual=x)

        mk = _matmul_ws(mem, wk_x, l, norm_g=_layer_vec(g_mem, l))
        mv = _matmul_ws(mem, wv_x, l, norm_g=_layer_vec(g_mem, l))
        q = _matmul_ws(x, wq_x, l, norm_g=_layer_vec(g_xa, l))
        obuf = _xattn_prompt(q, mk, mv, bp, seq)
        q_s = q[np_rows:].reshape(bs, XA_HEADS, XA_HD)
        q_s = jnp.concatenate([q_s, q_s], axis=1)
        obuf = _xattn_sample(q_s, cache_mem_k, cache_mem_v, obuf, l, np_rows)
        x = _matmul_ws(obuf, wo_x, l, residual=x)

        up = _matmul_ws(x, w_up, l, norm_g=_layer_vec(g_mlp, l), act="relu2", out_dtype=BF16)
        x = _matmul(up, w_down_b, l, residual=x)

        small["a_p"].append(na_p)
        small["ret_p"].append(nret_p)
        small["sh_p"].append(nsh_p.reshape(bp, SHIFT_W))
        small["sh_s"].append(proj[np_rows:, COL_C:COL_C + SHIFT_W])
        small["wkv_p"].append(nwkv_p)
        small["d_p"].append(nd_p)
        small["mk"].append(mk.reshape(bp, N_MEM, XA_HEADS, XA_HD))
        small["mv"].append(mv.reshape(bp, N_MEM, XA_HEADS, XA_HD))
    gf = (g_final.reshape(1, d), pl.BlockSpec((1, d), lambda *_: (0, 0)))
    y_p = _rmsnorm(x, gf, F32, 0, np_rows)
    y_s = _rmsnorm(x, gf, F32, np_rows, bs)
    st = lambda k: jnp.stack(small[k])
    return (y_p.reshape(bp, seq, d), y_s.reshape(bs, 1, d), st("a_p"), na_s, st("ret_p"), nret_s,
            st("sh_p"), st("sh_s"), st("wkv_p"), nwkv_s.reshape(depth, bs, RWKV_HEADS, RWKV_HEAD, RWKV_HEAD),
            st("d_p"), nd_s, st("mk"), st("mv"))
```

```python
import functools
import math

import jax
import jax.numpy as jnp
from jax import lax
from jax.experimental import pallas as pl
from jax.experimental.pallas import tpu as pltpu

F32 = jnp.float32
BF16 = jnp.bfloat16
HI = lax.Precision.HIGHEST

D_MODEL = 2048
G_WIDTH = 512
P_IN = 6400
RET_HEADS = 4
RET_D = 128
RET_CHUNK = 128
ROPE_BASE = 10000.0
RWKV_HEAD = 64
RWKV_HEADS = 8
WKV_CHUNK = 64
RWKV_GN_EPS = 64e-5
A_CONV = 3
D_CONV = 31
D_HIST = 32
N_MEM = 256
XA_HEADS = 4
XA_HD = 512
SHIFT_W = 1792
PAST_LEN = 16384
EPS = 1e-6
RET_LOG_DECAY = tuple(math.log1p(-(2.0 ** (-5.0 - h))) for h in range(RET_HEADS))
VMEM_LIMIT = 56 * 1024 * 1024
COL_A = 0
COL_B = 3 * G_WIDTH
COL_C = 7 * G_WIDTH
COL_D = 7 * G_WIDTH + SHIFT_W

ANY_SPEC = pl.BlockSpec(memory_space=pl.ANY)


def _cparams(*sem):
    return pltpu.CompilerParams(dimension_semantics=sem, vmem_limit_bytes=VMEM_LIMIT)


def _pick(n, cands):
    for c in cands:
        if n % c == 0:
            return c
    raise ValueError(f"no tile for {n}")


def _sigmoid(x):
    return 1.0 / (1.0 + jnp.exp(-x))


def _dot(a, b, precision=None):
    return jnp.dot(a, b, preferred_element_type=F32, precision=precision)


def _dot_nt(a, b, precision=None):
    return lax.dot_general(a, b, (((1,), (1,)), ((), ())), preferred_element_type=F32, precision=precision)


def _dot_tn(a, b, precision=None):
    return lax.dot_general(a, b, (((0,), (0,)), ((), ())), preferred_element_type=F32, precision=precision)


def _layernorm(x, eps):
    mu = jnp.mean(x, axis=-1, keepdims=True)
    xc = x - mu
    var = jnp.mean(xc * xc, axis=-1, keepdims=True)
    return xc * lax.rsqrt(var + eps)


def _layer_vec(a, l):
    n = a.shape[-1]
    return a.reshape(a.shape[0], 1, n), pl.BlockSpec((None, 1, n), lambda *_: (l, 0, 0))


def _layer_mat(a, l):
    return pl.BlockSpec((None,) + a.shape[1:], lambda *_: (l, 0, 0))


def _rmsnorm_body(x_ref, g_ref, o_ref):
    x = x_ref[...]
    ms = jnp.mean(x * x, axis=-1, keepdims=True)
    o_ref[...] = ((x * lax.rsqrt(ms + EPS)) * g_ref[...]).astype(o_ref.dtype)


def _rmsnorm(x, g_spec, out_dtype, row0=0, nrows=None):
    m, d = x.shape
    nrows = nrows or m
    tm = _pick(math.gcd(nrows, row0) if row0 else nrows, (1040, 1024, 640, 512, 256, 128))
    r0 = row0 // tm
    g, gspec = g_spec
    return pl.pallas_call(
        _rmsnorm_body,
        grid=(nrows // tm,),
        in_specs=[pl.BlockSpec((tm, d), lambda i: (r0 + i, 0)), gspec],
        out_specs=pl.BlockSpec((tm, d), lambda i: (i, 0)),
        out_shape=jax.ShapeDtypeStruct((nrows, d), out_dtype),
        compiler_params=_cparams("parallel"),
        name="rmsnorm",
    )(x, g)


def _mm_body(*refs, act, has_res, nk):
    a_ref, w_ref = refs[0], refs[1]
    res_ref = refs[2] if has_res else None
    o_ref = refs[3] if has_res else refs[2]

    def finish(r):
        if act == "relu2":
            r = jnp.square(jnp.maximum(r, 0.0))
        if has_res:
            r = res_ref[...] + r
        o_ref[...] = r.astype(o_ref.dtype)

    if nk == 1:
        finish(_dot(a_ref[...], w_ref[...]))
        return
    acc_ref = refs[-1]
    k = pl.program_id(2)

    @pl.when(k == 0)
    def _():
        acc_ref[...] = jnp.zeros_like(acc_ref)

    acc_ref[...] += _dot(a_ref[...], w_ref[...])

    @pl.when(k == nk - 1)
    def _():
        finish(acc_ref[...])


def _matmul(a, w, l, *, residual=None, act=None, out_dtype=F32, tn=None):
    m, kdim = a.shape
    n = w.shape[2]
    tm = _pick(m, (1040, 1024, 640, 512, 256, 128))
    tn = tn or _pick(n, (1024, 896, 768, 512, 256))
    tk = _pick(kdim, (2048, 1024, 512))
    nk = kdim // tk
    has_res = residual is not None
    in_specs = [pl.BlockSpec((tm, tk), lambda i, j, k: (i, k)), pl.BlockSpec((None, tk, tn), lambda i, j, k: (l, k, j))]
    args = [a, w]
    if has_res:
        in_specs.append(pl.BlockSpec((tm, tn), lambda i, j, k: (i, j)))
        args.append(residual)
    return pl.pallas_call(
        functools.partial(_mm_body, act=act, has_res=has_res, nk=nk),
        grid=(m // tm, n // tn, nk),
        in_specs=in_specs,
        out_specs=pl.BlockSpec((tm, tn), lambda i, j, k: (i, j)),
        out_shape=jax.ShapeDtypeStruct((m, n), out_dtype),
        scratch_shapes=[pltpu.VMEM((tm, tn), F32)] if nk > 1 else [],
        compiler_params=_cparams("parallel", "parallel", "arbitrary"),
        name="matmul",
    )(*args)


def _mm_ws_body(*refs, act, has_res, has_norm):
    a_ref, w_ref = refs[0], refs[1]
    g_ref = refs[2] if has_norm else None
    res_ref = refs[2 + has_norm] if has_res else None
    o_ref = refs[2 + has_norm + has_res]
    wb_ref = refs[-1]

    @pl.when(pl.program_id(1) == 0)
    def _():
        wb_ref[...] = w_ref[...].astype(BF16)

    a = a_ref[...]
    if has_norm:
        ms = jnp.mean(a * a, axis=-1, keepdims=True)
        a = ((a * lax.rsqrt(ms + EPS)) * g_ref[...]).astype(BF16)
    r = _dot(a, wb_ref[...])
    if act == "relu2":
        r = jnp.square(jnp.maximum(r, 0.0))
    if has_res:
        r = res_ref[...] + r
    o_ref[...] = r.astype(o_ref.dtype)


def _matmul_ws(a, w, l, *, norm_g=None, residual=None, act=None, out_dtype=F32, tn=1024, tm=None):
    m, kdim = a.shape
    n = w.shape[2]
    tm = tm or _pick(m, (1040, 1024, 640, 512, 256, 128))
    has_res = residual is not None
    has_norm = norm_g is not None
    in_specs = [pl.BlockSpec((tm, kdim), lambda j, i: (i, 0)), pl.BlockSpec((None, kdim, tn), lambda j, i: (l, 0, j))]
    args = [a, w]
    if has_norm:
        args.append(norm_g[0])
        in_specs.append(norm_g[1])
    if has_res:
        in_specs.append(pl.BlockSpec((tm, tn), lambda j, i: (i, j)))
        args.append(residual)
    return pl.pallas_call(
        functools.partial(_mm_ws_body, act=act, has_res=has_res, has_norm=has_norm),
        grid=(n // tn, m // tm),
        in_specs=in_specs,
        out_specs=pl.BlockSpec((tm, tn), lambda j, i: (i, j)),
        out_shape=jax.ShapeDtypeStruct((m, n), out_dtype),
        scratch_shapes=[pltpu.VMEM((kdim, tn), BF16)],
        compiler_params=_cparams("parallel", "arbitrary"),
        name="matmul_ws",
    )(*args)


def _conv_a_body(b_ref, c_ref, h_ref, w_ref, y_ref, st_ref, ext_ref, *, tl):
    t = pl.program_id(1)

    @pl.when(t == 0)
    def _():
        ext_ref[0:8, :] = jnp.zeros((8, G_WIDTH), F32)

    u = c_ref[...] * h_ref[...]
    ext_ref[8:, :] = u
    u1 = ext_ref[7:7 + tl, :]
    u2 = ext_ref[6:6 + tl, :]
    y = b_ref[...] * (w_ref[0:1, :] * u2 + w_ref[1:2, :] * u1 + w_ref[2:3, :] * u)
    y_ref[...] = y.astype(y_ref.dtype)
    st_ref[0] = ext_ref[tl + 6:tl + 8, :]
    ext_ref[0:8, :] = ext_ref[tl:tl + 8, :]


def _conv_a_prompt(proj, conv_w, l, bp, seq):
    tl = _pick(seq, (512, 256, 128))
    nt = seq // tl
    c0 = COL_A // G_WIDTH
    col = lambda j: pl.BlockSpec((tl, G_WIDTH), lambda b, t: (b * nt + t, c0 + j))
    return pl.pallas_call(
        functools.partial(_conv_a_body, tl=tl),
        grid=(bp, nt),
        in_specs=[col(0), col(1), col(2), _layer_mat(conv_w, l)],
        out_specs=[pl.BlockSpec((tl, G_WIDTH), lambda b, t: (b * nt + t, 0)),
                   pl.BlockSpec((1, A_CONV - 1, G_WIDTH), lambda b, t: (b, 0, 0))],
        out_shape=[jax.ShapeDtypeStruct((proj.shape[0], D_MODEL), BF16),
                   jax.ShapeDtypeStruct((bp, A_CONV - 1, G_WIDTH), F32)],
        scratch_shapes=[pltpu.VMEM((tl + 8, G_WIDTH), F32)],
        compiler_params=_cparams("parallel", "arbitrary"),
        name="conv_a_prompt",
    )(proj, proj, proj, conv_w)


CONV_D_ROWS = 32


def _conv_d_body(d1a_ref, d1b_ref, d2a_ref, d2b_ref, w_ref, bias_ref, g_ref, beta_ref, ymix_ref, y_ref, st_ref, ext_ref,
                 acc_ref, *, tl):
    del ymix_ref
    t = pl.program_id(1)
    next_ = D_HIST + tl

    @pl.when(t == 0)
    def _():
        ext_ref[0, 0:D_HIST, :] = jnp.zeros((D_HIST, G_WIDTH), F32)

    d1 = jnp.concatenate([d1a_ref[...], d1b_ref[...]], axis=1)
    d2 = jnp.concatenate([d2a_ref[...], d2b_ref[...]], axis=1)
    ext_ref[0, D_HIST:, :] = d1 * _sigmoid(d2)
    ext = ext_ref[0]
    for s in range(1, 8):
        ext_ref[s] = pltpu.roll(ext, next_ - s, 0)
    first = D_HIST - (D_CONV - 1)

    def blk(i, carry):
        base = pl.multiple_of(i * CONV_D_ROWS, CONV_D_ROWS)
        acc = jnp.broadcast_to(bias_ref[...], (CONV_D_ROWS, G_WIDTH))
        for k in range(D_CONV):
            s = (first + k) % 8
            a0 = first + k - s
            acc = acc + ext_ref[s, pl.ds(base + a0, CONV_D_ROWS), :] * w_ref[k:k + 1, :]
        acc_ref[pl.ds(base, CONV_D_ROWS), :] = acc
        return carry

    lax.fori_loop(0, tl // CONV_D_ROWS, blk, 0)
    c = _layernorm(acc_ref[...], EPS) * g_ref[...] + beta_ref[...]
    y_ref[...] = (c * _sigmoid(c)).astype(y_ref.dtype)
    st_ref[0] = ext_ref[0, tl + first:tl + D_HIST, :]
    ext_ref[0, 0:D_HIST, :] = ext_ref[0, tl:tl + D_HIST, :]


def _conv_d_prompt(proj, w, bias, g, beta, ymix, l, bp, seq):
    tl = _pick(seq, (512, 256, 128))
    nt = seq // tl
    half = G_WIDTH // 2
    c0 = COL_D // half
    col = lambda j: pl.BlockSpec((tl, half), lambda b, t: (b * nt + t, c0 + j))
    (bias, bias_s), (g, g_s), (beta, beta_s) = _layer_vec(bias, l), _layer_vec(g, l), _layer_vec(beta, l)
    return pl.pallas_call(
        functools.partial(_conv_d_body, tl=tl),
        grid=(bp, nt),
        in_specs=[col(0), col(1), col(2), col(3), _layer_mat(w, l), bias_s, g_s, beta_s, ANY_SPEC],
        out_specs=[pl.BlockSpec((tl, G_WIDTH), lambda b, t: (b * nt + t, 3)),
                   pl.BlockSpec((1, D_CONV - 1, G_WIDTH), lambda b, t: (b, 0, 0))],
        out_shape=[jax.ShapeDtypeStruct(ymix.shape, ymix.dtype),
                   jax.ShapeDtypeStruct((bp, D_CONV - 1, G_WIDTH), F32)],
        scratch_shapes=[pltpu.VMEM((8, tl + D_HIST, G_WIDTH), F32), pltpu.VMEM((tl, G_WIDTH), F32)],
        input_output_aliases={8: 0},
        compiler_params=_cparams("parallel", "arbitrary"),
        name="conv_d_prompt",
    )(proj, proj, proj, proj, w, bias, g, beta, ymix)


def _rope_tables(pos):
    inv = ROPE_BASE ** (-jnp.arange(0, RET_D, 2, dtype=F32) / RET_D)
    ang = pos.astype(F32)[:, None] * inv[None, :]
    cos = jnp.cos(ang)
    sin = jnp.sin(ang)
    return jnp.concatenate([cos, cos], axis=-1), jnp.concatenate([-sin, sin], axis=-1)


def _rope(x, cos2, sin2):
    return x * cos2 + pltpu.roll(x, RET_D // 2, 1) * sin2


def _ret_body(q_ref, k_ref, v_ref, g_ref, cos_ref, sin_ref, ymix_ref, y_ref, st_ref, s_ref, *, nchunks):
    del ymix_ref
    c = pl.program_id(1)
    C = RET_CHUNK

    @pl.when(c == 0)
    def _():
        s_ref[...] = jnp.zeros_like(s_ref)

    ii = lax.broadcasted_iota(jnp.int32, (C, C), 0).astype(F32)
    jj = lax.broadcasted_iota(jnp.int32, (C, C), 1).astype(F32)
    diff = ii - jj
    cos2 = cos_ref[...]
    sin2 = sin_ref[...]
    H = range(RET_HEADS)
    sl = [slice(h * RET_D, (h + 1) * RET_D) for h in H]
    lg = RET_LOG_DECAY
    q = [_rope(q_ref[:, sl[h]], cos2, sin2) for h in H]
    k = [_rope(k_ref[:, sl[h]], cos2, sin2) * (RET_D ** -0.5) for h in H]
    qb = [q[h].astype(BF16) for h in H]
    vb = [v_ref[:, sl[h]].astype(BF16) for h in H]
    s_old = [s_ref[h] for h in H]
    scores = [_dot_nt(qb[h], k[h].astype(BF16)) for h in H]
    cross = [_dot(qb[h], s_old[h].astype(BF16)) for h in H]
    kv = [_dot_tn((k[h] * jnp.exp(lg[h] * (C - 1.0 - ii))).astype(BF16), vb[h]) for h in H]
    dmat = [jnp.where(diff >= 0.0, jnp.exp(lg[h] * jnp.maximum(diff, 0.0)), 0.0) for h in H]
    inner = [_dot((scores[h] * dmat[h]).astype(BF16), vb[h]) for h in H]
    for h in H:
        s_ref[h] = math.exp(lg[h] * C) * s_old[h] + kv[h]
        o = _layernorm(inner[h] + cross[h] * jnp.exp(lg[h] * (ii + 1.0)), EPS)
        g = g_ref[:, sl[h]]
        y_ref[:, sl[h]] = (g * _sigmoid(g) * o).astype(y_ref.dtype)

    @pl.when(c == nchunks - 1)
    def _():
        st_ref[0] = s_ref[...]


def _retention_prompt(proj, cos2, sin2, ymix, bp, seq):
    C = RET_CHUNK
    nc = seq // C
    c0 = COL_B // G_WIDTH
    col = lambda j: pl.BlockSpec((C, G_WIDTH), lambda b, c: (b * nc + c, c0 + j))
    tab = pl.BlockSpec((C, RET_D), lambda b, c: (c, 0))
    return pl.pallas_call(
        functools.partial(_ret_body, nchunks=nc),
        grid=(bp, nc),
        in_specs=[col(0), col(1), col(2), col(3), tab, tab, ANY_SPEC],
        out_specs=[pl.BlockSpec((C, G_WIDTH), lambda b, c: (b * nc + c, 1)),
                   pl.BlockSpec((1, RET_HEADS, RET_D, RET_D), lambda b, c: (b, 0, 0, 0))],
        out_shape=[jax.ShapeDtypeStruct(ymix.shape, ymix.dtype),
                   jax.ShapeDtypeStruct((bp, RET_HEADS, RET_D, RET_D), F32)],
        scratch_shapes=[pltpu.VMEM((RET_HEADS, RET_D, RET_D), F32)],
        input_output_aliases={6: 0},
        compiler_params=_cparams("parallel", "arbitrary"),
        name="retention_prompt",
    )(proj, proj, proj, proj, cos2, sin2, ymix)


def _softplus(z):
    return jnp.maximum(z, 0.0) + jnp.log(1.0 + jnp.exp(-jnp.abs(z)))


def _rwkv_pre(r, kc, vc, lora_x, w2_w, w2_a, g2, w0, a0, k_k, k_a):
    lin = lora_x[:, 0:128]
    lane = lax.broadcasted_iota(jnp.int32, lin.shape, 1)
    lin = jnp.where(lane < 64, jnp.tanh(lin), lin).astype(BF16)
    gate = _dot(_sigmoid(lora_x[:, 128:256]).astype(BF16), g2)
    w_pre = -_softplus(-(w0 + _dot(lin, w2_w))) - 0.5
    log_decay = -jnp.exp(w_pre)
    a = _sigmoid(a0 + _dot(lin, w2_a))
    kk = kc * k_k
    k4 = kc * (1.0 + (a - 1.0) * k_a)
    return r, log_decay, k4, vc, kk, a, gate


def _l2_normalize(kk):
    ss = jnp.sum(kk * kk, axis=-1, keepdims=True)
    return kk * jnp.minimum(lax.rsqrt(ss), 1e12)


def _bdot(a, b):
    return _dot(a.astype(BF16), b.astype(BF16))


def _wkv_chunk(r, lw, cum, k, v, kk, a, z, mask2, eye):
    T = WKV_CHUNK
    H = range(len(r))
    g_inc = [jnp.exp(cum[h]) for h in H]
    g_tot = [g_inc[h][T - 1:T, :] for h in H]
    ah = [-(kk[h] * jnp.exp(cum[h] - lw[h])) for h in H]
    rg = [r[h] * g_inc[h] for h in H]
    g_inv = [jnp.exp(-cum[h]) for h in H]
    bd = [kk[h] * a[h] * g_inv[h] for h in H]
    kd = [k[h] * g_inv[h] for h in H]
    ar = [jnp.concatenate([ah[h], rg[h]], axis=0).astype(BF16) for h in H]
    xb = [jnp.where(mask2, _dot_nt(ar[h], bd[h].astype(BF16)), 0.0) for h in H]
    xk = [jnp.where(mask2, _dot_nt(ar[h], kd[h].astype(BF16)), 0.0).astype(BF16) for h in H]
    vb = [v[h].astype(BF16) for h in H]
    xkv = [_dot(xk[h], vb[h]) for h in H]
    acur = [xb[h][0:T] for h in H]
    p = [eye for _ in H]
    for it in range(5):
        x = [_bdot(jnp.concatenate([acur[h], p[h]], axis=0), acur[h]) for h in H]
        acur = [x[h][0:T] for h in H]
        p = [p[h] + x[h][T:2 * T] for h in H]
    x = [_bdot(p[h], acur[h]) for h in H]
    p = [p[h] + x[h] for h in H]
    wub = [_bdot(p[h], jnp.concatenate([ah[h], xkv[h][0:T]], axis=1)).astype(BF16) for h in H]
    qo = [_dot(xb[h][T:2 * T].astype(BF16), wub[h]) for h in H]
    lhs_t = [jnp.concatenate([bd[h] * g_tot[h], kd[h] * g_tot[h]], axis=0).astype(BF16) for h in H]
    zero = jnp.zeros((T, 64), BF16)
    mn = [_dot_tn(lhs_t[h], jnp.concatenate([wub[h], jnp.concatenate([zero, vb[h]], axis=1)], axis=0)) for h in H]
    qm = [jnp.concatenate([rg[h] + qo[h][:, 0:64], mn[h][:, 0:64] + jnp.where(eye > 0.0, g_tot[h], 0.0)], axis=0) for h in H]
    oz = [_bdot(qm[h], z[h]) for h in H]
    o = [oz[h][0:T] + qo[h][:, 64:128] + xkv[h][T:2 * T] for h in H]
    z_new = [oz[h][T:2 * T] + mn[h][:, 64:128] for h in H]
    return o, z_new


def _rwkv_post(o, r, k4, v, gate, rk, lnx_g, lnx_b):
    gn = _layernorm(o, RWKV_GN_EPS) * lnx_g + lnx_b
    bonus = jnp.sum(r * k4 * rk, axis=-1, keepdims=True) * v
    return (gn + bonus) * gate


def _rwkv_prompt_body(*refs, nb, nchunks):
    pc_refs = refs[:nb]
    mu_ref, w2a_ref, g2_ref, w0_ref, a0_ref, kk_ref, ka_ref, rk_ref, lg_ref, lb_ref = refs[nb:nb + 10]
    y_ref, shift_ref, st_ref, carry_ref, z_ref = refs[nb + 10:]
    c = pl.program_id(0)
    T = WKV_CHUNK
    NH = RWKV_HEADS

    @pl.when(c == 0)
    def _():
        carry_ref[...] = jnp.zeros_like(carry_ref)
        z_ref[...] = jnp.zeros_like(z_ref)

    pcs, prevs = [], []
    for b in range(nb):
        pc = pc_refs[b][...]
        row = lax.broadcasted_iota(jnp.int32, pc.shape, 0)
        prevs.append(jnp.where(row == 0, carry_ref[b, 0:1, :], pltpu.roll(pc, 1, 0)))
        carry_ref[b, 0:1, :] = pc[T - 1:T, :]
        shift_ref[b] = pc[T - 1:T, :]
        pcs.append(pc)
    pc = jnp.concatenate(pcs, axis=0)
    xs = pc + (jnp.concatenate(prevs, axis=0) - pc) * mu_ref[...]
    G = G_WIDTH
    r, lw, k4, v, kk, a, gate = _rwkv_pre(xs[:, 0:G], xs[:, G:2 * G], xs[:, 2 * G:3 * G], xs[:, 3 * G:3 * G + 256],
                                          w2a_ref[:, 0:G], w2a_ref[:, G:2 * G], g2_ref[...], w0_ref[...], a0_ref[...],
                                          kk_ref[...], ka_ref[...])
    ti = lax.broadcasted_iota(jnp.int32, (T, T), 0)
    si = lax.broadcasted_iota(jnp.int32, (T, T), 1)
    tril = (ti >= si).astype(F32)
    eye = (ti == si).astype(F32)
    cum = jnp.concatenate([_dot(tril, lw[b * T:(b + 1) * T, :], HI) for b in range(nb)], axis=0)
    t2 = lax.broadcasted_iota(jnp.int32, (2 * T, T), 0)
    s2 = lax.broadcasted_iota(jnp.int32, (2 * T, T), 1)
    mask2 = jnp.where(t2 < T, t2 - 1, t2 - T) >= s2
    heads = lambda x: [x[b * T:(b + 1) * T, h * RWKV_HEAD:(h + 1) * RWKV_HEAD] for b in range(nb) for h in range(NH)]
    r_h, k_h, v_h = heads(r), heads(k4), heads(v)
    o, z_new = _wkv_chunk(r_h, heads(lw), heads(cum), k_h, v_h, [_l2_normalize(x) for x in heads(kk)], heads(a),
                          [z_ref[i] for i in range(nb * NH)], mask2, eye)
    for i in range(nb * NH):
        z_ref[i] = z_new[i]
    gate_h = heads(gate)
    vec_heads = lambda ref: [ref[:, h * RWKV_HEAD:(h + 1) * RWKV_HEAD] for h in range(NH)]
    rk_h, lg_h, lb_h = vec_heads(rk_ref), vec_heads(lg_ref), vec_heads(lb_ref)
    for b in range(nb):
        ys = [_rwkv_post(o[b * NH + h], r_h[b * NH + h], k_h[b * NH + h], v_h[b * NH + h], gate_h[b * NH + h],
                         rk_h[h], lg_h[h], lb_h[h]) for h in range(NH)]
        y_ref[b] = jnp.concatenate(ys, axis=1).astype(y_ref.dtype)

    @pl.when(c == nchunks - 1)
    def _():
        for b in range(nb):
            for h in range(NH):
                st_ref[b, h] = z_ref[b * NH + h].T


def _rwkv_prompt(proj, rw, bp, seq):
    T = WKV_CHUNK
    nc = seq // T
    G = G_WIDTH
    arrs, specs = zip(*rw)
    pc_specs = [pl.BlockSpec((T, SHIFT_W), lambda c, b=b: (b * nc + c, COL_C // SHIFT_W)) for b in range(bp)]
    return pl.pallas_call(
        functools.partial(_rwkv_prompt_body, nb=bp, nchunks=nc),
        grid=(nc,),
        in_specs=pc_specs + list(specs),
        out_specs=[pl.BlockSpec((bp, T, G), lambda c: (0, c, 0)),
                   pl.BlockSpec((bp, 1, SHIFT_W), lambda c: (0, 0, 0)),
                   pl.BlockSpec((bp, RWKV_HEADS, RWKV_HEAD, RWKV_HEAD), lambda c: (0, 0, 0, 0))],
        out_shape=[jax.ShapeDtypeStruct((bp, seq, G), BF16),
                   jax.ShapeDtypeStruct((bp, 1, SHIFT_W), F32),
                   jax.ShapeDtypeStruct((bp, RWKV_HEADS, RWKV_HEAD, RWKV_HEAD), F32)],
        scratch_shapes=[pltpu.VMEM((bp, 8, SHIFT_W), F32), pltpu.VMEM((bp * RWKV_HEADS, RWKV_HEAD, RWKV_HEAD), F32)],
        compiler_params=_cparams("arbitrary"),
        name="rwkv_prompt",
    )(*([proj] * bp), *arrs)


def _xattn_prompt_body(q_ref, k_ref, v_ref, o_ref, kb_ref, vb_ref):
    @pl.when(pl.program_id(1) == 0)
    def _():
        kb_ref[...] = k_ref[...].astype(BF16)
        vb_ref[...] = v_ref[...].astype(BF16)

    for h in range(XA_HEADS):
        sl = slice(h * XA_HD, (h + 1) * XA_HD)
        s = _dot_nt(q_ref[:, sl].astype(BF16), kb_ref[:, sl]) * (XA_HD ** -0.5)
        e = jnp.exp(s - jnp.max(s, axis=-1, keepdims=True))
        p = e / jnp.sum(e, axis=-1, keepdims=True)
        o_ref[:, sl] = _dot(p.astype(BF16), vb_ref[:, sl]).astype(o_ref.dtype)


def _xattn_prompt(q, mk, mv, bp, seq):
    tq = _pick(seq, (512, 256, 128))
    nt = seq // tq
    kv = pl.BlockSpec((N_MEM, D_MODEL), lambda b, t: (b, 0))
    return pl.pallas_call(
        _xattn_prompt_body,
        grid=(bp, nt),
        in_specs=[pl.BlockSpec((tq, D_MODEL), lambda b, t: (b * nt + t, 0)), kv, kv],
        out_specs=pl.BlockSpec((tq, D_MODEL), lambda b, t: (b * nt + t, 0)),
        out_shape=jax.ShapeDtypeStruct((q.shape[0], D_MODEL), BF16),
        scratch_shapes=[pltpu.VMEM((N_MEM, D_MODEL), BF16), pltpu.VMEM((N_MEM, D_MODEL), BF16)],
        compiler_params=_cparams("parallel", "arbitrary"),
        name="xattn_prompt",
    )(q, mk, mv)


XA_SAMPLES_PER_STEP = 2


def _xattn_sample_body(q_ref, k_ref, v_ref, obuf_ref, o_ref, acc_ref, *, nsteps):
    del obuf_ref
    i = pl.program_id(0)
    H = XA_HEADS
    groups = N_MEM * H // (2 * H)
    for j in range(XA_SAMPLES_PER_STEP):
        b = i * XA_SAMPLES_PER_STEP + j
        k3 = k_ref[j].reshape(N_MEM * H, XA_HD).reshape(groups, 2 * H, XA_HD)
        s = jnp.sum(k3 * q_ref[b], axis=-1, keepdims=True) * (XA_HD ** -0.5)
        mx = jnp.max(s, axis=0, keepdims=True)
        mx = jnp.maximum(mx, pltpu.roll(mx, H, 1))
        e = jnp.exp(s - mx)
        den = jnp.sum(e, axis=0, keepdims=True)
        den = den + pltpu.roll(den, H, 1)
        v3 = v_ref[j].reshape(N_MEM * H, XA_HD).reshape(groups, 2 * H, XA_HD)
        o8 = jnp.sum((e / den) * v3, axis=0)
        o = o8[0:H] + o8[H:2 * H]
        acc_ref[pl.ds(b, 1), :] = jnp.concatenate([o[h:h + 1, :] for h in range(H)], axis=1)

    @pl.when(i == nsteps - 1)
    def _():
        o_ref[...] = acc_ref[...].astype(o_ref.dtype)


def _xattn_sample(q, ck, cv, obuf, l, row0):
    bs = q.shape[0]
    nb = XA_SAMPLES_PER_STEP
    kv = pl.BlockSpec((None, nb, N_MEM, XA_HEADS, XA_HD), lambda i: (l, i, 0, 0, 0))
    return pl.pallas_call(
        functools.partial(_xattn_sample_body, nsteps=bs // nb),
        grid=(bs // nb,),
        in_specs=[pl.BlockSpec((bs, 2 * XA_HEADS, XA_HD), lambda i: (0, 0, 0)), kv, kv, ANY_SPEC],
        out_specs=pl.BlockSpec((bs, D_MODEL), lambda i: (row0 // bs, 0)),
        out_shape=jax.ShapeDtypeStruct(obuf.shape, obuf.dtype),
        scratch_shapes=[pltpu.VMEM((bs, D_MODEL), F32)],
        input_output_aliases={3: 0},
        compiler_params=_cparams("arbitrary"),
        name="xattn_sample",
    )(q, ck, cv, obuf)


def _alias_args(prev):
    return ([], []) if prev is None else ([prev], [ANY_SPEC])


def _sample_ad_body(*refs, has_prev):
    (ab_ref, ac_ref, ah_ref, sta_ref, wa_ref, d1a_ref, d1b_ref, d2a_ref, d2b_ref, std_ref, wd_ref, bias_ref, g_ref,
     beta_ref) = refs[:14]
    ya_ref, yd_ref, na_ref, nd_ref = refs[14 + 2 * has_prev:]
    u = ac_ref[...] * ah_ref[...]
    s0 = sta_ref[:, 0, :]
    s1 = sta_ref[:, 1, :]
    ya = ab_ref[...] * (wa_ref[0:1, :] * s0 + wa_ref[1:2, :] * s1 + wa_ref[2:3, :] * u)
    ya_ref[...] = ya.astype(ya_ref.dtype)
    na_ref[:, 0, :] = s1
    na_ref[:, 1, :] = u
    d1 = jnp.concatenate([d1a_ref[...], d1b_ref[...]], axis=1)
    d2 = jnp.concatenate([d2a_ref[...], d2b_ref[...]], axis=1)
    ud = d1 * _sigmoid(d2)
    acc = jnp.broadcast_to(bias_ref[...], ud.shape)
    nh = D_CONV - 1
    for k in range(nh):
        s_k = std_ref[k]
        acc = acc + s_k * wd_ref[k:k + 1, :]
        if k > 0:
            nd_ref[k - 1] = s_k
    acc = acc + ud * wd_ref[nh:nh + 1, :]
    nd_ref[nh - 1] = ud
    c = _layernorm(acc, EPS) * g_ref[...] + beta_ref[...]
    yd_ref[...] = (c * _sigmoid(c)).astype(yd_ref.dtype)


def _sample_ad(proj, st_a, st_d, wa, wd, bias, g, beta, l, row0, prev_a, prev_d):
    bs = st_a.shape[1]
    G = G_WIDTH
    tb = 32
    r0 = row0 // tb
    half = G // 2
    cola = lambda j: pl.BlockSpec((tb, G), lambda i: (r0 + i, COL_A // G + j))
    cold = lambda j: pl.BlockSpec((tb, half), lambda i: (r0 + i, COL_D // half + j))
    sta = pl.BlockSpec((None, tb, A_CONV - 1, G), lambda i: (l, i, 0, 0))
    std = pl.BlockSpec((None, D_CONV - 1, tb, G), lambda i: (l, 0, i, 0))
    yrow = pl.BlockSpec((tb, G), lambda i: (i, 0))
    (bias, bias_s), (g, g_s), (beta, beta_s) = _layer_vec(bias, l), _layer_vec(g, l), _layer_vec(beta, l)
    nh = D_CONV - 1
    has_prev = prev_a is not None
    extra = [prev_a, prev_d] if has_prev else []
    return pl.pallas_call(
        functools.partial(_sample_ad_body, has_prev=has_prev),
        grid=(bs // tb,),
        in_specs=[cola(0), cola(1), cola(2), sta, _layer_mat(wa, l), cold(0), cold(1), cold(2), cold(3), std,
                  _layer_mat(wd, l), bias_s, g_s, beta_s] + [ANY_SPEC] * len(extra),
        out_specs=[yrow, yrow, sta, std],
        out_shape=[jax.ShapeDtypeStruct((bs, G), BF16), jax.ShapeDtypeStruct((bs, G), BF16),
                   jax.ShapeDtypeStruct(st_a.shape, F32), jax.ShapeDtypeStruct(st_d.shape, F32)],
        input_output_aliases={14: 2, 15: 3} if has_prev else {},
        compiler_params=_cparams("parallel"),
        name="sample_conv_ad",
    )(proj, proj, proj, st_a, wa, proj, proj, proj, proj, st_d, wd, bias, g, beta, *extra)


RET_D_BLOCK = 16


def _ret_sample_body(*refs, has_prev):
    q_ref, k_ref, v_ref, g_ref, cos_ref, sin_ref, s_ref = refs[:7]
    y_ref, ns_ref, qt_ref, kt_ref, acc_ref = refs[7 + has_prev:]
    h = pl.program_id(0)
    j = pl.program_id(1)
    nd = RET_D // RET_D_BLOCK
    gammas = [math.exp(lg) for lg in RET_LOG_DECAY]
    gamma = jnp.where(h == 0, gammas[0], jnp.where(h == 1, gammas[1], jnp.where(h == 2, gammas[2], gammas[3])))
    gamma = gamma.astype(F32)

    @pl.when(j == 0)
    def _():
        q = _rope(q_ref[...], cos_ref[...], sin_ref[...])
        k = _rope(k_ref[...], cos_ref[...], sin_ref[...]) * (RET_D ** -0.5)
        qt_ref[...] = q.T
        kt_ref[...] = k.T
        acc_ref[...] = jnp.zeros_like(acc_ref)

    vt = v_ref[...].T
    acc = acc_ref[...]
    for d in range(RET_D_BLOCK):
        s_d = s_ref[:, d, :].T
        row = j * RET_D_BLOCK + d
        q_d = qt_ref[pl.ds(row, 1), :]
        k_d = kt_ref[pl.ds(row, 1), :]
        acc = acc + q_d * s_d
        ns_ref[:, d, :] = (gamma * s_d + k_d * vt).T
    acc_ref[...] = acc

    @pl.when(j == nd - 1)
    def _():
        qk = jnp.sum(qt_ref[...] * kt_ref[...], axis=0, keepdims=True)
        o = qk * vt + acc * gamma
        mu = jnp.mean(o, axis=0, keepdims=True)
        oc = o - mu
        var = jnp.mean(oc * oc, axis=0, keepdims=True)
        ln = (oc * lax.rsqrt(var + EPS)).T
        g = g_ref[...]
        y_ref[...] = (g * _sigmoid(g) * ln).astype(y_ref.dtype)


def _retention_sample(proj, cos2, sin2, state, l, row0, prev):
    bs = state.shape[1]
    r0 = row0 // bs
    nd = RET_D // RET_D_BLOCK
    col = lambda c: pl.BlockSpec((bs, RET_D), lambda h, j: (r0, COL_B // RET_D + c * RET_HEADS + h))
    tab = pl.BlockSpec((1, RET_D), lambda h, j: (0, 0))
    sblk = pl.BlockSpec((None, bs, None, RET_D_BLOCK, RET_D), lambda h, j: (l, 0, h, j, 0))
    extra, extra_specs = _alias_args(prev)
    return pl.pallas_call(
        functools.partial(_ret_sample_body, has_prev=prev is not None),
        grid=(RET_HEADS, nd),
        in_specs=[col(0), col(1), col(2), col(3), tab, tab, sblk] + extra_specs,
        out_specs=[pl.BlockSpec((bs, RET_D), lambda h, j: (0, h)), sblk],
        out_shape=[jax.ShapeDtypeStruct((bs, G_WIDTH), BF16), jax.ShapeDtypeStruct(state.shape, F32)],
        scratch_shapes=[pltpu.VMEM((RET_D, bs), F32), pltpu.VMEM((RET_D, bs), F32), pltpu.VMEM((RET_D, bs), F32)],
        input_output_aliases={7: 1} if prev is not None else {},
        compiler_params=_cparams("parallel", "arbitrary"),
        name="retention_sample",
    )(proj, proj, proj, proj, cos2, sin2, state, *extra)


def _rwkv_sample_body(*refs, has_prev):
    (xr_ref, xk_ref, xv_ref, xl_ref, pr_ref, pk_ref, pv_ref, pl_ref, mr_ref, mk_ref, mv_ref, ml_ref,
     w2w_ref, w2a_ref, g2_ref, w0_ref, a0_ref, kk_ref, ka_ref, rk_ref, lg_ref, lb_ref, s_ref) = refs[:23]
    y_ref, ns_ref, vec_ref, ot_ref = refs[23 + has_prev:]
    N = RWKV_HEAD

    def shifted(x_ref, prev_ref, mu_ref):
        x = x_ref[...]
        return x + (prev_ref[...] - x) * mu_ref[...]

    r, lw, k4, v, kk, a, gate = _rwkv_pre(shifted(xr_ref, pr_ref, mr_ref), shifted(xk_ref, pk_ref, mk_ref),
                                          shifted(xv_ref, pv_ref, mv_ref), shifted(xl_ref, pl_ref, ml_ref),
                                          w2w_ref[...], w2a_ref[...], g2_ref[...], w0_ref[...], a0_ref[...],
                                          kk_ref[...], ka_ref[...])
    kkn = jnp.concatenate([_l2_normalize(kk[:, 0:N]), _l2_normalize(kk[:, N:2 * N])], axis=1)
    w = jnp.exp(lw)
    vec_ref[0] = kkn.T
    vec_ref[1] = w.T
    vec_ref[2] = (kkn * a).T
    vec_ref[3] = k4.T
    vec_ref[4] = r.T
    vec_ref[5] = v.T
    for hh in range(2):
        ch = slice(hh * N, (hh + 1) * N)
        kk_t = vec_ref[0, ch, :]
        w_t = vec_ref[1, ch, :]
        kka_t = vec_ref[2, ch, :]
        k_t = vec_ref[3, ch, :]
        r_t = vec_ref[4, ch, :]
        for i in range(N):
            s_i = s_ref[hh, i]
            sa = -jnp.sum(s_i * kk_t, axis=0, keepdims=True)
            v_i = vec_ref[5, hh * N + i:hh * N + i + 1, :]
            s_new = s_i * w_t + sa * kka_t + v_i * k_t
            ns_ref[hh, i] = s_new
            ot_ref[hh * N + i:hh * N + i + 1, :] = jnp.sum(s_new * r_t, axis=0, keepdims=True)
    o = ot_ref[...].T
    rk = rk_ref[...]
    lg = lg_ref[...]
    lb = lb_ref[...]
    ys = []
    for hh in range(2):
        ch = slice(hh * N, (hh + 1) * N)
        ys.append(_rwkv_post(o[:, ch], r[:, ch], k4[:, ch], v[:, ch], gate[:, ch], rk[:, ch], lg[:, ch], lb[:, ch]))
    y_ref[...] = jnp.concatenate(ys, axis=1).astype(y_ref.dtype)


def _rwkv_sample(proj, prev_shift, rw_arrs, state, l, row0, prev):
    mu, w2a, g2, w0, a0, k_k, k_a, r_k, lnx_g, lnx_b = rw_arrs
    bs = state.shape[-1]
    r0 = row0 // bs
    G = G_WIDTH
    N = RWKV_HEAD
    W = 2 * N
    npair = G // W
    def cols(shape_lead, lead_idx, base):
        mk = lambda width, cidx: pl.BlockSpec(shape_lead + (width,), lambda p: lead_idx + (cidx(p),))
        return [mk(W, lambda p, c=c: base // W + c * npair + p) for c in range(3)] + [
            mk(2 * W, lambda p: (base + 3 * G) // (2 * W))]
    pairv = pl.BlockSpec((None, 1, W), lambda p: (l, 0, p))
    pairm = pl.BlockSpec((None, 128, W), lambda p: (l, 0, p))
    sblk = pl.BlockSpec((None, 2, N, N, bs), lambda p: (l, p, 0, 0, 0))
    extra, extra_specs = _alias_args(prev)
    return pl.pallas_call(
        functools.partial(_rwkv_sample_body, has_prev=prev is not None),
        grid=(npair,),
        in_specs=cols((bs,), (r0,), COL_C) + cols((None, bs), (l, 0), 0) + cols((None, 1), (l, 0), 0) + [
            pairm, pl.BlockSpec((None, 128, W), lambda p: (l, 0, npair + p)), pairm,
            pairv, pairv, pairv, pairv, pairv, pairv, pairv, sblk] + extra_specs,
        out_specs=[pl.BlockSpec((bs, W), lambda p: (0, p)), sblk],
        out_shape=[jax.ShapeDtypeStruct((bs, G), BF16), jax.ShapeDtypeStruct(state.shape, F32)],
        scratch_shapes=[pltpu.VMEM((6, W, bs), F32), pltpu.VMEM((W, bs), F32)],
        input_output_aliases={23: 1} if prev is not None else {},
        compiler_params=_cparams("parallel"),
        name="rwkv_sample",
    )(proj, proj, proj, proj, prev_shift, prev_shift, prev_shift, prev_shift, mu, mu, mu, mu, w2a, w2a, g2,
      w0, a0, k_k, k_a, r_k, lnx_g, lnx_b, state, *extra)


def _place_cols_body(src_ref, buf_ref, o_ref):
    del buf_ref
    o_ref[...] = src_ref[...]


def _place_cols(src, buf, col):
    rows, w = src.shape
    tr = _pick(rows, (1024, 512, 256, 128))
    return pl.pallas_call(
        _place_cols_body,
        grid=(rows // tr,),
        in_specs=[pl.BlockSpec((tr, w), lambda i: (i, 0)), ANY_SPEC],
        out_specs=pl.BlockSpec((tr, w), lambda i: (i, col)),
        out_shape=jax.ShapeDtypeStruct(buf.shape, buf.dtype),
        input_output_aliases={1: 0},
        compiler_params=_cparams("parallel"),
        name="place_cols",
    )(src, buf)


def _place_heads_body(*refs):
    src_ref, o_ref = refs[0], refs[-1]
    for h in range(XA_HEADS):
        o_ref[0, :, h, :] = src_ref[:, h * XA_HD:(h + 1) * XA_HD]


def _place_heads(src, l, depth, bp, prev):
    extra, extra_specs = _alias_args(prev)
    return pl.pallas_call(
        _place_heads_body,
        grid=(bp,),
        in_specs=[pl.BlockSpec((N_MEM, D_MODEL), lambda b: (b, 0))] + extra_specs,
        out_specs=pl.BlockSpec((None, 1, N_MEM, XA_HEADS, XA_HD), lambda b: (l, b, 0, 0, 0)),
        out_shape=jax.ShapeDtypeStruct((depth, bp, N_MEM, XA_HEADS, XA_HD), F32),
        input_output_aliases={1: 0} if prev is not None else {},
        compiler_params=_cparams("parallel"),
        name="place_heads",
    )(src, *extra)


def _assemble_body(a_ref, b_ref, c_ref, d_ref, buf_ref, o_ref):
    del buf_ref
    o_ref[...] = jnp.concatenate([a_ref[...], b_ref[...], c_ref[...], d_ref[...]], axis=1)


def _assemble_rows(parts, buf, row0):
    bs = parts[0].shape[0]
    part = pl.BlockSpec((bs, G_WIDTH), lambda i: (0, 0))
    return pl.pallas_call(
        _assemble_body,
        grid=(1,),
        in_specs=[part, part, part, part, ANY_SPEC],
        out_specs=pl.BlockSpec((bs, D_MODEL), lambda i: (row0 // bs, 0)),
        out_shape=jax.ShapeDtypeStruct(buf.shape, buf.dtype),
        input_output_aliases={4: 0},
        compiler_params=_cparams("arbitrary"),
        name="assemble_sample_rows",
    )(*parts, buf)


def kernel(x_prompt, x_sample, mem_prompt, state_conv_a, state_ret, state_shift, state_wkv, state_conv_d, cache_mem_k, cache_mem_v, g_mix, w_in, conv_a_w, mu_c, w0, w2, a0, a2, g2, k_k, k_a, r_k, lnx_g, lnx_b, conv_d_w, conv_d_b, ln_d_g, ln_d_b, w_out, g_xa, g_mem, wq_x, wk_x, wv_x, wo_x, g_mlp, w_up, w_down, g_final):
    bp, seq, d = x_prompt.shape
    bs = x_sample.shape[0]
    depth = w_in.shape[0]
    np_rows = bp * seq
    G = G_WIDTH
    x = jnp.concatenate([x_prompt.reshape(np_rows, d), x_sample.reshape(bs, d)], axis=0)
    mem = mem_prompt.reshape(bp * N_MEM, d)
    cos_p, sin_p = _rope_tables(jnp.arange(seq))
    cos_s, sin_s = _rope_tables(PAST_LEN + jnp.arange(1))

    w_down_b = w_down.astype(BF16)
    z64 = jnp.zeros((depth, 64, G), F32)
    w2a = jnp.concatenate([jnp.concatenate([w2, z64], axis=2), jnp.concatenate([z64, a2], axis=2)], axis=1).astype(BF16)
    g2_b = g2.astype(BF16)
    wkv_t = jnp.transpose(state_wkv, (0, 2, 3, 4, 1))
    conv_d_t = jnp.transpose(state_conv_d, (0, 2, 1, 3))
    vec3 = lambda a: a.reshape(depth, 1, -1)
    rw_arrs = (vec3(mu_c), w2a, g2_b, vec3(w0), vec3(a0), vec3(k_k), vec3(k_a), vec3(r_k), vec3(lnx_g), vec3(lnx_b))

    small = {k: [] for k in ("a_p", "ret_p", "sh_p", "sh_s", "wkv_p", "d_p")}
    na_s = nd_s = nret_s = nwkv_s = mk_all = mv_all = None
    for l in range(depth):
        proj = _matmul_ws(x, w_in, l, norm_g=_layer_vec(g_mix, l), tn=1280, tm=_pick(x.shape[0], (640, 512, 256, 128)))
        ymix, na_p = _conv_a_prompt(proj, conv_a_w, l, bp, seq)
        ymix, nd_p = _conv_d_prompt(proj, conv_d_w, conv_d_b, ln_d_g, ln_d_b, ymix, l, bp, seq)
        ymix, nret_p = _retention_prompt(proj, cos_p, sin_p, ymix, bp, seq)
        rw = [(rw_arrs[0], pl.BlockSpec((None, 1, SHIFT_W), lambda *_: (l, 0, 0))),
              (w2a, _layer_mat(w2a, l)), (g2_b, _layer_mat(g2_b, l))] + [
              (a, pl.BlockSpec((None, 1, G), lambda *_: (l, 0, 0))) for a in rw_arrs[3:]]
        yc_p, nsh_p, nwkv_p = _rwkv_prompt(proj, rw, bp, seq)
        ymix = _place_cols(yc_p.reshape(np_rows, G), ymix, 2)
        ya_s, yd_s, na_s, nd_s = _sample_ad(proj, state_conv_a, conv_d_t, conv_a_w, conv_d_w, conv_d_b, ln_d_g, ln_d_b,
                                            l, np_rows, na_s, nd_s)
        yb_s, nret_s = _retention_sample(proj, cos_s, sin_s, state_ret, l, np_rows, nret_s)
        yc_s, nwkv_s = _rwkv_sample(proj, state_shift, rw_arrs, wkv_t, l, np_rows, nwkv_s)
        ymix = _assemble_rows([ya_s, yb_s, yc_s, yd_s], ymix, np_rows)
        x = _matmul_ws(ymix, w_out, l, residual=x)

        mk = _matmul_ws(mem, wk_x, l, norm_g=_layer_vec(g_mem, l))
        mv = _matmul_ws(mem, wv_x, l, norm_g=_layer_vec(g_mem, l))
        mk_all = _place_heads(mk, l, depth, bp, mk_all)
        mv_all = _place_heads(mv, l, depth, bp, mv_all)
        q = _matmul_ws(x, wq_x, l, norm_g=_layer_vec(g_xa, l))
        obuf = _xattn_prompt(q, mk, mv, bp, seq)
        q_s = q[np_rows:].reshape(bs, XA_HEADS, XA_HD)
        q_s = jnp.concatenate([q_s, q_s], axis=1)
        obuf = _xattn_sample(q_s, cache_mem_k, cache_mem_v, obuf, l, np_rows)
        x = _matmul_ws(obuf, wo_x, l, residual=x)

        up = _matmul_ws(x, w_up, l, norm_g=_layer_vec(g_mlp, l), act="relu2", out_dtype=BF16)
        x = _matmul(up, w_down_b, l, residual=x)

        small["a_p"].append(na_p)
        small["ret_p"].append(nret_p)
        small["sh_p"].append(nsh_p.reshape(bp, SHIFT_W))
        small["sh_s"].append(proj[np_rows:, COL_C:COL_C + SHIFT_W])
        small["wkv_p"].append(nwkv_p)
        small["d_p"].append(nd_p)
    gf = (g_final.reshape(1, d), pl.BlockSpec((1, d), lambda *_: (0, 0)))
    y_p = _rmsnorm(x, gf, F32, 0, np_rows)
    y_s = _rmsnorm(x, gf, F32, np_rows, bs)
    st = lambda k: jnp.stack(small[k])
    return (y_p.reshape(bp, seq, d), y_s.reshape(bs, 1, d), st("a_p"), na_s, st("ret_p"), nret_s,
            st("sh_p"), st("sh_s"), st("wkv_p"), jnp.transpose(nwkv_s, (0, 4, 1, 2, 3)),
            st("d_p"), jnp.transpose(nd_s, (0, 2, 1, 3)), mk_all, mv_all)
```

```python
import functools
import math

import jax
import jax.numpy as jnp
from jax import lax
from jax.experimental import pallas as pl
from jax.experimental.pallas import tpu as pltpu

F32 = jnp.float32
BF16 = jnp.bfloat16
HI = lax.Precision.HIGHEST

D_MODEL = 2048
G_WIDTH = 512
P_IN = 6400
RET_HEADS = 4
RET_D = 128
RET_CHUNK = 128
ROPE_BASE = 10000.0
RWKV_HEAD = 64
RWKV_HEADS = 8
WKV_CHUNK = 64
RWKV_GN_EPS = 64e-5
A_CONV = 3
D_CONV = 31
D_HIST = 32
N_MEM = 256
XA_HEADS = 4
XA_HD = 512
SHIFT_W = 1792
PAST_LEN = 16384
EPS = 1e-6
RET_LOG_DECAY = tuple(math.log1p(-(2.0 ** (-5.0 - h))) for h in range(RET_HEADS))
VMEM_LIMIT = 56 * 1024 * 1024
COL_A = 0
COL_B = 3 * G_WIDTH
COL_C = 7 * G_WIDTH
COL_D = 7 * G_WIDTH + SHIFT_W

ANY_SPEC = pl.BlockSpec(memory_space=pl.ANY)


def _cparams(*sem):
    return pltpu.CompilerParams(dimension_semantics=sem, vmem_limit_bytes=VMEM_LIMIT)


def _pick(n, cands):
    for c in cands:
        if n % c == 0:
            return c
    raise ValueError(f"no tile for {n}")


def _sigmoid(x):
    return 1.0 / (1.0 + jnp.exp(-x))


def _dot(a, b, precision=None):
    return jnp.dot(a, b, preferred_element_type=F32, precision=precision)


def _dot_nt(a, b, precision=None):
    return lax.dot_general(a, b, (((1,), (1,)), ((), ())), preferred_element_type=F32, precision=precision)


def _dot_tn(a, b, precision=None):
    return lax.dot_general(a, b, (((0,), (0,)), ((), ())), preferred_element_type=F32, precision=precision)


def _layernorm(x, eps):
    mu = jnp.mean(x, axis=-1, keepdims=True)
    xc = x - mu
    var = jnp.mean(xc * xc, axis=-1, keepdims=True)
    return xc * lax.rsqrt(var + eps)


def _layer_vec(a, l):
    n = a.shape[-1]
    return a.reshape(a.shape[0], 1, n), pl.BlockSpec((None, 1, n), lambda *_: (l, 0, 0))


def _layer_mat(a, l):
    return pl.BlockSpec((None,) + a.shape[1:], lambda *_: (l, 0, 0))


def _rmsnorm_body(x_ref, g_ref, o_ref):
    x = x_ref[...]
    ms = jnp.mean(x * x, axis=-1, keepdims=True)
    o_ref[...] = ((x * lax.rsqrt(ms + EPS)) * g_ref[...]).astype(o_ref.dtype)


def _rmsnorm(x, g_spec, out_dtype, row0=0, nrows=None):
    m, d = x.shape
    nrows = nrows or m
    tm = _pick(math.gcd(nrows, row0) if row0 else nrows, (1040, 1024, 640, 512, 256, 128))
    r0 = row0 // tm
    g, gspec = g_spec
    return pl.pallas_call(
        _rmsnorm_body,
        grid=(nrows // tm,),
        in_specs=[pl.BlockSpec((tm, d), lambda i: (r0 + i, 0)), gspec],
        out_specs=pl.BlockSpec((tm, d), lambda i: (i, 0)),
        out_shape=jax.ShapeDtypeStruct((nrows, d), out_dtype),
        compiler_params=_cparams("parallel"),
        name="rmsnorm",
    )(x, g)


def _mm_ws_body(*refs, act, has_res, has_norm):
    a_ref, w_ref = refs[0], refs[1]
    g_ref = refs[2] if has_norm else None
    res_ref = refs[2 + has_norm] if has_res else None
    o_ref = refs[2 + has_norm + has_res]
    wb_ref = refs[-1]

    @pl.when(pl.program_id(1) == 0)
    def _():
        wb_ref[...] = w_ref[...].astype(BF16)

    a = a_ref[...]
    if has_norm:
        ms = jnp.mean(a * a, axis=-1, keepdims=True)
        a = ((a * lax.rsqrt(ms + EPS)) * g_ref[...]).astype(BF16)
    r = _dot(a, wb_ref[...])
    if act == "relu2":
        r = jnp.square(jnp.maximum(r, 0.0))
    if has_res:
        r = res_ref[...] + r
    o_ref[...] = r.astype(o_ref.dtype)


def _matmul_ws(a, w, l, *, norm_g=None, residual=None, act=None, out_dtype=F32, tn=1024, tm=None, kblock=None):
    m = a.shape[0]
    kb, kdim = kblock if kblock is not None else (0, a.shape[1])
    n = w.shape[2]
    tm = tm or _pick(m, (1040, 1024, 640, 512, 256, 128))
    has_res = residual is not None
    has_norm = norm_g is not None
    in_specs = [pl.BlockSpec((tm, kdim), lambda j, i: (i, kb)), pl.BlockSpec((None, kdim, tn), lambda j, i: (l, kb, j))]
    args = [a, w]
    if has_norm:
        args.append(norm_g[0])
        in_specs.append(norm_g[1])
    if has_res:
        in_specs.append(pl.BlockSpec((tm, tn), lambda j, i: (i, j)))
        args.append(residual)
    return pl.pallas_call(
        functools.partial(_mm_ws_body, act=act, has_res=has_res, has_norm=has_norm),
        grid=(n // tn, m // tm),
        in_specs=in_specs,
        out_specs=pl.BlockSpec((tm, tn), lambda j, i: (i, j)),
        out_shape=jax.ShapeDtypeStruct((m, n), out_dtype),
        scratch_shapes=[pltpu.VMEM((kdim, tn), BF16)],
        compiler_params=_cparams("parallel", "arbitrary"),
        name="matmul_ws",
    )(*args)


def _conv_a_body(b_ref, c_ref, h_ref, w_ref, y_ref, st_ref, ext_ref, *, tl):
    t = pl.program_id(1)

    @pl.when(t == 0)
    def _():
        ext_ref[0:8, :] = jnp.zeros((8, G_WIDTH), F32)

    u = c_ref[...] * h_ref[...]
    ext_ref[8:, :] = u
    u1 = ext_ref[7:7 + tl, :]
    u2 = ext_ref[6:6 + tl, :]
    y = b_ref[...] * (w_ref[0:1, :] * u2 + w_ref[1:2, :] * u1 + w_ref[2:3, :] * u)
    y_ref[...] = y.astype(y_ref.dtype)
    st_ref[0] = ext_ref[tl + 6:tl + 8, :]
    ext_ref[0:8, :] = ext_ref[tl:tl + 8, :]


def _conv_a_prompt(proj, conv_w, l, bp, seq):
    tl = _pick(seq, (512, 256, 128))
    nt = seq // tl
    c0 = COL_A // G_WIDTH
    col = lambda j: pl.BlockSpec((tl, G_WIDTH), lambda b, t: (b * nt + t, c0 + j))
    return pl.pallas_call(
        functools.partial(_conv_a_body, tl=tl),
        grid=(bp, nt),
        in_specs=[col(0), col(1), col(2), _layer_mat(conv_w, l)],
        out_specs=[pl.BlockSpec((tl, G_WIDTH), lambda b, t: (b * nt + t, 0)),
                   pl.BlockSpec((1, A_CONV - 1, G_WIDTH), lambda b, t: (b, 0, 0))],
        out_shape=[jax.ShapeDtypeStruct((proj.shape[0], D_MODEL), BF16),
                   jax.ShapeDtypeStruct((bp, A_CONV - 1, G_WIDTH), F32)],
        scratch_shapes=[pltpu.VMEM((tl + 8, G_WIDTH), F32)],
        compiler_params=_cparams("parallel", "arbitrary"),
        name="conv_a_prompt",
    )(proj, proj, proj, conv_w)


CONV_D_ROWS = 32


def _conv_d_body(d1a_ref, d1b_ref, d2a_ref, d2b_ref, w_ref, bias_ref, g_ref, beta_ref, ymix_ref, y_ref, st_ref, ext_ref,
                 acc_ref, *, tl):
    del ymix_ref
    t = pl.program_id(1)
    next_ = D_HIST + tl

    @pl.when(t == 0)
    def _():
        ext_ref[0, 0:D_HIST, :] = jnp.zeros((D_HIST, G_WIDTH), F32)

    d1 = jnp.concatenate([d1a_ref[...], d1b_ref[...]], axis=1)
    d2 = jnp.concatenate([d2a_ref[...], d2b_ref[...]], axis=1)
    ext_ref[0, D_HIST:, :] = d1 * _sigmoid(d2)
    ext = ext_ref[0]
    for s in range(1, 8):
        ext_ref[s] = pltpu.roll(ext, next_ - s, 0)
    first = D_HIST - (D_CONV - 1)

    def blk(i, carry):
        base = pl.multiple_of(i * CONV_D_ROWS, CONV_D_ROWS)
        acc = jnp.broadcast_to(bias_ref[...], (CONV_D_ROWS, G_WIDTH))
        for k in range(D_CONV):
            s = (first + k) % 8
            a0 = first + k - s
            acc = acc + ext_ref[s, pl.ds(base + a0, CONV_D_ROWS), :] * w_ref[k:k + 1, :]
        acc_ref[pl.ds(base, CONV_D_ROWS), :] = acc
        return carry

    lax.fori_loop(0, tl // CONV_D_ROWS, blk, 0)
    c = _layernorm(acc_ref[...], EPS) * g_ref[...] + beta_ref[...]
    y_ref[...] = (c * _sigmoid(c)).astype(y_ref.dtype)
    st_ref[0] = ext_ref[0, tl + first:tl + D_HIST, :]
    ext_ref[0, 0:D_HIST, :] = ext_ref[0, tl:tl + D_HIST, :]


def _conv_d_prompt(proj, w, bias, g, beta, ymix, l, bp, seq):
    tl = _pick(seq, (512, 256, 128))
    nt = seq // tl
    half = G_WIDTH // 2
    c0 = COL_D // half
    col = lambda j: pl.BlockSpec((tl, half), lambda b, t: (b * nt + t, c0 + j))
    (bias, bias_s), (g, g_s), (beta, beta_s) = _layer_vec(bias, l), _layer_vec(g, l), _layer_vec(beta, l)
    return pl.pallas_call(
        functools.partial(_conv_d_body, tl=tl),
        grid=(bp, nt),
        in_specs=[col(0), col(1), col(2), col(3), _layer_mat(w, l), bias_s, g_s, beta_s, ANY_SPEC],
        out_specs=[pl.BlockSpec((tl, G_WIDTH), lambda b, t: (b * nt + t, 3)),
                   pl.BlockSpec((1, D_CONV - 1, G_WIDTH), lambda b, t: (b, 0, 0))],
        out_shape=[jax.ShapeDtypeStruct(ymix.shape, ymix.dtype),
                   jax.ShapeDtypeStruct((bp, D_CONV - 1, G_WIDTH), F32)],
        scratch_shapes=[pltpu.VMEM((8, tl + D_HIST, G_WIDTH), F32), pltpu.VMEM((tl, G_WIDTH), F32)],
        input_output_aliases={8: 0},
        compiler_params=_cparams("parallel", "arbitrary"),
        name="conv_d_prompt",
    )(proj, proj, proj, proj, w, bias, g, beta, ymix)


def _rope_tables(pos):
    inv = ROPE_BASE ** (-jnp.arange(0, RET_D, 2, dtype=F32) / RET_D)
    ang = pos.astype(F32)[:, None] * inv[None, :]
    cos = jnp.cos(ang)
    sin = jnp.sin(ang)
    return jnp.concatenate([cos, cos], axis=-1), jnp.concatenate([-sin, sin], axis=-1)


def _rope(x, cos2, sin2):
    return x * cos2 + pltpu.roll(x, RET_D // 2, 1) * sin2


def _ret_body(*refs, nb, nchunks):
    qkvg = refs[:4 * nb]
    cos_ref, sin_ref, y_ref, st_ref, s_ref = refs[4 * nb:]
    c = pl.program_id(0)
    C = RET_CHUNK

    @pl.when(c == 0)
    def _():
        s_ref[...] = jnp.zeros_like(s_ref)

    ii = lax.broadcasted_iota(jnp.int32, (C, C), 0).astype(F32)
    jj = lax.broadcasted_iota(jnp.int32, (C, C), 1).astype(F32)
    diff = ii - jj
    cos2 = cos_ref[...]
    sin2 = sin_ref[...]
    NH = RET_HEADS
    chains = [(b, h) for b in range(nb) for h in range(NH)]
    I = range(len(chains))
    sl = [slice(h * RET_D, (h + 1) * RET_D) for h in range(NH)]
    lg = RET_LOG_DECAY
    q = [_rope(qkvg[4 * b][:, sl[h]], cos2, sin2) for b, h in chains]
    k = [_rope(qkvg[4 * b + 1][:, sl[h]], cos2, sin2) * (RET_D ** -0.5) for b, h in chains]
    qb = [q[i].astype(BF16) for i in I]
    vb = [qkvg[4 * b + 2][:, sl[h]].astype(BF16) for b, h in chains]
    s_old = [s_ref[i] for i in I]
    scores = [_dot_nt(qb[i], k[i].astype(BF16)) for i in I]
    cross = [_dot(qb[i], s_old[i].astype(BF16)) for i in I]
    k_dec = [jnp.exp(lg[h] * (C - 1.0 - ii)) for h in range(NH)]
    kv = [_dot_tn((k[i] * k_dec[chains[i][1]]).astype(BF16), vb[i]) for i in I]
    dmat = [jnp.where(diff >= 0.0, jnp.exp(lg[h] * jnp.maximum(diff, 0.0)), 0.0) for h in range(NH)]
    inner = [_dot((scores[i] * dmat[chains[i][1]]).astype(BF16), vb[i]) for i in I]
    q_dec = [jnp.exp(lg[h] * (ii + 1.0)) for h in range(NH)]
    for i, (b, h) in enumerate(chains):
        s_ref[i] = math.exp(lg[h] * C) * s_old[i] + kv[i]
        o = _layernorm(inner[i] + cross[i] * q_dec[h], EPS)
        g = qkvg[4 * b + 3][:, sl[h]]
        y_ref[b, :, sl[h]] = (g * _sigmoid(g) * o).astype(y_ref.dtype)

    @pl.when(c == nchunks - 1)
    def _():
        for i, (b, h) in enumerate(chains):
            st_ref[b, h] = s_ref[i]


def _retention_prompt(proj, cos2, sin2, bp, seq):
    C = RET_CHUNK
    nc = seq // C
    c0 = COL_B // G_WIDTH
    cols = [pl.BlockSpec((C, G_WIDTH), lambda c, b=b, j=j: (b * nc + c, c0 + j)) for b in range(bp) for j in range(4)]
    tab = pl.BlockSpec((C, RET_D), lambda c: (c, 0))
    return pl.pallas_call(
        functools.partial(_ret_body, nb=bp, nchunks=nc),
        grid=(nc,),
        in_specs=cols + [tab, tab],
        out_specs=[pl.BlockSpec((bp, C, G_WIDTH), lambda c: (0, c, 0)),
                   pl.BlockSpec((bp, RET_HEADS, RET_D, RET_D), lambda c: (0, 0, 0, 0))],
        out_shape=[jax.ShapeDtypeStruct((bp, seq, G_WIDTH), BF16),
                   jax.ShapeDtypeStruct((bp, RET_HEADS, RET_D, RET_D), F32)],
        scratch_shapes=[pltpu.VMEM((bp * RET_HEADS, RET_D, RET_D), F32)],
        compiler_params=_cparams("arbitrary"),
        name="retention_prompt",
    )(*([proj] * (4 * bp)), cos2, sin2)


def _softplus(z):
    return jnp.maximum(z, 0.0) + jnp.log(1.0 + jnp.exp(-jnp.abs(z)))


def _rwkv_pre(r, kc, vc, lora_x, w2_w, w2_a, g2, w0, a0, k_k, k_a):
    lin = lora_x[:, 0:128]
    lane = lax.broadcasted_iota(jnp.int32, lin.shape, 1)
    lin = jnp.where(lane < 64, jnp.tanh(lin), lin).astype(BF16)
    gate = _dot(_sigmoid(lora_x[:, 128:256]).astype(BF16), g2)
    w_pre = -_softplus(-(w0 + _dot(lin, w2_w))) - 0.5
    log_decay = -jnp.exp(w_pre)
    a = _sigmoid(a0 + _dot(lin, w2_a))
    kk = kc * k_k
    k4 = kc * (1.0 + (a - 1.0) * k_a)
    return r, log_decay, k4, vc, kk, a, gate


def _l2_normalize(kk):
    ss = jnp.sum(kk * kk, axis=-1, keepdims=True)
    return kk * jnp.minimum(lax.rsqrt(ss), 1e12)


def _bdot(a, b):
    return _dot(a.astype(BF16), b.astype(BF16))


def _wkv_chunk(r, lw, cum, k, v, kk, a, z, mask2, eye):
    T = WKV_CHUNK
    H = range(len(r))
    g_inc = [jnp.exp(cum[h]) for h in H]
    g_tot = [g_inc[h][T - 1:T, :] for h in H]
    ah = [-(kk[h] * jnp.exp(cum[h] - lw[h])) for h in H]
    rg = [r[h] * g_inc[h] for h in H]
    g_inv = [jnp.exp(-cum[h]) for h in H]
    bd = [kk[h] * a[h] * g_inv[h] for h in H]
    kd = [k[h] * g_inv[h] for h in H]
    ar = [jnp.concatenate([ah[h], rg[h]], axis=0).astype(BF16) for h in H]
    xb = [jnp.where(mask2, _dot_nt(ar[h], bd[h].astype(BF16)), 0.0) for h in H]
    xk = [jnp.where(mask2, _dot_nt(ar[h], kd[h].astype(BF16)), 0.0).astype(BF16) for h in H]
    vb = [v[h].astype(BF16) for h in H]
    xkv = [_dot(xk[h], vb[h]) for h in H]
    acur = [xb[h][0:T] for h in H]
    p = [eye for _ in H]
    for it in range(5):
        x = [_bdot(jnp.concatenate([acur[h], p[h]], axis=0), acur[h]) for h in H]
        acur = [x[h][0:T] for h in H]
        p = [p[h] + x[h][T:2 * T] for h in H]
    x = [_bdot(p[h], acur[h]) for h in H]
    p = [p[h] + x[h] for h in H]
    wub = [_bdot(p[h], jnp.concatenate([ah[h], xkv[h][0:T]], axis=1)).astype(BF16) for h in H]
    qo = [_dot(xb[h][T:2 * T].astype(BF16), wub[h]) for h in H]
    lhs_t = [jnp.concatenate([bd[h] * g_tot[h], kd[h] * g_tot[h]], axis=0).astype(BF16) for h in H]
    zero = jnp.zeros((T, 64), BF16)
    mn = [_dot_tn(lhs_t[h], jnp.concatenate([wub[h], jnp.concatenate([zero, vb[h]], axis=1)], axis=0)) for h in H]
    qm = [jnp.concatenate([rg[h] + qo[h][:, 0:64], mn[h][:, 0:64] + jnp.where(eye > 0.0, g_tot[h], 0.0)], axis=0) for h in H]
    oz = [_bdot(qm[h], z[h]) for h in H]
    o = [oz[h][0:T] + qo[h][:, 64:128] + xkv[h][T:2 * T] for h in H]
    z_new = [oz[h][T:2 * T] + mn[h][:, 64:128] for h in H]
    return o, z_new


def _rwkv_post(o, r, k4, v, gate, rk, lnx_g, lnx_b):
    gn = _layernorm(o, RWKV_GN_EPS) * lnx_g + lnx_b
    bonus = jnp.sum(r * k4 * rk, axis=-1, keepdims=True) * v
    return (gn + bonus) * gate


def _rwkv_prompt_body(*refs, nb, nchunks):
    pc_refs = refs[:nb]
    mu_ref, w2a_ref, g2_ref, w0_ref, a0_ref, kk_ref, ka_ref, rk_ref, lg_ref, lb_ref = refs[nb:nb + 10]
    y_ref, shift_ref, st_ref, carry_ref, z_ref = refs[nb + 10:]
    c = pl.program_id(0)
    T = WKV_CHUNK
    NH = RWKV_HEADS

    @pl.when(c == 0)
    def _():
        carry_ref[...] = jnp.zeros_like(carry_ref)
        z_ref[...] = jnp.zeros_like(z_ref)

    pcs, prevs = [], []
    for b in range(nb):
        pc = pc_refs[b][...]
        row = lax.broadcasted_iota(jnp.int32, pc.shape, 0)
        prevs.append(jnp.where(row == 0, carry_ref[b, 0:1, :], pltpu.roll(pc, 1, 0)))
        carry_ref[b, 0:1, :] = pc[T - 1:T, :]
        shift_ref[b] = pc[T - 1:T, :]
        pcs.append(pc)
    pc = jnp.concatenate(pcs, axis=0)
    xs = pc + (jnp.concatenate(prevs, axis=0) - pc) * mu_ref[...]
    G = G_WIDTH
    r, lw, k4, v, kk, a, gate = _rwkv_pre(xs[:, 0:G], xs[:, G:2 * G], xs[:, 2 * G:3 * G], xs[:, 3 * G:3 * G + 256],
                                          w2a_ref[:, 0:G], w2a_ref[:, G:2 * G], g2_ref[...], w0_ref[...], a0_ref[...],
                                          kk_ref[...], ka_ref[...])
    ti = lax.broadcasted_iota(jnp.int32, (T, T), 0)
    si = lax.broadcasted_iota(jnp.int32, (T, T), 1)
    tril = (ti >= si).astype(F32)
    eye = (ti == si).astype(F32)
    cum = jnp.concatenate([_dot(tril, lw[b * T:(b + 1) * T, :], HI) for b in range(nb)], axis=0)
    t2 = lax.broadcasted_iota(jnp.int32, (2 * T, T), 0)
    s2 = lax.broadcasted_iota(jnp.int32, (2 * T, T), 1)
    mask2 = jnp.where(t2 < T, t2 - 1, t2 - T) >= s2
    heads = lambda x: [x[b * T:(b + 1) * T, h * RWKV_HEAD:(h + 1) * RWKV_HEAD] for b in range(nb) for h in range(NH)]
    r_h, k_h, v_h = heads(r), heads(k4), heads(v)
    o, z_new = _wkv_chunk(r_h, heads(lw), heads(cum), k_h, v_h, [_l2_normalize(x) for x in heads(kk)], heads(a),
                          [z_ref[i] for i in range(nb * NH)], mask2, eye)
    for i in range(nb * NH):
        z_ref[i] = z_new[i]
    gate_h = heads(gate)
    vec_heads = lambda ref: [ref[:, h * RWKV_HEAD:(h + 1) * RWKV_HEAD] for h in range(NH)]
    rk_h, lg_h, lb_h = vec_heads(rk_ref), vec_heads(lg_ref), vec_heads(lb_ref)
    for b in range(nb):
        ys = [_rwkv_post(o[b * NH + h], r_h[b * NH + h], k_h[b * NH + h], v_h[b * NH + h], gate_h[b * NH + h],
                         rk_h[h], lg_h[h], lb_h[h]) for h in range(NH)]
        y_ref[b] = jnp.concatenate(ys, axis=1).astype(y_ref.dtype)

    @pl.when(c == nchunks - 1)
    def _():
        for b in range(nb):
            for h in range(NH):
                st_ref[b, h] = z_ref[b * NH + h].T


def _rwkv_prompt(proj, rw, bp, seq):
    T = WKV_CHUNK
    nc = seq // T
    G = G_WIDTH
    arrs, specs = zip(*rw)
    pc_specs = [pl.BlockSpec((T, SHIFT_W), lambda c, b=b: (b * nc + c, COL_C // SHIFT_W)) for b in range(bp)]
    return pl.pallas_call(
        functools.partial(_rwkv_prompt_body, nb=bp, nchunks=nc),
        grid=(nc,),
        in_specs=pc_specs + list(specs),
        out_specs=[pl.BlockSpec((bp, T, G), lambda c: (0, c, 0)),
                   pl.BlockSpec((bp, 1, SHIFT_W), lambda c: (0, 0, 0)),
                   pl.BlockSpec((bp, RWKV_HEADS, RWKV_HEAD, RWKV_HEAD), lambda c: (0, 0, 0, 0))],
        out_shape=[jax.ShapeDtypeStruct((bp, seq, G), BF16),
                   jax.ShapeDtypeStruct((bp, 1, SHIFT_W), F32),
                   jax.ShapeDtypeStruct((bp, RWKV_HEADS, RWKV_HEAD, RWKV_HEAD), F32)],
        scratch_shapes=[pltpu.VMEM((bp, 8, SHIFT_W), F32), pltpu.VMEM((bp * RWKV_HEADS, RWKV_HEAD, RWKV_HEAD), F32)],
        compiler_params=_cparams("arbitrary"),
        name="rwkv_prompt",
    )(*([proj] * bp), *arrs)


def _xattn_prompt_body(q_ref, k_ref, v_ref, o_ref, kb_ref, vb_ref):
    @pl.when(pl.program_id(1) == 0)
    def _():
        kb_ref[...] = k_ref[...].astype(BF16)
        vb_ref[...] = v_ref[...].astype(BF16)

    for h in range(XA_HEADS):
        sl = slice(h * XA_HD, (h + 1) * XA_HD)
        s = _dot_nt(q_ref[:, sl].astype(BF16), kb_ref[:, sl]) * (XA_HD ** -0.5)
        e = jnp.exp(s - jnp.max(s, axis=-1, keepdims=True))
        p = e / jnp.sum(e, axis=-1, keepdims=True)
        o_ref[:, sl] = _dot(p.astype(BF16), vb_ref[:, sl]).astype(o_ref.dtype)


def _xattn_prompt(q, mk, mv, bp, seq):
    tq = _pick(seq, (512, 256, 128))
    nt = seq // tq
    kv = pl.BlockSpec((N_MEM, D_MODEL), lambda b, t: (b, 0))
    return pl.pallas_call(
        _xattn_prompt_body,
        grid=(bp, nt),
        in_specs=[pl.BlockSpec((tq, D_MODEL), lambda b, t: (b * nt + t, 0)), kv, kv],
        out_specs=pl.BlockSpec((tq, D_MODEL), lambda b, t: (b * nt + t, 0)),
        out_shape=jax.ShapeDtypeStruct((q.shape[0], D_MODEL), BF16),
        scratch_shapes=[pltpu.VMEM((N_MEM, D_MODEL), BF16), pltpu.VMEM((N_MEM, D_MODEL), BF16)],
        compiler_params=_cparams("parallel", "arbitrary"),
        name="xattn_prompt",
    )(q, mk, mv)


XA_SAMPLES_PER_STEP = 2


def _xattn_sample_body(q_ref, k_ref, v_ref, obuf_ref, o_ref, acc_ref, *, nsteps):
    del obuf_ref
    i = pl.program_id(0)
    H = XA_HEADS
    groups = N_MEM * H // (2 * H)
    for j in range(XA_SAMPLES_PER_STEP):
        b = i * XA_SAMPLES_PER_STEP + j
        k3 = k_ref[j].reshape(N_MEM * H, XA_HD).reshape(groups, 2 * H, XA_HD)
        s = jnp.sum(k3 * q_ref[b], axis=-1, keepdims=True) * (XA_HD ** -0.5)
        mx = jnp.max(s, axis=0, keepdims=True)
        mx = jnp.maximum(mx, pltpu.roll(mx, H, 1))
        e = jnp.exp(s - mx)
        den = jnp.sum(e, axis=0, keepdims=True)
        den = den + pltpu.roll(den, H, 1)
        v3 = v_ref[j].reshape(N_MEM * H, XA_HD).reshape(groups, 2 * H, XA_HD)
        o8 = jnp.sum((e / den) * v3, axis=0)
        o = o8[0:H] + o8[H:2 * H]
        acc_ref[pl.ds(b, 1), :] = jnp.concatenate([o[h:h + 1, :] for h in range(H)], axis=1)

    @pl.when(i == nsteps - 1)
    def _():
        o_ref[...] = acc_ref[...].astype(o_ref.dtype)


def _xattn_sample(q, ck, cv, obuf, l, row0):
    bs = q.shape[0]
    nb = XA_SAMPLES_PER_STEP
    kv = pl.BlockSpec((None, nb, N_MEM, XA_HEADS, XA_HD), lambda i: (l, i, 0, 0, 0))
    return pl.pallas_call(
        functools.partial(_xattn_sample_body, nsteps=bs // nb),
        grid=(bs // nb,),
        in_specs=[pl.BlockSpec((bs, 2 * XA_HEADS, XA_HD), lambda i: (0, 0, 0)), kv, kv, ANY_SPEC],
        out_specs=pl.BlockSpec((bs, D_MODEL), lambda i: (row0 // bs, 0)),
        out_shape=jax.ShapeDtypeStruct(obuf.shape, obuf.dtype),
        scratch_shapes=[pltpu.VMEM((bs, D_MODEL), F32)],
        input_output_aliases={3: 0},
        compiler_params=_cparams("arbitrary"),
        name="xattn_sample",
    )(q, ck, cv, obuf)


def _alias_args(prev):
    return ([], []) if prev is None else ([prev], [ANY_SPEC])


def _sample_ad_body(*refs, has_prev):
    (ab_ref, ac_ref, ah_ref, sta_ref, wa_ref, d1a_ref, d1b_ref, d2a_ref, d2b_ref, std_ref, wd_ref, bias_ref, g_ref,
     beta_ref) = refs[:14]
    ya_ref, yd_ref, na_ref, nd_ref = refs[14 + 2 * has_prev:]
    u = ac_ref[...] * ah_ref[...]
    s0 = sta_ref[:, 0, :]
    s1 = sta_ref[:, 1, :]
    ya = ab_ref[...] * (wa_ref[0:1, :] * s0 + wa_ref[1:2, :] * s1 + wa_ref[2:3, :] * u)
    ya_ref[...] = ya.astype(ya_ref.dtype)
    na_ref[:, 0, :] = s1
    na_ref[:, 1, :] = u
    d1 = jnp.concatenate([d1a_ref[...], d1b_ref[...]], axis=1)
    d2 = jnp.concatenate([d2a_ref[...], d2b_ref[...]], axis=1)
    ud = d1 * _sigmoid(d2)
    acc = jnp.broadcast_to(bias_ref[...], ud.shape)
    nh = D_CONV - 1
    for k in range(nh):
        s_k = std_ref[k]
        acc = acc + s_k * wd_ref[k:k + 1, :]
        if k > 0:
            nd_ref[k - 1] = s_k
    acc = acc + ud * wd_ref[nh:nh + 1, :]
    nd_ref[nh - 1] = ud
    c = _layernorm(acc, EPS) * g_ref[...] + beta_ref[...]
    yd_ref[...] = (c * _sigmoid(c)).astype(yd_ref.dtype)


def _sample_ad(proj, st_a, st_d, wa, wd, bias, g, beta, l, row0, prev_a, prev_d):
    bs = st_a.shape[1]
    G = G_WIDTH
    tb = 32
    r0 = row0 // tb
    half = G // 2
    cola = lambda j: pl.BlockSpec((tb, G), lambda i: (r0 + i, COL_A // G + j))
    cold = lambda j: pl.BlockSpec((tb, half), lambda i: (r0 + i, COL_D // half + j))
    sta = pl.BlockSpec((None, tb, A_CONV - 1, G), lambda i: (l, i, 0, 0))
    std = pl.BlockSpec((None, D_CONV - 1, tb, G), lambda i: (l, 0, i, 0))
    yrow = pl.BlockSpec((tb, G), lambda i: (i, 0))
    (bias, bias_s), (g, g_s), (beta, beta_s) = _layer_vec(bias, l), _layer_vec(g, l), _layer_vec(beta, l)
    nh = D_CONV - 1
    has_prev = prev_a is not None
    extra = [prev_a, prev_d] if has_prev else []
    return pl.pallas_call(
        functools.partial(_sample_ad_body, has_prev=has_prev),
        grid=(bs // tb,),
        in_specs=[cola(0), cola(1), cola(2), sta, _layer_mat(wa, l), cold(0), cold(1), cold(2), cold(3), std,
                  _layer_mat(wd, l), bias_s, g_s, beta_s] + [ANY_SPEC] * len(extra),
        out_specs=[yrow, yrow, sta, std],
        out_shape=[jax.ShapeDtypeStruct((bs, G), BF16), jax.ShapeDtypeStruct((bs, G), BF16),
                   jax.ShapeDtypeStruct(st_a.shape, F32), jax.ShapeDtypeStruct(st_d.shape, F32)],
        input_output_aliases={14: 2, 15: 3} if has_prev else {},
        compiler_params=_cparams("parallel"),
        name="sample_conv_ad",
    )(proj, proj, proj, st_a, wa, proj, proj, proj, proj, st_d, wd, bias, g, beta, *extra)


RET_D_BLOCK = 16


def _ret_sample_body(*refs, has_prev):
    q_ref, k_ref, v_ref, g_ref, cos_ref, sin_ref, s_ref = refs[:7]
    y_ref, ns_ref, qt_ref, kt_ref, acc_ref = refs[7 + has_prev:]
    h = pl.program_id(0)
    j = pl.program_id(1)
    nd = RET_D // RET_D_BLOCK
    gammas = [math.exp(lg) for lg in RET_LOG_DECAY]
    gamma = jnp.where(h == 0, gammas[0], jnp.where(h == 1, gammas[1], jnp.where(h == 2, gammas[2], gammas[3])))
    gamma = gamma.astype(F32)

    @pl.when(j == 0)
    def _():
        q = _rope(q_ref[...], cos_ref[...], sin_ref[...])
        k = _rope(k_ref[...], cos_ref[...], sin_ref[...]) * (RET_D ** -0.5)
        qt_ref[...] = q.T
        kt_ref[...] = k.T
        acc_ref[...] = jnp.zeros_like(acc_ref)

    vt = v_ref[...].T
    acc = acc_ref[...]
    for d in range(RET_D_BLOCK):
        s_d = s_ref[:, d, :].T
        row = j * RET_D_BLOCK + d
        q_d = qt_ref[pl.ds(row, 1), :]
        k_d = kt_ref[pl.ds(row, 1), :]
        acc = acc + q_d * s_d
        ns_ref[:, d, :] = (gamma * s_d + k_d * vt).T
    acc_ref[...] = acc

    @pl.when(j == nd - 1)
    def _():
        qk = jnp.sum(qt_ref[...] * kt_ref[...], axis=0, keepdims=True)
        o = qk * vt + acc * gamma
        mu = jnp.mean(o, axis=0, keepdims=True)
        oc = o - mu
        var = jnp.mean(oc * oc, axis=0, keepdims=True)
        ln = (oc * lax.rsqrt(var + EPS)).T
        g = g_ref[...]
        y_ref[...] = (g * _sigmoid(g) * ln).astype(y_ref.dtype)


def _retention_sample(proj, cos2, sin2, state, l, row0, prev):
    bs = state.shape[1]
    r0 = row0 // bs
    nd = RET_D // RET_D_BLOCK
    col = lambda c: pl.BlockSpec((bs, RET_D), lambda h, j: (r0, COL_B // RET_D + c * RET_HEADS + h))
    tab = pl.BlockSpec((1, RET_D), lambda h, j: (0, 0))
    sblk = pl.BlockSpec((None, bs, None, RET_D_BLOCK, RET_D), lambda h, j: (l, 0, h, j, 0))
    extra, extra_specs = _alias_args(prev)
    return pl.pallas_call(
        functools.partial(_ret_sample_body, has_prev=prev is not None),
        grid=(RET_HEADS, nd),
        in_specs=[col(0), col(1), col(2), col(3), tab, tab, sblk] + extra_specs,
        out_specs=[pl.BlockSpec((bs, RET_D), lambda h, j: (0, h)), sblk],
        out_shape=[jax.ShapeDtypeStruct((bs, G_WIDTH), BF16), jax.ShapeDtypeStruct(state.shape, F32)],
        scratch_shapes=[pltpu.VMEM((RET_D, bs), F32), pltpu.VMEM((RET_D, bs), F32), pltpu.VMEM((RET_D, bs), F32)],
        input_output_aliases={7: 1} if prev is not None else {},
        compiler_params=_cparams("parallel", "arbitrary"),
        name="retention_sample",
    )(proj, proj, proj, proj, cos2, sin2, state, *extra)


def _rwkv_sample_body(*refs, has_prev):
    (xr_ref, xk_ref, xv_ref, xl_ref, pr_ref, pk_ref, pv_ref, pl_ref, mr_ref, mk_ref, mv_ref, ml_ref,
     w2w_ref, w2a_ref, g2_ref, w0_ref, a0_ref, kk_ref, ka_ref, rk_ref, lg_ref, lb_ref, s_ref) = refs[:23]
    y_ref, ns_ref, vec_ref, ot_ref = refs[23 + has_prev:]
    N = RWKV_HEAD

    def shifted(x_ref, prev_ref, mu_ref):
        x = x_ref[...]
        return x + (prev_ref[...] - x) * mu_ref[...]

    r, lw, k4, v, kk, a, gate = _rwkv_pre(shifted(xr_ref, pr_ref, mr_ref), shifted(xk_ref, pk_ref, mk_ref),
                                          shifted(xv_ref, pv_ref, mv_ref), shifted(xl_ref, pl_ref, ml_ref),
                                          w2w_ref[...], w2a_ref[...], g2_ref[...], w0_ref[...], a0_ref[...],
                                          kk_ref[...], ka_ref[...])
    kkn = jnp.concatenate([_l2_normalize(kk[:, 0:N]), _l2_normalize(kk[:, N:2 * N])], axis=1)
    w = jnp.exp(lw)
    vec_ref[0] = kkn.T
    vec_ref[1] = w.T
    vec_ref[2] = (kkn * a).T
    vec_ref[3] = k4.T
    vec_ref[4] = r.T
    vec_ref[5] = v.T
    for hh in range(2):
        ch = slice(hh * N, (hh + 1) * N)
        kk_t = vec_ref[0, ch, :]
        w_t = vec_ref[1, ch, :]
        kka_t = vec_ref[2, ch, :]
        k_t = vec_ref[3, ch, :]
        r_t = vec_ref[4, ch, :]
        for i in range(N):
            s_i = s_ref[hh, i]
            sa = -jnp.sum(s_i * kk_t, axis=0, keepdims=True)
            v_i = vec_ref[5, hh * N + i:hh * N + i + 1, :]
            s_new = s_i * w_t + sa * kka_t + v_i * k_t
            ns_ref[hh, i] = s_new
            ot_ref[hh * N + i:hh * N + i + 1, :] = jnp.sum(s_new * r_t, axis=0, keepdims=True)
    o = ot_ref[...].T
    rk = rk_ref[...]
    lg = lg_ref[...]
    lb = lb_ref[...]
    ys = []
    for hh in range(2):
        ch = slice(hh * N, (hh + 1) * N)
        ys.append(_rwkv_post(o[:, ch], r[:, ch], k4[:, ch], v[:, ch], gate[:, ch], rk[:, ch], lg[:, ch], lb[:, ch]))
    y_ref[...] = jnp.concatenate(ys, axis=1).astype(y_ref.dtype)


def _rwkv_sample(proj, prev_shift, rw_arrs, state, l, row0, prev):
    mu, w2a, g2, w0, a0, k_k, k_a, r_k, lnx_g, lnx_b = rw_arrs
    bs = state.shape[-1]
    r0 = row0 // bs
    G = G_WIDTH
    N = RWKV_HEAD
    W = 2 * N
    npair = G // W
    def cols(shape_lead, lead_idx, base):
        mk = lambda width, cidx: pl.BlockSpec(shape_lead + (width,), lambda p: lead_idx + (cidx(p),))
        return [mk(W, lambda p, c=c: base // W + c * npair + p) for c in range(3)] + [
            mk(2 * W, lambda p: (base + 3 * G) // (2 * W))]
    pairv = pl.BlockSpec((None, 1, W), lambda p: (l, 0, p))
    pairm = pl.BlockSpec((None, 128, W), lambda p: (l, 0, p))
    sblk = pl.BlockSpec((None, 2, N, N, bs), lambda p: (l, p, 0, 0, 0))
    extra, extra_specs = _alias_args(prev)
    return pl.pallas_call(
        functools.partial(_rwkv_sample_body, has_prev=prev is not None),
        grid=(npair,),
        in_specs=cols((bs,), (r0,), COL_C) + cols((None, bs), (l, 0), 0) + cols((None, 1), (l, 0), 0) + [
            pairm, pl.BlockSpec((None, 128, W), lambda p: (l, 0, npair + p)), pairm,
            pairv, pairv, pairv, pairv, pairv, pairv, pairv, sblk] + extra_specs,
        out_specs=[pl.BlockSpec((bs, W), lambda p: (0, p)), sblk],
        out_shape=[jax.ShapeDtypeStruct((bs, G), BF16), jax.ShapeDtypeStruct(state.shape, F32)],
        scratch_shapes=[pltpu.VMEM((6, W, bs), F32), pltpu.VMEM((W, bs), F32)],
        input_output_aliases={23: 1} if prev is not None else {},
        compiler_params=_cparams("parallel"),
        name="rwkv_sample",
    )(proj, proj, proj, proj, prev_shift, prev_shift, prev_shift, prev_shift, mu, mu, mu, mu, w2a, w2a, g2,
      w0, a0, k_k, k_a, r_k, lnx_g, lnx_b, state, *extra)


def _place_cols_body(src_ref, buf_ref, o_ref):
    del buf_ref
    o_ref[...] = src_ref[...]


def _place_cols(src, buf, col):
    rows, w = src.shape
    tr = _pick(rows, (1024, 512, 256, 128))
    return pl.pallas_call(
        _place_cols_body,
        grid=(rows // tr,),
        in_specs=[pl.BlockSpec((tr, w), lambda i: (i, 0)), ANY_SPEC],
        out_specs=pl.BlockSpec((tr, w), lambda i: (i, col)),
        out_shape=jax.ShapeDtypeStruct(buf.shape, buf.dtype),
        input_output_aliases={1: 0},
        compiler_params=_cparams("parallel"),
        name="place_cols",
    )(src, buf)


def _place_heads_body(*refs):
    src_ref, o_ref = refs[0], refs[-1]
    for h in range(XA_HEADS):
        o_ref[0, :, h, :] = src_ref[:, h * XA_HD:(h + 1) * XA_HD]


def _place_heads(src, l, depth, bp, prev):
    extra, extra_specs = _alias_args(prev)
    return pl.pallas_call(
        _place_heads_body,
        grid=(bp,),
        in_specs=[pl.BlockSpec((N_MEM, D_MODEL), lambda b: (b, 0))] + extra_specs,
        out_specs=pl.BlockSpec((None, 1, N_MEM, XA_HEADS, XA_HD), lambda b: (l, b, 0, 0, 0)),
        out_shape=jax.ShapeDtypeStruct((depth, bp, N_MEM, XA_HEADS, XA_HD), F32),
        input_output_aliases={1: 0} if prev is not None else {},
        compiler_params=_cparams("parallel"),
        name="place_heads",
    )(src, *extra)


def _assemble_body(a_ref, b_ref, c_ref, d_ref, buf_ref, o_ref):
    del buf_ref
    o_ref[...] = jnp.concatenate([a_ref[...], b_ref[...], c_ref[...], d_ref[...]], axis=1)


def _assemble_rows(parts, buf, row0):
    bs = parts[0].shape[0]
    part = pl.BlockSpec((bs, G_WIDTH), lambda i: (0, 0))
    return pl.pallas_call(
        _assemble_body,
        grid=(1,),
        in_specs=[part, part, part, part, ANY_SPEC],
        out_specs=pl.BlockSpec((bs, D_MODEL), lambda i: (row0 // bs, 0)),
        out_shape=jax.ShapeDtypeStruct(buf.shape, buf.dtype),
        input_output_aliases={4: 0},
        compiler_params=_cparams("arbitrary"),
        name="assemble_sample_rows",
    )(*parts, buf)


def kernel(x_prompt, x_sample, mem_prompt, state_conv_a, state_ret, state_shift, state_wkv, state_conv_d, cache_mem_k, cache_mem_v, g_mix, w_in, conv_a_w, mu_c, w0, w2, a0, a2, g2, k_k, k_a, r_k, lnx_g, lnx_b, conv_d_w, conv_d_b, ln_d_g, ln_d_b, w_out, g_xa, g_mem, wq_x, wk_x, wv_x, wo_x, g_mlp, w_up, w_down, g_final):
    bp, seq, d = x_prompt.shape
    bs = x_sample.shape[0]
    depth = w_in.shape[0]
    np_rows = bp * seq
    G = G_WIDTH
    x = jnp.concatenate([x_prompt.reshape(np_rows, d), x_sample.reshape(bs, d)], axis=0)
    mem = mem_prompt.reshape(bp * N_MEM, d)
    cos_p, sin_p = _rope_tables(jnp.arange(seq))
    cos_s, sin_s = _rope_tables(PAST_LEN + jnp.arange(1))

    z64 = jnp.zeros((depth, 64, G), F32)
    w2a = jnp.concatenate([jnp.concatenate([w2, z64], axis=2), jnp.concatenate([z64, a2], axis=2)], axis=1).astype(BF16)
    g2_b = g2.astype(BF16)
    wkv_t = jnp.transpose(state_wkv, (0, 2, 3, 4, 1))
    conv_d_t = jnp.transpose(state_conv_d, (0, 2, 1, 3))
    vec3 = lambda a: a.reshape(depth, 1, -1)
    rw_arrs = (vec3(mu_c), w2a, g2_b, vec3(w0), vec3(a0), vec3(k_k), vec3(k_a), vec3(r_k), vec3(lnx_g), vec3(lnx_b))

    small = {k: [] for k in ("a_p", "ret_p", "sh_p", "sh_s", "wkv_p", "d_p")}
    na_s = nd_s = nret_s = nwkv_s = mk_all = mv_all = None
    for l in range(depth):
        proj = _matmul_ws(x, w_in, l, norm_g=_layer_vec(g_mix, l), tn=1280, tm=_pick(x.shape[0], (640, 512, 256, 128)))
        ymix, na_p = _conv_a_prompt(proj, conv_a_w, l, bp, seq)
        ymix, nd_p = _conv_d_prompt(proj, conv_d_w, conv_d_b, ln_d_g, ln_d_b, ymix, l, bp, seq)
        yb_p, nret_p = _retention_prompt(proj, cos_p, sin_p, bp, seq)
        ymix = _place_cols(yb_p.reshape(np_rows, G), ymix, 1)
        rw = [(rw_arrs[0], pl.BlockSpec((None, 1, SHIFT_W), lambda *_: (l, 0, 0))),
              (w2a, _layer_mat(w2a, l)), (g2_b, _layer_mat(g2_b, l))] + [
              (a, pl.BlockSpec((None, 1, G), lambda *_: (l, 0, 0))) for a in rw_arrs[3:]]
        yc_p, nsh_p, nwkv_p = _rwkv_prompt(proj, rw, bp, seq)
        ymix = _place_cols(yc_p.reshape(np_rows, G), ymix, 2)
        ya_s, yd_s, na_s, nd_s = _sample_ad(proj, state_conv_a, conv_d_t, conv_a_w, conv_d_w, conv_d_b, ln_d_g, ln_d_b,
                                            l, np_rows, na_s, nd_s)
        yb_s, nret_s = _retention_sample(proj, cos_s, sin_s, state_ret, l, np_rows, nret_s)
        yc_s, nwkv_s = _rwkv_sample(proj, state_shift, rw_arrs, wkv_t, l, np_rows, nwkv_s)
        ymix = _assemble_rows([ya_s, yb_s, yc_s, yd_s], ymix, np_rows)
        x = _matmul_ws(ymix, w_out, l, residual=x)

        mk = _matmul_ws(mem, wk_x, l, norm_g=_layer_vec(g_mem, l))
        mv = _matmul_ws(mem, wv_x, l, norm_g=_layer_vec(g_mem, l))
        mk_all = _place_heads(mk, l, depth, bp, mk_all)
        mv_all = _place_heads(mv, l, depth, bp, mv_all)
        q = _matmul_ws(x, wq_x, l, norm_g=_layer_vec(g_xa, l))
        obuf = _xattn_prompt(q, mk, mv, bp, seq)
        q_s = q[np_rows:].reshape(bs, XA_HEADS, XA_HD)
        q_s = jnp.concatenate([q_s, q_s], axis=1)
        obuf = _xattn_sample(q_s, cache_mem_k, cache_mem_v, obuf, l, np_rows)
        x = _matmul_ws(obuf, wo_x, l, residual=x)

        up = _matmul_ws(x, w_up, l, norm_g=_layer_vec(g_mlp, l), act="relu2", out_dtype=BF16)
        for kb in range(w_down.shape[1] // d):
            x = _matmul_ws(up, w_down, l, residual=x, kblock=(kb, d))

        small["a_p"].append(na_p)
        small["ret_p"].append(nret_p)
        small["sh_p"].append(nsh_p.reshape(bp, SHIFT_W))
        small["sh_s"].append(proj[np_rows:, COL_C:COL_C + SHIFT_W])
        small["wkv_p"].append(nwkv_p)
        small["d_p"].append(nd_p)
    gf = (g_final.reshape(1, d), pl.BlockSpec((1, d), lambda *_: (0, 0)))
    y_p = _rmsnorm(x, gf, F32, 0, np_rows)
    y_s = _rmsnorm(x, gf, F32, np_rows, bs)
    st = lambda k: jnp.stack(small[k])
    return (y_p.reshape(bp, seq, d), y_s.reshape(bs, 1, d), st("a_p"), na_s, st("ret_p"), nret_s,
            st("sh_p"), st("sh_s"), st("wkv_p"), jnp.transpose(nwkv_s, (0, 4, 1, 2, 3)),
            st("d_p"), jnp.transpose(nd_s, (0, 2, 1, 3)), mk_all, mv_all)
```

```python
import functools
import math

import jax
import jax.numpy as jnp
from jax import lax
from jax.experimental import pallas as pl
from jax.experimental.pallas import tpu as pltpu

F32 = jnp.float32
BF16 = jnp.bfloat16
HI = lax.Precision.HIGHEST

D_MODEL = 2048
G_WIDTH = 512
P_IN = 6400
RET_HEADS = 4
RET_D = 128
RET_CHUNK = 128
ROPE_BASE = 10000.0
RWKV_HEAD = 64
RWKV_HEADS = 8
WKV_CHUNK = 64
RWKV_GN_EPS = 64e-5
A_CONV = 3
D_CONV = 31
D_HIST = 32
N_MEM = 256
XA_HEADS = 4
XA_HD = 512
SHIFT_W = 1792
PAST_LEN = 16384
EPS = 1e-6
RET_LOG_DECAY = tuple(math.log1p(-(2.0 ** (-5.0 - h))) for h in range(RET_HEADS))
VMEM_LIMIT = 56 * 1024 * 1024
COL_A = 0
COL_B = 3 * G_WIDTH
COL_C = 7 * G_WIDTH
COL_D = 7 * G_WIDTH + SHIFT_W

ANY_SPEC = pl.BlockSpec(memory_space=pl.ANY)


def _cparams(*sem):
    return pltpu.CompilerParams(dimension_semantics=sem, vmem_limit_bytes=VMEM_LIMIT)


def _pick(n, cands):
    for c in cands:
        if n % c == 0:
            return c
    raise ValueError(f"no tile for {n}")


def _sigmoid(x):
    return 1.0 / (1.0 + jnp.exp(-x))


def _dot(a, b, precision=None):
    return jnp.dot(a, b, preferred_element_type=F32, precision=precision)


def _dot_nt(a, b, precision=None):
    return lax.dot_general(a, b, (((1,), (1,)), ((), ())), preferred_element_type=F32, precision=precision)


def _dot_tn(a, b, precision=None):
    return lax.dot_general(a, b, (((0,), (0,)), ((), ())), preferred_element_type=F32, precision=precision)


def _layernorm(x, eps):
    mu = jnp.mean(x, axis=-1, keepdims=True)
    xc = x - mu
    var = jnp.mean(xc * xc, axis=-1, keepdims=True)
    return xc * lax.rsqrt(var + eps)


def _layer_vec(a, l):
    n = a.shape[-1]
    return a.reshape(a.shape[0], 1, n), pl.BlockSpec((None, 1, n), lambda *_: (l, 0, 0))


def _layer_mat(a, l):
    return pl.BlockSpec((None,) + a.shape[1:], lambda *_: (l, 0, 0))


def _rmsnorm_body(x_ref, g_ref, o_ref):
    x = x_ref[...]
    ms = jnp.mean(x * x, axis=-1, keepdims=True)
    o_ref[...] = ((x * lax.rsqrt(ms + EPS)) * g_ref[...]).astype(o_ref.dtype)


def _rmsnorm(x, g_spec, out_dtype, row0=0, nrows=None):
    m, d = x.shape
    nrows = nrows or m
    tm = _pick(math.gcd(nrows, row0) if row0 else nrows, (1040, 1024, 640, 512, 256, 128))
    r0 = row0 // tm
    g, gspec = g_spec
    return pl.pallas_call(
        _rmsnorm_body,
        grid=(nrows // tm,),
        in_specs=[pl.BlockSpec((tm, d), lambda i: (r0 + i, 0)), gspec],
        out_specs=pl.BlockSpec((tm, d), lambda i: (i, 0)),
        out_shape=jax.ShapeDtypeStruct((nrows, d), out_dtype),
        compiler_params=_cparams("parallel"),
        name="rmsnorm",
    )(x, g)


def _mm_ws_body(*refs, act, has_res, has_norm):
    a_ref, w_ref = refs[0], refs[1]
    g_ref = refs[2] if has_norm else None
    res_ref = refs[2 + has_norm] if has_res else None
    o_ref = refs[2 + has_norm + has_res]
    wb_ref = refs[-1]

    @pl.when(pl.program_id(1) == 0)
    def _():
        wb_ref[...] = w_ref[...].astype(BF16)

    a = a_ref[...]
    if has_norm:
        ms = jnp.mean(a * a, axis=-1, keepdims=True)
        a = ((a * lax.rsqrt(ms + EPS)) * g_ref[...]).astype(BF16)
    r = _dot(a, wb_ref[...])
    if act == "relu2":
        r = jnp.square(jnp.maximum(r, 0.0))
    if has_res:
        r = res_ref[...] + r
    o_ref[...] = r.astype(o_ref.dtype)


def _matmul_ws(a, w, l, *, norm_g=None, residual=None, act=None, out_dtype=F32, tn=1024, tm=None, kblock=None):
    m = a.shape[0]
    kb, kdim = kblock if kblock is not None else (0, a.shape[1])
    n = w.shape[2]
    tm = tm or _pick(m, (1040, 1024, 640, 512, 256, 128))
    has_res = residual is not None
    has_norm = norm_g is not None
    in_specs = [pl.BlockSpec((tm, kdim), lambda j, i: (i, kb)), pl.BlockSpec((None, kdim, tn), lambda j, i: (l, kb, j))]
    args = [a, w]
    if has_norm:
        args.append(norm_g[0])
        in_specs.append(norm_g[1])
    if has_res:
        in_specs.append(pl.BlockSpec((tm, tn), lambda j, i: (i, j)))
        args.append(residual)
    return pl.pallas_call(
        functools.partial(_mm_ws_body, act=act, has_res=has_res, has_norm=has_norm),
        grid=(n // tn, m // tm),
        in_specs=in_specs,
        out_specs=pl.BlockSpec((tm, tn), lambda j, i: (i, j)),
        out_shape=jax.ShapeDtypeStruct((m, n), out_dtype),
        scratch_shapes=[pltpu.VMEM((kdim, tn), BF16)],
        compiler_params=_cparams("parallel", "arbitrary"),
        name="matmul_ws",
    )(*args)


def _conv_a_body(b_ref, c_ref, h_ref, w_ref, y_ref, st_ref, ext_ref, *, tl):
    t = pl.program_id(1)

    @pl.when(t == 0)
    def _():
        ext_ref[0:8, :] = jnp.zeros((8, G_WIDTH), F32)

    u = c_ref[...] * h_ref[...]
    ext_ref[8:, :] = u
    u1 = ext_ref[7:7 + tl, :]
    u2 = ext_ref[6:6 + tl, :]
    y = b_ref[...] * (w_ref[0:1, :] * u2 + w_ref[1:2, :] * u1 + w_ref[2:3, :] * u)
    y_ref[...] = y.astype(y_ref.dtype)
    st_ref[0] = ext_ref[tl + 6:tl + 8, :]
    ext_ref[0:8, :] = ext_ref[tl:tl + 8, :]


def _conv_a_prompt(proj, conv_w, l, bp, seq):
    tl = _pick(seq, (512, 256, 128))
    nt = seq // tl
    c0 = COL_A // G_WIDTH
    col = lambda j: pl.BlockSpec((tl, G_WIDTH), lambda b, t: (b * nt + t, c0 + j))
    return pl.pallas_call(
        functools.partial(_conv_a_body, tl=tl),
        grid=(bp, nt),
        in_specs=[col(0), col(1), col(2), _layer_mat(conv_w, l)],
        out_specs=[pl.BlockSpec((tl, G_WIDTH), lambda b, t: (b * nt + t, 0)),
                   pl.BlockSpec((1, A_CONV - 1, G_WIDTH), lambda b, t: (b, 0, 0))],
        out_shape=[jax.ShapeDtypeStruct((proj.shape[0], D_MODEL), BF16),
                   jax.ShapeDtypeStruct((bp, A_CONV - 1, G_WIDTH), F32)],
        scratch_shapes=[pltpu.VMEM((tl + 8, G_WIDTH), F32)],
        compiler_params=_cparams("parallel", "arbitrary"),
        name="conv_a_prompt",
    )(proj, proj, proj, conv_w)


CONV_D_ROWS = 64


def _conv_d_body(d1a_ref, d1b_ref, d2a_ref, d2b_ref, w_ref, bias_ref, g_ref, beta_ref, ymix_ref, y_ref, st_ref, ext_ref,
                 acc_ref, *, tl):
    del ymix_ref
    t = pl.program_id(1)
    next_ = D_HIST + tl

    @pl.when(t == 0)
    def _():
        ext_ref[0, 0:D_HIST, :] = jnp.zeros((D_HIST, G_WIDTH), F32)

    d1 = jnp.concatenate([d1a_ref[...], d1b_ref[...]], axis=1)
    d2 = jnp.concatenate([d2a_ref[...], d2b_ref[...]], axis=1)
    ext_ref[0, D_HIST:, :] = d1 * _sigmoid(d2)
    ext = ext_ref[0]
    for s in range(1, 8):
        ext_ref[s] = pltpu.roll(ext, next_ - s, 0)
    first = D_HIST - (D_CONV - 1)

    def blk(i, carry):
        base = pl.multiple_of(i * CONV_D_ROWS, CONV_D_ROWS)
        acc = jnp.broadcast_to(bias_ref[...], (CONV_D_ROWS, G_WIDTH))
        for k in range(D_CONV):
            s = (first + k) % 8
            a0 = first + k - s
            acc = acc + ext_ref[s, pl.ds(base + a0, CONV_D_ROWS), :] * w_ref[k:k + 1, :]
        acc_ref[pl.ds(base, CONV_D_ROWS), :] = acc
        return carry

    lax.fori_loop(0, tl // CONV_D_ROWS, blk, 0)
    c = _layernorm(acc_ref[...], EPS) * g_ref[...] + beta_ref[...]
    y_ref[...] = (c * _sigmoid(c)).astype(y_ref.dtype)
    st_ref[0] = ext_ref[0, tl + first:tl + D_HIST, :]
    ext_ref[0, 0:D_HIST, :] = ext_ref[0, tl:tl + D_HIST, :]


def _conv_d_prompt(proj, w, bias, g, beta, ymix, l, bp, seq):
    tl = _pick(seq, (512, 256, 128))
    nt = seq // tl
    half = G_WIDTH // 2
    c0 = COL_D // half
    col = lambda j: pl.BlockSpec((tl, half), lambda b, t: (b * nt + t, c0 + j))
    (bias, bias_s), (g, g_s), (beta, beta_s) = _layer_vec(bias, l), _layer_vec(g, l), _layer_vec(beta, l)
    return pl.pallas_call(
        functools.partial(_conv_d_body, tl=tl),
        grid=(bp, nt),
        in_specs=[col(0), col(1), col(2), col(3), _layer_mat(w, l), bias_s, g_s, beta_s, ANY_SPEC],
        out_specs=[pl.BlockSpec((tl, G_WIDTH), lambda b, t: (b * nt + t, 3)),
                   pl.BlockSpec((1, D_CONV - 1, G_WIDTH), lambda b, t: (b, 0, 0))],
        out_shape=[jax.ShapeDtypeStruct(ymix.shape, ymix.dtype),
                   jax.ShapeDtypeStruct((bp, D_CONV - 1, G_WIDTH), F32)],
        scratch_shapes=[pltpu.VMEM((8, tl + D_HIST, G_WIDTH), F32), pltpu.VMEM((tl, G_WIDTH), F32)],
        input_output_aliases={8: 0},
        compiler_params=_cparams("parallel", "arbitrary"),
        name="conv_d_prompt",
    )(proj, proj, proj, proj, w, bias, g, beta, ymix)


def _rope_tables(pos):
    inv = ROPE_BASE ** (-jnp.arange(0, RET_D, 2, dtype=F32) / RET_D)
    ang = pos.astype(F32)[:, None] * inv[None, :]
    cos = jnp.cos(ang)
    sin = jnp.sin(ang)
    return jnp.concatenate([cos, cos], axis=-1), jnp.concatenate([-sin, sin], axis=-1)


def _rope(x, cos2, sin2):
    return x * cos2 + pltpu.roll(x, RET_D // 2, 1) * sin2


def _ret_body(*refs, nb, nchunks):
    qkvg = refs[:4 * nb]
    cos_ref, sin_ref, y_ref, st_ref, s_ref = refs[4 * nb:]
    c = pl.program_id(0)
    C = RET_CHUNK

    @pl.when(c == 0)
    def _():
        s_ref[...] = jnp.zeros_like(s_ref)

    ii = lax.broadcasted_iota(jnp.int32, (C, C), 0).astype(F32)
    jj = lax.broadcasted_iota(jnp.int32, (C, C), 1).astype(F32)
    diff = ii - jj
    cos2 = cos_ref[...]
    sin2 = sin_ref[...]
    NH = RET_HEADS
    chains = [(b, h) for b in range(nb) for h in range(NH)]
    I = range(len(chains))
    sl = [slice(h * RET_D, (h + 1) * RET_D) for h in range(NH)]
    lg = RET_LOG_DECAY
    q = [_rope(qkvg[4 * b][:, sl[h]], cos2, sin2) for b, h in chains]
    k = [_rope(qkvg[4 * b + 1][:, sl[h]], cos2, sin2) * (RET_D ** -0.5) for b, h in chains]
    qb = [q[i].astype(BF16) for i in I]
    vb = [qkvg[4 * b + 2][:, sl[h]].astype(BF16) for b, h in chains]
    s_old = [s_ref[i] for i in I]
    scores = [_dot_nt(qb[i], k[i].astype(BF16)) for i in I]
    cross = [_dot(qb[i], s_old[i].astype(BF16)) for i in I]
    k_dec = [jnp.exp(lg[h] * (C - 1.0 - ii)) for h in range(NH)]
    kv = [_dot_tn((k[i] * k_dec[chains[i][1]]).astype(BF16), vb[i]) for i in I]
    dmat = [jnp.where(diff >= 0.0, jnp.exp(lg[h] * jnp.maximum(diff, 0.0)), 0.0) for h in range(NH)]
    inner = [_dot((scores[i] * dmat[chains[i][1]]).astype(BF16), vb[i]) for i in I]
    q_dec = [jnp.exp(lg[h] * (ii + 1.0)) for h in range(NH)]
    for i, (b, h) in enumerate(chains):
        s_ref[i] = math.exp(lg[h] * C) * s_old[i] + kv[i]
        o = _layernorm(inner[i] + cross[i] * q_dec[h], EPS)
        g = qkvg[4 * b + 3][:, sl[h]]
        y_ref[b, :, sl[h]] = (g * _sigmoid(g) * o).astype(y_ref.dtype)

    @pl.when(c == nchunks - 1)
    def _():
        for i, (b, h) in enumerate(chains):
            st_ref[b, h] = s_ref[i]


def _retention_prompt(proj, cos2, sin2, bp, seq):
    C = RET_CHUNK
    nc = seq // C
    c0 = COL_B // G_WIDTH
    cols = [pl.BlockSpec((C, G_WIDTH), lambda c, b=b, j=j: (b * nc + c, c0 + j)) for b in range(bp) for j in range(4)]
    tab = pl.BlockSpec((C, RET_D), lambda c: (c, 0))
    return pl.pallas_call(
        functools.partial(_ret_body, nb=bp, nchunks=nc),
        grid=(nc,),
        in_specs=cols + [tab, tab],
        out_specs=[pl.BlockSpec((bp, C, G_WIDTH), lambda c: (0, c, 0)),
                   pl.BlockSpec((bp, RET_HEADS, RET_D, RET_D), lambda c: (0, 0, 0, 0))],
        out_shape=[jax.ShapeDtypeStruct((bp, seq, G_WIDTH), BF16),
                   jax.ShapeDtypeStruct((bp, RET_HEADS, RET_D, RET_D), F32)],
        scratch_shapes=[pltpu.VMEM((bp * RET_HEADS, RET_D, RET_D), F32)],
        compiler_params=_cparams("arbitrary"),
        name="retention_prompt",
    )(*([proj] * (4 * bp)), cos2, sin2)


def _softplus(z):
    return jnp.maximum(z, 0.0) + jnp.log(1.0 + jnp.exp(-jnp.abs(z)))


def _rwkv_pre(r, kc, vc, lora_x, w2_w, w2_a, g2, w0, a0, k_k, k_a):
    lin = lora_x[:, 0:128]
    lane = lax.broadcasted_iota(jnp.int32, lin.shape, 1)
    lin = jnp.where(lane < 64, jnp.tanh(lin), lin).astype(BF16)
    gate = _dot(_sigmoid(lora_x[:, 128:256]).astype(BF16), g2)
    w_pre = -_softplus(-(w0 + _dot(lin, w2_w))) - 0.5
    log_decay = -jnp.exp(w_pre)
    a = _sigmoid(a0 + _dot(lin, w2_a))
    kk = kc * k_k
    k4 = kc * (1.0 + (a - 1.0) * k_a)
    return r, log_decay, k4, vc, kk, a, gate


def _l2_normalize(kk):
    ss = jnp.sum(kk * kk, axis=-1, keepdims=True)
    return kk * jnp.minimum(lax.rsqrt(ss), 1e12)


def _bdot(a, b):
    return _dot(a.astype(BF16), b.astype(BF16))


def _wkv_chunk(r, lw, cum, k, v, kk, a, z, mask2, eye):
    T = WKV_CHUNK
    H = range(len(r))
    g_inc = [jnp.exp(cum[h]) for h in H]
    g_tot = [g_inc[h][T - 1:T, :] for h in H]
    ah = [-(kk[h] * jnp.exp(cum[h] - lw[h])) for h in H]
    rg = [r[h] * g_inc[h] for h in H]
    g_inv = [jnp.exp(-cum[h]) for h in H]
    bd = [kk[h] * a[h] * g_inv[h] for h in H]
    kd = [k[h] * g_inv[h] for h in H]
    ar = [jnp.concatenate([ah[h], rg[h]], axis=0).astype(BF16) for h in H]
    bk = [jnp.concatenate([bd[h], kd[h]], axis=0).astype(BF16) for h in H]
    x = [jnp.where(mask2, _dot_nt(ar[h], bk[h]), 0.0) for h in H]
    xb = [x[h][:, 0:T] for h in H]
    vb = [v[h].astype(BF16) for h in H]
    zero = jnp.zeros((T, 64), BF16)
    xkv = [_dot(x[h].astype(BF16), jnp.concatenate([zero, vb[h]], axis=0)) for h in H]
    acur = [xb[h][0:T] for h in H]
    p = [eye for _ in H]
    for it in range(5):
        x = [_bdot(jnp.concatenate([acur[h], p[h]], axis=0), acur[h]) for h in H]
        acur = [x[h][0:T] for h in H]
        p = [p[h] + x[h][T:2 * T] for h in H]
    x = [_bdot(p[h], acur[h]) for h in H]
    p = [p[h] + x[h] for h in H]
    wub = [_bdot(p[h], jnp.concatenate([ah[h], xkv[h][0:T]], axis=1)).astype(BF16) for h in H]
    qo = [_dot(xb[h][T:2 * T].astype(BF16), wub[h]) for h in H]
    lhs_t = [jnp.concatenate([bd[h] * g_tot[h], kd[h] * g_tot[h]], axis=0).astype(BF16) for h in H]
    mn = [_dot_tn(lhs_t[h], jnp.concatenate([wub[h], jnp.concatenate([zero, vb[h]], axis=1)], axis=0)) for h in H]
    qm = [jnp.concatenate([rg[h] + qo[h][:, 0:64], mn[h][:, 0:64] + jnp.where(eye > 0.0, g_tot[h], 0.0)], axis=0) for h in H]
    oz = [_bdot(qm[h], z[h]) for h in H]
    o = [oz[h][0:T] + qo[h][:, 64:128] + xkv[h][T:2 * T] for h in H]
    z_new = [oz[h][T:2 * T] + mn[h][:, 64:128] for h in H]
    return o, z_new


def _rwkv_post(o, r, k4, v, gate, rk, lnx_g, lnx_b):
    gn = _layernorm(o, RWKV_GN_EPS) * lnx_g + lnx_b
    bonus = jnp.sum(r * k4 * rk, axis=-1, keepdims=True) * v
    return (gn + bonus) * gate


def _rwkv_prompt_body(*refs, nb, nchunks):
    pc_refs = refs[:nb]
    mu_ref, w2a_ref, g2_ref, w0_ref, a0_ref, kk_ref, ka_ref, rk_ref, lg_ref, lb_ref = refs[nb:nb + 10]
    y_ref, shift_ref, st_ref, carry_ref, z_ref = refs[nb + 10:]
    c = pl.program_id(0)
    T = WKV_CHUNK
    NH = RWKV_HEADS

    @pl.when(c == 0)
    def _():
        carry_ref[...] = jnp.zeros_like(carry_ref)
        z_ref[...] = jnp.zeros_like(z_ref)

    pcs, prevs = [], []
    for b in range(nb):
        pc = pc_refs[b][...]
        row = lax.broadcasted_iota(jnp.int32, pc.shape, 0)
        prevs.append(jnp.where(row == 0, carry_ref[b, 0:1, :], pltpu.roll(pc, 1, 0)))
        carry_ref[b, 0:1, :] = pc[T - 1:T, :]
        shift_ref[b] = pc[T - 1:T, :]
        pcs.append(pc)
    pc = jnp.concatenate(pcs, axis=0)
    xs = pc + (jnp.concatenate(prevs, axis=0) - pc) * mu_ref[...]
    G = G_WIDTH
    r, lw, k4, v, kk, a, gate = _rwkv_pre(xs[:, 0:G], xs[:, G:2 * G], xs[:, 2 * G:3 * G], xs[:, 3 * G:3 * G + 256],
                                          w2a_ref[:, 0:G], w2a_ref[:, G:2 * G], g2_ref[...], w0_ref[...], a0_ref[...],
                                          kk_ref[...], ka_ref[...])
    ti = lax.broadcasted_iota(jnp.int32, (T, T), 0)
    si = lax.broadcasted_iota(jnp.int32, (T, T), 1)
    tril = (ti >= si).astype(F32)
    eye = (ti == si).astype(F32)
    cum = jnp.concatenate([_dot(tril, lw[b * T:(b + 1) * T, :], HI) for b in range(nb)], axis=0)
    t2 = lax.broadcasted_iota(jnp.int32, (2 * T, 2 * T), 0)
    s2 = lax.broadcasted_iota(jnp.int32, (2 * T, 2 * T), 1)
    mask2 = jnp.where(t2 < T, t2 - 1, t2 - T) >= jnp.where(s2 < T, s2, s2 - T)
    heads = lambda x: [x[b * T:(b + 1) * T, h * RWKV_HEAD:(h + 1) * RWKV_HEAD] for b in range(nb) for h in range(NH)]
    r_h, k_h, v_h = heads(r), heads(k4), heads(v)
    o, z_new = _wkv_chunk(r_h, heads(lw), heads(cum), k_h, v_h, [_l2_normalize(x) for x in heads(kk)], heads(a),
                          [z_ref[i] for i in range(nb * NH)], mask2, eye)
    for i in range(nb * NH):
        z_ref[i] = z_new[i]
    gate_h = heads(gate)
    vec_heads = lambda ref: [ref[:, h * RWKV_HEAD:(h + 1) * RWKV_HEAD] for h in range(NH)]
    rk_h, lg_h, lb_h = vec_heads(rk_ref), vec_heads(lg_ref), vec_heads(lb_ref)
    for b in range(nb):
        ys = [_rwkv_post(o[b * NH + h], r_h[b * NH + h], k_h[b * NH + h], v_h[b * NH + h], gate_h[b * NH + h],
                         rk_h[h], lg_h[h], lb_h[h]) for h in range(NH)]
        y_ref[b] = jnp.concatenate(ys, axis=1).astype(y_ref.dtype)

    @pl.when(c == nchunks - 1)
    def _():
        for b in range(nb):
            for h in range(NH):
                st_ref[b, h] = z_ref[b * NH + h].T


def _rwkv_prompt(proj, rw, bp, seq):
    T = WKV_CHUNK
    nc = seq // T
    G = G_WIDTH
    arrs, specs = zip(*rw)
    pc_specs = [pl.BlockSpec((T, SHIFT_W), lambda c, b=b: (b * nc + c, COL_C // SHIFT_W)) for b in range(bp)]
    return pl.pallas_call(
        functools.partial(_rwkv_prompt_body, nb=bp, nchunks=nc),
        grid=(nc,),
        in_specs=pc_specs + list(specs),
        out_specs=[pl.BlockSpec((bp, T, G), lambda c: (0, c, 0)),
                   pl.BlockSpec((bp, 1, SHIFT_W), lambda c: (0, 0, 0)),
                   pl.BlockSpec((bp, RWKV_HEADS, RWKV_HEAD, RWKV_HEAD), lambda c: (0, 0, 0, 0))],
        out_shape=[jax.ShapeDtypeStruct((bp, seq, G), BF16),
                   jax.ShapeDtypeStruct((bp, 1, SHIFT_W), F32),
                   jax.ShapeDtypeStruct((bp, RWKV_HEADS, RWKV_HEAD, RWKV_HEAD), F32)],
        scratch_shapes=[pltpu.VMEM((bp, 8, SHIFT_W), F32), pltpu.VMEM((bp * RWKV_HEADS, RWKV_HEAD, RWKV_HEAD), F32)],
        compiler_params=_cparams("arbitrary"),
        name="rwkv_prompt",
    )(*([proj] * bp), *arrs)


def _xattn_prompt_body(q_ref, k_ref, v_ref, o_ref, kb_ref, vb_ref):
    @pl.when(pl.program_id(1) == 0)
    def _():
        kb_ref[...] = k_ref[...].astype(BF16)
        vb_ref[...] = v_ref[...].astype(BF16)

    H = range(XA_HEADS)
    sl = [slice(h * XA_HD, (h + 1) * XA_HD) for h in H]
    s = [_dot_nt(q_ref[:, sl[h]].astype(BF16), kb_ref[:, sl[h]]) * (XA_HD ** -0.5) for h in H]
    e = [jnp.exp(s[h] - jnp.max(s[h], axis=-1, keepdims=True)) for h in H]
    p = [(e[h] / jnp.sum(e[h], axis=-1, keepdims=True)).astype(BF16) for h in H]
    for h in H:
        o_ref[:, sl[h]] = _dot(p[h], vb_ref[:, sl[h]]).astype(o_ref.dtype)


def _xattn_prompt(q, mk, mv, bp, seq):
    tq = _pick(seq, (512, 256, 128))
    nt = seq // tq
    kv = pl.BlockSpec((N_MEM, D_MODEL), lambda b, t: (b, 0))
    return pl.pallas_call(
        _xattn_prompt_body,
        grid=(bp, nt),
        in_specs=[pl.BlockSpec((tq, D_MODEL), lambda b, t: (b * nt + t, 0)), kv, kv],
        out_specs=pl.BlockSpec((tq, D_MODEL), lambda b, t: (b * nt + t, 0)),
        out_shape=jax.ShapeDtypeStruct((q.shape[0], D_MODEL), BF16),
        scratch_shapes=[pltpu.VMEM((N_MEM, D_MODEL), BF16), pltpu.VMEM((N_MEM, D_MODEL), BF16)],
        compiler_params=_cparams("parallel", "arbitrary"),
        name="xattn_prompt",
    )(q, mk, mv)


XA_SAMPLES_PER_STEP = 2


def _xattn_sample_body(q_ref, k_ref, v_ref, obuf_ref, o_ref, acc_ref, *, nsteps):
    del obuf_ref
    i = pl.program_id(0)
    H = XA_HEADS
    groups = N_MEM * H // (2 * H)
    for j in range(XA_SAMPLES_PER_STEP):
        b = i * XA_SAMPLES_PER_STEP + j
        k3 = k_ref[j].reshape(N_MEM * H, XA_HD).reshape(groups, 2 * H, XA_HD)
        s = jnp.sum(k3 * q_ref[b], axis=-1, keepdims=True) * (XA_HD ** -0.5)
        mx = jnp.max(s, axis=0, keepdims=True)
        mx = jnp.maximum(mx, pltpu.roll(mx, H, 1))
        e = jnp.exp(s - mx)
        den = jnp.sum(e, axis=0, keepdims=True)
        den = den + pltpu.roll(den, H, 1)
        v3 = v_ref[j].reshape(N_MEM * H, XA_HD).reshape(groups, 2 * H, XA_HD)
        o8 = jnp.sum((e / den) * v3, axis=0)
        o = o8[0:H] + o8[H:2 * H]
        acc_ref[pl.ds(b, 1), :] = jnp.concatenate([o[h:h + 1, :] for h in range(H)], axis=1)

    @pl.when(i == nsteps - 1)
    def _():
        o_ref[...] = acc_ref[...].astype(o_ref.dtype)


def _xattn_sample(q, ck, cv, obuf, l, row0):
    bs = q.shape[0]
    nb = XA_SAMPLES_PER_STEP
    kv = pl.BlockSpec((None, nb, N_MEM, XA_HEADS, XA_HD), lambda i: (l, i, 0, 0, 0))
    return pl.pallas_call(
        functools.partial(_xattn_sample_body, nsteps=bs // nb),
        grid=(bs // nb,),
        in_specs=[pl.BlockSpec((bs, 2 * XA_HEADS, XA_HD), lambda i: (0, 0, 0)), kv, kv, ANY_SPEC],
        out_specs=pl.BlockSpec((bs, D_MODEL), lambda i: (row0 // bs, 0)),
        out_shape=jax.ShapeDtypeStruct(obuf.shape, obuf.dtype),
        scratch_shapes=[pltpu.VMEM((bs, D_MODEL), F32)],
        input_output_aliases={3: 0},
        compiler_params=_cparams("arbitrary"),
        name="xattn_sample",
    )(q, ck, cv, obuf)


def _alias_args(prev):
    return ([], []) if prev is None else ([prev], [ANY_SPEC])


def _sample_ad_body(*refs, has_prev):
    (ab_ref, ac_ref, ah_ref, sta_ref, wa_ref, d1a_ref, d1b_ref, d2a_ref, d2b_ref, std_ref, wd_ref, bias_ref, g_ref,
     beta_ref) = refs[:14]
    ya_ref, yd_ref, na_ref, nd_ref = refs[14 + 2 * has_prev:]
    u = ac_ref[...] * ah_ref[...]
    s0 = sta_ref[:, 0, :]
    s1 = sta_ref[:, 1, :]
    ya = ab_ref[...] * (wa_ref[0:1, :] * s0 + wa_ref[1:2, :] * s1 + wa_ref[2:3, :] * u)
    ya_ref[...] = ya.astype(ya_ref.dtype)
    na_ref[:, 0, :] = s1
    na_ref[:, 1, :] = u
    d1 = jnp.concatenate([d1a_ref[...], d1b_ref[...]], axis=1)
    d2 = jnp.concatenate([d2a_ref[...], d2b_ref[...]], axis=1)
    ud = d1 * _sigmoid(d2)
    acc = jnp.broadcast_to(bias_ref[...], ud.shape)
    nh = D_CONV - 1
    for k in range(nh):
        s_k = std_ref[k]
        acc = acc + s_k * wd_ref[k:k + 1, :]
        if k > 0:
            nd_ref[k - 1] = s_k
    acc = acc + ud * wd_ref[nh:nh + 1, :]
    nd_ref[nh - 1] = ud
    c = _layernorm(acc, EPS) * g_ref[...] + beta_ref[...]
    yd_ref[...] = (c * _sigmoid(c)).astype(yd_ref.dtype)


def _sample_ad(proj, st_a, st_d, wa, wd, bias, g, beta, l, row0, prev_a, prev_d):
    bs = st_a.shape[1]
    G = G_WIDTH
    tb = 32
    r0 = row0 // tb
    half = G // 2
    cola = lambda j: pl.BlockSpec((tb, G), lambda i: (r0 + i, COL_A // G + j))
    cold = lambda j: pl.BlockSpec((tb, half), lambda i: (r0 + i, COL_D // half + j))
    sta = pl.BlockSpec((None, tb, A_CONV - 1, G), lambda i: (l, i, 0, 0))
    std = pl.BlockSpec((None, D_CONV - 1, tb, G), lambda i: (l, 0, i, 0))
    yrow = pl.BlockSpec((tb, G), lambda i: (i, 0))
    (bias, bias_s), (g, g_s), (beta, beta_s) = _layer_vec(bias, l), _layer_vec(g, l), _layer_vec(beta, l)
    nh = D_CONV - 1
    has_prev = prev_a is not None
    extra = [prev_a, prev_d] if has_prev else []
    return pl.pallas_call(
        functools.partial(_sample_ad_body, has_prev=has_prev),
        grid=(bs // tb,),
        in_specs=[cola(0), cola(1), cola(2), sta, _layer_mat(wa, l), cold(0), cold(1), cold(2), cold(3), std,
                  _layer_mat(wd, l), bias_s, g_s, beta_s] + [ANY_SPEC] * len(extra),
        out_specs=[yrow, yrow, sta, std],
        out_shape=[jax.ShapeDtypeStruct((bs, G), BF16), jax.ShapeDtypeStruct((bs, G), BF16),
                   jax.ShapeDtypeStruct(st_a.shape, F32), jax.ShapeDtypeStruct(st_d.shape, F32)],
        input_output_aliases={14: 2, 15: 3} if has_prev else {},
        compiler_params=_cparams("parallel"),
        name="sample_conv_ad",
    )(proj, proj, proj, st_a, wa, proj, proj, proj, proj, st_d, wd, bias, g, beta, *extra)


RET_D_BLOCK = 16


def _ret_sample_body(*refs, has_prev):
    q_ref, k_ref, v_ref, g_ref, cos_ref, sin_ref, s_ref = refs[:7]
    y_ref, ns_ref, qt_ref, kt_ref, acc_ref = refs[7 + has_prev:]
    h = pl.program_id(0)
    j = pl.program_id(1)
    nd = RET_D // RET_D_BLOCK
    gammas = [math.exp(lg) for lg in RET_LOG_DECAY]
    gamma = jnp.where(h == 0, gammas[0], jnp.where(h == 1, gammas[1], jnp.where(h == 2, gammas[2], gammas[3])))
    gamma = gamma.astype(F32)

    @pl.when(j == 0)
    def _():
        q = _rope(q_ref[...], cos_ref[...], sin_ref[...])
        k = _rope(k_ref[...], cos_ref[...], sin_ref[...]) * (RET_D ** -0.5)
        qt_ref[...] = q.T
        kt_ref[...] = k.T
        acc_ref[...] = jnp.zeros_like(acc_ref)

    vt = v_ref[...].T
    acc = acc_ref[...]
    for d in range(RET_D_BLOCK):
        s_d = s_ref[:, d, :].T
        row = j * RET_D_BLOCK + d
        q_d = qt_ref[pl.ds(row, 1), :]
        k_d = kt_ref[pl.ds(row, 1), :]
        acc = acc + q_d * s_d
        ns_ref[:, d, :] = (gamma * s_d + k_d * vt).T
    acc_ref[...] = acc

    @pl.when(j == nd - 1)
    def _():
        qk = jnp.sum(qt_ref[...] * kt_ref[...], axis=0, keepdims=True)
        o = qk * vt + acc * gamma
        mu = jnp.mean(o, axis=0, keepdims=True)
        oc = o - mu
        var = jnp.mean(oc * oc, axis=0, keepdims=True)
        ln = (oc * lax.rsqrt(var + EPS)).T
        g = g_ref[...]
        y_ref[...] = (g * _sigmoid(g) * ln).astype(y_ref.dtype)


def _retention_sample(proj, cos2, sin2, state, l, row0, prev):
    bs = state.shape[1]
    r0 = row0 // bs
    nd = RET_D // RET_D_BLOCK
    col = lambda c: pl.BlockSpec((bs, RET_D), lambda h, j: (r0, COL_B // RET_D + c * RET_HEADS + h))
    tab = pl.BlockSpec((1, RET_D), lambda h, j: (0, 0))
    sblk = pl.BlockSpec((None, bs, None, RET_D_BLOCK, RET_D), lambda h, j: (l, 0, h, j, 0))
    extra, extra_specs = _alias_args(prev)
    return pl.pallas_call(
        functools.partial(_ret_sample_body, has_prev=prev is not None),
        grid=(RET_HEADS, nd),
        in_specs=[col(0), col(1), col(2), col(3), tab, tab, sblk] + extra_specs,
        out_specs=[pl.BlockSpec((bs, RET_D), lambda h, j: (0, h)), sblk],
        out_shape=[jax.ShapeDtypeStruct((bs, G_WIDTH), BF16), jax.ShapeDtypeStruct(state.shape, F32)],
        scratch_shapes=[pltpu.VMEM((RET_D, bs), F32), pltpu.VMEM((RET_D, bs), F32), pltpu.VMEM((RET_D, bs), F32)],
        input_output_aliases={7: 1} if prev is not None else {},
        compiler_params=_cparams("parallel", "arbitrary"),
        name="retention_sample",
    )(proj, proj, proj, proj, cos2, sin2, state, *extra)


def _rwkv_sample_body(*refs, has_prev):
    (xr_ref, xk_ref, xv_ref, xl_ref, pr_ref, pk_ref, pv_ref, pl_ref, mr_ref, mk_ref, mv_ref, ml_ref,
     w2w_ref, w2a_ref, g2_ref, w0_ref, a0_ref, kk_ref, ka_ref, rk_ref, lg_ref, lb_ref, s_ref) = refs[:23]
    y_ref, ns_ref, vec_ref, ot_ref = refs[23 + has_prev:]
    N = RWKV_HEAD

    def shifted(x_ref, prev_ref, mu_ref):
        x = x_ref[...]
        return x + (prev_ref[...] - x) * mu_ref[...]

    r, lw, k4, v, kk, a, gate = _rwkv_pre(shifted(xr_ref, pr_ref, mr_ref), shifted(xk_ref, pk_ref, mk_ref),
                                          shifted(xv_ref, pv_ref, mv_ref), shifted(xl_ref, pl_ref, ml_ref),
                                          w2w_ref[...], w2a_ref[...], g2_ref[...], w0_ref[...], a0_ref[...],
                                          kk_ref[...], ka_ref[...])
    kkn = jnp.concatenate([_l2_normalize(kk[:, 0:N]), _l2_normalize(kk[:, N:2 * N])], axis=1)
    w = jnp.exp(lw)
    vec_ref[0] = kkn.T
    vec_ref[1] = w.T
    vec_ref[2] = (kkn * a).T
    vec_ref[3] = k4.T
    vec_ref[4] = r.T
    vec_ref[5] = v.T
    for hh in range(2):
        ch = slice(hh * N, (hh + 1) * N)
        kk_t = vec_ref[0, ch, :]
        w_t = vec_ref[1, ch, :]
        kka_t = vec_ref[2, ch, :]
        k_t = vec_ref[3, ch, :]
        r_t = vec_ref[4, ch, :]
        for i in range(N):
            s_i = s_ref[hh, i]
            sa = -jnp.sum(s_i * kk_t, axis=0, keepdims=True)
            v_i = vec_ref[5, hh * N + i:hh * N + i + 1, :]
            s_new = s_i * w_t + sa * kka_t + v_i * k_t
            ns_ref[hh, i] = s_new
            ot_ref[hh * N + i:hh * N + i + 1, :] = jnp.sum(s_new * r_t, axis=0, keepdims=True)
    o = ot_ref[...].T
    rk = rk_ref[...]
    lg = lg_ref[...]
    lb = lb_ref[...]
    ys = []
    for hh in range(2):
        ch = slice(hh * N, (hh + 1) * N)
        ys.append(_rwkv_post(o[:, ch], r[:, ch], k4[:, ch], v[:, ch], gate[:, ch], rk[:, ch], lg[:, ch], lb[:, ch]))
    y_ref[...] = jnp.concatenate(ys, axis=1).astype(y_ref.dtype)


def _rwkv_sample(proj, prev_shift, rw_arrs, state, l, row0, prev):
    mu, w2a, g2, w0, a0, k_k, k_a, r_k, lnx_g, lnx_b = rw_arrs
    bs = state.shape[-1]
    r0 = row0 // bs
    G = G_WIDTH
    N = RWKV_HEAD
    W = 2 * N
    npair = G // W
    def cols(shape_lead, lead_idx, base):
        mk = lambda width, cidx: pl.BlockSpec(shape_lead + (width,), lambda p: lead_idx + (cidx(p),))
        return [mk(W, lambda p, c=c: base // W + c * npair + p) for c in range(3)] + [
            mk(2 * W, lambda p: (base + 3 * G) // (2 * W))]
    pairv = pl.BlockSpec((None, 1, W), lambda p: (l, 0, p))
    pairm = pl.BlockSpec((None, 128, W), lambda p: (l, 0, p))
    sblk = pl.BlockSpec((None, 2, N, N, bs), lambda p: (l, p, 0, 0, 0))
    extra, extra_specs = _alias_args(prev)
    return pl.pallas_call(
        functools.partial(_rwkv_sample_body, has_prev=prev is not None),
        grid=(npair,),
        in_specs=cols((bs,), (r0,), COL_C) + cols((None, bs), (l, 0), 0) + cols((None, 1), (l, 0), 0) + [
            pairm, pl.BlockSpec((None, 128, W), lambda p: (l, 0, npair + p)), pairm,
            pairv, pairv, pairv, pairv, pairv, pairv, pairv, sblk] + extra_specs,
        out_specs=[pl.BlockSpec((bs, W), lambda p: (0, p)), sblk],
        out_shape=[jax.ShapeDtypeStruct((bs, G), BF16), jax.ShapeDtypeStruct(state.shape, F32)],
        scratch_shapes=[pltpu.VMEM((6, W, bs), F32), pltpu.VMEM((W, bs), F32)],
        input_output_aliases={23: 1} if prev is not None else {},
        compiler_params=_cparams("parallel"),
        name="rwkv_sample",
    )(proj, proj, proj, proj, prev_shift, prev_shift, prev_shift, prev_shift, mu, mu, mu, mu, w2a, w2a, g2,
      w0, a0, k_k, k_a, r_k, lnx_g, lnx_b, state, *extra)


def _place_cols_body(src_ref, buf_ref, o_ref):
    del buf_ref
    o_ref[...] = src_ref[...]


def _place_cols(src, buf, col):
    rows, w = src.shape
    tr = _pick(rows, (1024, 512, 256, 128))
    return pl.pallas_call(
        _place_cols_body,
        grid=(rows // tr,),
        in_specs=[pl.BlockSpec((tr, w), lambda i: (i, 0)), ANY_SPEC],
        out_specs=pl.BlockSpec((tr, w), lambda i: (i, col)),
        out_shape=jax.ShapeDtypeStruct(buf.shape, buf.dtype),
        input_output_aliases={1: 0},
        compiler_params=_cparams("parallel"),
        name="place_cols",
    )(src, buf)


def _place_heads_body(*refs):
    src_ref, o_ref = refs[0], refs[-1]
    for h in range(XA_HEADS):
        o_ref[0, :, h, :] = src_ref[:, h * XA_HD:(h + 1) * XA_HD]


def _place_heads(src, l, depth, bp, prev):
    extra, extra_specs = _alias_args(prev)
    return pl.pallas_call(
        _place_heads_body,
        grid=(bp,),
        in_specs=[pl.BlockSpec((N_MEM, D_MODEL), lambda b: (b, 0))] + extra_specs,
        out_specs=pl.BlockSpec((None, 1, N_MEM, XA_HEADS, XA_HD), lambda b: (l, b, 0, 0, 0)),
        out_shape=jax.ShapeDtypeStruct((depth, bp, N_MEM, XA_HEADS, XA_HD), F32),
        input_output_aliases={1: 0} if prev is not None else {},
        compiler_params=_cparams("parallel"),
        name="place_heads",
    )(src, *extra)


def _assemble_body(a_ref, b_ref, c_ref, d_ref, buf_ref, o_ref):
    del buf_ref
    o_ref[...] = jnp.concatenate([a_ref[...], b_ref[...], c_ref[...], d_ref[...]], axis=1)


def _assemble_rows(parts, buf, row0):
    bs = parts[0].shape[0]
    part = pl.BlockSpec((bs, G_WIDTH), lambda i: (0, 0))
    return pl.pallas_call(
        _assemble_body,
        grid=(1,),
        in_specs=[part, part, part, part, ANY_SPEC],
        out_specs=pl.BlockSpec((bs, D_MODEL), lambda i: (row0 // bs, 0)),
        out_shape=jax.ShapeDtypeStruct(buf.shape, buf.dtype),
        input_output_aliases={4: 0},
        compiler_params=_cparams("arbitrary"),
        name="assemble_sample_rows",
    )(*parts, buf)


def kernel(x_prompt, x_sample, mem_prompt, state_conv_a, state_ret, state_shift, state_wkv, state_conv_d, cache_mem_k, cache_mem_v, g_mix, w_in, conv_a_w, mu_c, w0, w2, a0, a2, g2, k_k, k_a, r_k, lnx_g, lnx_b, conv_d_w, conv_d_b, ln_d_g, ln_d_b, w_out, g_xa, g_mem, wq_x, wk_x, wv_x, wo_x, g_mlp, w_up, w_down, g_final):
    bp, seq, d = x_prompt.shape
    bs = x_sample.shape[0]
    depth = w_in.shape[0]
    np_rows = bp * seq
    G = G_WIDTH
    x = jnp.concatenate([x_prompt.reshape(np_rows, d), x_sample.reshape(bs, d)], axis=0)
    mem = mem_prompt.reshape(bp * N_MEM, d)
    cos_p, sin_p = _rope_tables(jnp.arange(seq))
    cos_s, sin_s = _rope_tables(PAST_LEN + jnp.arange(1))

    z64 = jnp.zeros((depth, 64, G), F32)
    w2a = jnp.concatenate([jnp.concatenate([w2, z64], axis=2), jnp.concatenate([z64, a2], axis=2)], axis=1).astype(BF16)
    g2_b = g2.astype(BF16)
    wkv_t = jnp.transpose(state_wkv, (0, 2, 3, 4, 1))
    conv_d_t = jnp.transpose(state_conv_d, (0, 2, 1, 3))
    vec3 = lambda a: a.reshape(depth, 1, -1)
    rw_arrs = (vec3(mu_c), w2a, g2_b, vec3(w0), vec3(a0), vec3(k_k), vec3(k_a), vec3(r_k), vec3(lnx_g), vec3(lnx_b))

    small = {k: [] for k in ("a_p", "ret_p", "sh_p", "sh_s", "wkv_p", "d_p")}
    na_s = nd_s = nret_s = nwkv_s = mk_all = mv_all = None
    for l in range(depth):
        proj = _matmul_ws(x, w_in, l, norm_g=_layer_vec(g_mix, l), tn=1280, tm=_pick(x.shape[0], (640, 512, 256, 128)))
        ymix, na_p = _conv_a_prompt(proj, conv_a_w, l, bp, seq)
        ymix, nd_p = _conv_d_prompt(proj, conv_d_w, conv_d_b, ln_d_g, ln_d_b, ymix, l, bp, seq)
        yb_p, nret_p = _retention_prompt(proj, cos_p, sin_p, bp, seq)
        ymix = _place_cols(yb_p.reshape(np_rows, G), ymix, 1)
        rw = [(rw_arrs[0], pl.BlockSpec((None, 1, SHIFT_W), lambda *_: (l, 0, 0))),
              (w2a, _layer_mat(w2a, l)), (g2_b, _layer_mat(g2_b, l))] + [
              (a, pl.BlockSpec((None, 1, G), lambda *_: (l, 0, 0))) for a in rw_arrs[3:]]
        yc_p, nsh_p, nwkv_p = _rwkv_prompt(proj, rw, bp, seq)
        ymix = _place_cols(yc_p.reshape(np_rows, G), ymix, 2)
        ya_s, yd_s, na_s, nd_s = _sample_ad(proj, state_conv_a, conv_d_t, conv_a_w, conv_d_w, conv_d_b, ln_d_g, ln_d_b,
                                            l, np_rows, na_s, nd_s)
        yb_s, nret_s = _retention_sample(proj, cos_s, sin_s, state_ret, l, np_rows, nret_s)
        yc_s, nwkv_s = _rwkv_sample(proj, state_shift, rw_arrs, wkv_t, l, np_rows, nwkv_s)
        ymix = _assemble_rows([ya_s, yb_s, yc_s, yd_s], ymix, np_rows)
        x = _matmul_ws(ymix, w_out, l, residual=x)

        mk = _matmul_ws(mem, wk_x, l, norm_g=_layer_vec(g_mem, l))
        mv = _matmul_ws(mem, wv_x, l, norm_g=_layer_vec(g_mem, l))
        mk_all = _place_heads(mk, l, depth, bp, mk_all)
        mv_all = _place_heads(mv, l, depth, bp, mv_all)
        q = _matmul_ws(x, wq_x, l, norm_g=_layer_vec(g_xa, l))
        obuf = _xattn_prompt(q, mk, mv, bp, seq)
        q_s = q[np_rows:].reshape(bs, XA_HEADS, XA_HD)
        q_s = jnp.concatenate([q_s, q_s], axis=1)
        obuf = _xattn_sample(q_s, cache_mem_k, cache_mem_v, obuf, l, np_rows)
        x = _matmul_ws(obuf, wo_x, l, residual=x)

        up = _matmul_ws(x, w_up, l, norm_g=_layer_vec(g_mlp, l), act="relu2", out_dtype=BF16)
        for kb in range(w_down.shape[1] // d):
            x = _matmul_ws(up, w_down, l, residual=x, kblock=(kb, d))

        small["a_p"].append(na_p)
        small["ret_p"].append(nret_p)
        small["sh_p"].append(nsh_p.reshape(bp, SHIFT_W))
        small["sh_s"].append(proj[np_rows:, COL_C:COL_C + SHIFT_W])
        small["wkv_p"].append(nwkv_p)
        small["d_p"].append(nd_p)
    gf = (g_final.reshape(1, d), pl.BlockSpec((1, d), lambda *_: (0, 0)))
    y_p = _rmsnorm(x, gf, F32, 0, np_rows)
    y_s = _rmsnorm(x, gf, F32, np_rows, bs)
    st = lambda k: jnp.stack(small[k])
    return (y_p.reshape(bp, seq, d), y_s.reshape(bs, 1, d), st("a_p"), na_s, st("ret_p"), nret_s,
            st("sh_p"), st("sh_s"), st("wkv_p"), jnp.transpose(nwkv_s, (0, 4, 1, 2, 3)),
            st("d_p"), jnp.transpose(nd_s, (0, 2, 1, 3)), mk_all, mv_all)
```

```python
import functools
import math

import jax
import jax.numpy as jnp
from jax import lax
from jax.experimental import pallas as pl
from jax.experimental.pallas import tpu as pltpu

F32 = jnp.float32
BF16 = jnp.bfloat16
HI = lax.Precision.HIGHEST

D_MODEL = 2048
G_WIDTH = 512
P_IN = 6400
RET_HEADS = 4
RET_D = 128
RET_CHUNK = 128
ROPE_BASE = 10000.0
RWKV_HEAD = 64
RWKV_HEADS = 8
WKV_CHUNK = 64
RWKV_GN_EPS = 64e-5
A_CONV = 3
D_CONV = 31
D_HIST = 32
N_MEM = 256
XA_HEADS = 4
XA_HD = 512
SHIFT_W = 1792
PAST_LEN = 16384
EPS = 1e-6
RET_LOG_DECAY = tuple(math.log1p(-(2.0 ** (-5.0 - h))) for h in range(RET_HEADS))
VMEM_LIMIT = 56 * 1024 * 1024
COL_A = 0
COL_B = 3 * G_WIDTH
COL_C = 7 * G_WIDTH
COL_D = 7 * G_WIDTH + SHIFT_W

ANY_SPEC = pl.BlockSpec(memory_space=pl.ANY)


def _cparams(*sem):
    return pltpu.CompilerParams(dimension_semantics=sem, vmem_limit_bytes=VMEM_LIMIT)


def _pick(n, cands):
    for c in cands:
        if n % c == 0:
            return c
    raise ValueError(f"no tile for {n}")


def _sigmoid(x):
    return 1.0 / (1.0 + jnp.exp(-x))


def _dot(a, b, precision=None):
    return jnp.dot(a, b, preferred_element_type=F32, precision=precision)


def _dot_nt(a, b, precision=None):
    return lax.dot_general(a, b, (((1,), (1,)), ((), ())), preferred_element_type=F32, precision=precision)


def _dot_tn(a, b, precision=None):
    return lax.dot_general(a, b, (((0,), (0,)), ((), ())), preferred_element_type=F32, precision=precision)


def _layernorm(x, eps):
    mu = jnp.mean(x, axis=-1, keepdims=True)
    xc = x - mu
    var = jnp.mean(xc * xc, axis=-1, keepdims=True)
    return xc * lax.rsqrt(var + eps)


def _layer_vec(a, l):
    n = a.shape[-1]
    return a.reshape(a.shape[0], 1, n), pl.BlockSpec((None, 1, n), lambda *_: (l, 0, 0))


def _layer_mat(a, l):
    return pl.BlockSpec((None,) + a.shape[1:], lambda *_: (l, 0, 0))


def _rmsnorm_body(x_ref, g_ref, o_ref):
    x = x_ref[...]
    ms = jnp.mean(x * x, axis=-1, keepdims=True)
    o_ref[...] = ((x * lax.rsqrt(ms + EPS)) * g_ref[...]).astype(o_ref.dtype)


def _rmsnorm(x, g_spec, out_dtype, row0=0, nrows=None):
    m, d = x.shape
    nrows = nrows or m
    tm = _pick(math.gcd(nrows, row0) if row0 else nrows, (1040, 1024, 640, 512, 256, 128))
    r0 = row0 // tm
    g, gspec = g_spec
    return pl.pallas_call(
        _rmsnorm_body,
        grid=(nrows // tm,),
        in_specs=[pl.BlockSpec((tm, d), lambda i: (r0 + i, 0)), gspec],
        out_specs=pl.BlockSpec((tm, d), lambda i: (i, 0)),
        out_shape=jax.ShapeDtypeStruct((nrows, d), out_dtype),
        compiler_params=_cparams("parallel"),
        name="rmsnorm",
    )(x, g)


def _mm_ws_body(*refs, act, has_res, has_norm):
    a_ref, w_ref = refs[0], refs[1]
    g_ref = refs[2] if has_norm else None
    res_ref = refs[2 + has_norm] if has_res else None
    o_ref = refs[2 + has_norm + has_res]
    wb_ref = refs[-1]

    @pl.when(pl.program_id(1) == 0)
    def _():
        wb_ref[...] = w_ref[...].astype(BF16)

    a = a_ref[...]
    if has_norm:
        ms = jnp.mean(a * a, axis=-1, keepdims=True)
        a = ((a * lax.rsqrt(ms + EPS)) * g_ref[...]).astype(BF16)
    r = _dot(a, wb_ref[...])
    if act == "relu2":
        r = jnp.square(jnp.maximum(r, 0.0))
    if has_res:
        r = res_ref[...] + r
    o_ref[...] = r.astype(o_ref.dtype)


def _matmul_ws(a, w, l, *, norm_g=None, residual=None, act=None, out_dtype=F32, tn=1024, tm=None, kblock=None):
    m = a.shape[0]
    kb, kdim = kblock if kblock is not None else (0, a.shape[1])
    n = w.shape[2]
    tm = tm or _pick(m, (1040, 1024, 640, 512, 256, 128))
    has_res = residual is not None
    has_norm = norm_g is not None
    in_specs = [pl.BlockSpec((tm, kdim), lambda j, i: (i, kb)), pl.BlockSpec((None, kdim, tn), lambda j, i: (l, kb, j))]
    args = [a, w]
    if has_norm:
        args.append(norm_g[0])
        in_specs.append(norm_g[1])
    if has_res:
        in_specs.append(pl.BlockSpec((tm, tn), lambda j, i: (i, j)))
        args.append(residual)
    return pl.pallas_call(
        functools.partial(_mm_ws_body, act=act, has_res=has_res, has_norm=has_norm),
        grid=(n // tn, m // tm),
        in_specs=in_specs,
        out_specs=pl.BlockSpec((tm, tn), lambda j, i: (i, j)),
        out_shape=jax.ShapeDtypeStruct((m, n), out_dtype),
        scratch_shapes=[pltpu.VMEM((kdim, tn), BF16)],
        compiler_params=_cparams("parallel", "arbitrary"),
        name="matmul_ws",
    )(*args)


def _conv_a_body(b_ref, c_ref, h_ref, w_ref, y_ref, st_ref, ext_ref, *, tl):
    t = pl.program_id(1)

    @pl.when(t == 0)
    def _():
        ext_ref[0:8, :] = jnp.zeros((8, G_WIDTH), F32)

    u = c_ref[...] * h_ref[...]
    ext_ref[8:, :] = u
    u1 = ext_ref[7:7 + tl, :]
    u2 = ext_ref[6:6 + tl, :]
    y = b_ref[...] * (w_ref[0:1, :] * u2 + w_ref[1:2, :] * u1 + w_ref[2:3, :] * u)
    y_ref[...] = y.astype(y_ref.dtype)
    st_ref[0] = ext_ref[tl + 6:tl + 8, :]
    ext_ref[0:8, :] = ext_ref[tl:tl + 8, :]


def _conv_a_prompt(proj, conv_w, l, bp, seq):
    tl = _pick(seq, (512, 256, 128))
    nt = seq // tl
    c0 = COL_A // G_WIDTH
    col = lambda j: pl.BlockSpec((tl, G_WIDTH), lambda b, t: (b * nt + t, c0 + j))
    return pl.pallas_call(
        functools.partial(_conv_a_body, tl=tl),
        grid=(bp, nt),
        in_specs=[col(0), col(1), col(2), _layer_mat(conv_w, l)],
        out_specs=[pl.BlockSpec((tl, G_WIDTH), lambda b, t: (b * nt + t, 0)),
                   pl.BlockSpec((1, A_CONV - 1, G_WIDTH), lambda b, t: (b, 0, 0))],
        out_shape=[jax.ShapeDtypeStruct((proj.shape[0], D_MODEL), BF16),
                   jax.ShapeDtypeStruct((bp, A_CONV - 1, G_WIDTH), F32)],
        scratch_shapes=[pltpu.VMEM((tl + 8, G_WIDTH), F32)],
        compiler_params=_cparams("parallel", "arbitrary"),
        name="conv_a_prompt",
    )(proj, proj, proj, conv_w)


CONV_D_ROWS = 64


def _conv_d_body(d1a_ref, d1b_ref, d2a_ref, d2b_ref, w_ref, bias_ref, g_ref, beta_ref, ymix_ref, y_ref, st_ref, ext_ref,
                 acc_ref, *, tl):
    del ymix_ref
    t = pl.program_id(1)
    next_ = D_HIST + tl

    @pl.when(t == 0)
    def _():
        ext_ref[0, 0:D_HIST, :] = jnp.zeros((D_HIST, G_WIDTH), F32)

    d1 = jnp.concatenate([d1a_ref[...], d1b_ref[...]], axis=1)
    d2 = jnp.concatenate([d2a_ref[...], d2b_ref[...]], axis=1)
    ext_ref[0, D_HIST:, :] = d1 * _sigmoid(d2)
    ext = ext_ref[0]
    for s in range(1, 8):
        ext_ref[s] = pltpu.roll(ext, next_ - s, 0)
    first = D_HIST - (D_CONV - 1)

    def blk(i, carry):
        base = pl.multiple_of(i * CONV_D_ROWS, CONV_D_ROWS)
        acc = jnp.broadcast_to(bias_ref[...], (CONV_D_ROWS, G_WIDTH))
        for k in range(D_CONV):
            s = (first + k) % 8
            a0 = first + k - s
            acc = acc + ext_ref[s, pl.ds(base + a0, CONV_D_ROWS), :] * w_ref[k:k + 1, :]
        acc_ref[pl.ds(base, CONV_D_ROWS), :] = acc
        return carry

    lax.fori_loop(0, tl // CONV_D_ROWS, blk, 0)
    c = _layernorm(acc_ref[...], EPS) * g_ref[...] + beta_ref[...]
    y_ref[...] = (c * _sigmoid(c)).astype(y_ref.dtype)
    st_ref[0] = ext_ref[0, tl + first:tl + D_HIST, :]
    ext_ref[0, 0:D_HIST, :] = ext_ref[0, tl:tl + D_HIST, :]


def _conv_d_prompt(proj, w, bias, g, beta, ymix, l, bp, seq):
    tl = _pick(seq, (512, 256, 128))
    nt = seq // tl
    half = G_WIDTH // 2
    c0 = COL_D // half
    col = lambda j: pl.BlockSpec((tl, half), lambda b, t: (b * nt + t, c0 + j))
    (bias, bias_s), (g, g_s), (beta, beta_s) = _layer_vec(bias, l), _layer_vec(g, l), _layer_vec(beta, l)
    return pl.pallas_call(
        functools.partial(_conv_d_body, tl=tl),
        grid=(bp, nt),
        in_specs=[col(0), col(1), col(2), col(3), _layer_mat(w, l), bias_s, g_s, beta_s, ANY_SPEC],
        out_specs=[pl.BlockSpec((tl, G_WIDTH), lambda b, t: (b * nt + t, 3)),
                   pl.BlockSpec((1, D_CONV - 1, G_WIDTH), lambda b, t: (b, 0, 0))],
        out_shape=[jax.ShapeDtypeStruct(ymix.shape, ymix.dtype),
                   jax.ShapeDtypeStruct((bp, D_CONV - 1, G_WIDTH), F32)],
        scratch_shapes=[pltpu.VMEM((8, tl + D_HIST, G_WIDTH), F32), pltpu.VMEM((tl, G_WIDTH), F32)],
        input_output_aliases={8: 0},
        compiler_params=_cparams("parallel", "arbitrary"),
        name="conv_d_prompt",
    )(proj, proj, proj, proj, w, bias, g, beta, ymix)


def _rope_tables(pos):
    inv = ROPE_BASE ** (-jnp.arange(0, RET_D, 2, dtype=F32) / RET_D)
    ang = pos.astype(F32)[:, None] * inv[None, :]
    cos = jnp.cos(ang)
    sin = jnp.sin(ang)
    return jnp.concatenate([cos, cos], axis=-1), jnp.concatenate([-sin, sin], axis=-1)


def _rope(x, cos2, sin2):
    return x * cos2 + pltpu.roll(x, RET_D // 2, 1) * sin2


def _ret_body(*refs, nb, nchunks):
    qkvg = refs[:4 * nb]
    cos_ref, sin_ref, y_ref, st_ref, s_ref = refs[4 * nb:]
    c = pl.program_id(0)
    C = RET_CHUNK

    @pl.when(c == 0)
    def _():
        s_ref[...] = jnp.zeros_like(s_ref)

    ii = lax.broadcasted_iota(jnp.int32, (C, C), 0).astype(F32)
    jj = lax.broadcasted_iota(jnp.int32, (C, C), 1).astype(F32)
    diff = ii - jj
    cos2 = cos_ref[...]
    sin2 = sin_ref[...]
    NH = RET_HEADS
    chains = [(b, h) for b in range(nb) for h in range(NH)]
    I = range(len(chains))
    sl = [slice(h * RET_D, (h + 1) * RET_D) for h in range(NH)]
    lg = RET_LOG_DECAY
    q = [_rope(qkvg[4 * b][:, sl[h]], cos2, sin2) for b, h in chains]
    k = [_rope(qkvg[4 * b + 1][:, sl[h]], cos2, sin2) * (RET_D ** -0.5) for b, h in chains]
    qb = [q[i].astype(BF16) for i in I]
    vb = [qkvg[4 * b + 2][:, sl[h]].astype(BF16) for b, h in chains]
    s_old = [s_ref[i] for i in I]
    scores = [_dot_nt(qb[i], k[i].astype(BF16)) for i in I]
    cross = [_dot(qb[i], s_old[i].astype(BF16)) for i in I]
    k_dec = [jnp.exp(lg[h] * (C - 1.0 - ii)) for h in range(NH)]
    kv = [_dot_tn((k[i] * k_dec[chains[i][1]]).astype(BF16), vb[i]) for i in I]
    dmat = [jnp.where(diff >= 0.0, jnp.exp(lg[h] * jnp.maximum(diff, 0.0)), 0.0) for h in range(NH)]
    inner = [_dot((scores[i] * dmat[chains[i][1]]).astype(BF16), vb[i]) for i in I]
    q_dec = [jnp.exp(lg[h] * (ii + 1.0)) for h in range(NH)]
    for i, (b, h) in enumerate(chains):
        s_ref[i] = math.exp(lg[h] * C) * s_old[i] + kv[i]
        o = _layernorm(inner[i] + cross[i] * q_dec[h], EPS)
        g = qkvg[4 * b + 3][:, sl[h]]
        y_ref[b, :, sl[h]] = (g * _sigmoid(g) * o).astype(y_ref.dtype)

    @pl.when(c == nchunks - 1)
    def _():
        for i, (b, h) in enumerate(chains):
            st_ref[b, h] = s_ref[i]


def _retention_prompt(proj, cos2, sin2, bp, seq):
    C = RET_CHUNK
    nc = seq // C
    c0 = COL_B // G_WIDTH
    cols = [pl.BlockSpec((C, G_WIDTH), lambda c, b=b, j=j: (b * nc + c, c0 + j)) for b in range(bp) for j in range(4)]
    tab = pl.BlockSpec((C, RET_D), lambda c: (c, 0))
    return pl.pallas_call(
        functools.partial(_ret_body, nb=bp, nchunks=nc),
        grid=(nc,),
        in_specs=cols + [tab, tab],
        out_specs=[pl.BlockSpec((bp, C, G_WIDTH), lambda c: (0, c, 0)),
                   pl.BlockSpec((bp, RET_HEADS, RET_D, RET_D), lambda c: (0, 0, 0, 0))],
        out_shape=[jax.ShapeDtypeStruct((bp, seq, G_WIDTH), BF16),
                   jax.ShapeDtypeStruct((bp, RET_HEADS, RET_D, RET_D), F32)],
        scratch_shapes=[pltpu.VMEM((bp * RET_HEADS, RET_D, RET_D), F32)],
        compiler_params=_cparams("arbitrary"),
        name="retention_prompt",
    )(*([proj] * (4 * bp)), cos2, sin2)


def _softplus(z):
    return jnp.maximum(z, 0.0) + jnp.log(1.0 + jnp.exp(-jnp.abs(z)))


def _rwkv_pre(r, kc, vc, lora_x, w2_w, w2_a, g2, w0, a0, k_k, k_a):
    lin = lora_x[:, 0:128]
    lane = lax.broadcasted_iota(jnp.int32, lin.shape, 1)
    lin = jnp.where(lane < 64, jnp.tanh(lin), lin).astype(BF16)
    gate = _dot(_sigmoid(lora_x[:, 128:256]).astype(BF16), g2)
    w_pre = -_softplus(-(w0 + _dot(lin, w2_w))) - 0.5
    log_decay = -jnp.exp(w_pre)
    a = _sigmoid(a0 + _dot(lin, w2_a))
    kk = kc * k_k
    k4 = kc * (1.0 + (a - 1.0) * k_a)
    return r, log_decay, k4, vc, kk, a, gate


def _l2_normalize(kk):
    ss = jnp.sum(kk * kk, axis=-1, keepdims=True)
    return kk * jnp.minimum(lax.rsqrt(ss), 1e12)


def _bdot(a, b):
    return _dot(a.astype(BF16), b.astype(BF16))


def _seg_sum(x, lo):
    s_lo = jnp.sum(jnp.where(lo, x, 0.0), axis=-1, keepdims=True)
    s_hi = jnp.sum(jnp.where(lo, 0.0, x), axis=-1, keepdims=True)
    return jnp.where(lo, s_lo, s_hi)


def _wkv_chunk(r, lw, cum, k, v, kk, a, zbd, mask2, eye2):
    T = WKV_CHUNK
    N = RWKV_HEAD
    P = range(len(r))
    HH = range(2)
    lo = lax.broadcasted_iota(jnp.int32, (1, 2 * N), 1) < N
    lo2 = (lax.broadcasted_iota(jnp.int32, (1, 4 * N), 1) & (2 * N - 1)) < N
    g_inc = [jnp.exp(cum[p]) for p in P]
    g_tot = [g_inc[p][T - 1:T, :] for p in P]
    ah = [-(kk[p] * jnp.exp(cum[p] - lw[p])) for p in P]
    rg = [r[p] * g_inc[p] for p in P]
    g_inv = [jnp.exp(-cum[p]) for p in P]
    bk = [jnp.concatenate([kk[p] * a[p] * g_inv[p], k[p] * g_inv[p]], axis=0) for p in P]
    ar = [jnp.concatenate([ah[p], rg[p]], axis=0).astype(BF16) for p in P]
    bkm = [[jnp.where(lo, bk[p], 0.0).astype(BF16), jnp.where(lo, 0.0, bk[p]).astype(BF16)] for p in P]
    x = [[jnp.where(mask2, _dot_nt(ar[p], bkm[p][hh]), 0.0) for hh in HH] for p in P]
    vb = [v[p].astype(BF16) for p in P]
    zero = jnp.zeros((T, 2 * N), BF16)
    vz = [jnp.concatenate([zero, vb[p]], axis=0) for p in P]
    xkv2 = [[_dot(x[p][hh].astype(BF16), vz[p]) for hh in HH] for p in P]
    xkv = [jnp.where(lo, xkv2[p][0], xkv2[p][1]) for p in P]
    acur = [[x[p][hh][0:T, 0:T] for hh in HH] for p in P]
    inv = [[eye2[0:T, 0:T] for hh in HH] for p in P]
    for it in range(5):
        y = [[_bdot(jnp.concatenate([acur[p][hh], inv[p][hh]], axis=0), acur[p][hh]) for hh in HH] for p in P]
        acur = [[y[p][hh][0:T] for hh in HH] for p in P]
        inv = [[inv[p][hh] + y[p][hh][T:2 * T] for hh in HH] for p in P]
    y = [[_bdot(inv[p][hh], acur[p][hh]) for hh in HH] for p in P]
    inv = [[(inv[p][hh] + y[p][hh]).astype(BF16) for hh in HH] for p in P]
    rhs = [jnp.concatenate([ah[p], xkv[p][0:T]], axis=1).astype(BF16) for p in P]
    wu2 = [[_dot(inv[p][hh], rhs[p]) for hh in HH] for p in P]
    wub = [jnp.where(lo2, wu2[p][0], wu2[p][1]).astype(BF16) for p in P]
    qo2 = [[_dot(x[p][hh][T:2 * T, 0:T].astype(BF16), wub[p]) for hh in HH] for p in P]
    qo = [jnp.where(lo2, qo2[p][0], qo2[p][1]) for p in P]
    lhs_t = [(bk[p] * g_tot[p]).astype(BF16) for p in P]
    rhs_t = [jnp.concatenate([wub[p], jnp.concatenate([zero, vb[p]], axis=1)], axis=0) for p in P]
    mn = [_dot_tn(lhs_t[p], rhs_t[p]) for p in P]
    mz = [jnp.where(lo, mn[p][0:N, 0:2 * N], mn[p][N:2 * N, 0:2 * N]) + jnp.where(eye2[0:N, :] > 0.0, g_tot[p], 0.0)
          for p in P]
    nz = [jnp.where(lo, mn[p][0:N, 2 * N:4 * N], mn[p][N:2 * N, 2 * N:4 * N]) for p in P]
    qm = [jnp.concatenate([rg[p] + qo[p][:, 0:2 * N], mz[p]], axis=0) for p in P]
    oz = [_bdot(qm[p], zbd[p]) for p in P]
    o = [oz[p][0:T] + qo[p][:, 2 * N:4 * N] + xkv[p][T:2 * T] for p in P]
    z_new = [oz[p][T:T + N] + nz[p] for p in P]
    zbd_new = [jnp.concatenate([jnp.where(lo, z_new[p], 0.0), jnp.where(lo, 0.0, z_new[p])], axis=0) for p in P]
    return o, zbd_new


def _rwkv_post(o, r, k4, v, gate, rk, lnx_g, lnx_b):
    gn = _layernorm(o, RWKV_GN_EPS) * lnx_g + lnx_b
    bonus = jnp.sum(r * k4 * rk, axis=-1, keepdims=True) * v
    return (gn + bonus) * gate


def _rwkv_prompt_body(*refs, nb, nchunks):
    pc_refs = refs[:nb]
    mu_ref, w2a_ref, g2_ref, w0_ref, a0_ref, kk_ref, ka_ref, rk_ref, lg_ref, lb_ref = refs[nb:nb + 10]
    y_ref, shift_ref, st_ref, carry_ref, z_ref = refs[nb + 10:]
    c = pl.program_id(0)
    T = WKV_CHUNK
    NH = RWKV_HEADS

    @pl.when(c == 0)
    def _():
        carry_ref[...] = jnp.zeros_like(carry_ref)
        z_ref[...] = jnp.zeros_like(z_ref)

    pcs, prevs = [], []
    for b in range(nb):
        pc = pc_refs[b][...]
        row = lax.broadcasted_iota(jnp.int32, pc.shape, 0)
        prevs.append(jnp.where(row == 0, carry_ref[b, 0:1, :], pltpu.roll(pc, 1, 0)))
        carry_ref[b, 0:1, :] = pc[T - 1:T, :]
        shift_ref[b] = pc[T - 1:T, :]
        pcs.append(pc)
    pc = jnp.concatenate(pcs, axis=0)
    xs = pc + (jnp.concatenate(prevs, axis=0) - pc) * mu_ref[...]
    G = G_WIDTH
    r, lw, k4, v, kk, a, gate = _rwkv_pre(xs[:, 0:G], xs[:, G:2 * G], xs[:, 2 * G:3 * G], xs[:, 3 * G:3 * G + 256],
                                          w2a_ref[:, 0:G], w2a_ref[:, G:2 * G], g2_ref[...], w0_ref[...], a0_ref[...],
                                          kk_ref[...], ka_ref[...])
    N = RWKV_HEAD
    W = 2 * N
    NP = NH // 2
    ti = lax.broadcasted_iota(jnp.int32, (T, T), 0)
    si = lax.broadcasted_iota(jnp.int32, (T, T), 1)
    tril = (ti >= si).astype(F32)
    cum = jnp.concatenate([_dot(tril, lw[b * T:(b + 1) * T, :], HI) for b in range(nb)], axis=0)
    t2 = lax.broadcasted_iota(jnp.int32, (2 * T, 2 * T), 0)
    s2 = lax.broadcasted_iota(jnp.int32, (2 * T, 2 * T), 1)
    mask2 = jnp.where(t2 < T, t2 - 1, t2 - T) >= jnp.where(s2 < T, s2, s2 - T)
    eye2 = (lax.broadcasted_iota(jnp.int32, (N, W), 0) == (lax.broadcasted_iota(jnp.int32, (N, W), 1) & (N - 1))).astype(F32)
    lo = lax.broadcasted_iota(jnp.int32, (1, W), 1) < N

    def normalized(x):
        return x * jnp.minimum(lax.rsqrt(_seg_sum(x * x, lo)), 1e12)

    pairs = lambda x: [x[b * T:(b + 1) * T, p * W:(p + 1) * W] for b in range(nb) for p in range(NP)]
    r_p, k_p, v_p = pairs(r), pairs(k4), pairs(v)
    o, z_new = _wkv_chunk(r_p, pairs(lw), pairs(cum), k_p, v_p, [normalized(x) for x in pairs(kk)], pairs(a),
                          [z_ref[i] for i in range(nb * NP)], mask2, eye2)
    for i in range(nb * NP):
        z_ref[i] = z_new[i]
    gate_p = pairs(gate)
    vec_pairs = lambda ref: [ref[:, p * W:(p + 1) * W] for p in range(NP)]
    rk_p, lg_p, lb_p = vec_pairs(rk_ref), vec_pairs(lg_ref), vec_pairs(lb_ref)
    for b in range(nb):
        ys = []
        for p in range(NP):
            i = b * NP + p
            oc = o[i] - _seg_sum(o[i], lo) * (1.0 / N)
            var = _seg_sum(oc * oc, lo) * (1.0 / N)
            gn = oc * lax.rsqrt(var + RWKV_GN_EPS) * lg_p[p] + lb_p[p]
            bonus = _seg_sum(r_p[i] * k_p[i] * rk_p[p], lo) * v_p[i]
            ys.append((gn + bonus) * gate_p[i])
        y_ref[b] = jnp.concatenate(ys, axis=1).astype(y_ref.dtype)

    @pl.when(c == nchunks - 1)
    def _():
        for b in range(nb):
            for p in range(NP):
                zbd = z_ref[b * NP + p]
                zt = (zbd[0:N] + zbd[N:W]).T
                st_ref[b, 2 * p] = zt[0:N]
                st_ref[b, 2 * p + 1] = zt[N:W]


def _rwkv_prompt(proj, rw, bp, seq):
    T = WKV_CHUNK
    nc = seq // T
    G = G_WIDTH
    arrs, specs = zip(*rw)
    pc_specs = [pl.BlockSpec((T, SHIFT_W), lambda c, b=b: (b * nc + c, COL_C // SHIFT_W)) for b in range(bp)]
    return pl.pallas_call(
        functools.partial(_rwkv_prompt_body, nb=bp, nchunks=nc),
        grid=(nc,),
        in_specs=pc_specs + list(specs),
        out_specs=[pl.BlockSpec((bp, T, G), lambda c: (0, c, 0)),
                   pl.BlockSpec((bp, 1, SHIFT_W), lambda c: (0, 0, 0)),
                   pl.BlockSpec((bp, RWKV_HEADS, RWKV_HEAD, RWKV_HEAD), lambda c: (0, 0, 0, 0))],
        out_shape=[jax.ShapeDtypeStruct((bp, seq, G), BF16),
                   jax.ShapeDtypeStruct((bp, 1, SHIFT_W), F32),
                   jax.ShapeDtypeStruct((bp, RWKV_HEADS, RWKV_HEAD, RWKV_HEAD), F32)],
        scratch_shapes=[pltpu.VMEM((bp, 8, SHIFT_W), F32),
                        pltpu.VMEM((bp * RWKV_HEADS // 2, 2 * RWKV_HEAD, 2 * RWKV_HEAD), F32)],
        compiler_params=_cparams("arbitrary"),
        name="rwkv_prompt",
    )(*([proj] * bp), *arrs)


def _xattn_prompt_body(q_ref, k_ref, v_ref, o_ref, kb_ref, vb_ref):
    @pl.when(pl.program_id(1) == 0)
    def _():
        kb_ref[...] = k_ref[...].astype(BF16)
        vb_ref[...] = v_ref[...].astype(BF16)

    H = range(XA_HEADS)
    sl = [slice(h * XA_HD, (h + 1) * XA_HD) for h in H]
    s = [_dot_nt(q_ref[:, sl[h]].astype(BF16), kb_ref[:, sl[h]]) * (XA_HD ** -0.5) for h in H]
    e = [jnp.exp(s[h] - jnp.max(s[h], axis=-1, keepdims=True)) for h in H]
    p = [(e[h] / jnp.sum(e[h], axis=-1, keepdims=True)).astype(BF16) for h in H]
    for h in H:
        o_ref[:, sl[h]] = _dot(p[h], vb_ref[:, sl[h]]).astype(o_ref.dtype)


def _xattn_prompt(q, mk, mv, bp, seq):
    tq = _pick(seq, (512, 256, 128))
    nt = seq // tq
    kv = pl.BlockSpec((N_MEM, D_MODEL), lambda b, t: (b, 0))
    return pl.pallas_call(
        _xattn_prompt_body,
        grid=(bp, nt),
        in_specs=[pl.BlockSpec((tq, D_MODEL), lambda b, t: (b * nt + t, 0)), kv, kv],
        out_specs=pl.BlockSpec((tq, D_MODEL), lambda b, t: (b * nt + t, 0)),
        out_shape=jax.ShapeDtypeStruct((q.shape[0], D_MODEL), BF16),
        scratch_shapes=[pltpu.VMEM((N_MEM, D_MODEL), BF16), pltpu.VMEM((N_MEM, D_MODEL), BF16)],
        compiler_params=_cparams("parallel", "arbitrary"),
        name="xattn_prompt",
    )(q, mk, mv)


XA_SAMPLES_PER_STEP = 2


def _xattn_sample_body(q_ref, k_ref, v_ref, obuf_ref, o_ref, acc_ref, *, nsteps):
    del obuf_ref
    i = pl.program_id(0)
    H = XA_HEADS
    groups = N_MEM * H // (2 * H)
    for j in range(XA_SAMPLES_PER_STEP):
        b = i * XA_SAMPLES_PER_STEP + j
        k3 = k_ref[j].reshape(N_MEM * H, XA_HD).reshape(groups, 2 * H, XA_HD)
        s = jnp.sum(k3 * q_ref[b], axis=-1, keepdims=True) * (XA_HD ** -0.5)
        mx = jnp.max(s, axis=0, keepdims=True)
        mx = jnp.maximum(mx, pltpu.roll(mx, H, 1))
        e = jnp.exp(s - mx)
        den = jnp.sum(e, axis=0, keepdims=True)
        den = den + pltpu.roll(den, H, 1)
        v3 = v_ref[j].reshape(N_MEM * H, XA_HD).reshape(groups, 2 * H, XA_HD)
        o8 = jnp.sum((e / den) * v3, axis=0)
        o = o8[0:H] + o8[H:2 * H]
        acc_ref[pl.ds(b, 1), :] = jnp.concatenate([o[h:h + 1, :] for h in range(H)], axis=1)

    @pl.when(i == nsteps - 1)
    def _():
        o_ref[...] = acc_ref[...].astype(o_ref.dtype)


def _xattn_sample(q, ck, cv, obuf, l, row0):
    bs = q.shape[0]
    nb = XA_SAMPLES_PER_STEP
    kv = pl.BlockSpec((None, nb, N_MEM, XA_HEADS, XA_HD), lambda i: (l, i, 0, 0, 0))
    return pl.pallas_call(
        functools.partial(_xattn_sample_body, nsteps=bs // nb),
        grid=(bs // nb,),
        in_specs=[pl.BlockSpec((bs, 2 * XA_HEADS, XA_HD), lambda i: (0, 0, 0)), kv, kv, ANY_SPEC],
        out_specs=pl.BlockSpec((bs, D_MODEL), lambda i: (row0 // bs, 0)),
        out_shape=jax.ShapeDtypeStruct(obuf.shape, obuf.dtype),
        scratch_shapes=[pltpu.VMEM((bs, D_MODEL), F32)],
        input_output_aliases={3: 0},
        compiler_params=_cparams("arbitrary"),
        name="xattn_sample",
    )(q, ck, cv, obuf)


def _alias_args(prev):
    return ([], []) if prev is None else ([prev], [ANY_SPEC])


def _sample_ad_body(*refs, has_prev):
    (ab_ref, ac_ref, ah_ref, sta_ref, wa_ref, d1a_ref, d1b_ref, d2a_ref, d2b_ref, std_ref, wd_ref, bias_ref, g_ref,
     beta_ref) = refs[:14]
    ya_ref, yd_ref, na_ref, nd_ref = refs[14 + 2 * has_prev:]
    u = ac_ref[...] * ah_ref[...]
    s0 = sta_ref[:, 0, :]
    s1 = sta_ref[:, 1, :]
    ya = ab_ref[...] * (wa_ref[0:1, :] * s0 + wa_ref[1:2, :] * s1 + wa_ref[2:3, :] * u)
    ya_ref[...] = ya.astype(ya_ref.dtype)
    na_ref[:, 0, :] = s1
    na_ref[:, 1, :] = u
    d1 = jnp.concatenate([d1a_ref[...], d1b_ref[...]], axis=1)
    d2 = jnp.concatenate([d2a_ref[...], d2b_ref[...]], axis=1)
    ud = d1 * _sigmoid(d2)
    acc = jnp.broadcast_to(bias_ref[...], ud.shape)
    nh = D_CONV - 1
    for k in range(nh):
        s_k = std_ref[k]
        acc = acc + s_k * wd_ref[k:k + 1, :]
        if k > 0:
            nd_ref[k - 1] = s_k
    acc = acc + ud * wd_ref[nh:nh + 1, :]
    nd_ref[nh - 1] = ud
    c = _layernorm(acc, EPS) * g_ref[...] + beta_ref[...]
    yd_ref[...] = (c * _sigmoid(c)).astype(yd_ref.dtype)


def _sample_ad(proj, st_a, st_d, wa, wd, bias, g, beta, l, row0, prev_a, prev_d):
    bs = st_a.shape[1]
    G = G_WIDTH
    tb = 32
    r0 = row0 // tb
    half = G // 2
    cola = lambda j: pl.BlockSpec((tb, G), lambda i: (r0 + i, COL_A // G + j))
    cold = lambda j: pl.BlockSpec((tb, half), lambda i: (r0 + i, COL_D // half + j))
    sta = pl.BlockSpec((None, tb, A_CONV - 1, G), lambda i: (l, i, 0, 0))
    std = pl.BlockSpec((None, D_CONV - 1, tb, G), lambda i: (l, 0, i, 0))
    yrow = pl.BlockSpec((tb, G), lambda i: (i, 0))
    (bias, bias_s), (g, g_s), (beta, beta_s) = _layer_vec(bias, l), _layer_vec(g, l), _layer_vec(beta, l)
    nh = D_CONV - 1
    has_prev = prev_a is not None
    extra = [prev_a, prev_d] if has_prev else []
    return pl.pallas_call(
        functools.partial(_sample_ad_body, has_prev=has_prev),
        grid=(bs // tb,),
        in_specs=[cola(0), cola(1), cola(2), sta, _layer_mat(wa, l), cold(0), cold(1), cold(2), cold(3), std,
                  _layer_mat(wd, l), bias_s, g_s, beta_s] + [ANY_SPEC] * len(extra),
        out_specs=[yrow, yrow, sta, std],
        out_shape=[jax.ShapeDtypeStruct((bs, G), BF16), jax.ShapeDtypeStruct((bs, G), BF16),
                   jax.ShapeDtypeStruct(st_a.shape, F32), jax.ShapeDtypeStruct(st_d.shape, F32)],
        input_output_aliases={14: 2, 15: 3} if has_prev else {},
        compiler_params=_cparams("parallel"),
        name="sample_conv_ad",
    )(proj, proj, proj, st_a, wa, proj, proj, proj, proj, st_d, wd, bias, g, beta, *extra)


RET_D_BLOCK = 16


def _ret_sample_body(*refs, has_prev):
    q_ref, k_ref, v_ref, g_ref, cos_ref, sin_ref, s_ref = refs[:7]
    y_ref, ns_ref, qt_ref, kt_ref, acc_ref = refs[7 + has_prev:]
    h = pl.program_id(0)
    j = pl.program_id(1)
    nd = RET_D // RET_D_BLOCK
    gammas = [math.exp(lg) for lg in RET_LOG_DECAY]
    gamma = jnp.where(h == 0, gammas[0], jnp.where(h == 1, gammas[1], jnp.where(h == 2, gammas[2], gammas[3])))
    gamma = gamma.astype(F32)

    @pl.when(j == 0)
    def _():
        q = _rope(q_ref[...], cos_ref[...], sin_ref[...])
        k = _rope(k_ref[...], cos_ref[...], sin_ref[...]) * (RET_D ** -0.5)
        qt_ref[...] = q.T
        kt_ref[...] = k.T
        acc_ref[...] = jnp.zeros_like(acc_ref)

    vt = v_ref[...].T
    acc = acc_ref[...]
    for d in range(RET_D_BLOCK):
        s_d = s_ref[:, d, :].T
        row = j * RET_D_BLOCK + d
        q_d = qt_ref[pl.ds(row, 1), :]
        k_d = kt_ref[pl.ds(row, 1), :]
        acc = acc + q_d * s_d
        ns_ref[:, d, :] = (gamma * s_d + k_d * vt).T
    acc_ref[...] = acc

    @pl.when(j == nd - 1)
    def _():
        qk = jnp.sum(qt_ref[...] * kt_ref[...], axis=0, keepdims=True)
        o = qk * vt + acc * gamma
        mu = jnp.mean(o, axis=0, keepdims=True)
        oc = o - mu
        var = jnp.mean(oc * oc, axis=0, keepdims=True)
        ln = (oc * lax.rsqrt(var + EPS)).T
        g = g_ref[...]
        y_ref[...] = (g * _sigmoid(g) * ln).astype(y_ref.dtype)


def _retention_sample(proj, cos2, sin2, state, l, row0, prev):
    bs = state.shape[1]
    r0 = row0 // bs
    nd = RET_D // RET_D_BLOCK
    col = lambda c: pl.BlockSpec((bs, RET_D), lambda h, j: (r0, COL_B // RET_D + c * RET_HEADS + h))
    tab = pl.BlockSpec((1, RET_D), lambda h, j: (0, 0))
    sblk = pl.BlockSpec((None, bs, None, RET_D_BLOCK, RET_D), lambda h, j: (l, 0, h, j, 0))
    extra, extra_specs = _alias_args(prev)
    return pl.pallas_call(
        functools.partial(_ret_sample_body, has_prev=prev is not None),
        grid=(RET_HEADS, nd),
        in_specs=[col(0), col(1), col(2), col(3), tab, tab, sblk] + extra_specs,
        out_specs=[pl.BlockSpec((bs, RET_D), lambda h, j: (0, h)), sblk],
        out_shape=[jax.ShapeDtypeStruct((bs, G_WIDTH), BF16), jax.ShapeDtypeStruct(state.shape, F32)],
        scratch_shapes=[pltpu.VMEM((RET_D, bs), F32), pltpu.VMEM((RET_D, bs), F32), pltpu.VMEM((RET_D, bs), F32)],
        input_output_aliases={7: 1} if prev is not None else {},
        compiler_params=_cparams("parallel", "arbitrary"),
        name="retention_sample",
    )(proj, proj, proj, proj, cos2, sin2, state, *extra)


def _rwkv_sample_body(*refs, has_prev):
    (xr_ref, xk_ref, xv_ref, xl_ref, pr_ref, pk_ref, pv_ref, pl_ref, mr_ref, mk_ref, mv_ref, ml_ref,
     w2w_ref, w2a_ref, g2_ref, w0_ref, a0_ref, kk_ref, ka_ref, rk_ref, lg_ref, lb_ref, s_ref) = refs[:23]
    y_ref, ns_ref, vec_ref, ot_ref = refs[23 + has_prev:]
    N = RWKV_HEAD

    def shifted(x_ref, prev_ref, mu_ref):
        x = x_ref[...]
        return x + (prev_ref[...] - x) * mu_ref[...]

    r, lw, k4, v, kk, a, gate = _rwkv_pre(shifted(xr_ref, pr_ref, mr_ref), shifted(xk_ref, pk_ref, mk_ref),
                                          shifted(xv_ref, pv_ref, mv_ref), shifted(xl_ref, pl_ref, ml_ref),
                                          w2w_ref[...], w2a_ref[...], g2_ref[...], w0_ref[...], a0_ref[...],
                                          kk_ref[...], ka_ref[...])
    kkn = jnp.concatenate([_l2_normalize(kk[:, 0:N]), _l2_normalize(kk[:, N:2 * N])], axis=1)
    w = jnp.exp(lw)
    vec_ref[0] = kkn.T
    vec_ref[1] = w.T
    vec_ref[2] = (kkn * a).T
    vec_ref[3] = k4.T
    vec_ref[4] = r.T
    vec_ref[5] = v.T
    for hh in range(2):
        ch = slice(hh * N, (hh + 1) * N)
        kk_t = vec_ref[0, ch, :]
        w_t = vec_ref[1, ch, :]
        kka_t = vec_ref[2, ch, :]
        k_t = vec_ref[3, ch, :]
        r_t = vec_ref[4, ch, :]
        for i in range(N):
            s_i = s_ref[hh, i]
            sa = -jnp.sum(s_i * kk_t, axis=0, keepdims=True)
            v_i = vec_ref[5, hh * N + i:hh * N + i + 1, :]
            s_new = s_i * w_t + sa * kka_t + v_i * k_t
            ns_ref[hh, i] = s_new
            ot_ref[hh * N + i:hh * N + i + 1, :] = jnp.sum(s_new * r_t, axis=0, keepdims=True)
    o = ot_ref[...].T
    rk = rk_ref[...]
    lg = lg_ref[...]
    lb = lb_ref[...]
    ys = []
    for hh in range(2):
        ch = slice(hh * N, (hh + 1) * N)
        ys.append(_rwkv_post(o[:, ch], r[:, ch], k4[:, ch], v[:, ch], gate[:, ch], rk[:, ch], lg[:, ch], lb[:, ch]))
    y_ref[...] = jnp.concatenate(ys, axis=1).astype(y_ref.dtype)


def _rwkv_sample(proj, prev_shift, rw_arrs, state, l, row0, prev):
    mu, w2a, g2, w0, a0, k_k, k_a, r_k, lnx_g, lnx_b = rw_arrs
    bs = state.shape[-1]
    r0 = row0 // bs
    G = G_WIDTH
    N = RWKV_HEAD
    W = 2 * N
    npair = G // W
    def cols(shape_lead, lead_idx, base):
        mk = lambda width, cidx: pl.BlockSpec(shape_lead + (width,), lambda p: lead_idx + (cidx(p),))
        return [mk(W, lambda p, c=c: base // W + c * npair + p) for c in range(3)] + [
            mk(2 * W, lambda p: (base + 3 * G) // (2 * W))]
    pairv = pl.BlockSpec((None, 1, W), lambda p: (l, 0, p))
    pairm = pl.BlockSpec((None, 128, W), lambda p: (l, 0, p))
    sblk = pl.BlockSpec((None, 2, N, N, bs), lambda p: (l, p, 0, 0, 0))
    extra, extra_specs = _alias_args(prev)
    return pl.pallas_call(
        functools.partial(_rwkv_sample_body, has_prev=prev is not None),
        grid=(npair,),
        in_specs=cols((bs,), (r0,), COL_C) + cols((None, bs), (l, 0), 0) + cols((None, 1), (l, 0), 0) + [
            pairm, pl.BlockSpec((None, 128, W), lambda p: (l, 0, npair + p)), pairm,
            pairv, pairv, pairv, pairv, pairv, pairv, pairv, sblk] + extra_specs,
        out_specs=[pl.BlockSpec((bs, W), lambda p: (0, p)), sblk],
        out_shape=[jax.ShapeDtypeStruct((bs, G), BF16), jax.ShapeDtypeStruct(state.shape, F32)],
        scratch_shapes=[pltpu.VMEM((6, W, bs), F32), pltpu.VMEM((W, bs), F32)],
        input_output_aliases={23: 1} if prev is not None else {},
        compiler_params=_cparams("parallel"),
        name="rwkv_sample",
    )(proj, proj, proj, proj, prev_shift, prev_shift, prev_shift, prev_shift, mu, mu, mu, mu, w2a, w2a, g2,
      w0, a0, k_k, k_a, r_k, lnx_g, lnx_b, state, *extra)


def _place_cols_body(src_ref, buf_ref, o_ref):
    del buf_ref
    o_ref[...] = src_ref[...]


def _place_cols(src, buf, col):
    rows, w = src.shape
    tr = _pick(rows, (1024, 512, 256, 128))
    return pl.pallas_call(
        _place_cols_body,
        grid=(rows // tr,),
        in_specs=[pl.BlockSpec((tr, w), lambda i: (i, 0)), ANY_SPEC],
        out_specs=pl.BlockSpec((tr, w), lambda i: (i, col)),
        out_shape=jax.ShapeDtypeStruct(buf.shape, buf.dtype),
        input_output_aliases={1: 0},
        compiler_params=_cparams("parallel"),
        name="place_cols",
    )(src, buf)


def _place_heads_body(*refs):
    src_ref, o_ref = refs[0], refs[-1]
    for h in range(XA_HEADS):
        o_ref[0, :, h, :] = src_ref[:, h * XA_HD:(h + 1) * XA_HD]


def _place_heads(src, l, depth, bp, prev):
    extra, extra_specs = _alias_args(prev)
    return pl.pallas_call(
        _place_heads_body,
        grid=(bp,),
        in_specs=[pl.BlockSpec((N_MEM, D_MODEL), lambda b: (b, 0))] + extra_specs,
        out_specs=pl.BlockSpec((None, 1, N_MEM, XA_HEADS, XA_HD), lambda b: (l, b, 0, 0, 0)),
        out_shape=jax.ShapeDtypeStruct((depth, bp, N_MEM, XA_HEADS, XA_HD), F32),
        input_output_aliases={1: 0} if prev is not None else {},
        compiler_params=_cparams("parallel"),
        name="place_heads",
    )(src, *extra)


def _assemble_body(a_ref, b_ref, c_ref, d_ref, buf_ref, o_ref):
    del buf_ref
    o_ref[...] = jnp.concatenate([a_ref[...], b_ref[...], c_ref[...], d_ref[...]], axis=1)


def _assemble_rows(parts, buf, row0):
    bs = parts[0].shape[0]
    part = pl.BlockSpec((bs, G_WIDTH), lambda i: (0, 0))
    return pl.pallas_call(
        _assemble_body,
        grid=(1,),
        in_specs=[part, part, part, part, ANY_SPEC],
        out_specs=pl.BlockSpec((bs, D_MODEL), lambda i: (row0 // bs, 0)),
        out_shape=jax.ShapeDtypeStruct(buf.shape, buf.dtype),
        input_output_aliases={4: 0},
        compiler_params=_cparams("arbitrary"),
        name="assemble_sample_rows",
    )(*parts, buf)


def kernel(x_prompt, x_sample, mem_prompt, state_conv_a, state_ret, state_shift, state_wkv, state_conv_d, cache_mem_k, cache_mem_v, g_mix, w_in, conv_a_w, mu_c, w0, w2, a0, a2, g2, k_k, k_a, r_k, lnx_g, lnx_b, conv_d_w, conv_d_b, ln_d_g, ln_d_b, w_out, g_xa, g_mem, wq_x, wk_x, wv_x, wo_x, g_mlp, w_up, w_down, g_final):
    bp, seq, d = x_prompt.shape
    bs = x_sample.shape[0]
    depth = w_in.shape[0]
    np_rows = bp * seq
    G = G_WIDTH
    x = jnp.concatenate([x_prompt.reshape(np_rows, d), x_sample.reshape(bs, d)], axis=0)
    mem = mem_prompt.reshape(bp * N_MEM, d)
    cos_p, sin_p = _rope_tables(jnp.arange(seq))
    cos_s, sin_s = _rope_tables(PAST_LEN + jnp.arange(1))

    z64 = jnp.zeros((depth, 64, G), F32)
    w2a = jnp.concatenate([jnp.concatenate([w2, z64], axis=2), jnp.concatenate([z64, a2], axis=2)], axis=1).astype(BF16)
    g2_b = g2.astype(BF16)
    wkv_t = jnp.transpose(state_wkv, (0, 2, 3, 4, 1))
    conv_d_t = jnp.transpose(state_conv_d, (0, 2, 1, 3))
    vec3 = lambda a: a.reshape(depth, 1, -1)
    rw_arrs = (vec3(mu_c), w2a, g2_b, vec3(w0), vec3(a0), vec3(k_k), vec3(k_a), vec3(r_k), vec3(lnx_g), vec3(lnx_b))

    small = {k: [] for k in ("a_p", "ret_p", "sh_p", "sh_s", "wkv_p", "d_p")}
    na_s = nd_s = nret_s = nwkv_s = mk_all = mv_all = None
    for l in range(depth):
        proj = _matmul_ws(x, w_in, l, norm_g=_layer_vec(g_mix, l), tn=1280, tm=_pick(x.shape[0], (640, 512, 256, 128)))
        ymix, na_p = _conv_a_prompt(proj, conv_a_w, l, bp, seq)
        ymix, nd_p = _conv_d_prompt(proj, conv_d_w, conv_d_b, ln_d_g, ln_d_b, ymix, l, bp, seq)
        yb_p, nret_p = _retention_prompt(proj, cos_p, sin_p, bp, seq)
        ymix = _place_cols(yb_p.reshape(np_rows, G), ymix, 1)
        rw = [(rw_arrs[0], pl.BlockSpec((None, 1, SHIFT_W), lambda *_: (l, 0, 0))),
              (w2a, _layer_mat(w2a, l)), (g2_b, _layer_mat(g2_b, l))] + [
              (a, pl.BlockSpec((None, 1, G), lambda *_: (l, 0, 0))) for a in rw_arrs[3:]]
        yc_p, nsh_p, nwkv_p = _rwkv_prompt(proj, rw, bp, seq)
        ymix = _place_cols(yc_p.reshape(np_rows, G), ymix, 2)
        ya_s, yd_s, na_s, nd_s = _sample_ad(proj, state_conv_a, conv_d_t, conv_a_w, conv_d_w, conv_d_b, ln_d_g, ln_d_b,
                                            l, np_rows, na_s, nd_s)
        yb_s, nret_s = _retention_sample(proj, cos_s, sin_s, state_ret, l, np_rows, nret_s)
        yc_s, nwkv_s = _rwkv_sample(proj, state_shift, rw_arrs, wkv_t, l, np_rows, nwkv_s)
        ymix = _assemble_rows([ya_s, yb_s, yc_s, yd_s], ymix, np_rows)
        x = _matmul_ws(ymix, w_out, l, residual=x)

        mk = _matmul_ws(mem, wk_x, l, norm_g=_layer_vec(g_mem, l))
        mv = _matmul_ws(mem, wv_x, l, norm_g=_layer_vec(g_mem, l))
        mk_all = _place_heads(mk, l, depth, bp, mk_all)
        mv_all = _place_heads(mv, l, depth, bp, mv_all)
        q = _matmul_ws(x, wq_x, l, norm_g=_layer_vec(g_xa, l))
        obuf = _xattn_prompt(q, mk, mv, bp, seq)
        q_s = q[np_rows:].reshape(bs, XA_HEADS, XA_HD)
        q_s = jnp.concatenate([q_s, q_s], axis=1)
        obuf = _xattn_sample(q_s, cache_mem_k, cache_mem_v, obuf, l, np_rows)
        x = _matmul_ws(obuf, wo_x, l, residual=x)

        up = _matmul_ws(x, w_up, l, norm_g=_layer_vec(g_mlp, l), act="relu2", out_dtype=BF16)
        for kb in range(w_down.shape[1] // d):
            x = _matmul_ws(up, w_down, l, residual=x, kblock=(kb, d))

        small["a_p"].append(na_p)
        small["ret_p"].append(nret_p)
        small["sh_p"].append(nsh_p.reshape(bp, SHIFT_W))
        small["sh_s"].append(proj[np_rows:, COL_C:COL_C + SHIFT_W])
        small["wkv_p"].append(nwkv_p)
        small["d_p"].append(nd_p)
    gf = (g_final.reshape(1, d), pl.BlockSpec((1, d), lambda *_: (0, 0)))
    y_p = _rmsnorm(x, gf, F32, 0, np_rows)
    y_s = _rmsnorm(x, gf, F32, np_rows, bs)
    st = lambda k: jnp.stack(small[k])
    return (y_p.reshape(bp, seq, d), y_s.reshape(bs, 1, d), st("a_p"), na_s, st("ret_p"), nret_s,
            st("sh_p"), st("sh_s"), st("wkv_p"), jnp.transpose(nwkv_s, (0, 4, 1, 2, 3)),
            st("d_p"), jnp.transpose(nd_s, (0, 2, 1, 3)), mk_all, mv_all)
```

```python
import functools
import math

import jax
import jax.numpy as jnp
from jax import lax
from jax.experimental import pallas as pl
from jax.experimental.pallas import tpu as pltpu

F32 = jnp.float32
BF16 = jnp.bfloat16
HI = lax.Precision.HIGHEST

D_MODEL = 2048
G_WIDTH = 512
P_IN = 6400
RET_HEADS = 4
RET_D = 128
RET_CHUNK = 128
ROPE_BASE = 10000.0
RWKV_HEAD = 64
RWKV_HEADS = 8
WKV_CHUNK = 64
RWKV_GN_EPS = 64e-5
A_CONV = 3
D_CONV = 31
D_HIST = 32
N_MEM = 256
XA_HEADS = 4
XA_HD = 512
SHIFT_W = 1792
PAST_LEN = 16384
EPS = 1e-6
RET_LOG_DECAY = tuple(math.log1p(-(2.0 ** (-5.0 - h))) for h in range(RET_HEADS))
VMEM_LIMIT = 56 * 1024 * 1024
COL_A = 0
COL_B = 3 * G_WIDTH
COL_C = 7 * G_WIDTH
COL_D = 7 * G_WIDTH + SHIFT_W

ANY_SPEC = pl.BlockSpec(memory_space=pl.ANY)


def _cparams(*sem):
    return pltpu.CompilerParams(dimension_semantics=sem, vmem_limit_bytes=VMEM_LIMIT)


def _pick(n, cands):
    for c in cands:
        if n % c == 0:
            return c
    raise ValueError(f"no tile for {n}")


def _sigmoid(x):
    return 1.0 / (1.0 + jnp.exp(-x))


def _dot(a, b, precision=None):
    return jnp.dot(a, b, preferred_element_type=F32, precision=precision)


def _dot_nt(a, b, precision=None):
    return lax.dot_general(a, b, (((1,), (1,)), ((), ())), preferred_element_type=F32, precision=precision)


def _dot_tn(a, b, precision=None):
    return lax.dot_general(a, b, (((0,), (0,)), ((), ())), preferred_element_type=F32, precision=precision)


def _layernorm(x, eps):
    mu = jnp.mean(x, axis=-1, keepdims=True)
    xc = x - mu
    var = jnp.mean(xc * xc, axis=-1, keepdims=True)
    return xc * lax.rsqrt(var + eps)


def _layer_vec(a, l):
    n = a.shape[-1]
    return a.reshape(a.shape[0], 1, n), pl.BlockSpec((None, 1, n), lambda *_: (l, 0, 0))


def _layer_mat(a, l):
    return pl.BlockSpec((None,) + a.shape[1:], lambda *_: (l, 0, 0))


def _rmsnorm_body(x_ref, g_ref, o_ref):
    x = x_ref[...]
    ms = jnp.mean(x * x, axis=-1, keepdims=True)
    o_ref[...] = ((x * lax.rsqrt(ms + EPS)) * g_ref[...]).astype(o_ref.dtype)


def _rmsnorm(x, g_spec, out_dtype, row0=0, nrows=None):
    m, d = x.shape
    nrows = nrows or m
    tm = _pick(math.gcd(nrows, row0) if row0 else nrows, (1040, 1024, 640, 512, 256, 128))
    r0 = row0 // tm
    g, gspec = g_spec
    return pl.pallas_call(
        _rmsnorm_body,
        grid=(nrows // tm,),
        in_specs=[pl.BlockSpec((tm, d), lambda i: (r0 + i, 0)), gspec],
        out_specs=pl.BlockSpec((tm, d), lambda i: (i, 0)),
        out_shape=jax.ShapeDtypeStruct((nrows, d), out_dtype),
        compiler_params=_cparams("parallel"),
        name="rmsnorm",
    )(x, g)


def _mm_ws_body(*refs, act, has_res, has_norm):
    a_ref, w_ref = refs[0], refs[1]
    g_ref = refs[2] if has_norm else None
    res_ref = refs[2 + has_norm] if has_res else None
    o_ref = refs[2 + has_norm + has_res]
    wb_ref = refs[-1]

    @pl.when(pl.program_id(1) == 0)
    def _():
        wb_ref[...] = w_ref[...].astype(BF16)

    a = a_ref[...]
    if has_norm:
        ms = jnp.mean(a * a, axis=-1, keepdims=True)
        a = ((a * lax.rsqrt(ms + EPS)) * g_ref[...]).astype(BF16)
    r = _dot(a, wb_ref[...])
    if act == "relu2":
        r = jnp.square(jnp.maximum(r, 0.0))
    if has_res:
        r = res_ref[...] + r
    o_ref[...] = r.astype(o_ref.dtype)


def _matmul_ws(a, w, l, *, norm_g=None, residual=None, act=None, out_dtype=F32, tn=1024, tm=None, kblock=None):
    m = a.shape[0]
    kb, kdim = kblock if kblock is not None else (0, a.shape[1])
    n = w.shape[2]
    tm = tm or _pick(m, (1040, 1024, 640, 512, 256, 128))
    has_res = residual is not None
    has_norm = norm_g is not None
    in_specs = [pl.BlockSpec((tm, kdim), lambda j, i: (i, kb)), pl.BlockSpec((None, kdim, tn), lambda j, i: (l, kb, j))]
    args = [a, w]
    if has_norm:
        args.append(norm_g[0])
        in_specs.append(norm_g[1])
    if has_res:
        in_specs.append(pl.BlockSpec((tm, tn), lambda j, i: (i, j)))
        args.append(residual)
    return pl.pallas_call(
        functools.partial(_mm_ws_body, act=act, has_res=has_res, has_norm=has_norm),
        grid=(n // tn, m // tm),
        in_specs=in_specs,
        out_specs=pl.BlockSpec((tm, tn), lambda j, i: (i, j)),
        out_shape=jax.ShapeDtypeStruct((m, n), out_dtype),
        scratch_shapes=[pltpu.VMEM((kdim, tn), BF16)],
        compiler_params=_cparams("parallel", "arbitrary"),
        name="matmul_ws",
    )(*args)


def _conv_a_body(b_ref, c_ref, h_ref, w_ref, y_ref, st_ref, ext_ref, *, tl):
    t = pl.program_id(1)

    @pl.when(t == 0)
    def _():
        ext_ref[0:8, :] = jnp.zeros((8, G_WIDTH), F32)

    u = c_ref[...] * h_ref[...]
    ext_ref[8:, :] = u
    u1 = ext_ref[7:7 + tl, :]
    u2 = ext_ref[6:6 + tl, :]
    y = b_ref[...] * (w_ref[0:1, :] * u2 + w_ref[1:2, :] * u1 + w_ref[2:3, :] * u)
    y_ref[...] = y.astype(y_ref.dtype)
    st_ref[0] = ext_ref[tl + 6:tl + 8, :]
    ext_ref[0:8, :] = ext_ref[tl:tl + 8, :]


def _conv_a_prompt(proj, conv_w, l, bp, seq):
    tl = _pick(seq, (512, 256, 128))
    nt = seq // tl
    c0 = COL_A // G_WIDTH
    col = lambda j: pl.BlockSpec((tl, G_WIDTH), lambda b, t: (b * nt + t, c0 + j))
    return pl.pallas_call(
        functools.partial(_conv_a_body, tl=tl),
        grid=(bp, nt),
        in_specs=[col(0), col(1), col(2), _layer_mat(conv_w, l)],
        out_specs=[pl.BlockSpec((tl, G_WIDTH), lambda b, t: (b * nt + t, 0)),
                   pl.BlockSpec((1, A_CONV - 1, G_WIDTH), lambda b, t: (b, 0, 0))],
        out_shape=[jax.ShapeDtypeStruct((proj.shape[0], D_MODEL), BF16),
                   jax.ShapeDtypeStruct((bp, A_CONV - 1, G_WIDTH), F32)],
        scratch_shapes=[pltpu.VMEM((tl + 8, G_WIDTH), F32)],
        compiler_params=_cparams("parallel", "arbitrary"),
        name="conv_a_prompt",
    )(proj, proj, proj, conv_w)


CONV_D_ROWS = 64


def _conv_d_body(d1a_ref, d1b_ref, d2a_ref, d2b_ref, w_ref, bias_ref, g_ref, beta_ref, ymix_ref, y_ref, st_ref, ext_ref,
                 acc_ref, *, tl):
    del ymix_ref
    t = pl.program_id(1)
    next_ = D_HIST + tl

    @pl.when(t == 0)
    def _():
        ext_ref[0, 0:D_HIST, :] = jnp.zeros((D_HIST, G_WIDTH), F32)

    d1 = jnp.concatenate([d1a_ref[...], d1b_ref[...]], axis=1)
    d2 = jnp.concatenate([d2a_ref[...], d2b_ref[...]], axis=1)
    ext_ref[0, D_HIST:, :] = d1 * _sigmoid(d2)
    ext = ext_ref[0]
    for s in range(1, 8):
        ext_ref[s] = pltpu.roll(ext, next_ - s, 0)
    first = D_HIST - (D_CONV - 1)

    def blk(i, carry):
        base = pl.multiple_of(i * CONV_D_ROWS, CONV_D_ROWS)
        acc = jnp.broadcast_to(bias_ref[...], (CONV_D_ROWS, G_WIDTH))
        for k in range(D_CONV):
            s = (first + k) % 8
            a0 = first + k - s
            acc = acc + ext_ref[s, pl.ds(base + a0, CONV_D_ROWS), :] * w_ref[k:k + 1, :]
        acc_ref[pl.ds(base, CONV_D_ROWS), :] = acc
        return carry

    lax.fori_loop(0, tl // CONV_D_ROWS, blk, 0)
    c = _layernorm(acc_ref[...], EPS) * g_ref[...] + beta_ref[...]
    y_ref[...] = (c * _sigmoid(c)).astype(y_ref.dtype)
    st_ref[0] = ext_ref[0, tl + first:tl + D_HIST, :]
    ext_ref[0, 0:D_HIST, :] = ext_ref[0, tl:tl + D_HIST, :]


def _conv_d_prompt(proj, w, bias, g, beta, ymix, l, bp, seq):
    tl = _pick(seq, (512, 256, 128))
    nt = seq // tl
    half = G_WIDTH // 2
    c0 = COL_D // half
    col = lambda j: pl.BlockSpec((tl, half), lambda b, t: (b * nt + t, c0 + j))
    (bias, bias_s), (g, g_s), (beta, beta_s) = _layer_vec(bias, l), _layer_vec(g, l), _layer_vec(beta, l)
    return pl.pallas_call(
        functools.partial(_conv_d_body, tl=tl),
        grid=(bp, nt),
        in_specs=[col(0), col(1), col(2), col(3), _layer_mat(w, l), bias_s, g_s, beta_s, ANY_SPEC],
        out_specs=[pl.BlockSpec((tl, G_WIDTH), lambda b, t: (b * nt + t, 3)),
                   pl.BlockSpec((1, D_CONV - 1, G_WIDTH), lambda b, t: (b, 0, 0))],
        out_shape=[jax.ShapeDtypeStruct(ymix.shape, ymix.dtype),
                   jax.ShapeDtypeStruct((bp, D_CONV - 1, G_WIDTH), F32)],
        scratch_shapes=[pltpu.VMEM((8, tl + D_HIST, G_WIDTH), F32), pltpu.VMEM((tl, G_WIDTH), F32)],
        input_output_aliases={8: 0},
        compiler_params=_cparams("parallel", "arbitrary"),
        name="conv_d_prompt",
    )(proj, proj, proj, proj, w, bias, g, beta, ymix)


def _rope_tables(pos):
    inv = ROPE_BASE ** (-jnp.arange(0, RET_D, 2, dtype=F32) / RET_D)
    ang = pos.astype(F32)[:, None] * inv[None, :]
    cos = jnp.cos(ang)
    sin = jnp.sin(ang)
    return jnp.concatenate([cos, cos], axis=-1), jnp.concatenate([-sin, sin], axis=-1)


def _rope(x, cos2, sin2):
    return x * cos2 + pltpu.roll(x, RET_D // 2, 1) * sin2


def _ret_body(*refs, nb, nchunks):
    qkvg = refs[:4 * nb]
    cos_ref, sin_ref, y_ref, st_ref, s_ref = refs[4 * nb:]
    c = pl.program_id(0)
    C = RET_CHUNK

    @pl.when(c == 0)
    def _():
        s_ref[...] = jnp.zeros_like(s_ref)

    ii = lax.broadcasted_iota(jnp.int32, (C, C), 0).astype(F32)
    jj = lax.broadcasted_iota(jnp.int32, (C, C), 1).astype(F32)
    diff = ii - jj
    cos2 = cos_ref[...]
    sin2 = sin_ref[...]
    NH = RET_HEADS
    chains = [(b, h) for b in range(nb) for h in range(NH)]
    I = range(len(chains))
    sl = [slice(h * RET_D, (h + 1) * RET_D) for h in range(NH)]
    lg = RET_LOG_DECAY
    q = [_rope(qkvg[4 * b][:, sl[h]], cos2, sin2) for b, h in chains]
    k = [_rope(qkvg[4 * b + 1][:, sl[h]], cos2, sin2) * (RET_D ** -0.5) for b, h in chains]
    qb = [q[i].astype(BF16) for i in I]
    vb = [qkvg[4 * b + 2][:, sl[h]].astype(BF16) for b, h in chains]
    s_old = [s_ref[i] for i in I]
    scores = [_dot_nt(qb[i], k[i].astype(BF16)) for i in I]
    cross = [_dot(qb[i], s_old[i].astype(BF16)) for i in I]
    k_dec = [jnp.exp(lg[h] * (C - 1.0 - ii)) for h in range(NH)]
    kv = [_dot_tn((k[i] * k_dec[chains[i][1]]).astype(BF16), vb[i]) for i in I]
    dmat = [jnp.where(diff >= 0.0, jnp.exp(lg[h] * jnp.maximum(diff, 0.0)), 0.0) for h in range(NH)]
    inner = [_dot((scores[i] * dmat[chains[i][1]]).astype(BF16), vb[i]) for i in I]
    q_dec = [jnp.exp(lg[h] * (ii + 1.0)) for h in range(NH)]
    for i, (b, h) in enumerate(chains):
        s_ref[i] = math.exp(lg[h] * C) * s_old[i] + kv[i]
        o = _layernorm(inner[i] + cross[i] * q_dec[h], EPS)
        g = qkvg[4 * b + 3][:, sl[h]]
        y_ref[b, :, sl[h]] = (g * _sigmoid(g) * o).astype(y_ref.dtype)

    @pl.when(c == nchunks - 1)
    def _():
        for i, (b, h) in enumerate(chains):
            st_ref[b, h] = s_ref[i]


def _retention_prompt(proj, cos2, sin2, bp, seq):
    C = RET_CHUNK
    nc = seq // C
    c0 = COL_B // G_WIDTH
    cols = [pl.BlockSpec((C, G_WIDTH), lambda c, b=b, j=j: (b * nc + c, c0 + j)) for b in range(bp) for j in range(4)]
    tab = pl.BlockSpec((C, RET_D), lambda c: (c, 0))
    return pl.pallas_call(
        functools.partial(_ret_body, nb=bp, nchunks=nc),
        grid=(nc,),
        in_specs=cols + [tab, tab],
        out_specs=[pl.BlockSpec((bp, C, G_WIDTH), lambda c: (0, c, 0)),
                   pl.BlockSpec((bp, RET_HEADS, RET_D, RET_D), lambda c: (0, 0, 0, 0))],
        out_shape=[jax.ShapeDtypeStruct((bp, seq, G_WIDTH), BF16),
                   jax.ShapeDtypeStruct((bp, RET_HEADS, RET_D, RET_D), F32)],
        scratch_shapes=[pltpu.VMEM((bp * RET_HEADS, RET_D, RET_D), F32)],
        compiler_params=_cparams("arbitrary"),
        name="retention_prompt",
    )(*([proj] * (4 * bp)), cos2, sin2)


def _softplus(z):
    return jnp.maximum(z, 0.0) + jnp.log(1.0 + jnp.exp(-jnp.abs(z)))


def _rwkv_pre(r, kc, vc, lora_x, w2_w, w2_a, g2, w0, a0, k_k, k_a):
    lin = lora_x[:, 0:128]
    lane = lax.broadcasted_iota(jnp.int32, lin.shape, 1)
    lin = jnp.where(lane < 64, jnp.tanh(lin), lin).astype(BF16)
    gate = _dot(_sigmoid(lora_x[:, 128:256]).astype(BF16), g2)
    w_pre = -_softplus(-(w0 + _dot(lin, w2_w))) - 0.5
    log_decay = -jnp.exp(w_pre)
    a = _sigmoid(a0 + _dot(lin, w2_a))
    kk = kc * k_k
    k4 = kc * (1.0 + (a - 1.0) * k_a)
    return r, log_decay, k4, vc, kk, a, gate


def _l2_normalize(kk):
    ss = jnp.sum(kk * kk, axis=-1, keepdims=True)
    return kk * jnp.minimum(lax.rsqrt(ss), 1e12)


def _bdot(a, b):
    return _dot(a.astype(BF16), b.astype(BF16))


def _seg_sum(x, lo):
    s_lo = jnp.sum(jnp.where(lo, x, 0.0), axis=-1, keepdims=True)
    s_hi = jnp.sum(jnp.where(lo, 0.0, x), axis=-1, keepdims=True)
    return jnp.where(lo, s_lo, s_hi)


def _wkv_chunk(r, lw, cum, k, v, kk, a, zbd, mask2, eye2):
    T = WKV_CHUNK
    N = RWKV_HEAD
    P = range(len(r))
    HH = range(2)
    lo = lax.broadcasted_iota(jnp.int32, (1, 2 * N), 1) < N
    lo2 = (lax.broadcasted_iota(jnp.int32, (1, 4 * N), 1) & (2 * N - 1)) < N
    g_inc = [jnp.exp(cum[p]) for p in P]
    g_tot = [g_inc[p][T - 1:T, :] for p in P]
    ah = [-(kk[p] * jnp.exp(cum[p] - lw[p])) for p in P]
    rg = [r[p] * g_inc[p] for p in P]
    g_inv = [jnp.exp(-cum[p]) for p in P]
    bk = [jnp.concatenate([kk[p] * a[p] * g_inv[p], k[p] * g_inv[p]], axis=0) for p in P]
    ar = [jnp.concatenate([ah[p], rg[p]], axis=0).astype(BF16) for p in P]
    bkm = [jnp.concatenate([jnp.where(lo, bk[p], 0.0), jnp.where(lo, 0.0, bk[p])], axis=0).astype(BF16) for p in P]
    x2 = [_dot_nt(ar[p], bkm[p]) for p in P]
    x = [[jnp.where(mask2, x2[p][:, 2 * T * hh:2 * T * (hh + 1)], 0.0) for hh in HH] for p in P]
    vb = [v[p].astype(BF16) for p in P]
    zero = jnp.zeros((T, 2 * N), BF16)
    vz = [jnp.concatenate([zero, vb[p]], axis=0) for p in P]
    xkv2 = [_dot(jnp.concatenate([x[p][0], x[p][1]], axis=0).astype(BF16), vz[p]) for p in P]
    xkv = [jnp.where(lo, xkv2[p][0:2 * T], xkv2[p][2 * T:4 * T]) for p in P]
    acur = [[x[p][hh][0:T, 0:T] for hh in HH] for p in P]
    inv = [[eye2[0:T, 0:T] for hh in HH] for p in P]
    for it in range(5):
        y = [[_bdot(jnp.concatenate([acur[p][hh], inv[p][hh]], axis=0), acur[p][hh]) for hh in HH] for p in P]
        acur = [[y[p][hh][0:T] for hh in HH] for p in P]
        inv = [[inv[p][hh] + y[p][hh][T:2 * T] for hh in HH] for p in P]
    y = [[_bdot(inv[p][hh], acur[p][hh]) for hh in HH] for p in P]
    inv2 = [jnp.concatenate([inv[p][0] + y[p][0], inv[p][1] + y[p][1]], axis=0).astype(BF16) for p in P]
    rhs = [jnp.concatenate([ah[p], xkv[p][0:T]], axis=1).astype(BF16) for p in P]
    wu2 = [_dot(inv2[p], rhs[p]) for p in P]
    wub = [jnp.where(lo2, wu2[p][0:T], wu2[p][T:2 * T]).astype(BF16) for p in P]
    arb2 = [jnp.concatenate([x[p][0][T:2 * T, 0:T], x[p][1][T:2 * T, 0:T]], axis=0).astype(BF16) for p in P]
    qo2 = [_dot(arb2[p], wub[p]) for p in P]
    qo = [jnp.where(lo2, qo2[p][0:T], qo2[p][T:2 * T]) for p in P]
    lhs_t = [(bk[p] * g_tot[p]).astype(BF16) for p in P]
    rhs_t = [jnp.concatenate([wub[p], jnp.concatenate([zero, vb[p]], axis=1)], axis=0) for p in P]
    mn = [_dot_tn(lhs_t[p], rhs_t[p]) for p in P]
    mz = [jnp.where(lo, mn[p][0:N, 0:2 * N], mn[p][N:2 * N, 0:2 * N]) + jnp.where(eye2[0:N, :] > 0.0, g_tot[p], 0.0)
          for p in P]
    nz = [jnp.where(lo, mn[p][0:N, 2 * N:4 * N], mn[p][N:2 * N, 2 * N:4 * N]) for p in P]
    qm = [jnp.concatenate([rg[p] + qo[p][:, 0:2 * N], mz[p]], axis=0) for p in P]
    oz = [_bdot(qm[p], zbd[p]) for p in P]
    o = [oz[p][0:T] + qo[p][:, 2 * N:4 * N] + xkv[p][T:2 * T] for p in P]
    z_new = [oz[p][T:T + N] + nz[p] for p in P]
    zbd_new = [jnp.concatenate([jnp.where(lo, z_new[p], 0.0), jnp.where(lo, 0.0, z_new[p])], axis=0) for p in P]
    return o, zbd_new


def _rwkv_post(o, r, k4, v, gate, rk, lnx_g, lnx_b):
    gn = _layernorm(o, RWKV_GN_EPS) * lnx_g + lnx_b
    bonus = jnp.sum(r * k4 * rk, axis=-1, keepdims=True) * v
    return (gn + bonus) * gate


def _rwkv_prompt_body(*refs, nb, nchunks):
    pc_refs = refs[:nb]
    mu_ref, w2a_ref, g2_ref, w0_ref, a0_ref, kk_ref, ka_ref, rk_ref, lg_ref, lb_ref = refs[nb:nb + 10]
    y_ref, shift_ref, st_ref, carry_ref, z_ref = refs[nb + 10:]
    c = pl.program_id(0)
    T = WKV_CHUNK
    NH = RWKV_HEADS

    @pl.when(c == 0)
    def _():
        carry_ref[...] = jnp.zeros_like(carry_ref)
        z_ref[...] = jnp.zeros_like(z_ref)

    pcs, prevs = [], []
    for b in range(nb):
        pc = pc_refs[b][...]
        row = lax.broadcasted_iota(jnp.int32, pc.shape, 0)
        prevs.append(jnp.where(row == 0, carry_ref[b, 0:1, :], pltpu.roll(pc, 1, 0)))
        carry_ref[b, 0:1, :] = pc[T - 1:T, :]
        shift_ref[b] = pc[T - 1:T, :]
        pcs.append(pc)
    pc = jnp.concatenate(pcs, axis=0)
    xs = pc + (jnp.concatenate(prevs, axis=0) - pc) * mu_ref[...]
    G = G_WIDTH
    r, lw, k4, v, kk, a, gate = _rwkv_pre(xs[:, 0:G], xs[:, G:2 * G], xs[:, 2 * G:3 * G], xs[:, 3 * G:3 * G + 256],
                                          w2a_ref[:, 0:G], w2a_ref[:, G:2 * G], g2_ref[...], w0_ref[...], a0_ref[...],
                                          kk_ref[...], ka_ref[...])
    N = RWKV_HEAD
    W = 2 * N
    NP = NH // 2
    ti = lax.broadcasted_iota(jnp.int32, (T, T), 0)
    si = lax.broadcasted_iota(jnp.int32, (T, T), 1)
    tril = (ti >= si).astype(F32)
    cum = jnp.concatenate([_dot(tril, lw[b * T:(b + 1) * T, :], HI) for b in range(nb)], axis=0)
    t2 = lax.broadcasted_iota(jnp.int32, (2 * T, 2 * T), 0)
    s2 = lax.broadcasted_iota(jnp.int32, (2 * T, 2 * T), 1)
    mask2 = jnp.where(t2 < T, t2 - 1, t2 - T) >= jnp.where(s2 < T, s2, s2 - T)
    eye2 = (lax.broadcasted_iota(jnp.int32, (N, W), 0) == (lax.broadcasted_iota(jnp.int32, (N, W), 1) & (N - 1))).astype(F32)
    lo = lax.broadcasted_iota(jnp.int32, (1, W), 1) < N

    def normalized(x):
        return x * jnp.minimum(lax.rsqrt(_seg_sum(x * x, lo)), 1e12)

    pairs = lambda x: [x[b * T:(b + 1) * T, p * W:(p + 1) * W] for b in range(nb) for p in range(NP)]
    r_p, k_p, v_p = pairs(r), pairs(k4), pairs(v)
    o, z_new = _wkv_chunk(r_p, pairs(lw), pairs(cum), k_p, v_p, [normalized(x) for x in pairs(kk)], pairs(a),
                          [z_ref[i] for i in range(nb * NP)], mask2, eye2)
    for i in range(nb * NP):
        z_ref[i] = z_new[i]
    gate_p = pairs(gate)
    vec_pairs = lambda ref: [ref[:, p * W:(p + 1) * W] for p in range(NP)]
    rk_p, lg_p, lb_p = vec_pairs(rk_ref), vec_pairs(lg_ref), vec_pairs(lb_ref)
    for b in range(nb):
        ys = []
        for p in range(NP):
            i = b * NP + p
            oc = o[i] - _seg_sum(o[i], lo) * (1.0 / N)
            var = _seg_sum(oc * oc, lo) * (1.0 / N)
            gn = oc * lax.rsqrt(var + RWKV_GN_EPS) * lg_p[p] + lb_p[p]
            bonus = _seg_sum(r_p[i] * k_p[i] * rk_p[p], lo) * v_p[i]
            ys.append((gn + bonus) * gate_p[i])
        y_ref[b] = jnp.concatenate(ys, axis=1).astype(y_ref.dtype)

    @pl.when(c == nchunks - 1)
    def _():
        for b in range(nb):
            for p in range(NP):
                zbd = z_ref[b * NP + p]
                zt = (zbd[0:N] + zbd[N:W]).T
                st_ref[b, 2 * p] = zt[0:N]
                st_ref[b, 2 * p + 1] = zt[N:W]


def _rwkv_prompt(proj, rw, bp, seq):
    T = WKV_CHUNK
    nc = seq // T
    G = G_WIDTH
    arrs, specs = zip(*rw)
    pc_specs = [pl.BlockSpec((T, SHIFT_W), lambda c, b=b: (b * nc + c, COL_C // SHIFT_W)) for b in range(bp)]
    return pl.pallas_call(
        functools.partial(_rwkv_prompt_body, nb=bp, nchunks=nc),
        grid=(nc,),
        in_specs=pc_specs + list(specs),
        out_specs=[pl.BlockSpec((bp, T, G), lambda c: (0, c, 0)),
                   pl.BlockSpec((bp, 1, SHIFT_W), lambda c: (0, 0, 0)),
                   pl.BlockSpec((bp, RWKV_HEADS, RWKV_HEAD, RWKV_HEAD), lambda c: (0, 0, 0, 0))],
        out_shape=[jax.ShapeDtypeStruct((bp, seq, G), BF16),
                   jax.ShapeDtypeStruct((bp, 1, SHIFT_W), F32),
                   jax.ShapeDtypeStruct((bp, RWKV_HEADS, RWKV_HEAD, RWKV_HEAD), F32)],
        scratch_shapes=[pltpu.VMEM((bp, 8, SHIFT_W), F32),
                        pltpu.VMEM((bp * RWKV_HEADS // 2, 2 * RWKV_HEAD, 2 * RWKV_HEAD), F32)],
        compiler_params=_cparams("arbitrary"),
        name="rwkv_prompt",
    )(*([proj] * bp), *arrs)


def _xattn_prompt_body(q_ref, k_ref, v_ref, o_ref, kb_ref, vb_ref):
    @pl.when(pl.program_id(1) == 0)
    def _():
        kb_ref[...] = k_ref[...].astype(BF16)
        vb_ref[...] = v_ref[...].astype(BF16)

    H = range(XA_HEADS)
    sl = [slice(h * XA_HD, (h + 1) * XA_HD) for h in H]
    s = [_dot_nt(q_ref[:, sl[h]].astype(BF16), kb_ref[:, sl[h]]) * (XA_HD ** -0.5) for h in H]
    e = [jnp.exp(s[h] - jnp.max(s[h], axis=-1, keepdims=True)) for h in H]
    p = [(e[h] / jnp.sum(e[h], axis=-1, keepdims=True)).astype(BF16) for h in H]
    for h in H:
        o_ref[:, sl[h]] = _dot(p[h], vb_ref[:, sl[h]]).astype(o_ref.dtype)


def _xattn_prompt(q, mk, mv, bp, seq):
    tq = _pick(seq, (1024, 512, 256, 128))
    nt = seq // tq
    kv = pl.BlockSpec((N_MEM, D_MODEL), lambda b, t: (b, 0))
    return pl.pallas_call(
        _xattn_prompt_body,
        grid=(bp, nt),
        in_specs=[pl.BlockSpec((tq, D_MODEL), lambda b, t: (b * nt + t, 0)), kv, kv],
        out_specs=pl.BlockSpec((tq, D_MODEL), lambda b, t: (b * nt + t, 0)),
        out_shape=jax.ShapeDtypeStruct((q.shape[0], D_MODEL), BF16),
        scratch_shapes=[pltpu.VMEM((N_MEM, D_MODEL), BF16), pltpu.VMEM((N_MEM, D_MODEL), BF16)],
        compiler_params=_cparams("parallel", "arbitrary"),
        name="xattn_prompt",
    )(q, mk, mv)


XA_SAMPLES_PER_STEP = 4


def _xattn_sample_body(q_ref, k_ref, v_ref, obuf_ref, o_ref, acc_ref, *, nsteps):
    del obuf_ref
    i = pl.program_id(0)
    H = XA_HEADS
    groups = N_MEM * H // (2 * H)
    for j in range(XA_SAMPLES_PER_STEP):
        b = i * XA_SAMPLES_PER_STEP + j
        k3 = k_ref[j].reshape(N_MEM * H, XA_HD).reshape(groups, 2 * H, XA_HD)
        s = jnp.sum(k3 * q_ref[b], axis=-1, keepdims=True) * (XA_HD ** -0.5)
        mx = jnp.max(s, axis=0, keepdims=True)
        mx = jnp.maximum(mx, pltpu.roll(mx, H, 1))
        e = jnp.exp(s - mx)
        den = jnp.sum(e, axis=0, keepdims=True)
        den = den + pltpu.roll(den, H, 1)
        v3 = v_ref[j].reshape(N_MEM * H, XA_HD).reshape(groups, 2 * H, XA_HD)
        o8 = jnp.sum((e / den) * v3, axis=0)
        o = o8[0:H] + o8[H:2 * H]
        acc_ref[pl.ds(b, 1), :] = jnp.concatenate([o[h:h + 1, :] for h in range(H)], axis=1)

    @pl.when(i == nsteps - 1)
    def _():
        o_ref[...] = acc_ref[...].astype(o_ref.dtype)


def _xattn_sample(q, ck, cv, obuf, l, row0):
    bs = q.shape[0]
    nb = XA_SAMPLES_PER_STEP
    kv = pl.BlockSpec((None, nb, N_MEM, XA_HEADS, XA_HD), lambda i: (l, i, 0, 0, 0))
    return pl.pallas_call(
        functools.partial(_xattn_sample_body, nsteps=bs // nb),
        grid=(bs // nb,),
        in_specs=[pl.BlockSpec((bs, 2 * XA_HEADS, XA_HD), lambda i: (0, 0, 0)), kv, kv, ANY_SPEC],
        out_specs=pl.BlockSpec((bs, D_MODEL), lambda i: (row0 // bs, 0)),
        out_shape=jax.ShapeDtypeStruct(obuf.shape, obuf.dtype),
        scratch_shapes=[pltpu.VMEM((bs, D_MODEL), F32)],
        input_output_aliases={3: 0},
        compiler_params=_cparams("arbitrary"),
        name="xattn_sample",
    )(q, ck, cv, obuf)


def _alias_args(prev):
    return ([], []) if prev is None else ([prev], [ANY_SPEC])


def _sample_ad_body(*refs, has_prev):
    (ab_ref, ac_ref, ah_ref, sta_ref, wa_ref, d1a_ref, d1b_ref, d2a_ref, d2b_ref, std_ref, wd_ref, bias_ref, g_ref,
     beta_ref) = refs[:14]
    ya_ref, yd_ref, na_ref, nd_ref = refs[14 + 2 * has_prev:]
    u = ac_ref[...] * ah_ref[...]
    s0 = sta_ref[:, 0, :]
    s1 = sta_ref[:, 1, :]
    ya = ab_ref[...] * (wa_ref[0:1, :] * s0 + wa_ref[1:2, :] * s1 + wa_ref[2:3, :] * u)
    ya_ref[...] = ya.astype(ya_ref.dtype)
    na_ref[:, 0, :] = s1
    na_ref[:, 1, :] = u
    d1 = jnp.concatenate([d1a_ref[...], d1b_ref[...]], axis=1)
    d2 = jnp.concatenate([d2a_ref[...], d2b_ref[...]], axis=1)
    ud = d1 * _sigmoid(d2)
    acc = jnp.broadcast_to(bias_ref[...], ud.shape)
    nh = D_CONV - 1
    for k in range(nh):
        s_k = std_ref[k]
        acc = acc + s_k * wd_ref[k:k + 1, :]
        if k > 0:
            nd_ref[k - 1] = s_k
    acc = acc + ud * wd_ref[nh:nh + 1, :]
    nd_ref[nh - 1] = ud
    c = _layernorm(acc, EPS) * g_ref[...] + beta_ref[...]
    yd_ref[...] = (c * _sigmoid(c)).astype(yd_ref.dtype)


def _sample_ad(proj, st_a, st_d, wa, wd, bias, g, beta, l, row0, prev_a, prev_d):
    bs = st_a.shape[1]
    G = G_WIDTH
    tb = 32
    r0 = row0 // tb
    half = G // 2
    cola = lambda j: pl.BlockSpec((tb, G), lambda i: (r0 + i, COL_A // G + j))
    cold = lambda j: pl.BlockSpec((tb, half), lambda i: (r0 + i, COL_D // half + j))
    sta = pl.BlockSpec((None, tb, A_CONV - 1, G), lambda i: (l, i, 0, 0))
    std = pl.BlockSpec((None, D_CONV - 1, tb, G), lambda i: (l, 0, i, 0))
    yrow = pl.BlockSpec((tb, G), lambda i: (i, 0))
    (bias, bias_s), (g, g_s), (beta, beta_s) = _layer_vec(bias, l), _layer_vec(g, l), _layer_vec(beta, l)
    nh = D_CONV - 1
    has_prev = prev_a is not None
    extra = [prev_a, prev_d] if has_prev else []
    return pl.pallas_call(
        functools.partial(_sample_ad_body, has_prev=has_prev),
        grid=(bs // tb,),
        in_specs=[cola(0), cola(1), cola(2), sta, _layer_mat(wa, l), cold(0), cold(1), cold(2), cold(3), std,
                  _layer_mat(wd, l), bias_s, g_s, beta_s] + [ANY_SPEC] * len(extra),
        out_specs=[yrow, yrow, sta, std],
        out_shape=[jax.ShapeDtypeStruct((bs, G), BF16), jax.ShapeDtypeStruct((bs, G), BF16),
                   jax.ShapeDtypeStruct(st_a.shape, F32), jax.ShapeDtypeStruct(st_d.shape, F32)],
        input_output_aliases={14: 2, 15: 3} if has_prev else {},
        compiler_params=_cparams("parallel"),
        name="sample_conv_ad",
    )(proj, proj, proj, st_a, wa, proj, proj, proj, proj, st_d, wd, bias, g, beta, *extra)


RET_D_BLOCK = 16


def _ret_sample_body(*refs, has_prev):
    q_ref, k_ref, v_ref, g_ref, cos_ref, sin_ref, s_ref = refs[:7]
    y_ref, ns_ref, qt_ref, kt_ref, acc_ref = refs[7 + has_prev:]
    h = pl.program_id(0)
    j = pl.program_id(1)
    nd = RET_D // RET_D_BLOCK
    gammas = [math.exp(lg) for lg in RET_LOG_DECAY]
    gamma = jnp.where(h == 0, gammas[0], jnp.where(h == 1, gammas[1], jnp.where(h == 2, gammas[2], gammas[3])))
    gamma = gamma.astype(F32)

    @pl.when(j == 0)
    def _():
        q = _rope(q_ref[...], cos_ref[...], sin_ref[...])
        k = _rope(k_ref[...], cos_ref[...], sin_ref[...]) * (RET_D ** -0.5)
        qt_ref[...] = q.T
        kt_ref[...] = k.T
        acc_ref[...] = jnp.zeros_like(acc_ref)

    vt = v_ref[...].T
    acc = acc_ref[...]
    for d in range(RET_D_BLOCK):
        s_d = s_ref[:, d, :].T
        row = j * RET_D_BLOCK + d
        q_d = qt_ref[pl.ds(row, 1), :]
        k_d = kt_ref[pl.ds(row, 1), :]
        acc = acc + q_d * s_d
        ns_ref[:, d, :] = (gamma * s_d + k_d * vt).T
    acc_ref[...] = acc

    @pl.when(j == nd - 1)
    def _():
        qk = jnp.sum(qt_ref[...] * kt_ref[...], axis=0, keepdims=True)
        o = qk * vt + acc * gamma
        mu = jnp.mean(o, axis=0, keepdims=True)
        oc = o - mu
        var = jnp.mean(oc * oc, axis=0, keepdims=True)
        ln = (oc * lax.rsqrt(var + EPS)).T
        g = g_ref[...]
        y_ref[...] = (g * _sigmoid(g) * ln).astype(y_ref.dtype)


def _retention_sample(proj, cos2, sin2, state, l, row0, prev):
    bs = state.shape[1]
    r0 = row0 // bs
    nd = RET_D // RET_D_BLOCK
    col = lambda c: pl.BlockSpec((bs, RET_D), lambda h, j: (r0, COL_B // RET_D + c * RET_HEADS + h))
    tab = pl.BlockSpec((1, RET_D), lambda h, j: (0, 0))
    sblk = pl.BlockSpec((None, bs, None, RET_D_BLOCK, RET_D), lambda h, j: (l, 0, h, j, 0))
    extra, extra_specs = _alias_args(prev)
    return pl.pallas_call(
        functools.partial(_ret_sample_body, has_prev=prev is not None),
        grid=(RET_HEADS, nd),
        in_specs=[col(0), col(1), col(2), col(3), tab, tab, sblk] + extra_specs,
        out_specs=[pl.BlockSpec((bs, RET_D), lambda h, j: (0, h)), sblk],
        out_shape=[jax.ShapeDtypeStruct((bs, G_WIDTH), BF16), jax.ShapeDtypeStruct(state.shape, F32)],
        scratch_shapes=[pltpu.VMEM((RET_D, bs), F32), pltpu.VMEM((RET_D, bs), F32), pltpu.VMEM((RET_D, bs), F32)],
        input_output_aliases={7: 1} if prev is not None else {},
        compiler_params=_cparams("parallel", "arbitrary"),
        name="retention_sample",
    )(proj, proj, proj, proj, cos2, sin2, state, *extra)


def _rwkv_sample_body(*refs, has_prev):
    (xr_ref, xk_ref, xv_ref, xl_ref, pr_ref, pk_ref, pv_ref, pl_ref, mr_ref, mk_ref, mv_ref, ml_ref,
     w2w_ref, w2a_ref, g2_ref, w0_ref, a0_ref, kk_ref, ka_ref, rk_ref, lg_ref, lb_ref, s_ref) = refs[:23]
    y_ref, ns_ref, vec_ref, ot_ref = refs[23 + has_prev:]
    N = RWKV_HEAD

    def shifted(x_ref, prev_ref, mu_ref):
        x = x_ref[...]
        return x + (prev_ref[...] - x) * mu_ref[...]

    r, lw, k4, v, kk, a, gate = _rwkv_pre(shifted(xr_ref, pr_ref, mr_ref), shifted(xk_ref, pk_ref, mk_ref),
                                          shifted(xv_ref, pv_ref, mv_ref), shifted(xl_ref, pl_ref, ml_ref),
                                          w2w_ref[...], w2a_ref[...], g2_ref[...], w0_ref[...], a0_ref[...],
                                          kk_ref[...], ka_ref[...])
    kkn = jnp.concatenate([_l2_normalize(kk[:, 0:N]), _l2_normalize(kk[:, N:2 * N])], axis=1)
    w = jnp.exp(lw)
    vec_ref[0] = kkn.T
    vec_ref[1] = w.T
    vec_ref[2] = (kkn * a).T
    vec_ref[3] = k4.T
    vec_ref[4] = r.T
    vec_ref[5] = v.T
    for hh in range(2):
        ch = slice(hh * N, (hh + 1) * N)
        kk_t = vec_ref[0, ch, :]
        w_t = vec_ref[1, ch, :]
        kka_t = vec_ref[2, ch, :]
        k_t = vec_ref[3, ch, :]
        r_t = vec_ref[4, ch, :]
        for i in range(N):
            s_i = s_ref[hh, i]
            sa = -jnp.sum(s_i * kk_t, axis=0, keepdims=True)
            v_i = vec_ref[5, hh * N + i:hh * N + i + 1, :]
            s_new = s_i * w_t + sa * kka_t + v_i * k_t
            ns_ref[hh, i] = s_new
            ot_ref[hh * N + i:hh * N + i + 1, :] = jnp.sum(s_new * r_t, axis=0, keepdims=True)
    o = ot_ref[...].T
    rk = rk_ref[...]
    lg = lg_ref[...]
    lb = lb_ref[...]
    ys = []
    for hh in range(2):
        ch = slice(hh * N, (hh + 1) * N)
        ys.append(_rwkv_post(o[:, ch], r[:, ch], k4[:, ch], v[:, ch], gate[:, ch], rk[:, ch], lg[:, ch], lb[:, ch]))
    y_ref[...] = jnp.concatenate(ys, axis=1).astype(y_ref.dtype)


def _rwkv_sample(proj, prev_shift, rw_arrs, state, l, row0, prev):
    mu, w2a, g2, w0, a0, k_k, k_a, r_k, lnx_g, lnx_b = rw_arrs
    bs = state.shape[-1]
    r0 = row0 // bs
    G = G_WIDTH
    N = RWKV_HEAD
    W = 2 * N
    npair = G // W
    def cols(shape_lead, lead_idx, base):
        mk = lambda width, cidx: pl.BlockSpec(shape_lead + (width,), lambda p: lead_idx + (cidx(p),))
        return [mk(W, lambda p, c=c: base // W + c * npair + p) for c in range(3)] + [
            mk(2 * W, lambda p: (base + 3 * G) // (2 * W))]
    pairv = pl.BlockSpec((None, 1, W), lambda p: (l, 0, p))
    pairm = pl.BlockSpec((None, 128, W), lambda p: (l, 0, p))
    sblk = pl.BlockSpec((None, 2, N, N, bs), lambda p: (l, p, 0, 0, 0))
    extra, extra_specs = _alias_args(prev)
    return pl.pallas_call(
        functools.partial(_rwkv_sample_body, has_prev=prev is not None),
        grid=(npair,),
        in_specs=cols((bs,), (r0,), COL_C) + cols((None, bs), (l, 0), 0) + cols((None, 1), (l, 0), 0) + [
            pairm, pl.BlockSpec((None, 128, W), lambda p: (l, 0, npair + p)), pairm,
            pairv, pairv, pairv, pairv, pairv, pairv, pairv, sblk] + extra_specs,
        out_specs=[pl.BlockSpec((bs, W), lambda p: (0, p)), sblk],
        out_shape=[jax.ShapeDtypeStruct((bs, G), BF16), jax.ShapeDtypeStruct(state.shape, F32)],
        scratch_shapes=[pltpu.VMEM((6, W, bs), F32), pltpu.VMEM((W, bs), F32)],
        input_output_aliases={23: 1} if prev is not None else {},
        compiler_params=_cparams("parallel"),
        name="rwkv_sample",
    )(proj, proj, proj, proj, prev_shift, prev_shift, prev_shift, prev_shift, mu, mu, mu, mu, w2a, w2a, g2,
      w0, a0, k_k, k_a, r_k, lnx_g, lnx_b, state, *extra)


def _place_cols_body(src_ref, buf_ref, o_ref):
    del buf_ref
    o_ref[...] = src_ref[...]


def _place_cols(src, buf, col):
    rows, w = src.shape
    tr = _pick(rows, (1024, 512, 256, 128))
    return pl.pallas_call(
        _place_cols_body,
        grid=(rows // tr,),
        in_specs=[pl.BlockSpec((tr, w), lambda i: (i, 0)), ANY_SPEC],
        out_specs=pl.BlockSpec((tr, w), lambda i: (i, col)),
        out_shape=jax.ShapeDtypeStruct(buf.shape, buf.dtype),
        input_output_aliases={1: 0},
        compiler_params=_cparams("parallel"),
        name="place_cols",
    )(src, buf)


def _place_heads_body(*refs):
    src_ref, o_ref = refs[0], refs[-1]
    for h in range(XA_HEADS):
        o_ref[0, :, h, :] = src_ref[:, h * XA_HD:(h + 1) * XA_HD]


def _place_heads(src, l, depth, bp, prev):
    extra, extra_specs = _alias_args(prev)
    return pl.pallas_call(
        _place_heads_body,
        grid=(bp,),
        in_specs=[pl.BlockSpec((N_MEM, D_MODEL), lambda b: (b, 0))] + extra_specs,
        out_specs=pl.BlockSpec((None, 1, N_MEM, XA_HEADS, XA_HD), lambda b: (l, b, 0, 0, 0)),
        out_shape=jax.ShapeDtypeStruct((depth, bp, N_MEM, XA_HEADS, XA_HD), F32),
        input_output_aliases={1: 0} if prev is not None else {},
        compiler_params=_cparams("parallel"),
        name="place_heads",
    )(src, *extra)


def _assemble_body(a_ref, b_ref, c_ref, d_ref, buf_ref, o_ref):
    del buf_ref
    o_ref[...] = jnp.concatenate([a_ref[...], b_ref[...], c_ref[...], d_ref[...]], axis=1)


def _assemble_rows(parts, buf, row0):
    bs = parts[0].shape[0]
    part = pl.BlockSpec((bs, G_WIDTH), lambda i: (0, 0))
    return pl.pallas_call(
        _assemble_body,
        grid=(1,),
        in_specs=[part, part, part, part, ANY_SPEC],
        out_specs=pl.BlockSpec((bs, D_MODEL), lambda i: (row0 // bs, 0)),
        out_shape=jax.ShapeDtypeStruct(buf.shape, buf.dtype),
        input_output_aliases={4: 0},
        compiler_params=_cparams("arbitrary"),
        name="assemble_sample_rows",
    )(*parts, buf)


def kernel(x_prompt, x_sample, mem_prompt, state_conv_a, state_ret, state_shift, state_wkv, state_conv_d, cache_mem_k, cache_mem_v, g_mix, w_in, conv_a_w, mu_c, w0, w2, a0, a2, g2, k_k, k_a, r_k, lnx_g, lnx_b, conv_d_w, conv_d_b, ln_d_g, ln_d_b, w_out, g_xa, g_mem, wq_x, wk_x, wv_x, wo_x, g_mlp, w_up, w_down, g_final):
    bp, seq, d = x_prompt.shape
    bs = x_sample.shape[0]
    depth = w_in.shape[0]
    np_rows = bp * seq
    G = G_WIDTH
    x = jnp.concatenate([x_prompt.reshape(np_rows, d), x_sample.reshape(bs, d)], axis=0)
    mem = mem_prompt.reshape(bp * N_MEM, d)
    cos_p, sin_p = _rope_tables(jnp.arange(seq))
    cos_s, sin_s = _rope_tables(PAST_LEN + jnp.arange(1))

    z64 = jnp.zeros((depth, 64, G), F32)
    w2a = jnp.concatenate([jnp.concatenate([w2, z64], axis=2), jnp.concatenate([z64, a2], axis=2)], axis=1).astype(BF16)
    g2_b = g2.astype(BF16)
    wkv_t = jnp.transpose(state_wkv, (0, 2, 3, 4, 1))
    conv_d_t = jnp.transpose(state_conv_d, (0, 2, 1, 3))
    vec3 = lambda a: a.reshape(depth, 1, -1)
    rw_arrs = (vec3(mu_c), w2a, g2_b, vec3(w0), vec3(a0), vec3(k_k), vec3(k_a), vec3(r_k), vec3(lnx_g), vec3(lnx_b))

    small = {k: [] for k in ("a_p", "ret_p", "sh_p", "sh_s", "wkv_p", "d_p")}
    na_s = nd_s = nret_s = nwkv_s = mk_all = mv_all = None
    for l in range(depth):
        proj = _matmul_ws(x, w_in, l, norm_g=_layer_vec(g_mix, l), tn=1280, tm=_pick(x.shape[0], (640, 512, 256, 128)))
        ymix, na_p = _conv_a_prompt(proj, conv_a_w, l, bp, seq)
        ymix, nd_p = _conv_d_prompt(proj, conv_d_w, conv_d_b, ln_d_g, ln_d_b, ymix, l, bp, seq)
        yb_p, nret_p = _retention_prompt(proj, cos_p, sin_p, bp, seq)
        ymix = _place_cols(yb_p.reshape(np_rows, G), ymix, 1)
        rw = [(rw_arrs[0], pl.BlockSpec((None, 1, SHIFT_W), lambda *_: (l, 0, 0))),
              (w2a, _layer_mat(w2a, l)), (g2_b, _layer_mat(g2_b, l))] + [
              (a, pl.BlockSpec((None, 1, G), lambda *_: (l, 0, 0))) for a in rw_arrs[3:]]
        yc_p, nsh_p, nwkv_p = _rwkv_prompt(proj, rw, bp, seq)
        ymix = _place_cols(yc_p.reshape(np_rows, G), ymix, 2)
        ya_s, yd_s, na_s, nd_s = _sample_ad(proj, state_conv_a, conv_d_t, conv_a_w, conv_d_w, conv_d_b, ln_d_g, ln_d_b,
                                            l, np_rows, na_s, nd_s)
        yb_s, nret_s = _retention_sample(proj, cos_s, sin_s, state_ret, l, np_rows, nret_s)
        yc_s, nwkv_s = _rwkv_sample(proj, state_shift, rw_arrs, wkv_t, l, np_rows, nwkv_s)
        ymix = _assemble_rows([ya_s, yb_s, yc_s, yd_s], ymix, np_rows)
        x = _matmul_ws(ymix, w_out, l, residual=x)

        mk = _matmul_ws(mem, wk_x, l, norm_g=_layer_vec(g_mem, l))
        mv = _matmul_ws(mem, wv_x, l, norm_g=_layer_vec(g_mem, l))
        mk_all = _place_heads(mk, l, depth, bp, mk_all)
        mv_all = _place_heads(mv, l, depth, bp, mv_all)
        q = _matmul_ws(x, wq_x, l, norm_g=_layer_vec(g_xa, l))
        obuf = _xattn_prompt(q, mk, mv, bp, seq)
        q_s = q[np_rows:].reshape(bs, XA_HEADS, XA_HD)
        q_s = jnp.concatenate([q_s, q_s], axis=1)
        obuf = _xattn_sample(q_s, cache_mem_k, cache_mem_v, obuf, l, np_rows)
        x = _matmul_ws(obuf, wo_x, l, residual=x)

        up = _matmul_ws(x, w_up, l, norm_g=_layer_vec(g_mlp, l), act="relu2", out_dtype=BF16)
        for kb in range(w_down.shape[1] // d):
            x = _matmul_ws(up, w_down, l, residual=x, kblock=(kb, d))

        small["a_p"].append(na_p)
        small["ret_p"].append(nret_p)
        small["sh_p"].append(nsh_p.reshape(bp, SHIFT_W))
        small["sh_s"].append(proj[np_rows:, COL_C:COL_C + SHIFT_W])
        small["wkv_p"].append(nwkv_p)
        small["d_p"].append(nd_p)
    gf = (g_final.reshape(1, d), pl.BlockSpec((1, d), lambda *_: (0, 0)))
    y_p = _rmsnorm(x, gf, F32, 0, np_rows)
    y_s = _rmsnorm(x, gf, F32, np_rows, bs)
    st = lambda k: jnp.stack(small[k])
    return (y_p.reshape(bp, seq, d), y_s.reshape(bs, 1, d), st("a_p"), na_s, st("ret_p"), nret_s,
            st("sh_p"), st("sh_s"), st("wkv_p"), jnp.transpose(nwkv_s, (0, 4, 1, 2, 3)),
            st("d_p"), jnp.transpose(nd_s, (0, 2, 1, 3)), mk_all, mv_all)
```

```python
import functools
import math

import jax
import jax.numpy as jnp
from jax import lax
from jax.experimental import pallas as pl
from jax.experimental.pallas import tpu as pltpu

F32 = jnp.float32
BF16 = jnp.bfloat16
HI = lax.Precision.HIGHEST

D_MODEL = 2048
G_WIDTH = 512
P_IN = 6400
RET_HEADS = 4
RET_D = 128
RET_CHUNK = 128
ROPE_BASE = 10000.0
RWKV_HEAD = 64
RWKV_HEADS = 8
WKV_CHUNK = 64
RWKV_GN_EPS = 64e-5
A_CONV = 3
D_CONV = 31
D_HIST = 32
N_MEM = 256
XA_HEADS = 4
XA_HD = 512
SHIFT_W = 1792
PAST_LEN = 16384
EPS = 1e-6
RET_LOG_DECAY = tuple(math.log1p(-(2.0 ** (-5.0 - h))) for h in range(RET_HEADS))
VMEM_LIMIT = 56 * 1024 * 1024
COL_A = 0
COL_B = 3 * G_WIDTH
COL_C = 7 * G_WIDTH
COL_D = 7 * G_WIDTH + SHIFT_W

ANY_SPEC = pl.BlockSpec(memory_space=pl.ANY)


def _cparams(*sem):
    return pltpu.CompilerParams(dimension_semantics=sem, vmem_limit_bytes=VMEM_LIMIT)


def _pick(n, cands):
    for c in cands:
        if n % c == 0:
            return c
    raise ValueError(f"no tile for {n}")


def _sigmoid(x):
    return 1.0 / (1.0 + jnp.exp(-x))


def _dot(a, b, precision=None):
    return jnp.dot(a, b, preferred_element_type=F32, precision=precision)


def _dot_nt(a, b, precision=None):
    return lax.dot_general(a, b, (((1,), (1,)), ((), ())), preferred_element_type=F32, precision=precision)


def _dot_tn(a, b, precision=None):
    return lax.dot_general(a, b, (((0,), (0,)), ((), ())), preferred_element_type=F32, precision=precision)


def _layernorm(x, eps):
    mu = jnp.mean(x, axis=-1, keepdims=True)
    xc = x - mu
    var = jnp.mean(xc * xc, axis=-1, keepdims=True)
    return xc * lax.rsqrt(var + eps)


def _layer_vec(a, l):
    n = a.shape[-1]
    return a.reshape(a.shape[0], 1, n), pl.BlockSpec((None, 1, n), lambda *_: (l, 0, 0))


def _layer_mat(a, l):
    return pl.BlockSpec((None,) + a.shape[1:], lambda *_: (l, 0, 0))


def _rmsnorm_body(x_ref, g_ref, o_ref):
    x = x_ref[...]
    ms = jnp.mean(x * x, axis=-1, keepdims=True)
    o_ref[...] = ((x * lax.rsqrt(ms + EPS)) * g_ref[...]).astype(o_ref.dtype)


def _rmsnorm(x, g_spec, out_dtype, row0=0, nrows=None):
    m, d = x.shape
    nrows = nrows or m
    tm = _pick(math.gcd(nrows, row0) if row0 else nrows, (1040, 1024, 640, 512, 256, 128))
    r0 = row0 // tm
    g, gspec = g_spec
    return pl.pallas_call(
        _rmsnorm_body,
        grid=(nrows // tm,),
        in_specs=[pl.BlockSpec((tm, d), lambda i: (r0 + i, 0)), gspec],
        out_specs=pl.BlockSpec((tm, d), lambda i: (i, 0)),
        out_shape=jax.ShapeDtypeStruct((nrows, d), out_dtype),
        compiler_params=_cparams("parallel"),
        name="rmsnorm",
    )(x, g)


def _mm_ws_body(*refs, act, has_res, has_norm):
    a_ref, w_ref = refs[0], refs[1]
    g_ref = refs[2] if has_norm else None
    res_ref = refs[2 + has_norm] if has_res else None
    o_ref = refs[2 + has_norm + has_res]
    wb_ref = refs[-1]

    @pl.when(pl.program_id(1) == 0)
    def _():
        wb_ref[...] = w_ref[...].astype(BF16)

    a = a_ref[...]
    if has_norm:
        ms = jnp.mean(a * a, axis=-1, keepdims=True)
        a = ((a * lax.rsqrt(ms + EPS)) * g_ref[...]).astype(BF16)
    r = _dot(a, wb_ref[...])
    if act == "relu2":
        r = jnp.square(jnp.maximum(r, 0.0))
    if has_res:
        r = res_ref[...] + r
    o_ref[...] = r.astype(o_ref.dtype)


def _matmul_ws(a, w, l, *, norm_g=None, residual=None, act=None, out_dtype=F32, tn=1024, tm=None, kblock=None):
    m = a.shape[0]
    kb, kdim = kblock if kblock is not None else (0, a.shape[1])
    n = w.shape[2]
    tm = tm or _pick(m, (1040, 1024, 640, 512, 256, 128))
    has_res = residual is not None
    has_norm = norm_g is not None
    in_specs = [pl.BlockSpec((tm, kdim), lambda j, i: (i, kb)), pl.BlockSpec((None, kdim, tn), lambda j, i: (l, kb, j))]
    args = [a, w]
    if has_norm:
        args.append(norm_g[0])
        in_specs.append(norm_g[1])
    if has_res:
        in_specs.append(pl.BlockSpec((tm, tn), lambda j, i: (i, j)))
        args.append(residual)
    return pl.pallas_call(
        functools.partial(_mm_ws_body, act=act, has_res=has_res, has_norm=has_norm),
        grid=(n // tn, m // tm),
        in_specs=in_specs,
        out_specs=pl.BlockSpec((tm, tn), lambda j, i: (i, j)),
        out_shape=jax.ShapeDtypeStruct((m, n), out_dtype),
        scratch_shapes=[pltpu.VMEM((kdim, tn), BF16)],
        compiler_params=_cparams("parallel", "arbitrary"),
        name="matmul_ws",
    )(*args)


def _conv_a_body(b_ref, c_ref, h_ref, w_ref, y_ref, st_ref, ext_ref, *, tl):
    t = pl.program_id(1)

    @pl.when(t == 0)
    def _():
        ext_ref[0:8, :] = jnp.zeros((8, G_WIDTH), F32)

    u = c_ref[...] * h_ref[...]
    ext_ref[8:, :] = u
    u1 = ext_ref[7:7 + tl, :]
    u2 = ext_ref[6:6 + tl, :]
    y = b_ref[...] * (w_ref[0:1, :] * u2 + w_ref[1:2, :] * u1 + w_ref[2:3, :] * u)
    y_ref[...] = y.astype(y_ref.dtype)
    st_ref[0] = ext_ref[tl + 6:tl + 8, :]
    ext_ref[0:8, :] = ext_ref[tl:tl + 8, :]


def _conv_a_prompt(proj, conv_w, l, bp, seq):
    tl = _pick(seq, (1024, 512, 256, 128))
    nt = seq // tl
    c0 = COL_A // G_WIDTH
    col = lambda j: pl.BlockSpec((tl, G_WIDTH), lambda b, t: (b * nt + t, c0 + j))
    return pl.pallas_call(
        functools.partial(_conv_a_body, tl=tl),
        grid=(bp, nt),
        in_specs=[col(0), col(1), col(2), _layer_mat(conv_w, l)],
        out_specs=[pl.BlockSpec((tl, G_WIDTH), lambda b, t: (b * nt + t, 0)),
                   pl.BlockSpec((1, A_CONV - 1, G_WIDTH), lambda b, t: (b, 0, 0))],
        out_shape=[jax.ShapeDtypeStruct((proj.shape[0], D_MODEL), BF16),
                   jax.ShapeDtypeStruct((bp, A_CONV - 1, G_WIDTH), F32)],
        scratch_shapes=[pltpu.VMEM((tl + 8, G_WIDTH), F32)],
        compiler_params=_cparams("parallel", "arbitrary"),
        name="conv_a_prompt",
    )(proj, proj, proj, conv_w)


CONV_D_ROWS = 64


def _conv_d_body(d1a_ref, d1b_ref, d2a_ref, d2b_ref, w_ref, bias_ref, g_ref, beta_ref, ymix_ref, y_ref, st_ref, ext_ref,
                 acc_ref, *, tl):
    del ymix_ref
    t = pl.program_id(1)
    next_ = D_HIST + tl

    @pl.when(t == 0)
    def _():
        ext_ref[0, 0:D_HIST, :] = jnp.zeros((D_HIST, G_WIDTH), F32)

    d1 = jnp.concatenate([d1a_ref[...], d1b_ref[...]], axis=1)
    d2 = jnp.concatenate([d2a_ref[...], d2b_ref[...]], axis=1)
    ext_ref[0, D_HIST:, :] = d1 * _sigmoid(d2)
    ext = ext_ref[0]
    for s in range(1, 8):
        ext_ref[s] = pltpu.roll(ext, next_ - s, 0)
    first = D_HIST - (D_CONV - 1)

    def blk(i, carry):
        base = pl.multiple_of(i * CONV_D_ROWS, CONV_D_ROWS)
        acc = jnp.broadcast_to(bias_ref[...], (CONV_D_ROWS, G_WIDTH))
        for k in range(D_CONV):
            s = (first + k) % 8
            a0 = first + k - s
            acc = acc + ext_ref[s, pl.ds(base + a0, CONV_D_ROWS), :] * w_ref[k:k + 1, :]
        acc_ref[pl.ds(base, CONV_D_ROWS), :] = acc
        return carry

    lax.fori_loop(0, tl // CONV_D_ROWS, blk, 0)
    c = _layernorm(acc_ref[...], EPS) * g_ref[...] + beta_ref[...]
    y_ref[...] = (c * _sigmoid(c)).astype(y_ref.dtype)
    st_ref[0] = ext_ref[0, tl + first:tl + D_HIST, :]
    ext_ref[0, 0:D_HIST, :] = ext_ref[0, tl:tl + D_HIST, :]


def _conv_d_prompt(proj, w, bias, g, beta, ymix, l, bp, seq):
    tl = _pick(seq, (512, 256, 128))
    nt = seq // tl
    half = G_WIDTH // 2
    c0 = COL_D // half
    col = lambda j: pl.BlockSpec((tl, half), lambda b, t: (b * nt + t, c0 + j))
    (bias, bias_s), (g, g_s), (beta, beta_s) = _layer_vec(bias, l), _layer_vec(g, l), _layer_vec(beta, l)
    return pl.pallas_call(
        functools.partial(_conv_d_body, tl=tl),
        grid=(bp, nt),
        in_specs=[col(0), col(1), col(2), col(3), _layer_mat(w, l), bias_s, g_s, beta_s, ANY_SPEC],
        out_specs=[pl.BlockSpec((tl, G_WIDTH), lambda b, t: (b * nt + t, 3)),
                   pl.BlockSpec((1, D_CONV - 1, G_WIDTH), lambda b, t: (b, 0, 0))],
        out_shape=[jax.ShapeDtypeStruct(ymix.shape, ymix.dtype),
                   jax.ShapeDtypeStruct((bp, D_CONV - 1, G_WIDTH), F32)],
        scratch_shapes=[pltpu.VMEM((8, tl + D_HIST, G_WIDTH), F32), pltpu.VMEM((tl, G_WIDTH), F32)],
        input_output_aliases={8: 0},
        compiler_params=_cparams("parallel", "arbitrary"),
        name="conv_d_prompt",
    )(proj, proj, proj, proj, w, bias, g, beta, ymix)


def _rope_tables(pos):
    inv = ROPE_BASE ** (-jnp.arange(0, RET_D, 2, dtype=F32) / RET_D)
    ang = pos.astype(F32)[:, None] * inv[None, :]
    cos = jnp.cos(ang)
    sin = jnp.sin(ang)
    return jnp.concatenate([cos, cos], axis=-1), jnp.concatenate([-sin, sin], axis=-1)


def _rope(x, cos2, sin2):
    return x * cos2 + pltpu.roll(x, RET_D // 2, 1) * sin2


def _ret_body(*refs, nb, nchunks):
    qkvg = refs[:4 * nb]
    cos_ref, sin_ref, y_ref, st_ref, s_ref = refs[4 * nb:]
    c = pl.program_id(0)
    C = RET_CHUNK

    @pl.when(c == 0)
    def _():
        s_ref[...] = jnp.zeros_like(s_ref)

    ii = lax.broadcasted_iota(jnp.int32, (C, C), 0).astype(F32)
    jj = lax.broadcasted_iota(jnp.int32, (C, C), 1).astype(F32)
    diff = ii - jj
    cos2 = cos_ref[...]
    sin2 = sin_ref[...]
    NH = RET_HEADS
    chains = [(b, h) for b in range(nb) for h in range(NH)]
    I = range(len(chains))
    sl = [slice(h * RET_D, (h + 1) * RET_D) for h in range(NH)]
    lg = RET_LOG_DECAY
    q = [_rope(qkvg[4 * b][:, sl[h]], cos2, sin2) for b, h in chains]
    k = [_rope(qkvg[4 * b + 1][:, sl[h]], cos2, sin2) * (RET_D ** -0.5) for b, h in chains]
    qb = [q[i].astype(BF16) for i in I]
    vb = [qkvg[4 * b + 2][:, sl[h]].astype(BF16) for b, h in chains]
    s_old = [s_ref[i] for i in I]
    scores = [_dot_nt(qb[i], k[i].astype(BF16)) for i in I]
    cross = [_dot(qb[i], s_old[i].astype(BF16)) for i in I]
    k_dec = [jnp.exp(lg[h] * (C - 1.0 - ii)) for h in range(NH)]
    kv = [_dot_tn((k[i] * k_dec[chains[i][1]]).astype(BF16), vb[i]) for i in I]
    dmat = [jnp.where(diff >= 0.0, jnp.exp(lg[h] * jnp.maximum(diff, 0.0)), 0.0) for h in range(NH)]
    inner = [_dot((scores[i] * dmat[chains[i][1]]).astype(BF16), vb[i]) for i in I]
    q_dec = [jnp.exp(lg[h] * (ii + 1.0)) for h in range(NH)]
    for i, (b, h) in enumerate(chains):
        s_ref[i] = math.exp(lg[h] * C) * s_old[i] + kv[i]
        o = _layernorm(inner[i] + cross[i] * q_dec[h], EPS)
        g = qkvg[4 * b + 3][:, sl[h]]
        y_ref[b, :, sl[h]] = (g * _sigmoid(g) * o).astype(y_ref.dtype)

    @pl.when(c == nchunks - 1)
    def _():
        for i, (b, h) in enumerate(chains):
            st_ref[b, h] = s_ref[i]


def _retention_prompt(proj, cos2, sin2, bp, seq):
    C = RET_CHUNK
    nc = seq // C
    c0 = COL_B // G_WIDTH
    cols = [pl.BlockSpec((C, G_WIDTH), lambda c, b=b, j=j: (b * nc + c, c0 + j)) for b in range(bp) for j in range(4)]
    tab = pl.BlockSpec((C, RET_D), lambda c: (c, 0))
    return pl.pallas_call(
        functools.partial(_ret_body, nb=bp, nchunks=nc),
        grid=(nc,),
        in_specs=cols + [tab, tab],
        out_specs=[pl.BlockSpec((bp, C, G_WIDTH), lambda c: (0, c, 0)),
                   pl.BlockSpec((bp, RET_HEADS, RET_D, RET_D), lambda c: (0, 0, 0, 0))],
        out_shape=[jax.ShapeDtypeStruct((bp, seq, G_WIDTH), BF16),
                   jax.ShapeDtypeStruct((bp, RET_HEADS, RET_D, RET_D), F32)],
        scratch_shapes=[pltpu.VMEM((bp * RET_HEADS, RET_D, RET_D), F32)],
        compiler_params=_cparams("arbitrary"),
        name="retention_prompt",
    )(*([proj] * (4 * bp)), cos2, sin2)


def _softplus(z):
    return jnp.maximum(z, 0.0) + jnp.log(1.0 + jnp.exp(-jnp.abs(z)))


def _rwkv_pre(r, kc, vc, lora_x, w2_w, w2_a, g2, w0, a0, k_k, k_a):
    lin = lora_x[:, 0:128]
    lane = lax.broadcasted_iota(jnp.int32, lin.shape, 1)
    lin = jnp.where(lane < 64, jnp.tanh(lin), lin).astype(BF16)
    gate = _dot(_sigmoid(lora_x[:, 128:256]).astype(BF16), g2)
    w_pre = -_softplus(-(w0 + _dot(lin, w2_w))) - 0.5
    log_decay = -jnp.exp(w_pre)
    a = _sigmoid(a0 + _dot(lin, w2_a))
    kk = kc * k_k
    k4 = kc * (1.0 + (a - 1.0) * k_a)
    return r, log_decay, k4, vc, kk, a, gate


def _l2_normalize(kk):
    ss = jnp.sum(kk * kk, axis=-1, keepdims=True)
    return kk * jnp.minimum(lax.rsqrt(ss), 1e12)


def _bdot(a, b):
    return _dot(a.astype(BF16), b.astype(BF16))


def _seg_sum(x, lo):
    s_lo = jnp.sum(jnp.where(lo, x, 0.0), axis=-1, keepdims=True)
    s_hi = jnp.sum(jnp.where(lo, 0.0, x), axis=-1, keepdims=True)
    return jnp.where(lo, s_lo, s_hi)


def _wkv_chunk(r, lw, cum, k, v, kk, a, zbd, mask2, eye2):
    T = WKV_CHUNK
    N = RWKV_HEAD
    P = range(len(r))
    HH = range(2)
    lo = lax.broadcasted_iota(jnp.int32, (1, 2 * N), 1) < N
    lo2 = (lax.broadcasted_iota(jnp.int32, (1, 4 * N), 1) & (2 * N - 1)) < N
    g_inc = [jnp.exp(cum[p]) for p in P]
    g_tot = [g_inc[p][T - 1:T, :] for p in P]
    ah = [-(kk[p] * jnp.exp(cum[p] - lw[p])) for p in P]
    rg = [r[p] * g_inc[p] for p in P]
    g_inv = [jnp.exp(-cum[p]) for p in P]
    bk = [jnp.concatenate([kk[p] * a[p] * g_inv[p], k[p] * g_inv[p]], axis=0) for p in P]
    ar = [jnp.concatenate([ah[p], rg[p]], axis=0).astype(BF16) for p in P]
    bkm = [jnp.concatenate([jnp.where(lo, bk[p], 0.0), jnp.where(lo, 0.0, bk[p])], axis=0).astype(BF16) for p in P]
    x2 = [_dot_nt(ar[p], bkm[p]) for p in P]
    x = [[jnp.where(mask2, x2[p][:, 2 * T * hh:2 * T * (hh + 1)], 0.0) for hh in HH] for p in P]
    vb = [v[p].astype(BF16) for p in P]
    zero = jnp.zeros((T, 2 * N), BF16)
    vz = [jnp.concatenate([zero, vb[p]], axis=0) for p in P]
    xkv2 = [_dot(jnp.concatenate([x[p][0], x[p][1]], axis=0).astype(BF16), vz[p]) for p in P]
    xkv = [jnp.where(lo, xkv2[p][0:2 * T], xkv2[p][2 * T:4 * T]) for p in P]
    acur = [[x[p][hh][0:T, 0:T] for hh in HH] for p in P]
    inv = [[eye2[0:T, 0:T] for hh in HH] for p in P]
    for it in range(5):
        y = [[_bdot(jnp.concatenate([acur[p][hh], inv[p][hh]], axis=0), acur[p][hh]) for hh in HH] for p in P]
        acur = [[y[p][hh][0:T] for hh in HH] for p in P]
        inv = [[inv[p][hh] + y[p][hh][T:2 * T] for hh in HH] for p in P]
    y = [[_bdot(inv[p][hh], acur[p][hh]) for hh in HH] for p in P]
    inv2 = [jnp.concatenate([inv[p][0] + y[p][0], inv[p][1] + y[p][1]], axis=0).astype(BF16) for p in P]
    rhs = [jnp.concatenate([ah[p], xkv[p][0:T]], axis=1).astype(BF16) for p in P]
    wu2 = [_dot(inv2[p], rhs[p]) for p in P]
    wub = [jnp.where(lo2, wu2[p][0:T], wu2[p][T:2 * T]).astype(BF16) for p in P]
    arb2 = [jnp.concatenate([x[p][0][T:2 * T, 0:T], x[p][1][T:2 * T, 0:T]], axis=0).astype(BF16) for p in P]
    qo2 = [_dot(arb2[p], wub[p]) for p in P]
    qo = [jnp.where(lo2, qo2[p][0:T], qo2[p][T:2 * T]) for p in P]
    lhs_t = [(bk[p] * g_tot[p]).astype(BF16) for p in P]
    rhs_t = [jnp.concatenate([wub[p], jnp.concatenate([zero, vb[p]], axis=1)], axis=0) for p in P]
    mn = [_dot_tn(lhs_t[p], rhs_t[p]) for p in P]
    mz = [jnp.where(lo, mn[p][0:N, 0:2 * N], mn[p][N:2 * N, 0:2 * N]) + jnp.where(eye2[0:N, :] > 0.0, g_tot[p], 0.0)
          for p in P]
    nz = [jnp.where(lo, mn[p][0:N, 2 * N:4 * N], mn[p][N:2 * N, 2 * N:4 * N]) for p in P]
    qm = [jnp.concatenate([rg[p] + qo[p][:, 0:2 * N], mz[p]], axis=0) for p in P]
    oz = [_bdot(qm[p], zbd[p]) for p in P]
    o = [oz[p][0:T] + qo[p][:, 2 * N:4 * N] + xkv[p][T:2 * T] for p in P]
    z_new = [oz[p][T:T + N] + nz[p] for p in P]
    zbd_new = [jnp.concatenate([jnp.where(lo, z_new[p], 0.0), jnp.where(lo, 0.0, z_new[p])], axis=0) for p in P]
    return o, zbd_new


def _rwkv_post(o, r, k4, v, gate, rk, lnx_g, lnx_b):
    gn = _layernorm(o, RWKV_GN_EPS) * lnx_g + lnx_b
    bonus = jnp.sum(r * k4 * rk, axis=-1, keepdims=True) * v
    return (gn + bonus) * gate


def _rwkv_prompt_body(*refs, nb, nchunks):
    pc_refs = refs[:nb]
    mu_ref, w2a_ref, g2_ref, w0_ref, a0_ref, kk_ref, ka_ref, rk_ref, lg_ref, lb_ref = refs[nb:nb + 10]
    y_ref, shift_ref, st_ref, carry_ref, z_ref = refs[nb + 10:]
    c = pl.program_id(0)
    T = WKV_CHUNK
    NH = RWKV_HEADS

    @pl.when(c == 0)
    def _():
        carry_ref[...] = jnp.zeros_like(carry_ref)
        z_ref[...] = jnp.zeros_like(z_ref)

    pcs, prevs = [], []
    for b in range(nb):
        pc = pc_refs[b][...]
        row = lax.broadcasted_iota(jnp.int32, pc.shape, 0)
        prevs.append(jnp.where(row == 0, carry_ref[b, 0:1, :], pltpu.roll(pc, 1, 0)))
        carry_ref[b, 0:1, :] = pc[T - 1:T, :]
        shift_ref[b] = pc[T - 1:T, :]
        pcs.append(pc)
    pc = jnp.concatenate(pcs, axis=0)
    xs = pc + (jnp.concatenate(prevs, axis=0) - pc) * mu_ref[...]
    G = G_WIDTH
    r, lw, k4, v, kk, a, gate = _rwkv_pre(xs[:, 0:G], xs[:, G:2 * G], xs[:, 2 * G:3 * G], xs[:, 3 * G:3 * G + 256],
                                          w2a_ref[:, 0:G], w2a_ref[:, G:2 * G], g2_ref[...], w0_ref[...], a0_ref[...],
                                          kk_ref[...], ka_ref[...])
    N = RWKV_HEAD
    W = 2 * N
    NP = NH // 2
    ti = lax.broadcasted_iota(jnp.int32, (T, T), 0)
    si = lax.broadcasted_iota(jnp.int32, (T, T), 1)
    tril = (ti >= si).astype(F32)
    cum = jnp.concatenate([_dot(tril, lw[b * T:(b + 1) * T, :], HI) for b in range(nb)], axis=0)
    t2 = lax.broadcasted_iota(jnp.int32, (2 * T, 2 * T), 0)
    s2 = lax.broadcasted_iota(jnp.int32, (2 * T, 2 * T), 1)
    mask2 = jnp.where(t2 < T, t2 - 1, t2 - T) >= jnp.where(s2 < T, s2, s2 - T)
    eye2 = (lax.broadcasted_iota(jnp.int32, (N, W), 0) == (lax.broadcasted_iota(jnp.int32, (N, W), 1) & (N - 1))).astype(F32)
    lo = lax.broadcasted_iota(jnp.int32, (1, W), 1) < N

    def normalized(x):
        return x * jnp.minimum(lax.rsqrt(_seg_sum(x * x, lo)), 1e12)

    pairs = lambda x: [x[b * T:(b + 1) * T, p * W:(p + 1) * W] for b in range(nb) for p in range(NP)]
    r_p, k_p, v_p = pairs(r), pairs(k4), pairs(v)
    o, z_new = _wkv_chunk(r_p, pairs(lw), pairs(cum), k_p, v_p, [normalized(x) for x in pairs(kk)], pairs(a),
                          [z_ref[i] for i in range(nb * NP)], mask2, eye2)
    for i in range(nb * NP):
        z_ref[i] = z_new[i]
    gate_p = pairs(gate)
    vec_pairs = lambda ref: [ref[:, p * W:(p + 1) * W] for p in range(NP)]
    rk_p, lg_p, lb_p = vec_pairs(rk_ref), vec_pairs(lg_ref), vec_pairs(lb_ref)
    for b in range(nb):
        ys = []
        for p in range(NP):
            i = b * NP + p
            oc = o[i] - _seg_sum(o[i], lo) * (1.0 / N)
            var = _seg_sum(oc * oc, lo) * (1.0 / N)
            gn = oc * lax.rsqrt(var + RWKV_GN_EPS) * lg_p[p] + lb_p[p]
            bonus = _seg_sum(r_p[i] * k_p[i] * rk_p[p], lo) * v_p[i]
            ys.append((gn + bonus) * gate_p[i])
        y_ref[b] = jnp.concatenate(ys, axis=1).astype(y_ref.dtype)

    @pl.when(c == nchunks - 1)
    def _():
        for b in range(nb):
            for p in range(NP):
                zbd = z_ref[b * NP + p]
                zt = (zbd[0:N] + zbd[N:W]).T
                st_ref[b, 2 * p] = zt[0:N]
                st_ref[b, 2 * p + 1] = zt[N:W]


def _rwkv_prompt(proj, rw, bp, seq):
    T = WKV_CHUNK
    nc = seq // T
    G = G_WIDTH
    arrs, specs = zip(*rw)
    pc_specs = [pl.BlockSpec((T, SHIFT_W), lambda c, b=b: (b * nc + c, COL_C // SHIFT_W)) for b in range(bp)]
    return pl.pallas_call(
        functools.partial(_rwkv_prompt_body, nb=bp, nchunks=nc),
        grid=(nc,),
        in_specs=pc_specs + list(specs),
        out_specs=[pl.BlockSpec((bp, T, G), lambda c: (0, c, 0)),
                   pl.BlockSpec((bp, 1, SHIFT_W), lambda c: (0, 0, 0)),
                   pl.BlockSpec((bp, RWKV_HEADS, RWKV_HEAD, RWKV_HEAD), lambda c: (0, 0, 0, 0))],
        out_shape=[jax.ShapeDtypeStruct((bp, seq, G), BF16),
                   jax.ShapeDtypeStruct((bp, 1, SHIFT_W), F32),
                   jax.ShapeDtypeStruct((bp, RWKV_HEADS, RWKV_HEAD, RWKV_HEAD), F32)],
        scratch_shapes=[pltpu.VMEM((bp, 8, SHIFT_W), F32),
                        pltpu.VMEM((bp * RWKV_HEADS // 2, 2 * RWKV_HEAD, 2 * RWKV_HEAD), F32)],
        compiler_params=_cparams("arbitrary"),
        name="rwkv_prompt",
    )(*([proj] * bp), *arrs)


def _xattn_prompt_body(q_ref, k_ref, v_ref, o_ref, kb_ref, vb_ref):
    @pl.when(pl.program_id(1) == 0)
    def _():
        kb_ref[...] = k_ref[...].astype(BF16)
        vb_ref[...] = v_ref[...].astype(BF16)

    H = range(XA_HEADS)
    sl = [slice(h * XA_HD, (h + 1) * XA_HD) for h in H]
    s = [_dot_nt(q_ref[:, sl[h]].astype(BF16), kb_ref[:, sl[h]]) * (XA_HD ** -0.5) for h in H]
    e = [jnp.exp(s[h] - jnp.max(s[h], axis=-1, keepdims=True)) for h in H]
    p = [(e[h] / jnp.sum(e[h], axis=-1, keepdims=True)).astype(BF16) for h in H]
    for h in H:
        o_ref[:, sl[h]] = _dot(p[h], vb_ref[:, sl[h]]).astype(o_ref.dtype)


def _xattn_prompt(q, mk, mv, bp, seq):
    tq = _pick(seq, (1024, 512, 256, 128))
    nt = seq // tq
    kv = pl.BlockSpec((N_MEM, D_MODEL), lambda b, t: (b, 0))
    return pl.pallas_call(
        _xattn_prompt_body,
        grid=(bp, nt),
        in_specs=[pl.BlockSpec((tq, D_MODEL), lambda b, t: (b * nt + t, 0)), kv, kv],
        out_specs=pl.BlockSpec((tq, D_MODEL), lambda b, t: (b * nt + t, 0)),
        out_shape=jax.ShapeDtypeStruct((q.shape[0], D_MODEL), BF16),
        scratch_shapes=[pltpu.VMEM((N_MEM, D_MODEL), BF16), pltpu.VMEM((N_MEM, D_MODEL), BF16)],
        compiler_params=_cparams("parallel", "arbitrary"),
        name="xattn_prompt",
    )(q, mk, mv)


XA_SAMPLES_PER_STEP = 4


def _xattn_sample_body(q_ref, k_ref, v_ref, obuf_ref, o_ref, acc_ref, *, nsteps):
    del obuf_ref
    i = pl.program_id(0)
    H = XA_HEADS
    groups = N_MEM * H // (2 * H)
    for j in range(XA_SAMPLES_PER_STEP):
        b = i * XA_SAMPLES_PER_STEP + j
        k3 = k_ref[j].reshape(N_MEM * H, XA_HD).reshape(groups, 2 * H, XA_HD)
        s = jnp.sum(k3 * q_ref[b], axis=-1, keepdims=True) * (XA_HD ** -0.5)
        mx = jnp.max(s, axis=0, keepdims=True)
        mx = jnp.maximum(mx, pltpu.roll(mx, H, 1))
        e = jnp.exp(s - mx)
        den = jnp.sum(e, axis=0, keepdims=True)
        den = den + pltpu.roll(den, H, 1)
        v3 = v_ref[j].reshape(N_MEM * H, XA_HD).reshape(groups, 2 * H, XA_HD)
        o8 = jnp.sum((e / den) * v3, axis=0)
        o = o8[0:H] + o8[H:2 * H]
        acc_ref[pl.ds(b, 1), :] = jnp.concatenate([o[h:h + 1, :] for h in range(H)], axis=1)

    @pl.when(i == nsteps - 1)
    def _():
        o_ref[...] = acc_ref[...].astype(o_ref.dtype)


def _xattn_sample(q, ck, cv, obuf, l, row0):
    bs = q.shape[0]
    nb = XA_SAMPLES_PER_STEP
    kv = pl.BlockSpec((None, nb, N_MEM, XA_HEADS, XA_HD), lambda i: (l, i, 0, 0, 0))
    return pl.pallas_call(
        functools.partial(_xattn_sample_body, nsteps=bs // nb),
        grid=(bs // nb,),
        in_specs=[pl.BlockSpec((bs, 2 * XA_HEADS, XA_HD), lambda i: (0, 0, 0)), kv, kv, ANY_SPEC],
        out_specs=pl.BlockSpec((bs, D_MODEL), lambda i: (row0 // bs, 0)),
        out_shape=jax.ShapeDtypeStruct(obuf.shape, obuf.dtype),
        scratch_shapes=[pltpu.VMEM((bs, D_MODEL), F32)],
        input_output_aliases={3: 0},
        compiler_params=_cparams("arbitrary"),
        name="xattn_sample",
    )(q, ck, cv, obuf)


def _alias_args(prev):
    return ([], []) if prev is None else ([prev], [ANY_SPEC])


def _sample_ad_body(*refs, has_prev):
    (ab_ref, ac_ref, ah_ref, sta_ref, wa_ref, d1a_ref, d1b_ref, d2a_ref, d2b_ref, std_ref, wd_ref, bias_ref, g_ref,
     beta_ref) = refs[:14]
    ya_ref, yd_ref, na_ref, nd_ref = refs[14 + 2 * has_prev:]
    u = ac_ref[...] * ah_ref[...]
    s0 = sta_ref[:, 0, :]
    s1 = sta_ref[:, 1, :]
    ya = ab_ref[...] * (wa_ref[0:1, :] * s0 + wa_ref[1:2, :] * s1 + wa_ref[2:3, :] * u)
    ya_ref[...] = ya.astype(ya_ref.dtype)
    na_ref[:, 0, :] = s1
    na_ref[:, 1, :] = u
    d1 = jnp.concatenate([d1a_ref[...], d1b_ref[...]], axis=1)
    d2 = jnp.concatenate([d2a_ref[...], d2b_ref[...]], axis=1)
    ud = d1 * _sigmoid(d2)
    acc = jnp.broadcast_to(bias_ref[...], ud.shape)
    nh = D_CONV - 1
    for k in range(nh):
        s_k = std_ref[k]
        acc = acc + s_k * wd_ref[k:k + 1, :]
        if k > 0:
            nd_ref[k - 1] = s_k
    acc = acc + ud * wd_ref[nh:nh + 1, :]
    nd_ref[nh - 1] = ud
    c = _layernorm(acc, EPS) * g_ref[...] + beta_ref[...]
    yd_ref[...] = (c * _sigmoid(c)).astype(yd_ref.dtype)


def _sample_ad(proj, st_a, st_d, wa, wd, bias, g, beta, l, row0, prev_a, prev_d):
    bs = st_a.shape[1]
    G = G_WIDTH
    tb = 32
    r0 = row0 // tb
    half = G // 2
    cola = lambda j: pl.BlockSpec((tb, G), lambda i: (r0 + i, COL_A // G + j))
    cold = lambda j: pl.BlockSpec((tb, half), lambda i: (r0 + i, COL_D // half + j))
    sta = pl.BlockSpec((None, tb, A_CONV - 1, G), lambda i: (l, i, 0, 0))
    std = pl.BlockSpec((None, D_CONV - 1, tb, G), lambda i: (l, 0, i, 0))
    yrow = pl.BlockSpec((tb, G), lambda i: (i, 0))
    (bias, bias_s), (g, g_s), (beta, beta_s) = _layer_vec(bias, l), _layer_vec(g, l), _layer_vec(beta, l)
    nh = D_CONV - 1
    has_prev = prev_a is not None
    extra = [prev_a, prev_d] if has_prev else []
    return pl.pallas_call(
        functools.partial(_sample_ad_body, has_prev=has_prev),
        grid=(bs // tb,),
        in_specs=[cola(0), cola(1), cola(2), sta, _layer_mat(wa, l), cold(0), cold(1), cold(2), cold(3), std,
                  _layer_mat(wd, l), bias_s, g_s, beta_s] + [ANY_SPEC] * len(extra),
        out_specs=[yrow, yrow, sta, std],
        out_shape=[jax.ShapeDtypeStruct((bs, G), BF16), jax.ShapeDtypeStruct((bs, G), BF16),
                   jax.ShapeDtypeStruct(st_a.shape, F32), jax.ShapeDtypeStruct(st_d.shape, F32)],
        input_output_aliases={14: 2, 15: 3} if has_prev else {},
        compiler_params=_cparams("parallel"),
        name="sample_conv_ad",
    )(proj, proj, proj, st_a, wa, proj, proj, proj, proj, st_d, wd, bias, g, beta, *extra)


RET_D_BLOCK = 32


def _ret_sample_body(*refs, has_prev):
    q_ref, k_ref, v_ref, g_ref, cos_ref, sin_ref, s_ref = refs[:7]
    y_ref, ns_ref, qt_ref, kt_ref, acc_ref = refs[7 + has_prev:]
    h = pl.program_id(0)
    j = pl.program_id(1)
    nd = RET_D // RET_D_BLOCK
    gammas = [math.exp(lg) for lg in RET_LOG_DECAY]
    gamma = jnp.where(h == 0, gammas[0], jnp.where(h == 1, gammas[1], jnp.where(h == 2, gammas[2], gammas[3])))
    gamma = gamma.astype(F32)

    @pl.when(j == 0)
    def _():
        q = _rope(q_ref[...], cos_ref[...], sin_ref[...])
        k = _rope(k_ref[...], cos_ref[...], sin_ref[...]) * (RET_D ** -0.5)
        qt_ref[...] = q.T
        kt_ref[...] = k.T
        acc_ref[...] = jnp.zeros_like(acc_ref)

    vt = v_ref[...].T
    acc = acc_ref[...]
    for d in range(RET_D_BLOCK):
        s_d = s_ref[:, d, :].T
        row = j * RET_D_BLOCK + d
        q_d = qt_ref[pl.ds(row, 1), :]
        k_d = kt_ref[pl.ds(row, 1), :]
        acc = acc + q_d * s_d
        ns_ref[:, d, :] = (gamma * s_d + k_d * vt).T
    acc_ref[...] = acc

    @pl.when(j == nd - 1)
    def _():
        qk = jnp.sum(qt_ref[...] * kt_ref[...], axis=0, keepdims=True)
        o = qk * vt + acc * gamma
        mu = jnp.mean(o, axis=0, keepdims=True)
        oc = o - mu
        var = jnp.mean(oc * oc, axis=0, keepdims=True)
        ln = (oc * lax.rsqrt(var + EPS)).T
        g = g_ref[...]
        y_ref[...] = (g * _sigmoid(g) * ln).astype(y_ref.dtype)


def _retention_sample(proj, cos2, sin2, state, l, row0, prev):
    bs = state.shape[1]
    r0 = row0 // bs
    nd = RET_D // RET_D_BLOCK
    col = lambda c: pl.BlockSpec((bs, RET_D), lambda h, j: (r0, COL_B // RET_D + c * RET_HEADS + h))
    tab = pl.BlockSpec((1, RET_D), lambda h, j: (0, 0))
    sblk = pl.BlockSpec((None, bs, None, RET_D_BLOCK, RET_D), lambda h, j: (l, 0, h, j, 0))
    extra, extra_specs = _alias_args(prev)
    return pl.pallas_call(
        functools.partial(_ret_sample_body, has_prev=prev is not None),
        grid=(RET_HEADS, nd),
        in_specs=[col(0), col(1), col(2), col(3), tab, tab, sblk] + extra_specs,
        out_specs=[pl.BlockSpec((bs, RET_D), lambda h, j: (0, h)), sblk],
        out_shape=[jax.ShapeDtypeStruct((bs, G_WIDTH), BF16), jax.ShapeDtypeStruct(state.shape, F32)],
        scratch_shapes=[pltpu.VMEM((RET_D, bs), F32), pltpu.VMEM((RET_D, bs), F32), pltpu.VMEM((RET_D, bs), F32)],
        input_output_aliases={7: 1} if prev is not None else {},
        compiler_params=_cparams("parallel", "arbitrary"),
        name="retention_sample",
    )(proj, proj, proj, proj, cos2, sin2, state, *extra)


def _rwkv_sample_body(*refs, has_prev):
    (xr_ref, xk_ref, xv_ref, xl_ref, pr_ref, pk_ref, pv_ref, pl_ref, mr_ref, mk_ref, mv_ref, ml_ref,
     w2w_ref, w2a_ref, g2_ref, w0_ref, a0_ref, kk_ref, ka_ref, rk_ref, lg_ref, lb_ref, s_ref) = refs[:23]
    y_ref, ns_ref, vec_ref, ot_ref = refs[23 + has_prev:]
    N = RWKV_HEAD

    def shifted(x_ref, prev_ref, mu_ref):
        x = x_ref[...]
        return x + (prev_ref[...] - x) * mu_ref[...]

    r, lw, k4, v, kk, a, gate = _rwkv_pre(shifted(xr_ref, pr_ref, mr_ref), shifted(xk_ref, pk_ref, mk_ref),
                                          shifted(xv_ref, pv_ref, mv_ref), shifted(xl_ref, pl_ref, ml_ref),
                                          w2w_ref[...], w2a_ref[...], g2_ref[...], w0_ref[...], a0_ref[...],
                                          kk_ref[...], ka_ref[...])
    kkn = jnp.concatenate([_l2_normalize(kk[:, 0:N]), _l2_normalize(kk[:, N:2 * N])], axis=1)
    w = jnp.exp(lw)
    vec_ref[0] = kkn.T
    vec_ref[1] = w.T
    vec_ref[2] = (kkn * a).T
    vec_ref[3] = k4.T
    vec_ref[4] = r.T
    vec_ref[5] = v.T
    for hh in range(2):
        ch = slice(hh * N, (hh + 1) * N)
        kk_t = vec_ref[0, ch, :]
        w_t = vec_ref[1, ch, :]
        kka_t = vec_ref[2, ch, :]
        k_t = vec_ref[3, ch, :]
        r_t = vec_ref[4, ch, :]
        for i in range(N):
            s_i = s_ref[hh, i]
            sa = -jnp.sum(s_i * kk_t, axis=0, keepdims=True)
            v_i = vec_ref[5, hh * N + i:hh * N + i + 1, :]
            s_new = s_i * w_t + sa * kka_t + v_i * k_t
            ns_ref[hh, i] = s_new
            ot_ref[hh * N + i:hh * N + i + 1, :] = jnp.sum(s_new * r_t, axis=0, keepdims=True)
    o = ot_ref[...].T
    rk = rk_ref[...]
    lg = lg_ref[...]
    lb = lb_ref[...]
    ys = []
    for hh in range(2):
        ch = slice(hh * N, (hh + 1) * N)
        ys.append(_rwkv_post(o[:, ch], r[:, ch], k4[:, ch], v[:, ch], gate[:, ch], rk[:, ch], lg[:, ch], lb[:, ch]))
    y_ref[...] = jnp.concatenate(ys, axis=1).astype(y_ref.dtype)


def _rwkv_sample(proj, prev_shift, rw_arrs, state, l, row0, prev):
    mu, w2a, g2, w0, a0, k_k, k_a, r_k, lnx_g, lnx_b = rw_arrs
    bs = state.shape[-1]
    r0 = row0 // bs
    G = G_WIDTH
    N = RWKV_HEAD
    W = 2 * N
    npair = G // W
    def cols(shape_lead, lead_idx, base):
        mk = lambda width, cidx: pl.BlockSpec(shape_lead + (width,), lambda p: lead_idx + (cidx(p),))
        return [mk(W, lambda p, c=c: base // W + c * npair + p) for c in range(3)] + [
            mk(2 * W, lambda p: (base + 3 * G) // (2 * W))]
    pairv = pl.BlockSpec((None, 1, W), lambda p: (l, 0, p))
    pairm = pl.BlockSpec((None, 128, W), lambda p: (l, 0, p))
    sblk = pl.BlockSpec((None, 2, N, N, bs), lambda p: (l, p, 0, 0, 0))
    extra, extra_specs = _alias_args(prev)
    return pl.pallas_call(
        functools.partial(_rwkv_sample_body, has_prev=prev is not None),
        grid=(npair,),
        in_specs=cols((bs,), (r0,), COL_C) + cols((None, bs), (l, 0), 0) + cols((None, 1), (l, 0), 0) + [
            pairm, pl.BlockSpec((None, 128, W), lambda p: (l, 0, npair + p)), pairm,
            pairv, pairv, pairv, pairv, pairv, pairv, pairv, sblk] + extra_specs,
        out_specs=[pl.BlockSpec((bs, W), lambda p: (0, p)), sblk],
        out_shape=[jax.ShapeDtypeStruct((bs, G), BF16), jax.ShapeDtypeStruct(state.shape, F32)],
        scratch_shapes=[pltpu.VMEM((6, W, bs), F32), pltpu.VMEM((W, bs), F32)],
        input_output_aliases={23: 1} if prev is not None else {},
        compiler_params=_cparams("parallel"),
        name="rwkv_sample",
    )(proj, proj, proj, proj, prev_shift, prev_shift, prev_shift, prev_shift, mu, mu, mu, mu, w2a, w2a, g2,
      w0, a0, k_k, k_a, r_k, lnx_g, lnx_b, state, *extra)


def _place_cols_body(src_ref, buf_ref, o_ref):
    del buf_ref
    o_ref[...] = src_ref[...]


def _place_cols(src, buf, col):
    rows, w = src.shape
    tr = _pick(rows, (1024, 512, 256, 128))
    return pl.pallas_call(
        _place_cols_body,
        grid=(rows // tr,),
        in_specs=[pl.BlockSpec((tr, w), lambda i: (i, 0)), ANY_SPEC],
        out_specs=pl.BlockSpec((tr, w), lambda i: (i, col)),
        out_shape=jax.ShapeDtypeStruct(buf.shape, buf.dtype),
        input_output_aliases={1: 0},
        compiler_params=_cparams("parallel"),
        name="place_cols",
    )(src, buf)


def _place_heads_body(*refs):
    src_ref, o_ref = refs[0], refs[-1]
    for h in range(XA_HEADS):
        o_ref[0, :, h, :] = src_ref[:, h * XA_HD:(h + 1) * XA_HD]


def _place_heads(src, l, depth, bp, prev):
    extra, extra_specs = _alias_args(prev)
    return pl.pallas_call(
        _place_heads_body,
        grid=(bp,),
        in_specs=[pl.BlockSpec((N_MEM, D_MODEL), lambda b: (b, 0))] + extra_specs,
        out_specs=pl.BlockSpec((None, 1, N_MEM, XA_HEADS, XA_HD), lambda b: (l, b, 0, 0, 0)),
        out_shape=jax.ShapeDtypeStruct((depth, bp, N_MEM, XA_HEADS, XA_HD), F32),
        input_output_aliases={1: 0} if prev is not None else {},
        compiler_params=_cparams("parallel"),
        name="place_heads",
    )(src, *extra)


def _assemble_body(a_ref, b_ref, c_ref, d_ref, buf_ref, o_ref):
    del buf_ref
    o_ref[...] = jnp.concatenate([a_ref[...], b_ref[...], c_ref[...], d_ref[...]], axis=1)


def _assemble_rows(parts, buf, row0):
    bs = parts[0].shape[0]
    part = pl.BlockSpec((bs, G_WIDTH), lambda i: (0, 0))
    return pl.pallas_call(
        _assemble_body,
        grid=(1,),
        in_specs=[part, part, part, part, ANY_SPEC],
        out_specs=pl.BlockSpec((bs, D_MODEL), lambda i: (row0 // bs, 0)),
        out_shape=jax.ShapeDtypeStruct(buf.shape, buf.dtype),
        input_output_aliases={4: 0},
        compiler_params=_cparams("arbitrary"),
        name="assemble_sample_rows",
    )(*parts, buf)


def kernel(x_prompt, x_sample, mem_prompt, state_conv_a, state_ret, state_shift, state_wkv, state_conv_d, cache_mem_k, cache_mem_v, g_mix, w_in, conv_a_w, mu_c, w0, w2, a0, a2, g2, k_k, k_a, r_k, lnx_g, lnx_b, conv_d_w, conv_d_b, ln_d_g, ln_d_b, w_out, g_xa, g_mem, wq_x, wk_x, wv_x, wo_x, g_mlp, w_up, w_down, g_final):
    bp, seq, d = x_prompt.shape
    bs = x_sample.shape[0]
    depth = w_in.shape[0]
    np_rows = bp * seq
    G = G_WIDTH
    x = jnp.concatenate([x_prompt.reshape(np_rows, d), x_sample.reshape(bs, d)], axis=0)
    mem = mem_prompt.reshape(bp * N_MEM, d)
    cos_p, sin_p = _rope_tables(jnp.arange(seq))
    cos_s, sin_s = _rope_tables(PAST_LEN + jnp.arange(1))

    z64 = jnp.zeros((depth, 64, G), F32)
    w2a = jnp.concatenate([jnp.concatenate([w2, z64], axis=2), jnp.concatenate([z64, a2], axis=2)], axis=1).astype(BF16)
    g2_b = g2.astype(BF16)
    wkv_t = jnp.transpose(state_wkv, (0, 2, 3, 4, 1))
    conv_d_t = jnp.transpose(state_conv_d, (0, 2, 1, 3))
    vec3 = lambda a: a.reshape(depth, 1, -1)
    rw_arrs = (vec3(mu_c), w2a, g2_b, vec3(w0), vec3(a0), vec3(k_k), vec3(k_a), vec3(r_k), vec3(lnx_g), vec3(lnx_b))

    small = {k: [] for k in ("a_p", "ret_p", "sh_p", "sh_s", "wkv_p", "d_p")}
    na_s = nd_s = nret_s = nwkv_s = mk_all = mv_all = None
    for l in range(depth):
        proj = _matmul_ws(x, w_in, l, norm_g=_layer_vec(g_mix, l), tn=1280, tm=_pick(x.shape[0], (640, 512, 256, 128)))
        ymix, na_p = _conv_a_prompt(proj, conv_a_w, l, bp, seq)
        ymix, nd_p = _conv_d_prompt(proj, conv_d_w, conv_d_b, ln_d_g, ln_d_b, ymix, l, bp, seq)
        yb_p, nret_p = _retention_prompt(proj, cos_p, sin_p, bp, seq)
        ymix = _place_cols(yb_p.reshape(np_rows, G), ymix, 1)
        rw = [(rw_arrs[0], pl.BlockSpec((None, 1, SHIFT_W), lambda *_: (l, 0, 0))),
              (w2a, _layer_mat(w2a, l)), (g2_b, _layer_mat(g2_b, l))] + [
              (a, pl.BlockSpec((None, 1, G), lambda *_: (l, 0, 0))) for a in rw_arrs[3:]]
        yc_p, nsh_p, nwkv_p = _rwkv_prompt(proj, rw, bp, seq)
        ymix = _place_cols(yc_p.reshape(np_rows, G), ymix, 2)
        ya_s, yd_s, na_s, nd_s = _sample_ad(proj, state_conv_a, conv_d_t, conv_a_w, conv_d_w, conv_d_b, ln_d_g, ln_d_b,
                                            l, np_rows, na_s, nd_s)
        yb_s, nret_s = _retention_sample(proj, cos_s, sin_s, state_ret, l, np_rows, nret_s)
        yc_s, nwkv_s = _rwkv_sample(proj, state_shift, rw_arrs, wkv_t, l, np_rows, nwkv_s)
        ymix = _assemble_rows([ya_s, yb_s, yc_s, yd_s], ymix, np_rows)
        x = _matmul_ws(ymix, w_out, l, residual=x)

        mk = _matmul_ws(mem, wk_x, l, norm_g=_layer_vec(g_mem, l))
        mv = _matmul_ws(mem, wv_x, l, norm_g=_layer_vec(g_mem, l))
        mk_all = _place_heads(mk, l, depth, bp, mk_all)
        mv_all = _place_heads(mv, l, depth, bp, mv_all)
        q = _matmul_ws(x, wq_x, l, norm_g=_layer_vec(g_xa, l))
        obuf = _xattn_prompt(q, mk, mv, bp, seq)
        q_s = q[np_rows:].reshape(bs, XA_HEADS, XA_HD)
        q_s = jnp.concatenate([q_s, q_s], axis=1)
        obuf = _xattn_sample(q_s, cache_mem_k, cache_mem_v, obuf, l, np_rows)
        x = _matmul_ws(obuf, wo_x, l, residual=x)

        up = _matmul_ws(x, w_up, l, norm_g=_layer_vec(g_mlp, l), act="relu2", out_dtype=BF16)
        for kb in range(w_down.shape[1] // d):
            x = _matmul_ws(up, w_down, l, residual=x, kblock=(kb, d))

        small["a_p"].append(na_p)
        small["ret_p"].append(nret_p)
        small["sh_p"].append(nsh_p.reshape(bp, SHIFT_W))
        small["sh_s"].append(proj[np_rows:, COL_C:COL_C + SHIFT_W])
        small["wkv_p"].append(nwkv_p)
        small["d_p"].append(nd_p)
    gf = (g_final.reshape(1, d), pl.BlockSpec((1, d), lambda *_: (0, 0)))
    y_p = _rmsnorm(x, gf, F32, 0, np_rows)
    y_s = _rmsnorm(x, gf, F32, np_rows, bs)
    st = lambda k: jnp.stack(small[k])
    return (y_p.reshape(bp, seq, d), y_s.reshape(bs, 1, d), st("a_p"), na_s, st("ret_p"), nret_s,
            st("sh_p"), st("sh_s"), st("wkv_p"), jnp.transpose(nwkv_s, (0, 4, 1, 2, 3)),
            st("d_p"), jnp.transpose(nd_s, (0, 2, 1, 3)), mk_all, mv_all)
```

```python
import functools
import math

import jax
import jax.numpy as jnp
from jax import lax
from jax.experimental import pallas as pl
from jax.experimental.pallas import tpu as pltpu

F32 = jnp.float32
BF16 = jnp.bfloat16
HI = lax.Precision.HIGHEST

D_MODEL = 2048
G_WIDTH = 512
P_IN = 6400
RET_HEADS = 4
RET_D = 128
RET_CHUNK = 128
ROPE_BASE = 10000.0
RWKV_HEAD = 64
RWKV_HEADS = 8
WKV_CHUNK = 64
RWKV_GN_EPS = 64e-5
A_CONV = 3
D_CONV = 31
D_HIST = 32
N_MEM = 256
XA_HEADS = 4
XA_HD = 512
SHIFT_W = 1792
PAST_LEN = 16384
EPS = 1e-6
RET_LOG_DECAY = tuple(math.log1p(-(2.0 ** (-5.0 - h))) for h in range(RET_HEADS))
VMEM_LIMIT = 56 * 1024 * 1024
COL_A = 0
COL_B = 3 * G_WIDTH
COL_C = 7 * G_WIDTH
COL_D = 7 * G_WIDTH + SHIFT_W

ANY_SPEC = pl.BlockSpec(memory_space=pl.ANY)


def _cparams(*sem):
    return pltpu.CompilerParams(dimension_semantics=sem, vmem_limit_bytes=VMEM_LIMIT)


def _pick(n, cands):
    for c in cands:
        if n % c == 0:
            return c
    raise ValueError(f"no tile for {n}")


def _sigmoid(x):
    return 1.0 / (1.0 + jnp.exp(-x))


def _dot(a, b, precision=None):
    return jnp.dot(a, b, preferred_element_type=F32, precision=precision)


def _dot_nt(a, b, precision=None):
    return lax.dot_general(a, b, (((1,), (1,)), ((), ())), preferred_element_type=F32, precision=precision)


def _dot_tn(a, b, precision=None):
    return lax.dot_general(a, b, (((0,), (0,)), ((), ())), preferred_element_type=F32, precision=precision)


def _layernorm(x, eps):
    mu = jnp.mean(x, axis=-1, keepdims=True)
    xc = x - mu
    var = jnp.mean(xc * xc, axis=-1, keepdims=True)
    return xc * lax.rsqrt(var + eps)


def _layer_vec(a, l):
    n = a.shape[-1]
    return a.reshape(a.shape[0], 1, n), pl.BlockSpec((None, 1, n), lambda *_: (l, 0, 0))


def _layer_mat(a, l):
    return pl.BlockSpec((None,) + a.shape[1:], lambda *_: (l, 0, 0))


def _rmsnorm_body(x_ref, g_ref, o_ref):
    x = x_ref[...]
    ms = jnp.mean(x * x, axis=-1, keepdims=True)
    o_ref[...] = ((x * lax.rsqrt(ms + EPS)) * g_ref[...]).astype(o_ref.dtype)


def _rmsnorm(x, g_spec, out_dtype, row0=0, nrows=None):
    m, d = x.shape
    nrows = nrows or m
    tm = _pick(math.gcd(nrows, row0) if row0 else nrows, (1040, 1024, 640, 512, 256, 128))
    r0 = row0 // tm
    g, gspec = g_spec
    return pl.pallas_call(
        _rmsnorm_body,
        grid=(nrows // tm,),
        in_specs=[pl.BlockSpec((tm, d), lambda i: (r0 + i, 0)), gspec],
        out_specs=pl.BlockSpec((tm, d), lambda i: (i, 0)),
        out_shape=jax.ShapeDtypeStruct((nrows, d), out_dtype),
        compiler_params=_cparams("parallel"),
        name="rmsnorm",
    )(x, g)


def _mm_ws_body(*refs, act, has_res, has_norm):
    a_ref, w_ref = refs[0], refs[1]
    g_ref = refs[2] if has_norm else None
    res_ref = refs[2 + has_norm] if has_res else None
    o_ref = refs[2 + has_norm + has_res]
    wb_ref = refs[-1]

    @pl.when(pl.program_id(1) == 0)
    def _():
        wb_ref[...] = w_ref[...].astype(BF16)

    a = a_ref[...]
    if has_norm:
        ms = jnp.mean(a * a, axis=-1, keepdims=True)
        a = (a * g_ref[...]).astype(BF16)
    r = _dot(a, wb_ref[...])
    if has_norm:
        r = r * lax.rsqrt(ms + EPS)
    if act == "relu2":
        r = jnp.square(jnp.maximum(r, 0.0))
    if has_res:
        r = res_ref[...] + r
    o_ref[...] = r.astype(o_ref.dtype)


def _matmul_ws(a, w, l, *, norm_g=None, residual=None, act=None, out_dtype=F32, tn=1024, tm=None, kblock=None):
    m = a.shape[0]
    kb, kdim = kblock if kblock is not None else (0, a.shape[1])
    n = w.shape[2]
    tm = tm or _pick(m, (1040, 1024, 640, 512, 256, 128))
    has_res = residual is not None
    has_norm = norm_g is not None
    in_specs = [pl.BlockSpec((tm, kdim), lambda j, i: (i, kb)), pl.BlockSpec((None, kdim, tn), lambda j, i: (l, kb, j))]
    args = [a, w]
    if has_norm:
        args.append(norm_g[0])
        in_specs.append(norm_g[1])
    if has_res:
        in_specs.append(pl.BlockSpec((tm, tn), lambda j, i: (i, j)))
        args.append(residual)
    return pl.pallas_call(
        functools.partial(_mm_ws_body, act=act, has_res=has_res, has_norm=has_norm),
        grid=(n // tn, m // tm),
        in_specs=in_specs,
        out_specs=pl.BlockSpec((tm, tn), lambda j, i: (i, j)),
        out_shape=jax.ShapeDtypeStruct((m, n), out_dtype),
        scratch_shapes=[pltpu.VMEM((kdim, tn), BF16)],
        compiler_params=_cparams("parallel", "arbitrary"),
        name="matmul_ws",
    )(*args)


def _conv_a_body(b_ref, c_ref, h_ref, w_ref, y_ref, st_ref, ext_ref, *, tl):
    t = pl.program_id(1)

    @pl.when(t == 0)
    def _():
        ext_ref[0:8, :] = jnp.zeros((8, G_WIDTH), F32)

    u = c_ref[...] * h_ref[...]
    ext_ref[8:, :] = u
    u1 = ext_ref[7:7 + tl, :]
    u2 = ext_ref[6:6 + tl, :]
    y = b_ref[...] * (w_ref[0:1, :] * u2 + w_ref[1:2, :] * u1 + w_ref[2:3, :] * u)
    y_ref[...] = y.astype(y_ref.dtype)
    st_ref[0] = ext_ref[tl + 6:tl + 8, :]
    ext_ref[0:8, :] = ext_ref[tl:tl + 8, :]


def _conv_a_prompt(proj, conv_w, l, bp, seq):
    tl = _pick(seq, (512, 256, 128))
    nt = seq // tl
    c0 = COL_A // G_WIDTH
    col = lambda j: pl.BlockSpec((tl, G_WIDTH), lambda b, t: (b * nt + t, c0 + j))
    return pl.pallas_call(
        functools.partial(_conv_a_body, tl=tl),
        grid=(bp, nt),
        in_specs=[col(0), col(1), col(2), _layer_mat(conv_w, l)],
        out_specs=[pl.BlockSpec((tl, G_WIDTH), lambda b, t: (b * nt + t, 0)),
                   pl.BlockSpec((1, A_CONV - 1, G_WIDTH), lambda b, t: (b, 0, 0))],
        out_shape=[jax.ShapeDtypeStruct((proj.shape[0], D_MODEL), BF16),
                   jax.ShapeDtypeStruct((bp, A_CONV - 1, G_WIDTH), F32)],
        scratch_shapes=[pltpu.VMEM((tl + 8, G_WIDTH), F32)],
        compiler_params=_cparams("parallel", "arbitrary"),
        name="conv_a_prompt",
    )(proj, proj, proj, conv_w)


CONV_D_ROWS = 64


def _conv_d_body(d1a_ref, d1b_ref, d2a_ref, d2b_ref, w_ref, bias_ref, g_ref, beta_ref, ymix_ref, y_ref, st_ref, ext_ref,
                 acc_ref, *, tl):
    del ymix_ref
    t = pl.program_id(1)
    next_ = D_HIST + tl

    @pl.when(t == 0)
    def _():
        ext_ref[0, 0:D_HIST, :] = jnp.zeros((D_HIST, G_WIDTH), F32)

    d1 = jnp.concatenate([d1a_ref[...], d1b_ref[...]], axis=1)
    d2 = jnp.concatenate([d2a_ref[...], d2b_ref[...]], axis=1)
    ext_ref[0, D_HIST:, :] = d1 * _sigmoid(d2)
    ext = ext_ref[0]
    for s in range(1, 8):
        ext_ref[s] = pltpu.roll(ext, next_ - s, 0)
    first = D_HIST - (D_CONV - 1)

    def blk(i, carry):
        base = pl.multiple_of(i * CONV_D_ROWS, CONV_D_ROWS)
        acc = jnp.broadcast_to(bias_ref[...], (CONV_D_ROWS, G_WIDTH))
        for k in range(D_CONV):
            s = (first + k) % 8
            a0 = first + k - s
            acc = acc + ext_ref[s, pl.ds(base + a0, CONV_D_ROWS), :] * w_ref[k:k + 1, :]
        acc_ref[pl.ds(base, CONV_D_ROWS), :] = acc
        return carry

    lax.fori_loop(0, tl // CONV_D_ROWS, blk, 0)
    c = _layernorm(acc_ref[...], EPS) * g_ref[...] + beta_ref[...]
    y_ref[...] = (c * _sigmoid(c)).astype(y_ref.dtype)
    st_ref[0] = ext_ref[0, tl + first:tl + D_HIST, :]
    ext_ref[0, 0:D_HIST, :] = ext_ref[0, tl:tl + D_HIST, :]


def _conv_d_prompt(proj, w, bias, g, beta, ymix, l, bp, seq):
    tl = _pick(seq, (512, 256, 128))
    nt = seq // tl
    half = G_WIDTH // 2
    c0 = COL_D // half
    col = lambda j: pl.BlockSpec((tl, half), lambda b, t: (b * nt + t, c0 + j))
    (bias, bias_s), (g, g_s), (beta, beta_s) = _layer_vec(bias, l), _layer_vec(g, l), _layer_vec(beta, l)
    return pl.pallas_call(
        functools.partial(_conv_d_body, tl=tl),
        grid=(bp, nt),
        in_specs=[col(0), col(1), col(2), col(3), _layer_mat(w, l), bias_s, g_s, beta_s, ANY_SPEC],
        out_specs=[pl.BlockSpec((tl, G_WIDTH), lambda b, t: (b * nt + t, 3)),
                   pl.BlockSpec((1, D_CONV - 1, G_WIDTH), lambda b, t: (b, 0, 0))],
        out_shape=[jax.ShapeDtypeStruct(ymix.shape, ymix.dtype),
                   jax.ShapeDtypeStruct((bp, D_CONV - 1, G_WIDTH), F32)],
        scratch_shapes=[pltpu.VMEM((8, tl + D_HIST, G_WIDTH), F32), pltpu.VMEM((tl, G_WIDTH), F32)],
        input_output_aliases={8: 0},
        compiler_params=_cparams("parallel", "arbitrary"),
        name="conv_d_prompt",
    )(proj, proj, proj, proj, w, bias, g, beta, ymix)


def _rope_tables(pos):
    inv = ROPE_BASE ** (-jnp.arange(0, RET_D, 2, dtype=F32) / RET_D)
    ang = pos.astype(F32)[:, None] * inv[None, :]
    cos = jnp.cos(ang)
    sin = jnp.sin(ang)
    return jnp.concatenate([cos, cos], axis=-1), jnp.concatenate([-sin, sin], axis=-1)


def _rope(x, cos2, sin2):
    return x * cos2 + pltpu.roll(x, RET_D // 2, 1) * sin2


def _ret_body(*refs, nb, nchunks):
    qkvg = refs[:4 * nb]
    cos_ref, sin_ref, y_ref, st_ref, s_ref = refs[4 * nb:]
    c = pl.program_id(0)
    C = RET_CHUNK

    @pl.when(c == 0)
    def _():
        s_ref[...] = jnp.zeros_like(s_ref)

    ii = lax.broadcasted_iota(jnp.int32, (C, C), 0).astype(F32)
    jj = lax.broadcasted_iota(jnp.int32, (C, C), 1).astype(F32)
    diff = ii - jj
    cos2 = cos_ref[...]
    sin2 = sin_ref[...]
    NH = RET_HEADS
    chains = [(b, h) for b in range(nb) for h in range(NH)]
    I = range(len(chains))
    sl = [slice(h * RET_D, (h + 1) * RET_D) for h in range(NH)]
    lg = RET_LOG_DECAY
    q = [_rope(qkvg[4 * b][:, sl[h]], cos2, sin2) for b, h in chains]
    k = [_rope(qkvg[4 * b + 1][:, sl[h]], cos2, sin2) * (RET_D ** -0.5) for b, h in chains]
    qb = [q[i].astype(BF16) for i in I]
    vb = [qkvg[4 * b + 2][:, sl[h]].astype(BF16) for b, h in chains]
    s_old = [s_ref[i] for i in I]
    scores = [_dot_nt(qb[i], k[i].astype(BF16)) for i in I]
    cross = [_dot(qb[i], s_old[i].astype(BF16)) for i in I]
    k_dec = [jnp.exp(lg[h] * (C - 1.0 - ii)) for h in range(NH)]
    kv = [_dot_tn((k[i] * k_dec[chains[i][1]]).astype(BF16), vb[i]) for i in I]
    dmat = [jnp.where(diff >= 0.0, jnp.exp(lg[h] * jnp.maximum(diff, 0.0)), 0.0) for h in range(NH)]
    inner = [_dot((scores[i] * dmat[chains[i][1]]).astype(BF16), vb[i]) for i in I]
    q_dec = [jnp.exp(lg[h] * (ii + 1.0)) for h in range(NH)]
    for i, (b, h) in enumerate(chains):
        s_ref[i] = math.exp(lg[h] * C) * s_old[i] + kv[i]
        o = _layernorm(inner[i] + cross[i] * q_dec[h], EPS)
        g = qkvg[4 * b + 3][:, sl[h]]
        y_ref[b, :, sl[h]] = (g * _sigmoid(g) * o).astype(y_ref.dtype)

    @pl.when(c == nchunks - 1)
    def _():
        for i, (b, h) in enumerate(chains):
            st_ref[b, h] = s_ref[i]


def _retention_prompt(proj, cos2, sin2, bp, seq):
    C = RET_CHUNK
    nc = seq // C
    c0 = COL_B // G_WIDTH
    cols = [pl.BlockSpec((C, G_WIDTH), lambda c, b=b, j=j: (b * nc + c, c0 + j)) for b in range(bp) for j in range(4)]
    tab = pl.BlockSpec((C, RET_D), lambda c: (c, 0))
    return pl.pallas_call(
        functools.partial(_ret_body, nb=bp, nchunks=nc),
        grid=(nc,),
        in_specs=cols + [tab, tab],
        out_specs=[pl.BlockSpec((bp, C, G_WIDTH), lambda c: (0, c, 0)),
                   pl.BlockSpec((bp, RET_HEADS, RET_D, RET_D), lambda c: (0, 0, 0, 0))],
        out_shape=[jax.ShapeDtypeStruct((bp, seq, G_WIDTH), BF16),
                   jax.ShapeDtypeStruct((bp, RET_HEADS, RET_D, RET_D), F32)],
        scratch_shapes=[pltpu.VMEM((bp * RET_HEADS, RET_D, RET_D), F32)],
        compiler_params=_cparams("arbitrary"),
        name="retention_prompt",
    )(*([proj] * (4 * bp)), cos2, sin2)


def _softplus(z):
    return jnp.maximum(z, 0.0) + jnp.log(1.0 + jnp.exp(-jnp.abs(z)))


def _rwkv_pre(r, kc, vc, lora_x, w2_w, w2_a, g2, w0, a0, k_k, k_a):
    lin = lora_x[:, 0:128]
    lane = lax.broadcasted_iota(jnp.int32, lin.shape, 1)
    lin = jnp.where(lane < 64, jnp.tanh(lin), lin).astype(BF16)
    gate = _dot(_sigmoid(lora_x[:, 128:256]).astype(BF16), g2)
    w_pre = -_softplus(-(w0 + _dot(lin, w2_w))) - 0.5
    log_decay = -jnp.exp(w_pre)
    a = _sigmoid(a0 + _dot(lin, w2_a))
    kk = kc * k_k
    k4 = kc * (1.0 + (a - 1.0) * k_a)
    return r, log_decay, k4, vc, kk, a, gate


def _l2_normalize(kk):
    ss = jnp.sum(kk * kk, axis=-1, keepdims=True)
    return kk * jnp.minimum(lax.rsqrt(ss), 1e12)


def _bdot(a, b):
    return _dot(a.astype(BF16), b.astype(BF16))


def _seg_sum(x, lo):
    s_lo = jnp.sum(jnp.where(lo, x, 0.0), axis=-1, keepdims=True)
    s_hi = jnp.sum(jnp.where(lo, 0.0, x), axis=-1, keepdims=True)
    return jnp.where(lo, s_lo, s_hi)


def _wkv_chunk(r, lw, cum, k, v, kk, a, zbd, mask2, eye2):
    T = WKV_CHUNK
    N = RWKV_HEAD
    P = range(len(r))
    HH = range(2)
    lo = lax.broadcasted_iota(jnp.int32, (1, 2 * N), 1) < N
    lo2 = (lax.broadcasted_iota(jnp.int32, (1, 4 * N), 1) & (2 * N - 1)) < N
    g_inc = [jnp.exp(cum[p]) for p in P]
    g_tot = [g_inc[p][T - 1:T, :] for p in P]
    ah = [-(kk[p] * jnp.exp(cum[p] - lw[p])) for p in P]
    rg = [r[p] * g_inc[p] for p in P]
    g_inv = [jnp.exp(-cum[p]) for p in P]
    bk = [jnp.concatenate([kk[p] * a[p] * g_inv[p], k[p] * g_inv[p]], axis=0) for p in P]
    ar = [jnp.concatenate([ah[p], rg[p]], axis=0).astype(BF16) for p in P]
    bkm = [jnp.concatenate([jnp.where(lo, bk[p], 0.0), jnp.where(lo, 0.0, bk[p])], axis=0).astype(BF16) for p in P]
    x2 = [_dot_nt(ar[p], bkm[p]) for p in P]
    x = [[jnp.where(mask2, x2[p][:, 2 * T * hh:2 * T * (hh + 1)], 0.0) for hh in HH] for p in P]
    vb = [v[p].astype(BF16) for p in P]
    zero = jnp.zeros((T, 2 * N), BF16)
    vz = [jnp.concatenate([zero, vb[p]], axis=0) for p in P]
    xkv2 = [_dot(jnp.concatenate([x[p][0], x[p][1]], axis=0).astype(BF16), vz[p]) for p in P]
    xkv = [jnp.where(lo, xkv2[p][0:2 * T], xkv2[p][2 * T:4 * T]) for p in P]
    acur = [[x[p][hh][0:T, 0:T] for hh in HH] for p in P]
    inv = [[eye2[0:T, 0:T] for hh in HH] for p in P]
    for it in range(5):
        y = [[_bdot(jnp.concatenate([acur[p][hh], inv[p][hh]], axis=0), acur[p][hh]) for hh in HH] for p in P]
        acur = [[y[p][hh][0:T] for hh in HH] for p in P]
        inv = [[inv[p][hh] + y[p][hh][T:2 * T] for hh in HH] for p in P]
    y = [[_bdot(inv[p][hh], acur[p][hh]) for hh in HH] for p in P]
    inv2 = [jnp.concatenate([inv[p][0] + y[p][0], inv[p][1] + y[p][1]], axis=0).astype(BF16) for p in P]
    rhs = [jnp.concatenate([ah[p], xkv[p][0:T]], axis=1).astype(BF16) for p in P]
    wu2 = [_dot(inv2[p], rhs[p]) for p in P]
    wub = [jnp.where(lo2, wu2[p][0:T], wu2[p][T:2 * T]).astype(BF16) for p in P]
    arb2 = [jnp.concatenate([x[p][0][T:2 * T, 0:T], x[p][1][T:2 * T, 0:T]], axis=0).astype(BF16) for p in P]
    qo2 = [_dot(arb2[p], wub[p]) for p in P]
    qo = [jnp.where(lo2, qo2[p][0:T], qo2[p][T:2 * T]) for p in P]
    lhs_t = [(bk[p] * g_tot[p]).astype(BF16) for p in P]
    rhs_t = [jnp.concatenate([wub[p], jnp.concatenate([zero, vb[p]], axis=1)], axis=0) for p in P]
    mn = [_dot_tn(lhs_t[p], rhs_t[p]) for p in P]
    mz = [jnp.where(lo, mn[p][0:N, 0:2 * N], mn[p][N:2 * N, 0:2 * N]) + jnp.where(eye2[0:N, :] > 0.0, g_tot[p], 0.0)
          for p in P]
    nz = [jnp.where(lo, mn[p][0:N, 2 * N:4 * N], mn[p][N:2 * N, 2 * N:4 * N]) for p in P]
    qm = [jnp.concatenate([rg[p] + qo[p][:, 0:2 * N], mz[p]], axis=0) for p in P]
    oz = [_bdot(qm[p], zbd[p]) for p in P]
    o = [oz[p][0:T] + qo[p][:, 2 * N:4 * N] + xkv[p][T:2 * T] for p in P]
    z_new = [oz[p][T:T + N] + nz[p] for p in P]
    zbd_new = [jnp.concatenate([jnp.where(lo, z_new[p], 0.0), jnp.where(lo, 0.0, z_new[p])], axis=0) for p in P]
    return o, zbd_new


def _rwkv_post(o, r, k4, v, gate, rk, lnx_g, lnx_b):
    gn = _layernorm(o, RWKV_GN_EPS) * lnx_g + lnx_b
    bonus = jnp.sum(r * k4 * rk, axis=-1, keepdims=True) * v
    return (gn + bonus) * gate


def _rwkv_prompt_body(*refs, nb, nchunks):
    pc_refs = refs[:nb]
    mu_ref, w2a_ref, g2_ref, w0_ref, a0_ref, kk_ref, ka_ref, rk_ref, lg_ref, lb_ref = refs[nb:nb + 10]
    y_ref, shift_ref, st_ref, carry_ref, z_ref = refs[nb + 10:]
    c = pl.program_id(0)
    T = WKV_CHUNK
    NH = RWKV_HEADS

    @pl.when(c == 0)
    def _():
        carry_ref[...] = jnp.zeros_like(carry_ref)
        z_ref[...] = jnp.zeros_like(z_ref)

    pcs, prevs = [], []
    for b in range(nb):
        pc = pc_refs[b][...]
        row = lax.broadcasted_iota(jnp.int32, pc.shape, 0)
        prevs.append(jnp.where(row == 0, carry_ref[b, 0:1, :], pltpu.roll(pc, 1, 0)))
        carry_ref[b, 0:1, :] = pc[T - 1:T, :]
        shift_ref[b] = pc[T - 1:T, :]
        pcs.append(pc)
    pc = jnp.concatenate(pcs, axis=0)
    xs = pc + (jnp.concatenate(prevs, axis=0) - pc) * mu_ref[...]
    G = G_WIDTH
    r, lw, k4, v, kk, a, gate = _rwkv_pre(xs[:, 0:G], xs[:, G:2 * G], xs[:, 2 * G:3 * G], xs[:, 3 * G:3 * G + 256],
                                          w2a_ref[:, 0:G], w2a_ref[:, G:2 * G], g2_ref[...], w0_ref[...], a0_ref[...],
                                          kk_ref[...], ka_ref[...])
    N = RWKV_HEAD
    W = 2 * N
    NP = NH // 2
    ti = lax.broadcasted_iota(jnp.int32, (T, T), 0)
    si = lax.broadcasted_iota(jnp.int32, (T, T), 1)
    tril = (ti >= si).astype(F32)
    cum = jnp.concatenate([_dot(tril, lw[b * T:(b + 1) * T, :], HI) for b in range(nb)], axis=0)
    t2 = lax.broadcasted_iota(jnp.int32, (2 * T, 2 * T), 0)
    s2 = lax.broadcasted_iota(jnp.int32, (2 * T, 2 * T), 1)
    mask2 = jnp.where(t2 < T, t2 - 1, t2 - T) >= jnp.where(s2 < T, s2, s2 - T)
    eye2 = (lax.broadcasted_iota(jnp.int32, (N, W), 0) == (lax.broadcasted_iota(jnp.int32, (N, W), 1) & (N - 1))).astype(F32)
    lo = lax.broadcasted_iota(jnp.int32, (1, W), 1) < N

    def normalized(x):
        return x * jnp.minimum(lax.rsqrt(_seg_sum(x * x, lo)), 1e12)

    pairs = lambda x: [x[b * T:(b + 1) * T, p * W:(p + 1) * W] for b in range(nb) for p in range(NP)]
    r_p, k_p, v_p = pairs(r), pairs(k4), pairs(v)
    o, z_new = _wkv_chunk(r_p, pairs(lw), pairs(cum), k_p, v_p, [normalized(x) for x in pairs(kk)], pairs(a),
                          [z_ref[i] for i in range(nb * NP)], mask2, eye2)
    for i in range(nb * NP):
        z_ref[i] = z_new[i]
    gate_p = pairs(gate)
    vec_pairs = lambda ref: [ref[:, p * W:(p + 1) * W] for p in range(NP)]
    rk_p, lg_p, lb_p = vec_pairs(rk_ref), vec_pairs(lg_ref), vec_pairs(lb_ref)
    for b in range(nb):
        ys = []
        for p in range(NP):
            i = b * NP + p
            oc = o[i] - _seg_sum(o[i], lo) * (1.0 / N)
            var = _seg_sum(oc * oc, lo) * (1.0 / N)
            gn = oc * lax.rsqrt(var + RWKV_GN_EPS) * lg_p[p] + lb_p[p]
            bonus = _seg_sum(r_p[i] * k_p[i] * rk_p[p], lo) * v_p[i]
            ys.append((gn + bonus) * gate_p[i])
        y_ref[b] = jnp.concatenate(ys, axis=1).astype(y_ref.dtype)

    @pl.when(c == nchunks - 1)
    def _():
        for b in range(nb):
            for p in range(NP):
                zbd = z_ref[b * NP + p]
                zt = (zbd[0:N] + zbd[N:W]).T
                st_ref[b, 2 * p] = zt[0:N]
                st_ref[b, 2 * p + 1] = zt[N:W]


def _rwkv_prompt(proj, rw, bp, seq):
    T = WKV_CHUNK
    nc = seq // T
    G = G_WIDTH
    arrs, specs = zip(*rw)
    pc_specs = [pl.BlockSpec((T, SHIFT_W), lambda c, b=b: (b * nc + c, COL_C // SHIFT_W)) for b in range(bp)]
    return pl.pallas_call(
        functools.partial(_rwkv_prompt_body, nb=bp, nchunks=nc),
        grid=(nc,),
        in_specs=pc_specs + list(specs),
        out_specs=[pl.BlockSpec((bp, T, G), lambda c: (0, c, 0)),
                   pl.BlockSpec((bp, 1, SHIFT_W), lambda c: (0, 0, 0)),
                   pl.BlockSpec((bp, RWKV_HEADS, RWKV_HEAD, RWKV_HEAD), lambda c: (0, 0, 0, 0))],
        out_shape=[jax.ShapeDtypeStruct((bp, seq, G), BF16),
                   jax.ShapeDtypeStruct((bp, 1, SHIFT_W), F32),
                   jax.ShapeDtypeStruct((bp, RWKV_HEADS, RWKV_HEAD, RWKV_HEAD), F32)],
        scratch_shapes=[pltpu.VMEM((bp, 8, SHIFT_W), F32),
                        pltpu.VMEM((bp * RWKV_HEADS // 2, 2 * RWKV_HEAD, 2 * RWKV_HEAD), F32)],
        compiler_params=_cparams("arbitrary"),
        name="rwkv_prompt",
    )(*([proj] * bp), *arrs)


def _xattn_prompt_body(q_ref, k_ref, v_ref, o_ref, kb_ref, vb_ref):
    @pl.when(pl.program_id(1) == 0)
    def _():
        kb_ref[...] = k_ref[...].astype(BF16)
        vb_ref[...] = v_ref[...].astype(BF16)

    H = range(XA_HEADS)
    sl = [slice(h * XA_HD, (h + 1) * XA_HD) for h in H]
    s = [_dot_nt(q_ref[:, sl[h]].astype(BF16), kb_ref[:, sl[h]]) * (XA_HD ** -0.5) for h in H]
    e = [jnp.exp(s[h] - jnp.max(s[h], axis=-1, keepdims=True)) for h in H]
    p = [(e[h] / jnp.sum(e[h], axis=-1, keepdims=True)).astype(BF16) for h in H]
    for h in H:
        o_ref[:, sl[h]] = _dot(p[h], vb_ref[:, sl[h]]).astype(o_ref.dtype)


def _xattn_prompt(q, mk, mv, bp, seq):
    tq = _pick(seq, (1024, 512, 256, 128))
    nt = seq // tq
    kv = pl.BlockSpec((N_MEM, D_MODEL), lambda b, t: (b, 0))
    return pl.pallas_call(
        _xattn_prompt_body,
        grid=(bp, nt),
        in_specs=[pl.BlockSpec((tq, D_MODEL), lambda b, t: (b * nt + t, 0)), kv, kv],
        out_specs=pl.BlockSpec((tq, D_MODEL), lambda b, t: (b * nt + t, 0)),
        out_shape=jax.ShapeDtypeStruct((q.shape[0], D_MODEL), BF16),
        scratch_shapes=[pltpu.VMEM((N_MEM, D_MODEL), BF16), pltpu.VMEM((N_MEM, D_MODEL), BF16)],
        compiler_params=_cparams("parallel", "arbitrary"),
        name="xattn_prompt",
    )(q, mk, mv)


XA_SAMPLES_PER_STEP = 4


def _xattn_sample_body(q_ref, k_ref, v_ref, obuf_ref, o_ref, acc_ref, *, nsteps):
    del obuf_ref
    i = pl.program_id(0)
    H = XA_HEADS
    groups = N_MEM * H // (2 * H)
    for j in range(XA_SAMPLES_PER_STEP):
        b = i * XA_SAMPLES_PER_STEP + j
        k3 = k_ref[j].reshape(N_MEM * H, XA_HD).reshape(groups, 2 * H, XA_HD)
        s = jnp.sum(k3 * q_ref[b], axis=-1, keepdims=True) * (XA_HD ** -0.5)
        mx = jnp.max(s, axis=0, keepdims=True)
        mx = jnp.maximum(mx, pltpu.roll(mx, H, 1))
        e = jnp.exp(s - mx)
        den = jnp.sum(e, axis=0, keepdims=True)
        den = den + pltpu.roll(den, H, 1)
        v3 = v_ref[j].reshape(N_MEM * H, XA_HD).reshape(groups, 2 * H, XA_HD)
        o8 = jnp.sum((e / den) * v3, axis=0)
        o = o8[0:H] + o8[H:2 * H]
        acc_ref[pl.ds(b, 1), :] = jnp.concatenate([o[h:h + 1, :] for h in range(H)], axis=1)

    @pl.when(i == nsteps - 1)
    def _():
        o_ref[...] = acc_ref[...].astype(o_ref.dtype)


def _xattn_sample(q, ck, cv, obuf, l, row0):
    bs = q.shape[0]
    nb = XA_SAMPLES_PER_STEP
    kv = pl.BlockSpec((None, nb, N_MEM, XA_HEADS, XA_HD), lambda i: (l, i, 0, 0, 0))
    return pl.pallas_call(
        functools.partial(_xattn_sample_body, nsteps=bs // nb),
        grid=(bs // nb,),
        in_specs=[pl.BlockSpec((bs, 2 * XA_HEADS, XA_HD), lambda i: (0, 0, 0)), kv, kv, ANY_SPEC],
        out_specs=pl.BlockSpec((bs, D_MODEL), lambda i: (row0 // bs, 0)),
        out_shape=jax.ShapeDtypeStruct(obuf.shape, obuf.dtype),
        scratch_shapes=[pltpu.VMEM((bs, D_MODEL), F32)],
        input_output_aliases={3: 0},
        compiler_params=_cparams("arbitrary"),
        name="xattn_sample",
    )(q, ck, cv, obuf)


def _alias_args(prev):
    return ([], []) if prev is None else ([prev], [ANY_SPEC])


def _sample_ad_body(*refs, has_prev):
    (ab_ref, ac_ref, ah_ref, sta_ref, wa_ref, d1a_ref, d1b_ref, d2a_ref, d2b_ref, std_ref, wd_ref, bias_ref, g_ref,
     beta_ref) = refs[:14]
    ya_ref, yd_ref, na_ref, nd_ref = refs[14 + 2 * has_prev:]
    u = ac_ref[...] * ah_ref[...]
    s0 = sta_ref[:, 0, :]
    s1 = sta_ref[:, 1, :]
    ya = ab_ref[...] * (wa_ref[0:1, :] * s0 + wa_ref[1:2, :] * s1 + wa_ref[2:3, :] * u)
    ya_ref[...] = ya.astype(ya_ref.dtype)
    na_ref[:, 0, :] = s1
    na_ref[:, 1, :] = u
    d1 = jnp.concatenate([d1a_ref[...], d1b_ref[...]], axis=1)
    d2 = jnp.concatenate([d2a_ref[...], d2b_ref[...]], axis=1)
    ud = d1 * _sigmoid(d2)
    acc = jnp.broadcast_to(bias_ref[...], ud.shape)
    nh = D_CONV - 1
    for k in range(nh):
        s_k = std_ref[k]
        acc = acc + s_k * wd_ref[k:k + 1, :]
        if k > 0:
            nd_ref[k - 1] = s_k
    acc = acc + ud * wd_ref[nh:nh + 1, :]
    nd_ref[nh - 1] = ud
    c = _layernorm(acc, EPS) * g_ref[...] + beta_ref[...]
    yd_ref[...] = (c * _sigmoid(c)).astype(yd_ref.dtype)


def _sample_ad(proj, st_a, st_d, wa, wd, bias, g, beta, l, row0, prev_a, prev_d):
    bs = st_a.shape[1]
    G = G_WIDTH
    tb = 32
    r0 = row0 // tb
    half = G // 2
    cola = lambda j: pl.BlockSpec((tb, G), lambda i: (r0 + i, COL_A // G + j))
    cold = lambda j: pl.BlockSpec((tb, half), lambda i: (r0 + i, COL_D // half + j))
    sta = pl.BlockSpec((None, tb, A_CONV - 1, G), lambda i: (l, i, 0, 0))
    std = pl.BlockSpec((None, D_CONV - 1, tb, G), lambda i: (l, 0, i, 0))
    yrow = pl.BlockSpec((tb, G), lambda i: (i, 0))
    (bias, bias_s), (g, g_s), (beta, beta_s) = _layer_vec(bias, l), _layer_vec(g, l), _layer_vec(beta, l)
    nh = D_CONV - 1
    has_prev = prev_a is not None
    extra = [prev_a, prev_d] if has_prev else []
    return pl.pallas_call(
        functools.partial(_sample_ad_body, has_prev=has_prev),
        grid=(bs // tb,),
        in_specs=[cola(0), cola(1), cola(2), sta, _layer_mat(wa, l), cold(0), cold(1), cold(2), cold(3), std,
                  _layer_mat(wd, l), bias_s, g_s, beta_s] + [ANY_SPEC] * len(extra),
        out_specs=[yrow, yrow, sta, std],
        out_shape=[jax.ShapeDtypeStruct((bs, G), BF16), jax.ShapeDtypeStruct((bs, G), BF16),
                   jax.ShapeDtypeStruct(st_a.shape, F32), jax.ShapeDtypeStruct(st_d.shape, F32)],
        input_output_aliases={14: 2, 15: 3} if has_prev else {},
        compiler_params=_cparams("parallel"),
        name="sample_conv_ad",
    )(proj, proj, proj, st_a, wa, proj, proj, proj, proj, st_d, wd, bias, g, beta, *extra)


RET_D_BLOCK = 16


def _ret_sample_body(*refs, has_prev):
    q_ref, k_ref, v_ref, g_ref, cos_ref, sin_ref, s_ref = refs[:7]
    y_ref, ns_ref, qt_ref, kt_ref, acc_ref = refs[7 + has_prev:]
    h = pl.program_id(0)
    j = pl.program_id(1)
    nd = RET_D // RET_D_BLOCK
    gammas = [math.exp(lg) for lg in RET_LOG_DECAY]
    gamma = jnp.where(h == 0, gammas[0], jnp.where(h == 1, gammas[1], jnp.where(h == 2, gammas[2], gammas[3])))
    gamma = gamma.astype(F32)

    @pl.when(j == 0)
    def _():
        q = _rope(q_ref[...], cos_ref[...], sin_ref[...])
        k = _rope(k_ref[...], cos_ref[...], sin_ref[...]) * (RET_D ** -0.5)
        qt_ref[...] = q.T
        kt_ref[...] = k.T
        acc_ref[...] = jnp.zeros_like(acc_ref)

    vt = v_ref[...].T
    acc = acc_ref[...]
    for d in range(RET_D_BLOCK):
        s_d = s_ref[:, d, :].T
        row = j * RET_D_BLOCK + d
        q_d = qt_ref[pl.ds(row, 1), :]
        k_d = kt_ref[pl.ds(row, 1), :]
        acc = acc + q_d * s_d
        ns_ref[:, d, :] = (gamma * s_d + k_d * vt).T
    acc_ref[...] = acc

    @pl.when(j == nd - 1)
    def _():
        qk = jnp.sum(qt_ref[...] * kt_ref[...], axis=0, keepdims=True)
        o = qk * vt + acc * gamma
        mu = jnp.mean(o, axis=0, keepdims=True)
        oc = o - mu
        var = jnp.mean(oc * oc, axis=0, keepdims=True)
        ln = (oc * lax.rsqrt(var + EPS)).T
        g = g_ref[...]
        y_ref[...] = (g * _sigmoid(g) * ln).astype(y_ref.dtype)


def _retention_sample(proj, cos2, sin2, state, l, row0, prev):
    bs = state.shape[1]
    r0 = row0 // bs
    nd = RET_D // RET_D_BLOCK
    col = lambda c: pl.BlockSpec((bs, RET_D), lambda h, j: (r0, COL_B // RET_D + c * RET_HEADS + h))
    tab = pl.BlockSpec((1, RET_D), lambda h, j: (0, 0))
    sblk = pl.BlockSpec((None, bs, None, RET_D_BLOCK, RET_D), lambda h, j: (l, 0, h, j, 0))
    extra, extra_specs = _alias_args(prev)
    return pl.pallas_call(
        functools.partial(_ret_sample_body, has_prev=prev is not None),
        grid=(RET_HEADS, nd),
        in_specs=[col(0), col(1), col(2), col(3), tab, tab, sblk] + extra_specs,
        out_specs=[pl.BlockSpec((bs, RET_D), lambda h, j: (0, h)), sblk],
        out_shape=[jax.ShapeDtypeStruct((bs, G_WIDTH), BF16), jax.ShapeDtypeStruct(state.shape, F32)],
        scratch_shapes=[pltpu.VMEM((RET_D, bs), F32), pltpu.VMEM((RET_D, bs), F32), pltpu.VMEM((RET_D, bs), F32)],
        input_output_aliases={7: 1} if prev is not None else {},
        compiler_params=_cparams("parallel", "arbitrary"),
        name="retention_sample",
    )(proj, proj, proj, proj, cos2, sin2, state, *extra)


def _rwkv_sample_body(*refs, has_prev):
    (xr_ref, xk_ref, xv_ref, xl_ref, pr_ref, pk_ref, pv_ref, pl_ref, mr_ref, mk_ref, mv_ref, ml_ref,
     w2w_ref, w2a_ref, g2_ref, w0_ref, a0_ref, kk_ref, ka_ref, rk_ref, lg_ref, lb_ref, s_ref) = refs[:23]
    y_ref, ns_ref, vec_ref, ot_ref = refs[23 + has_prev:]
    N = RWKV_HEAD

    def shifted(x_ref, prev_ref, mu_ref):
        x = x_ref[...]
        return x + (prev_ref[...] - x) * mu_ref[...]

    r, lw, k4, v, kk, a, gate = _rwkv_pre(shifted(xr_ref, pr_ref, mr_ref), shifted(xk_ref, pk_ref, mk_ref),
                                          shifted(xv_ref, pv_ref, mv_ref), shifted(xl_ref, pl_ref, ml_ref),
                                          w2w_ref[...], w2a_ref[...], g2_ref[...], w0_ref[...], a0_ref[...],
                                          kk_ref[...], ka_ref[...])
    kkn = jnp.concatenate([_l2_normalize(kk[:, 0:N]), _l2_normalize(kk[:, N:2 * N])], axis=1)
    w = jnp.exp(lw)
    vec_ref[0] = kkn.T
    vec_ref[1] = w.T
    vec_ref[2] = (kkn * a).T
    vec_ref[3] = k4.T
    vec_ref[4] = r.T
    vec_ref[5] = v.T
    for hh in range(2):
        ch = slice(hh * N, (hh + 1) * N)
        kk_t = vec_ref[0, ch, :]
        w_t = vec_ref[1, ch, :]
        kka_t = vec_ref[2, ch, :]
        k_t = vec_ref[3, ch, :]
        r_t = vec_ref[4, ch, :]
        for i in range(N):
            s_i = s_ref[hh, i]
            sa = -jnp.sum(s_i * kk_t, axis=0, keepdims=True)
            v_i = vec_ref[5, hh * N + i:hh * N + i + 1, :]
            s_new = s_i * w_t + sa * kka_t + v_i * k_t
            ns_ref[hh, i] = s_new
            ot_ref[hh * N + i:hh * N + i + 1, :] = jnp.sum(s_new * r_t, axis=0, keepdims=True)
    o = ot_ref[...].T
    rk = rk_ref[...]
    lg = lg_ref[...]
    lb = lb_ref[...]
    ys = []
    for hh in range(2):
        ch = slice(hh * N, (hh + 1) * N)
        ys.append(_rwkv_post(o[:, ch], r[:, ch], k4[:, ch], v[:, ch], gate[:, ch], rk[:, ch], lg[:, ch], lb[:, ch]))
    y_ref[...] = jnp.concatenate(ys, axis=1).astype(y_ref.dtype)


def _rwkv_sample(proj, prev_shift, rw_arrs, state, l, row0, prev):
    mu, w2a, g2, w0, a0, k_k, k_a, r_k, lnx_g, lnx_b = rw_arrs
    bs = state.shape[-1]
    r0 = row0 // bs
    G = G_WIDTH
    N = RWKV_HEAD
    W = 2 * N
    npair = G // W
    def cols(shape_lead, lead_idx, base):
        mk = lambda width, cidx: pl.BlockSpec(shape_lead + (width,), lambda p: lead_idx + (cidx(p),))
        return [mk(W, lambda p, c=c: base // W + c * npair + p) for c in range(3)] + [
            mk(2 * W, lambda p: (base + 3 * G) // (2 * W))]
    pairv = pl.BlockSpec((None, 1, W), lambda p: (l, 0, p))
    pairm = pl.BlockSpec((None, 128, W), lambda p: (l, 0, p))
    sblk = pl.BlockSpec((None, 2, N, N, bs), lambda p: (l, p, 0, 0, 0))
    extra, extra_specs = _alias_args(prev)
    return pl.pallas_call(
        functools.partial(_rwkv_sample_body, has_prev=prev is not None),
        grid=(npair,),
        in_specs=cols((bs,), (r0,), COL_C) + cols((None, bs), (l, 0), 0) + cols((None, 1), (l, 0), 0) + [
            pairm, pl.BlockSpec((None, 128, W), lambda p: (l, 0, npair + p)), pairm,
            pairv, pairv, pairv, pairv, pairv, pairv, pairv, sblk] + extra_specs,
        out_specs=[pl.BlockSpec((bs, W), lambda p: (0, p)), sblk],
        out_shape=[jax.ShapeDtypeStruct((bs, G), BF16), jax.ShapeDtypeStruct(state.shape, F32)],
        scratch_shapes=[pltpu.VMEM((6, W, bs), F32), pltpu.VMEM((W, bs), F32)],
        input_output_aliases={23: 1} if prev is not None else {},
        compiler_params=_cparams("parallel"),
        name="rwkv_sample",
    )(proj, proj, proj, proj, prev_shift, prev_shift, prev_shift, prev_shift, mu, mu, mu, mu, w2a, w2a, g2,
      w0, a0, k_k, k_a, r_k, lnx_g, lnx_b, state, *extra)


def _place_cols_body(src_ref, buf_ref, o_ref):
    del buf_ref
    o_ref[...] = src_ref[...]


def _place_cols(src, buf, col):
    rows, w = src.shape
    tr = _pick(rows, (1024, 512, 256, 128))
    return pl.pallas_call(
        _place_cols_body,
        grid=(rows // tr,),
        in_specs=[pl.BlockSpec((tr, w), lambda i: (i, 0)), ANY_SPEC],
        out_specs=pl.BlockSpec((tr, w), lambda i: (i, col)),
        out_shape=jax.ShapeDtypeStruct(buf.shape, buf.dtype),
        input_output_aliases={1: 0},
        compiler_params=_cparams("parallel"),
        name="place_cols",
    )(src, buf)


def _place_heads_body(*refs):
    src_ref, o_ref = refs[0], refs[-1]
    for h in range(XA_HEADS):
        o_ref[0, :, h, :] = src_ref[:, h * XA_HD:(h + 1) * XA_HD]


def _place_heads(src, l, depth, bp, prev):
    extra, extra_specs = _alias_args(prev)
    return pl.pallas_call(
        _place_heads_body,
        grid=(bp,),
        in_specs=[pl.BlockSpec((N_MEM, D_MODEL), lambda b: (b, 0))] + extra_specs,
        out_specs=pl.BlockSpec((None, 1, N_MEM, XA_HEADS, XA_HD), lambda b: (l, b, 0, 0, 0)),
        out_shape=jax.ShapeDtypeStruct((depth, bp, N_MEM, XA_HEADS, XA_HD), F32),
        input_output_aliases={1: 0} if prev is not None else {},
        compiler_params=_cparams("parallel"),
        name="place_heads",
    )(src, *extra)


def _assemble_body(a_ref, b_ref, c_ref, d_ref, buf_ref, o_ref):
    del buf_ref
    o_ref[...] = jnp.concatenate([a_ref[...], b_ref[...], c_ref[...], d_ref[...]], axis=1)


def _assemble_rows(parts, buf, row0):
    bs = parts[0].shape[0]
    part = pl.BlockSpec((bs, G_WIDTH), lambda i: (0, 0))
    return pl.pallas_call(
        _assemble_body,
        grid=(1,),
        in_specs=[part, part, part, part, ANY_SPEC],
        out_specs=pl.BlockSpec((bs, D_MODEL), lambda i: (row0 // bs, 0)),
        out_shape=jax.ShapeDtypeStruct(buf.shape, buf.dtype),
        input_output_aliases={4: 0},
        compiler_params=_cparams("arbitrary"),
        name="assemble_sample_rows",
    )(*parts, buf)


def kernel(x_prompt, x_sample, mem_prompt, state_conv_a, state_ret, state_shift, state_wkv, state_conv_d, cache_mem_k, cache_mem_v, g_mix, w_in, conv_a_w, mu_c, w0, w2, a0, a2, g2, k_k, k_a, r_k, lnx_g, lnx_b, conv_d_w, conv_d_b, ln_d_g, ln_d_b, w_out, g_xa, g_mem, wq_x, wk_x, wv_x, wo_x, g_mlp, w_up, w_down, g_final):
    bp, seq, d = x_prompt.shape
    bs = x_sample.shape[0]
    depth = w_in.shape[0]
    np_rows = bp * seq
    G = G_WIDTH
    x = jnp.concatenate([x_prompt.reshape(np_rows, d), x_sample.reshape(bs, d)], axis=0)
    mem = mem_prompt.reshape(bp * N_MEM, d)
    cos_p, sin_p = _rope_tables(jnp.arange(seq))
    cos_s, sin_s = _rope_tables(PAST_LEN + jnp.arange(1))

    z64 = jnp.zeros((depth, 64, G), F32)
    w2a = jnp.concatenate([jnp.concatenate([w2, z64], axis=2), jnp.concatenate([z64, a2], axis=2)], axis=1).astype(BF16)
    g2_b = g2.astype(BF16)
    wkv_t = jnp.transpose(state_wkv, (0, 2, 3, 4, 1))
    conv_d_t = jnp.transpose(state_conv_d, (0, 2, 1, 3))
    vec3 = lambda a: a.reshape(depth, 1, -1)
    rw_arrs = (vec3(mu_c), w2a, g2_b, vec3(w0), vec3(a0), vec3(k_k), vec3(k_a), vec3(r_k), vec3(lnx_g), vec3(lnx_b))

    small = {k: [] for k in ("a_p", "ret_p", "sh_p", "sh_s", "wkv_p", "d_p")}
    na_s = nd_s = nret_s = nwkv_s = mk_all = mv_all = None
    for l in range(depth):
        proj = _matmul_ws(x, w_in, l, norm_g=_layer_vec(g_mix, l), tn=1280, tm=_pick(x.shape[0], (640, 512, 256, 128)))
        ymix, na_p = _conv_a_prompt(proj, conv_a_w, l, bp, seq)
        ymix, nd_p = _conv_d_prompt(proj, conv_d_w, conv_d_b, ln_d_g, ln_d_b, ymix, l, bp, seq)
        yb_p, nret_p = _retention_prompt(proj, cos_p, sin_p, bp, seq)
        ymix = _place_cols(yb_p.reshape(np_rows, G), ymix, 1)
        rw = [(rw_arrs[0], pl.BlockSpec((None, 1, SHIFT_W), lambda *_: (l, 0, 0))),
              (w2a, _layer_mat(w2a, l)), (g2_b, _layer_mat(g2_b, l))] + [
              (a, pl.BlockSpec((None, 1, G), lambda *_: (l, 0, 0))) for a in rw_arrs[3:]]
        yc_p, nsh_p, nwkv_p = _rwkv_prompt(proj, rw, bp, seq)
        ymix = _place_cols(yc_p.reshape(np_rows, G), ymix, 2)
        ya_s, yd_s, na_s, nd_s = _sample_ad(proj, state_conv_a, conv_d_t, conv_a_w, conv_d_w, conv_d_b, ln_d_g, ln_d_b,
                                            l, np_rows, na_s, nd_s)
        yb_s, nret_s = _retention_sample(proj, cos_s, sin_s, state_ret, l, np_rows, nret_s)
        yc_s, nwkv_s = _rwkv_sample(proj, state_shift, rw_arrs, wkv_t, l, np_rows, nwkv_s)
        ymix = _assemble_rows([ya_s, yb_s, yc_s, yd_s], ymix, np_rows)
        x = _matmul_ws(ymix, w_out, l, residual=x)

        mk = _matmul_ws(mem, wk_x, l, norm_g=_layer_vec(g_mem, l))
        mv = _matmul_ws(mem, wv_x, l, norm_g=_layer_vec(g_mem, l))
        mk_all = _place_heads(mk, l, depth, bp, mk_all)
        mv_all = _place_heads(mv, l, depth, bp, mv_all)
        q = _matmul_ws(x, wq_x, l, norm_g=_layer_vec(g_xa, l))
        obuf = _xattn_prompt(q, mk, mv, bp, seq)
        q_s = q[np_rows:].reshape(bs, XA_HEADS, XA_HD)
        q_s = jnp.concatenate([q_s, q_s], axis=1)
        obuf = _xattn_sample(q_s, cache_mem_k, cache_mem_v, obuf, l, np_rows)
        x = _matmul_ws(obuf, wo_x, l, residual=x)

        up = _matmul_ws(x, w_up, l, norm_g=_layer_vec(g_mlp, l), act="relu2", out_dtype=BF16)
        for kb in range(w_down.shape[1] // d):
            x = _matmul_ws(up, w_down, l, residual=x, kblock=(kb, d))

        small["a_p"].append(na_p)
        small["ret_p"].append(nret_p)
        small["sh_p"].append(nsh_p.reshape(bp, SHIFT_W))
        small["sh_s"].append(proj[np_rows:, COL_C:COL_C + SHIFT_W])
        small["wkv_p"].append(nwkv_p)
        small["d_p"].append(nd_p)
    gf = (g_final.reshape(1, d), pl.BlockSpec((1, d), lambda *_: (0, 0)))
    y_p = _rmsnorm(x, gf, F32, 0, np_rows)
    y_s = _rmsnorm(x, gf, F32, np_rows, bs)
    st = lambda k: jnp.stack(small[k])
    return (y_p.reshape(bp, seq, d), y_s.reshape(bs, 1, d), st("a_p"), na_s, st("ret_p"), nret_s,
            st("sh_p"), st("sh_s"), st("wkv_p"), jnp.transpose(nwkv_s, (0, 4, 1, 2, 3)),
            st("d_p"), jnp.transpose(nd_s, (0, 2, 1, 3)), mk_all, mv_all)
```

```python
import functools
import math

import jax
import jax.numpy as jnp
from jax import lax
from jax.experimental import pallas as pl
from jax.experimental.pallas import tpu as pltpu

F32 = jnp.float32
BF16 = jnp.bfloat16
HI = lax.Precision.HIGHEST

D_MODEL = 2048
G_WIDTH = 512
P_IN = 6400
RET_HEADS = 4
RET_D = 128
RET_CHUNK = 128
ROPE_BASE = 10000.0
RWKV_HEAD = 64
RWKV_HEADS = 8
WKV_CHUNK = 64
RWKV_GN_EPS = 64e-5
A_CONV = 3
D_CONV = 31
D_HIST = 32
N_MEM = 256
XA_HEADS = 4
XA_HD = 512
SHIFT_W = 1792
PAST_LEN = 16384
EPS = 1e-6
RET_LOG_DECAY = tuple(math.log1p(-(2.0 ** (-5.0 - h))) for h in range(RET_HEADS))
VMEM_LIMIT = 56 * 1024 * 1024
COL_A = 0
COL_B = 3 * G_WIDTH
COL_C = 7 * G_WIDTH
COL_D = 7 * G_WIDTH + SHIFT_W

ANY_SPEC = pl.BlockSpec(memory_space=pl.ANY)


def _cparams(*sem):
    return pltpu.CompilerParams(dimension_semantics=sem, vmem_limit_bytes=VMEM_LIMIT)


def _pick(n, cands):
    for c in cands:
        if n % c == 0:
            return c
    raise ValueError(f"no tile for {n}")


def _sigmoid(x):
    return 1.0 / (1.0 + jnp.exp(-x))


def _dot(a, b, precision=None):
    return jnp.dot(a, b, preferred_element_type=F32, precision=precision)


def _dot_nt(a, b, precision=None):
    return lax.dot_general(a, b, (((1,), (1,)), ((), ())), preferred_element_type=F32, precision=precision)


def _dot_tn(a, b, precision=None):
    return lax.dot_general(a, b, (((0,), (0,)), ((), ())), preferred_element_type=F32, precision=precision)


def _layernorm(x, eps):
    mu = jnp.mean(x, axis=-1, keepdims=True)
    xc = x - mu
    var = jnp.mean(xc * xc, axis=-1, keepdims=True)
    return xc * lax.rsqrt(var + eps)


def _layer_vec(a, l):
    n = a.shape[-1]
    return a.reshape(a.shape[0], 1, n), pl.BlockSpec((None, 1, n), lambda *_: (l, 0, 0))


def _layer_mat(a, l):
    return pl.BlockSpec((None,) + a.shape[1:], lambda *_: (l, 0, 0))


def _rmsnorm_body(x_ref, g_ref, o_ref):
    x = x_ref[...]
    ms = jnp.mean(x * x, axis=-1, keepdims=True)
    o_ref[...] = ((x * lax.rsqrt(ms + EPS)) * g_ref[...]).astype(o_ref.dtype)


def _rmsnorm(x, g_spec, out_dtype, row0=0, nrows=None):
    m, d = x.shape
    nrows = nrows or m
    tm = _pick(math.gcd(nrows, row0) if row0 else nrows, (1040, 1024, 640, 512, 256, 128))
    r0 = row0 // tm
    g, gspec = g_spec
    return pl.pallas_call(
        _rmsnorm_body,
        grid=(nrows // tm,),
        in_specs=[pl.BlockSpec((tm, d), lambda i: (r0 + i, 0)), gspec],
        out_specs=pl.BlockSpec((tm, d), lambda i: (i, 0)),
        out_shape=jax.ShapeDtypeStruct((nrows, d), out_dtype),
        compiler_params=_cparams("parallel"),
        name="rmsnorm",
    )(x, g)


def _mm_ws_body(*refs, act, has_res, has_norm):
    a_ref, w_ref = refs[0], refs[1]
    g_ref = refs[2] if has_norm else None
    res_ref = refs[2 + has_norm] if has_res else None
    o_ref = refs[2 + has_norm + has_res]
    wb_ref = refs[-1]

    @pl.when(pl.program_id(1) == 0)
    def _():
        wb_ref[...] = w_ref[...].astype(BF16)

    a = a_ref[...]
    if has_norm:
        ms = jnp.mean(a * a, axis=-1, keepdims=True)
        a = (a * g_ref[...]).astype(BF16)
    r = _dot(a, wb_ref[...])
    if has_norm:
        r = r * lax.rsqrt(ms + EPS)
    if act == "relu2":
        r = jnp.square(jnp.maximum(r, 0.0))
    if has_res:
        r = res_ref[...] + r
    o_ref[...] = r.astype(o_ref.dtype)


def _matmul_ws(a, w, l, *, norm_g=None, residual=None, act=None, out_dtype=F32, tn=1024, tm=None, kblock=None):
    m = a.shape[0]
    kb, kdim = kblock if kblock is not None else (0, a.shape[1])
    n = w.shape[2]
    tm = tm or _pick(m, (1040, 1024, 640, 512, 256, 128))
    has_res = residual is not None
    has_norm = norm_g is not None
    in_specs = [pl.BlockSpec((tm, kdim), lambda j, i: (i, kb)), pl.BlockSpec((None, kdim, tn), lambda j, i: (l, kb, j))]
    args = [a, w]
    if has_norm:
        args.append(norm_g[0])
        in_specs.append(norm_g[1])
    if has_res:
        in_specs.append(pl.BlockSpec((tm, tn), lambda j, i: (i, j)))
        args.append(residual)
    return pl.pallas_call(
        functools.partial(_mm_ws_body, act=act, has_res=has_res, has_norm=has_norm),
        grid=(n // tn, m // tm),
        in_specs=in_specs,
        out_specs=pl.BlockSpec((tm, tn), lambda j, i: (i, j)),
        out_shape=jax.ShapeDtypeStruct((m, n), out_dtype),
        scratch_shapes=[pltpu.VMEM((kdim, tn), BF16)],
        input_output_aliases={len(args) - 1: 0} if has_res and residual.dtype == out_dtype else {},
        compiler_params=_cparams("parallel", "arbitrary"),
        name="matmul_ws",
    )(*args)


def _conv_a_body(b_ref, c_ref, h_ref, w_ref, y_ref, st_ref, ext_ref, *, tl):
    t = pl.program_id(1)

    @pl.when(t == 0)
    def _():
        ext_ref[0:8, :] = jnp.zeros((8, G_WIDTH), F32)

    u = c_ref[...] * h_ref[...]
    ext_ref[8:, :] = u
    u1 = ext_ref[7:7 + tl, :]
    u2 = ext_ref[6:6 + tl, :]
    y = b_ref[...] * (w_ref[0:1, :] * u2 + w_ref[1:2, :] * u1 + w_ref[2:3, :] * u)
    y_ref[...] = y.astype(y_ref.dtype)
    st_ref[0] = ext_ref[tl + 6:tl + 8, :]
    ext_ref[0:8, :] = ext_ref[tl:tl + 8, :]


def _conv_a_prompt(proj, conv_w, l, bp, seq):
    tl = _pick(seq, (512, 256, 128))
    nt = seq // tl
    c0 = COL_A // G_WIDTH
    col = lambda j: pl.BlockSpec((tl, G_WIDTH), lambda b, t: (b * nt + t, c0 + j))
    return pl.pallas_call(
        functools.partial(_conv_a_body, tl=tl),
        grid=(bp, nt),
        in_specs=[col(0), col(1), col(2), _layer_mat(conv_w, l)],
        out_specs=[pl.BlockSpec((tl, G_WIDTH), lambda b, t: (b * nt + t, 0)),
                   pl.BlockSpec((1, A_CONV - 1, G_WIDTH), lambda b, t: (b, 0, 0))],
        out_shape=[jax.ShapeDtypeStruct((proj.shape[0], D_MODEL), BF16),
                   jax.ShapeDtypeStruct((bp, A_CONV - 1, G_WIDTH), F32)],
        scratch_shapes=[pltpu.VMEM((tl + 8, G_WIDTH), F32)],
        compiler_params=_cparams("parallel", "arbitrary"),
        name="conv_a_prompt",
    )(proj, proj, proj, conv_w)


CONV_D_ROWS = 64


def _conv_d_body(d1a_ref, d1b_ref, d2a_ref, d2b_ref, w_ref, bias_ref, g_ref, beta_ref, ymix_ref, y_ref, st_ref, ext_ref,
                 acc_ref, *, tl):
    del ymix_ref
    t = pl.program_id(1)
    next_ = D_HIST + tl

    @pl.when(t == 0)
    def _():
        ext_ref[0, 0:D_HIST, :] = jnp.zeros((D_HIST, G_WIDTH), F32)

    d1 = jnp.concatenate([d1a_ref[...], d1b_ref[...]], axis=1)
    d2 = jnp.concatenate([d2a_ref[...], d2b_ref[...]], axis=1)
    ext_ref[0, D_HIST:, :] = d1 * _sigmoid(d2)
    ext = ext_ref[0]
    for s in range(1, 8):
        ext_ref[s] = pltpu.roll(ext, next_ - s, 0)
    first = D_HIST - (D_CONV - 1)

    def blk(i, carry):
        base = pl.multiple_of(i * CONV_D_ROWS, CONV_D_ROWS)
        acc = jnp.broadcast_to(bias_ref[...], (CONV_D_ROWS, G_WIDTH))
        for k in range(D_CONV):
            s = (first + k) % 8
            a0 = first + k - s
            acc = acc + ext_ref[s, pl.ds(base + a0, CONV_D_ROWS), :] * w_ref[k:k + 1, :]
        acc_ref[pl.ds(base, CONV_D_ROWS), :] = acc
        return carry

    lax.fori_loop(0, tl // CONV_D_ROWS, blk, 0)
    c = _layernorm(acc_ref[...], EPS) * g_ref[...] + beta_ref[...]
    y_ref[...] = (c * _sigmoid(c)).astype(y_ref.dtype)
    st_ref[0] = ext_ref[0, tl + first:tl + D_HIST, :]
    ext_ref[0, 0:D_HIST, :] = ext_ref[0, tl:tl + D_HIST, :]


def _conv_d_prompt(proj, w, bias, g, beta, ymix, l, bp, seq):
    tl = _pick(seq, (512, 256, 128))
    nt = seq // tl
    half = G_WIDTH // 2
    c0 = COL_D // half
    col = lambda j: pl.BlockSpec((tl, half), lambda b, t: (b * nt + t, c0 + j))
    (bias, bias_s), (g, g_s), (beta, beta_s) = _layer_vec(bias, l), _layer_vec(g, l), _layer_vec(beta, l)
    return pl.pallas_call(
        functools.partial(_conv_d_body, tl=tl),
        grid=(bp, nt),
        in_specs=[col(0), col(1), col(2), col(3), _layer_mat(w, l), bias_s, g_s, beta_s, ANY_SPEC],
        out_specs=[pl.BlockSpec((tl, G_WIDTH), lambda b, t: (b * nt + t, 3)),
                   pl.BlockSpec((1, D_CONV - 1, G_WIDTH), lambda b, t: (b, 0, 0))],
        out_shape=[jax.ShapeDtypeStruct(ymix.shape, ymix.dtype),
                   jax.ShapeDtypeStruct((bp, D_CONV - 1, G_WIDTH), F32)],
        scratch_shapes=[pltpu.VMEM((8, tl + D_HIST, G_WIDTH), F32), pltpu.VMEM((tl, G_WIDTH), F32)],
        input_output_aliases={8: 0},
        compiler_params=_cparams("parallel", "arbitrary"),
        name="conv_d_prompt",
    )(proj, proj, proj, proj, w, bias, g, beta, ymix)


def _rope_tables(pos):
    inv = ROPE_BASE ** (-jnp.arange(0, RET_D, 2, dtype=F32) / RET_D)
    ang = pos.astype(F32)[:, None] * inv[None, :]
    cos = jnp.cos(ang)
    sin = jnp.sin(ang)
    return jnp.concatenate([cos, cos], axis=-1), jnp.concatenate([-sin, sin], axis=-1)


def _rope(x, cos2, sin2):
    return x * cos2 + pltpu.roll(x, RET_D // 2, 1) * sin2


def _ret_body(*refs, nb, nchunks):
    qkvg = refs[:4 * nb]
    cos_ref, sin_ref, y_ref, st_ref, s_ref = refs[4 * nb:]
    c = pl.program_id(0)
    C = RET_CHUNK

    @pl.when(c == 0)
    def _():
        s_ref[...] = jnp.zeros_like(s_ref)

    ii = lax.broadcasted_iota(jnp.int32, (C, C), 0).astype(F32)
    jj = lax.broadcasted_iota(jnp.int32, (C, C), 1).astype(F32)
    diff = ii - jj
    cos2 = cos_ref[...]
    sin2 = sin_ref[...]
    NH = RET_HEADS
    chains = [(b, h) for b in range(nb) for h in range(NH)]
    I = range(len(chains))
    sl = [slice(h * RET_D, (h + 1) * RET_D) for h in range(NH)]
    lg = RET_LOG_DECAY
    q = [_rope(qkvg[4 * b][:, sl[h]], cos2, sin2) for b, h in chains]
    k = [_rope(qkvg[4 * b + 1][:, sl[h]], cos2, sin2) * (RET_D ** -0.5) for b, h in chains]
    qb = [q[i].astype(BF16) for i in I]
    vb = [qkvg[4 * b + 2][:, sl[h]].astype(BF16) for b, h in chains]
    s_old = [s_ref[i] for i in I]
    scores = [_dot_nt(qb[i], k[i].astype(BF16)) for i in I]
    cross = [_dot(qb[i], s_old[i].astype(BF16)) for i in I]
    k_dec = [jnp.exp(lg[h] * (C - 1.0 - ii)) for h in range(NH)]
    kv = [_dot_tn((k[i] * k_dec[chains[i][1]]).astype(BF16), vb[i]) for i in I]
    dmat = [jnp.where(diff >= 0.0, jnp.exp(lg[h] * jnp.maximum(diff, 0.0)), 0.0) for h in range(NH)]
    inner = [_dot((scores[i] * dmat[chains[i][1]]).astype(BF16), vb[i]) for i in I]
    q_dec = [jnp.exp(lg[h] * (ii + 1.0)) for h in range(NH)]
    for i, (b, h) in enumerate(chains):
        s_ref[i] = math.exp(lg[h] * C) * s_old[i] + kv[i]
        o = _layernorm(inner[i] + cross[i] * q_dec[h], EPS)
        g = qkvg[4 * b + 3][:, sl[h]]
        y_ref[b, :, sl[h]] = (g * _sigmoid(g) * o).astype(y_ref.dtype)

    @pl.when(c == nchunks - 1)
    def _():
        for i, (b, h) in enumerate(chains):
            st_ref[b, h] = s_ref[i]


def _retention_prompt(proj, cos2, sin2, bp, seq):
    C = RET_CHUNK
    nc = seq // C
    c0 = COL_B // G_WIDTH
    cols = [pl.BlockSpec((C, G_WIDTH), lambda c, b=b, j=j: (b * nc + c, c0 + j)) for b in range(bp) for j in range(4)]
    tab = pl.BlockSpec((C, RET_D), lambda c: (c, 0))
    return pl.pallas_call(
        functools.partial(_ret_body, nb=bp, nchunks=nc),
        grid=(nc,),
        in_specs=cols + [tab, tab],
        out_specs=[pl.BlockSpec((bp, C, G_WIDTH), lambda c: (0, c, 0)),
                   pl.BlockSpec((bp, RET_HEADS, RET_D, RET_D), lambda c: (0, 0, 0, 0))],
        out_shape=[jax.ShapeDtypeStruct((bp, seq, G_WIDTH), BF16),
                   jax.ShapeDtypeStruct((bp, RET_HEADS, RET_D, RET_D), F32)],
        scratch_shapes=[pltpu.VMEM((bp * RET_HEADS, RET_D, RET_D), F32)],
        compiler_params=_cparams("arbitrary"),
        name="retention_prompt",
    )(*([proj] * (4 * bp)), cos2, sin2)


def _softplus(z):
    return jnp.maximum(z, 0.0) + jnp.log(1.0 + jnp.exp(-jnp.abs(z)))


def _rwkv_pre(r, kc, vc, lora_x, w2_w, w2_a, g2, w0, a0, k_k, k_a):
    lin = lora_x[:, 0:128]
    lane = lax.broadcasted_iota(jnp.int32, lin.shape, 1)
    lin = jnp.where(lane < 64, jnp.tanh(lin), lin).astype(BF16)
    gate = _dot(_sigmoid(lora_x[:, 128:256]).astype(BF16), g2)
    w_pre = -_softplus(-(w0 + _dot(lin, w2_w))) - 0.5
    log_decay = -jnp.exp(w_pre)
    a = _sigmoid(a0 + _dot(lin, w2_a))
    kk = kc * k_k
    k4 = kc * (1.0 + (a - 1.0) * k_a)
    return r, log_decay, k4, vc, kk, a, gate


def _l2_normalize(kk):
    ss = jnp.sum(kk * kk, axis=-1, keepdims=True)
    return kk * jnp.minimum(lax.rsqrt(ss), 1e12)


def _bdot(a, b):
    return _dot(a.astype(BF16), b.astype(BF16))


def _seg_sum(x, lo):
    s_lo = jnp.sum(jnp.where(lo, x, 0.0), axis=-1, keepdims=True)
    s_hi = jnp.sum(jnp.where(lo, 0.0, x), axis=-1, keepdims=True)
    return jnp.where(lo, s_lo, s_hi)


def _wkv_chunk(r, lw, cum, k, v, kk, a, zbd, mask2, eye2):
    T = WKV_CHUNK
    N = RWKV_HEAD
    P = range(len(r))
    HH = range(2)
    lo = lax.broadcasted_iota(jnp.int32, (1, 2 * N), 1) < N
    lo2 = (lax.broadcasted_iota(jnp.int32, (1, 4 * N), 1) & (2 * N - 1)) < N
    g_inc = [jnp.exp(cum[p]) for p in P]
    g_tot = [g_inc[p][T - 1:T, :] for p in P]
    ah = [-(kk[p] * jnp.exp(cum[p] - lw[p])) for p in P]
    rg = [r[p] * g_inc[p] for p in P]
    g_inv = [jnp.exp(-cum[p]) for p in P]
    bk = [jnp.concatenate([kk[p] * a[p] * g_inv[p], k[p] * g_inv[p]], axis=0) for p in P]
    ar = [jnp.concatenate([ah[p], rg[p]], axis=0).astype(BF16) for p in P]
    bkm = [jnp.concatenate([jnp.where(lo, bk[p], 0.0), jnp.where(lo, 0.0, bk[p])], axis=0).astype(BF16) for p in P]
    x2 = [_dot_nt(ar[p], bkm[p]) for p in P]
    x = [[jnp.where(mask2, x2[p][:, 2 * T * hh:2 * T * (hh + 1)], 0.0) for hh in HH] for p in P]
    vb = [v[p].astype(BF16) for p in P]
    zero = jnp.zeros((T, 2 * N), BF16)
    vz = [jnp.concatenate([zero, vb[p]], axis=0) for p in P]
    xkv2 = [_dot(jnp.concatenate([x[p][0], x[p][1]], axis=0).astype(BF16), vz[p]) for p in P]
    xkv = [jnp.where(lo, xkv2[p][0:2 * T], xkv2[p][2 * T:4 * T]) for p in P]
    acur = [[x[p][hh][0:T, 0:T] for hh in HH] for p in P]
    inv = [[eye2[0:T, 0:T] for hh in HH] for p in P]
    for it in range(5):
        y = [[_bdot(jnp.concatenate([acur[p][hh], inv[p][hh]], axis=0), acur[p][hh]) for hh in HH] for p in P]
        acur = [[y[p][hh][0:T] for hh in HH] for p in P]
        inv = [[inv[p][hh] + y[p][hh][T:2 * T] for hh in HH] for p in P]
    y = [[_bdot(inv[p][hh], acur[p][hh]) for hh in HH] for p in P]
    inv2 = [jnp.concatenate([inv[p][0] + y[p][0], inv[p][1] + y[p][1]], axis=0).astype(BF16) for p in P]
    rhs = [jnp.concatenate([ah[p], xkv[p][0:T]], axis=1).astype(BF16) for p in P]
    wu2 = [_dot(inv2[p], rhs[p]) for p in P]
    wub = [jnp.where(lo2, wu2[p][0:T], wu2[p][T:2 * T]).astype(BF16) for p in P]
    arb2 = [jnp.concatenate([x[p][0][T:2 * T, 0:T], x[p][1][T:2 * T, 0:T]], axis=0).astype(BF16) for p in P]
    qo2 = [_dot(arb2[p], wub[p]) for p in P]
    qo = [jnp.where(lo2, qo2[p][0:T], qo2[p][T:2 * T]) for p in P]
    lhs_t = [(bk[p] * g_tot[p]).astype(BF16) for p in P]
    rhs_t = [jnp.concatenate([wub[p], jnp.concatenate([zero, vb[p]], axis=1)], axis=0) for p in P]
    mn = [_dot_tn(lhs_t[p], rhs_t[p]) for p in P]
    mz = [jnp.where(lo, mn[p][0:N, 0:2 * N], mn[p][N:2 * N, 0:2 * N]) + jnp.where(eye2[0:N, :] > 0.0, g_tot[p], 0.0)
          for p in P]
    nz = [jnp.where(lo, mn[p][0:N, 2 * N:4 * N], mn[p][N:2 * N, 2 * N:4 * N]) for p in P]
    qm = [jnp.concatenate([rg[p] + qo[p][:, 0:2 * N], mz[p]], axis=0) for p in P]
    oz = [_bdot(qm[p], zbd[p]) for p in P]
    o = [oz[p][0:T] + qo[p][:, 2 * N:4 * N] + xkv[p][T:2 * T] for p in P]
    z_new = [oz[p][T:T + N] + nz[p] for p in P]
    zbd_new = [jnp.concatenate([jnp.where(lo, z_new[p], 0.0), jnp.where(lo, 0.0, z_new[p])], axis=0) for p in P]
    return o, zbd_new


def _rwkv_post(o, r, k4, v, gate, rk, lnx_g, lnx_b):
    gn = _layernorm(o, RWKV_GN_EPS) * lnx_g + lnx_b
    bonus = jnp.sum(r * k4 * rk, axis=-1, keepdims=True) * v
    return (gn + bonus) * gate


def _rwkv_prompt_body(*refs, nb, nchunks):
    pc_refs = refs[:nb]
    mu_ref, w2a_ref, g2_ref, w0_ref, a0_ref, kk_ref, ka_ref, rk_ref, lg_ref, lb_ref = refs[nb:nb + 10]
    y_ref, shift_ref, st_ref, carry_ref, z_ref = refs[nb + 10:]
    c = pl.program_id(0)
    T = WKV_CHUNK
    NH = RWKV_HEADS

    @pl.when(c == 0)
    def _():
        carry_ref[...] = jnp.zeros_like(carry_ref)
        z_ref[...] = jnp.zeros_like(z_ref)

    pcs, prevs = [], []
    for b in range(nb):
        pc = pc_refs[b][...]
        row = lax.broadcasted_iota(jnp.int32, pc.shape, 0)
        prevs.append(jnp.where(row == 0, carry_ref[b, 0:1, :], pltpu.roll(pc, 1, 0)))
        carry_ref[b, 0:1, :] = pc[T - 1:T, :]
        shift_ref[b] = pc[T - 1:T, :]
        pcs.append(pc)
    pc = jnp.concatenate(pcs, axis=0)
    xs = pc + (jnp.concatenate(prevs, axis=0) - pc) * mu_ref[...]
    G = G_WIDTH
    r, lw, k4, v, kk, a, gate = _rwkv_pre(xs[:, 0:G], xs[:, G:2 * G], xs[:, 2 * G:3 * G], xs[:, 3 * G:3 * G + 256],
                                          w2a_ref[:, 0:G], w2a_ref[:, G:2 * G], g2_ref[...], w0_ref[...], a0_ref[...],
                                          kk_ref[...], ka_ref[...])
    N = RWKV_HEAD
    W = 2 * N
    NP = NH // 2
    ti = lax.broadcasted_iota(jnp.int32, (T, T), 0)
    si = lax.broadcasted_iota(jnp.int32, (T, T), 1)
    tril = (ti >= si).astype(F32)
    cum = jnp.concatenate([_dot(tril, lw[b * T:(b + 1) * T, :], HI) for b in range(nb)], axis=0)
    t2 = lax.broadcasted_iota(jnp.int32, (2 * T, 2 * T), 0)
    s2 = lax.broadcasted_iota(jnp.int32, (2 * T, 2 * T), 1)
    mask2 = jnp.where(t2 < T, t2 - 1, t2 - T) >= jnp.where(s2 < T, s2, s2 - T)
    eye2 = (lax.broadcasted_iota(jnp.int32, (N, W), 0) == (lax.broadcasted_iota(jnp.int32, (N, W), 1) & (N - 1))).astype(F32)
    lo = lax.broadcasted_iota(jnp.int32, (1, W), 1) < N

    def normalized(x):
        return x * jnp.minimum(lax.rsqrt(_seg_sum(x * x, lo)), 1e12)

    pairs = lambda x: [x[b * T:(b + 1) * T, p * W:(p + 1) * W] for b in range(nb) for p in range(NP)]
    r_p, k_p, v_p = pairs(r), pairs(k4), pairs(v)
    o, z_new = _wkv_chunk(r_p, pairs(lw), pairs(cum), k_p, v_p, [normalized(x) for x in pairs(kk)], pairs(a),
                          [z_ref[i] for i in range(nb * NP)], mask2, eye2)
    for i in range(nb * NP):
        z_ref[i] = z_new[i]
    gate_p = pairs(gate)
    vec_pairs = lambda ref: [ref[:, p * W:(p + 1) * W] for p in range(NP)]
    rk_p, lg_p, lb_p = vec_pairs(rk_ref), vec_pairs(lg_ref), vec_pairs(lb_ref)
    for b in range(nb):
        ys = []
        for p in range(NP):
            i = b * NP + p
            oc = o[i] - _seg_sum(o[i], lo) * (1.0 / N)
            var = _seg_sum(oc * oc, lo) * (1.0 / N)
            gn = oc * lax.rsqrt(var + RWKV_GN_EPS) * lg_p[p] + lb_p[p]
            bonus = _seg_sum(r_p[i] * k_p[i] * rk_p[p], lo) * v_p[i]
            ys.append((gn + bonus) * gate_p[i])
        y_ref[b] = jnp.concatenate(ys, axis=1).astype(y_ref.dtype)

    @pl.when(c == nchunks - 1)
    def _():
        for b in range(nb):
            for p in range(NP):
                zbd = z_ref[b * NP + p]
                zt = (zbd[0:N] + zbd[N:W]).T
                st_ref[b, 2 * p] = zt[0:N]
                st_ref[b, 2 * p + 1] = zt[N:W]


def _rwkv_prompt(proj, rw, bp, seq):
    T = WKV_CHUNK
    nc = seq // T
    G = G_WIDTH
    arrs, specs = zip(*rw)
    pc_specs = [pl.BlockSpec((T, SHIFT_W), lambda c, b=b: (b * nc + c, COL_C // SHIFT_W)) for b in range(bp)]
    return pl.pallas_call(
        functools.partial(_rwkv_prompt_body, nb=bp, nchunks=nc),
        grid=(nc,),
        in_specs=pc_specs + list(specs),
        out_specs=[pl.BlockSpec((bp, T, G), lambda c: (0, c, 0)),
                   pl.BlockSpec((bp, 1, SHIFT_W), lambda c: (0, 0, 0)),
                   pl.BlockSpec((bp, RWKV_HEADS, RWKV_HEAD, RWKV_HEAD), lambda c: (0, 0, 0, 0))],
        out_shape=[jax.ShapeDtypeStruct((bp, seq, G), BF16),
                   jax.ShapeDtypeStruct((bp, 1, SHIFT_W), F32),
                   jax.ShapeDtypeStruct((bp, RWKV_HEADS, RWKV_HEAD, RWKV_HEAD), F32)],
        scratch_shapes=[pltpu.VMEM((bp, 8, SHIFT_W), F32),
                        pltpu.VMEM((bp * RWKV_HEADS // 2, 2 * RWKV_HEAD, 2 * RWKV_HEAD), F32)],
        compiler_params=_cparams("arbitrary"),
        name="rwkv_prompt",
    )(*([proj] * bp), *arrs)


def _xattn_prompt_body(q_ref, k_ref, v_ref, o_ref, kb_ref, vb_ref):
    @pl.when(pl.program_id(1) == 0)
    def _():
        kb_ref[...] = k_ref[...].astype(BF16)
        vb_ref[...] = v_ref[...].astype(BF16)

    H = range(XA_HEADS)
    sl = [slice(h * XA_HD, (h + 1) * XA_HD) for h in H]
    s = [_dot_nt(q_ref[:, sl[h]].astype(BF16), kb_ref[:, sl[h]]) * (XA_HD ** -0.5) for h in H]
    e = [jnp.exp(s[h] - jnp.max(s[h], axis=-1, keepdims=True)) for h in H]
    p = [(e[h] / jnp.sum(e[h], axis=-1, keepdims=True)).astype(BF16) for h in H]
    for h in H:
        o_ref[:, sl[h]] = _dot(p[h], vb_ref[:, sl[h]]).astype(o_ref.dtype)


def _xattn_prompt(q, mk, mv, bp, seq):
    tq = _pick(seq, (1024, 512, 256, 128))
    nt = seq // tq
    kv = pl.BlockSpec((N_MEM, D_MODEL), lambda b, t: (b, 0))
    return pl.pallas_call(
        _xattn_prompt_body,
        grid=(bp, nt),
        in_specs=[pl.BlockSpec((tq, D_MODEL), lambda b, t: (b * nt + t, 0)), kv, kv],
        out_specs=pl.BlockSpec((tq, D_MODEL), lambda b, t: (b * nt + t, 0)),
        out_shape=jax.ShapeDtypeStruct((q.shape[0], D_MODEL), BF16),
        scratch_shapes=[pltpu.VMEM((N_MEM, D_MODEL), BF16), pltpu.VMEM((N_MEM, D_MODEL), BF16)],
        compiler_params=_cparams("parallel", "arbitrary"),
        name="xattn_prompt",
    )(q, mk, mv)


XA_SAMPLES_PER_STEP = 4


def _xattn_sample_body(q_ref, k_ref, v_ref, obuf_ref, o_ref, acc_ref, *, nsteps):
    del obuf_ref
    i = pl.program_id(0)
    H = XA_HEADS
    groups = N_MEM * H // (2 * H)
    for j in range(XA_SAMPLES_PER_STEP):
        b = i * XA_SAMPLES_PER_STEP + j
        k3 = k_ref[j].reshape(N_MEM * H, XA_HD).reshape(groups, 2 * H, XA_HD)
        s = jnp.sum(k3 * q_ref[b], axis=-1, keepdims=True) * (XA_HD ** -0.5)
        mx = jnp.max(s, axis=0, keepdims=True)
        mx = jnp.maximum(mx, pltpu.roll(mx, H, 1))
        e = jnp.exp(s - mx)
        den = jnp.sum(e, axis=0, keepdims=True)
        den = den + pltpu.roll(den, H, 1)
        v3 = v_ref[j].reshape(N_MEM * H, XA_HD).reshape(groups, 2 * H, XA_HD)
        o8 = jnp.sum((e / den) * v3, axis=0)
        o = o8[0:H] + o8[H:2 * H]
        acc_ref[pl.ds(b, 1), :] = jnp.concatenate([o[h:h + 1, :] for h in range(H)], axis=1)

    @pl.when(i == nsteps - 1)
    def _():
        o_ref[...] = acc_ref[...].astype(o_ref.dtype)


def _xattn_sample(q, ck, cv, obuf, l, row0):
    bs = q.shape[0]
    nb = XA_SAMPLES_PER_STEP
    kv = pl.BlockSpec((None, nb, N_MEM, XA_HEADS, XA_HD), lambda i: (l, i, 0, 0, 0))
    return pl.pallas_call(
        functools.partial(_xattn_sample_body, nsteps=bs // nb),
        grid=(bs // nb,),
        in_specs=[pl.BlockSpec((bs, 2 * XA_HEADS, XA_HD), lambda i: (0, 0, 0)), kv, kv, ANY_SPEC],
        out_specs=pl.BlockSpec((bs, D_MODEL), lambda i: (row0 // bs, 0)),
        out_shape=jax.ShapeDtypeStruct(obuf.shape, obuf.dtype),
        scratch_shapes=[pltpu.VMEM((bs, D_MODEL), F32)],
        input_output_aliases={3: 0},
        compiler_params=_cparams("arbitrary"),
        name="xattn_sample",
    )(q, ck, cv, obuf)


def _alias_args(prev):
    return ([], []) if prev is None else ([prev], [ANY_SPEC])


def _sample_ad_body(*refs, has_prev):
    (ab_ref, ac_ref, ah_ref, sta_ref, wa_ref, d1a_ref, d1b_ref, d2a_ref, d2b_ref, std_ref, wd_ref, bias_ref, g_ref,
     beta_ref) = refs[:14]
    ya_ref, yd_ref, na_ref, nd_ref = refs[14 + 2 * has_prev:]
    u = ac_ref[...] * ah_ref[...]
    s0 = sta_ref[:, 0, :]
    s1 = sta_ref[:, 1, :]
    ya = ab_ref[...] * (wa_ref[0:1, :] * s0 + wa_ref[1:2, :] * s1 + wa_ref[2:3, :] * u)
    ya_ref[...] = ya.astype(ya_ref.dtype)
    na_ref[:, 0, :] = s1
    na_ref[:, 1, :] = u
    d1 = jnp.concatenate([d1a_ref[...], d1b_ref[...]], axis=1)
    d2 = jnp.concatenate([d2a_ref[...], d2b_ref[...]], axis=1)
    ud = d1 * _sigmoid(d2)
    acc = jnp.broadcast_to(bias_ref[...], ud.shape)
    nh = D_CONV - 1
    for k in range(nh):
        s_k = std_ref[k]
        acc = acc + s_k * wd_ref[k:k + 1, :]
        if k > 0:
            nd_ref[k - 1] = s_k
    acc = acc + ud * wd_ref[nh:nh + 1, :]
    nd_ref[nh - 1] = ud
    c = _layernorm(acc, EPS) * g_ref[...] + beta_ref[...]
    yd_ref[...] = (c * _sigmoid(c)).astype(yd_ref.dtype)


def _sample_ad(proj, st_a, st_d, wa, wd, bias, g, beta, l, row0, prev_a, prev_d):
    bs = st_a.shape[1]
    G = G_WIDTH
    tb = 32
    r0 = row0 // tb
    half = G // 2
    cola = lambda j: pl.BlockSpec((tb, G), lambda i: (r0 + i, COL_A // G + j))
    cold = lambda j: pl.BlockSpec((tb, half), lambda i: (r0 + i, COL_D // half + j))
    sta = pl.BlockSpec((None, tb, A_CONV - 1, G), lambda i: (l, i, 0, 0))
    std = pl.BlockSpec((None, D_CONV - 1, tb, G), lambda i: (l, 0, i, 0))
    yrow = pl.BlockSpec((tb, G), lambda i: (i, 0))
    (bias, bias_s), (g, g_s), (beta, beta_s) = _layer_vec(bias, l), _layer_vec(g, l), _layer_vec(beta, l)
    nh = D_CONV - 1
    has_prev = prev_a is not None
    extra = [prev_a, prev_d] if has_prev else []
    return pl.pallas_call(
        functools.partial(_sample_ad_body, has_prev=has_prev),
        grid=(bs // tb,),
        in_specs=[cola(0), cola(1), cola(2), sta, _layer_mat(wa, l), cold(0), cold(1), cold(2), cold(3), std,
                  _layer_mat(wd, l), bias_s, g_s, beta_s] + [ANY_SPEC] * len(extra),
        out_specs=[yrow, yrow, sta, std],
        out_shape=[jax.ShapeDtypeStruct((bs, G), BF16), jax.ShapeDtypeStruct((bs, G), BF16),
                   jax.ShapeDtypeStruct(st_a.shape, F32), jax.ShapeDtypeStruct(st_d.shape, F32)],
        input_output_aliases={14: 2, 15: 3} if has_prev else {},
        compiler_params=_cparams("parallel"),
        name="sample_conv_ad",
    )(proj, proj, proj, st_a, wa, proj, proj, proj, proj, st_d, wd, bias, g, beta, *extra)


RET_D_BLOCK = 16


def _ret_sample_body(*refs, has_prev):
    q_ref, k_ref, v_ref, g_ref, cos_ref, sin_ref, s_ref = refs[:7]
    y_ref, ns_ref, qt_ref, kt_ref, acc_ref = refs[7 + has_prev:]
    h = pl.program_id(0)
    j = pl.program_id(1)
    nd = RET_D // RET_D_BLOCK
    gammas = [math.exp(lg) for lg in RET_LOG_DECAY]
    gamma = jnp.where(h == 0, gammas[0], jnp.where(h == 1, gammas[1], jnp.where(h == 2, gammas[2], gammas[3])))
    gamma = gamma.astype(F32)

    @pl.when(j == 0)
    def _():
        q = _rope(q_ref[...], cos_ref[...], sin_ref[...])
        k = _rope(k_ref[...], cos_ref[...], sin_ref[...]) * (RET_D ** -0.5)
        qt_ref[...] = q.T
        kt_ref[...] = k.T
        acc_ref[...] = jnp.zeros_like(acc_ref)

    vt = v_ref[...].T
    acc = acc_ref[...]
    for d in range(RET_D_BLOCK):
        s_d = s_ref[:, d, :].T
        row = j * RET_D_BLOCK + d
        q_d = qt_ref[pl.ds(row, 1), :]
        k_d = kt_ref[pl.ds(row, 1), :]
        acc = acc + q_d * s_d
        ns_ref[:, d, :] = (gamma * s_d + k_d * vt).T
    acc_ref[...] = acc

    @pl.when(j == nd - 1)
    def _():
        qk = jnp.sum(qt_ref[...] * kt_ref[...], axis=0, keepdims=True)
        o = qk * vt + acc * gamma
        mu = jnp.mean(o, axis=0, keepdims=True)
        oc = o - mu
        var = jnp.mean(oc * oc, axis=0, keepdims=True)
        ln = (oc * lax.rsqrt(var + EPS)).T
        g = g_ref[...]
        y_ref[...] = (g * _sigmoid(g) * ln).astype(y_ref.dtype)


def _retention_sample(proj, cos2, sin2, state, l, row0, prev):
    bs = state.shape[1]
    r0 = row0 // bs
    nd = RET_D // RET_D_BLOCK
    col = lambda c: pl.BlockSpec((bs, RET_D), lambda h, j: (r0, COL_B // RET_D + c * RET_HEADS + h))
    tab = pl.BlockSpec((1, RET_D), lambda h, j: (0, 0))
    sblk = pl.BlockSpec((None, bs, None, RET_D_BLOCK, RET_D), lambda h, j: (l, 0, h, j, 0))
    extra, extra_specs = _alias_args(prev)
    return pl.pallas_call(
        functools.partial(_ret_sample_body, has_prev=prev is not None),
        grid=(RET_HEADS, nd),
        in_specs=[col(0), col(1), col(2), col(3), tab, tab, sblk] + extra_specs,
        out_specs=[pl.BlockSpec((bs, RET_D), lambda h, j: (0, h)), sblk],
        out_shape=[jax.ShapeDtypeStruct((bs, G_WIDTH), BF16), jax.ShapeDtypeStruct(state.shape, F32)],
        scratch_shapes=[pltpu.VMEM((RET_D, bs), F32), pltpu.VMEM((RET_D, bs), F32), pltpu.VMEM((RET_D, bs), F32)],
        input_output_aliases={7: 1} if prev is not None else {},
        compiler_params=_cparams("parallel", "arbitrary"),
        name="retention_sample",
    )(proj, proj, proj, proj, cos2, sin2, state, *extra)


def _rwkv_sample_body(*refs, has_prev):
    (xr_ref, xk_ref, xv_ref, xl_ref, pr_ref, pk_ref, pv_ref, pl_ref, mr_ref, mk_ref, mv_ref, ml_ref,
     w2w_ref, w2a_ref, g2_ref, w0_ref, a0_ref, kk_ref, ka_ref, rk_ref, lg_ref, lb_ref, s_ref) = refs[:23]
    y_ref, ns_ref, vec_ref, ot_ref = refs[23 + has_prev:]
    N = RWKV_HEAD

    def shifted(x_ref, prev_ref, mu_ref):
        x = x_ref[...]
        return x + (prev_ref[...] - x) * mu_ref[...]

    r, lw, k4, v, kk, a, gate = _rwkv_pre(shifted(xr_ref, pr_ref, mr_ref), shifted(xk_ref, pk_ref, mk_ref),
                                          shifted(xv_ref, pv_ref, mv_ref), shifted(xl_ref, pl_ref, ml_ref),
                                          w2w_ref[...], w2a_ref[...], g2_ref[...], w0_ref[...], a0_ref[...],
                                          kk_ref[...], ka_ref[...])
    kkn = jnp.concatenate([_l2_normalize(kk[:, 0:N]), _l2_normalize(kk[:, N:2 * N])], axis=1)
    w = jnp.exp(lw)
    vec_ref[0] = kkn.T
    vec_ref[1] = w.T
    vec_ref[2] = (kkn * a).T
    vec_ref[3] = k4.T
    vec_ref[4] = r.T
    vec_ref[5] = v.T
    for hh in range(2):
        ch = slice(hh * N, (hh + 1) * N)
        kk_t = vec_ref[0, ch, :]
        w_t = vec_ref[1, ch, :]
        kka_t = vec_ref[2, ch, :]
        k_t = vec_ref[3, ch, :]
        r_t = vec_ref[4, ch, :]
        for i in range(N):
            s_i = s_ref[hh, i]
            sa = -jnp.sum(s_i * kk_t, axis=0, keepdims=True)
            v_i = vec_ref[5, hh * N + i:hh * N + i + 1, :]
            s_new = s_i * w_t + sa * kka_t + v_i * k_t
            ns_ref[hh, i] = s_new
            ot_ref[hh * N + i:hh * N + i + 1, :] = jnp.sum(s_new * r_t, axis=0, keepdims=True)
    o = ot_ref[...].T
    rk = rk_ref[...]
    lg = lg_ref[...]
    lb = lb_ref[...]
    ys = []
    for hh in range(2):
        ch = slice(hh * N, (hh + 1) * N)
        ys.append(_rwkv_post(o[:, ch], r[:, ch], k4[:, ch], v[:, ch], gate[:, ch], rk[:, ch], lg[:, ch], lb[:, ch]))
    y_ref[...] = jnp.concatenate(ys, axis=1).astype(y_ref.dtype)


def _rwkv_sample(proj, prev_shift, rw_arrs, state, l, row0, prev):
    mu, w2a, g2, w0, a0, k_k, k_a, r_k, lnx_g, lnx_b = rw_arrs
    bs = state.shape[-1]
    r0 = row0 // bs
    G = G_WIDTH
    N = RWKV_HEAD
    W = 2 * N
    npair = G // W
    def cols(shape_lead, lead_idx, base):
        mk = lambda width, cidx: pl.BlockSpec(shape_lead + (width,), lambda p: lead_idx + (cidx(p),))
        return [mk(W, lambda p, c=c: base // W + c * npair + p) for c in range(3)] + [
            mk(2 * W, lambda p: (base + 3 * G) // (2 * W))]
    pairv = pl.BlockSpec((None, 1, W), lambda p: (l, 0, p))
    pairm = pl.BlockSpec((None, 128, W), lambda p: (l, 0, p))
    sblk = pl.BlockSpec((None, 2, N, N, bs), lambda p: (l, p, 0, 0, 0))
    extra, extra_specs = _alias_args(prev)
    return pl.pallas_call(
        functools.partial(_rwkv_sample_body, has_prev=prev is not None),
        grid=(npair,),
        in_specs=cols((bs,), (r0,), COL_C) + cols((None, bs), (l, 0), 0) + cols((None, 1), (l, 0), 0) + [
            pairm, pl.BlockSpec((None, 128, W), lambda p: (l, 0, npair + p)), pairm,
            pairv, pairv, pairv, pairv, pairv, pairv, pairv, sblk] + extra_specs,
        out_specs=[pl.BlockSpec((bs, W), lambda p: (0, p)), sblk],
        out_shape=[jax.ShapeDtypeStruct((bs, G), BF16), jax.ShapeDtypeStruct(state.shape, F32)],
        scratch_shapes=[pltpu.VMEM((6, W, bs), F32), pltpu.VMEM((W, bs), F32)],
        input_output_aliases={23: 1} if prev is not None else {},
        compiler_params=_cparams("parallel"),
        name="rwkv_sample",
    )(proj, proj, proj, proj, prev_shift, prev_shift, prev_shift, prev_shift, mu, mu, mu, mu, w2a, w2a, g2,
      w0, a0, k_k, k_a, r_k, lnx_g, lnx_b, state, *extra)


def _place_cols_body(src_ref, buf_ref, o_ref):
    del buf_ref
    o_ref[...] = src_ref[...]


def _place_cols(src, buf, col):
    rows, w = src.shape
    tr = _pick(rows, (1024, 512, 256, 128))
    return pl.pallas_call(
        _place_cols_body,
        grid=(rows // tr,),
        in_specs=[pl.BlockSpec((tr, w), lambda i: (i, 0)), ANY_SPEC],
        out_specs=pl.BlockSpec((tr, w), lambda i: (i, col)),
        out_shape=jax.ShapeDtypeStruct(buf.shape, buf.dtype),
        input_output_aliases={1: 0},
        compiler_params=_cparams("parallel"),
        name="place_cols",
    )(src, buf)


def _place_heads_body(*refs):
    src_ref, o_ref = refs[0], refs[-1]
    for h in range(XA_HEADS):
        o_ref[0, :, h, :] = src_ref[:, h * XA_HD:(h + 1) * XA_HD]


def _place_heads(src, l, depth, bp, prev):
    extra, extra_specs = _alias_args(prev)
    return pl.pallas_call(
        _place_heads_body,
        grid=(bp,),
        in_specs=[pl.BlockSpec((N_MEM, D_MODEL), lambda b: (b, 0))] + extra_specs,
        out_specs=pl.BlockSpec((None, 1, N_MEM, XA_HEADS, XA_HD), lambda b: (l, b, 0, 0, 0)),
        out_shape=jax.ShapeDtypeStruct((depth, bp, N_MEM, XA_HEADS, XA_HD), F32),
        input_output_aliases={1: 0} if prev is not None else {},
        compiler_params=_cparams("parallel"),
        name="place_heads",
    )(src, *extra)


def _assemble_body(a_ref, b_ref, c_ref, d_ref, buf_ref, o_ref):
    del buf_ref
    o_ref[...] = jnp.concatenate([a_ref[...], b_ref[...], c_ref[...], d_ref[...]], axis=1)


def _assemble_rows(parts, buf, row0):
    bs = parts[0].shape[0]
    part = pl.BlockSpec((bs, G_WIDTH), lambda i: (0, 0))
    return pl.pallas_call(
        _assemble_body,
        grid=(1,),
        in_specs=[part, part, part, part, ANY_SPEC],
        out_specs=pl.BlockSpec((bs, D_MODEL), lambda i: (row0 // bs, 0)),
        out_shape=jax.ShapeDtypeStruct(buf.shape, buf.dtype),
        input_output_aliases={4: 0},
        compiler_params=_cparams("arbitrary"),
        name="assemble_sample_rows",
    )(*parts, buf)


def kernel(x_prompt, x_sample, mem_prompt, state_conv_a, state_ret, state_shift, state_wkv, state_conv_d, cache_mem_k, cache_mem_v, g_mix, w_in, conv_a_w, mu_c, w0, w2, a0, a2, g2, k_k, k_a, r_k, lnx_g, lnx_b, conv_d_w, conv_d_b, ln_d_g, ln_d_b, w_out, g_xa, g_mem, wq_x, wk_x, wv_x, wo_x, g_mlp, w_up, w_down, g_final):
    bp, seq, d = x_prompt.shape
    bs = x_sample.shape[0]
    depth = w_in.shape[0]
    np_rows = bp * seq
    G = G_WIDTH
    x = jnp.concatenate([x_prompt.reshape(np_rows, d), x_sample.reshape(bs, d)], axis=0)
    mem = mem_prompt.reshape(bp * N_MEM, d)
    cos_p, sin_p = _rope_tables(jnp.arange(seq))
    cos_s, sin_s = _rope_tables(PAST_LEN + jnp.arange(1))

    z64 = jnp.zeros((depth, 64, G), F32)
    w2a = jnp.concatenate([jnp.concatenate([w2, z64], axis=2), jnp.concatenate([z64, a2], axis=2)], axis=1).astype(BF16)
    g2_b = g2.astype(BF16)
    wkv_t = jnp.transpose(state_wkv, (0, 2, 3, 4, 1))
    conv_d_t = jnp.transpose(state_conv_d, (0, 2, 1, 3))
    vec3 = lambda a: a.reshape(depth, 1, -1)
    rw_arrs = (vec3(mu_c), w2a, g2_b, vec3(w0), vec3(a0), vec3(k_k), vec3(k_a), vec3(r_k), vec3(lnx_g), vec3(lnx_b))

    small = {k: [] for k in ("a_p", "ret_p", "sh_p", "sh_s", "wkv_p", "d_p")}
    na_s = nd_s = nret_s = nwkv_s = mk_all = mv_all = None
    for l in range(depth):
        proj = _matmul_ws(x, w_in, l, norm_g=_layer_vec(g_mix, l), tn=1280, tm=_pick(x.shape[0], (640, 512, 256, 128)))
        ymix, na_p = _conv_a_prompt(proj, conv_a_w, l, bp, seq)
        ymix, nd_p = _conv_d_prompt(proj, conv_d_w, conv_d_b, ln_d_g, ln_d_b, ymix, l, bp, seq)
        yb_p, nret_p = _retention_prompt(proj, cos_p, sin_p, bp, seq)
        ymix = _place_cols(yb_p.reshape(np_rows, G), ymix, 1)
        rw = [(rw_arrs[0], pl.BlockSpec((None, 1, SHIFT_W), lambda *_: (l, 0, 0))),
              (w2a, _layer_mat(w2a, l)), (g2_b, _layer_mat(g2_b, l))] + [
              (a, pl.BlockSpec((None, 1, G), lambda *_: (l, 0, 0))) for a in rw_arrs[3:]]
        yc_p, nsh_p, nwkv_p = _rwkv_prompt(proj, rw, bp, seq)
        ymix = _place_cols(yc_p.reshape(np_rows, G), ymix, 2)
        ya_s, yd_s, na_s, nd_s = _sample_ad(proj, state_conv_a, conv_d_t, conv_a_w, conv_d_w, conv_d_b, ln_d_g, ln_d_b,
                                            l, np_rows, na_s, nd_s)
        yb_s, nret_s = _retention_sample(proj, cos_s, sin_s, state_ret, l, np_rows, nret_s)
        yc_s, nwkv_s = _rwkv_sample(proj, state_shift, rw_arrs, wkv_t, l, np_rows, nwkv_s)
        ymix = _assemble_rows([ya_s, yb_s, yc_s, yd_s], ymix, np_rows)
        x = _matmul_ws(ymix, w_out, l, residual=x)

        mk = _matmul_ws(mem, wk_x, l, norm_g=_layer_vec(g_mem, l))
        mv = _matmul_ws(mem, wv_x, l, norm_g=_layer_vec(g_mem, l))
        mk_all = _place_heads(mk, l, depth, bp, mk_all)
        mv_all = _place_heads(mv, l, depth, bp, mv_all)
        q = _matmul_ws(x, wq_x, l, norm_g=_layer_vec(g_xa, l))
        obuf = _xattn_prompt(q, mk, mv, bp, seq)
        q_s = q[np_rows:].reshape(bs, XA_HEADS, XA_HD)
        q_s = jnp.concatenate([q_s, q_s], axis=1)
        obuf = _xattn_sample(q_s, cache_mem_k, cache_mem_v, obuf, l, np_rows)
        x = _matmul_ws(obuf, wo_x, l, residual=x)

        up = _matmul_ws(x, w_up, l, norm_g=_layer_vec(g_mlp, l), act="relu2", out_dtype=BF16)
        for kb in range(w_down.shape[1] // d):
            x = _matmul_ws(up, w_down, l, residual=x, kblock=(kb, d))

        small["a_p"].append(na_p)
        small["ret_p"].append(nret_p)
        small["sh_p"].append(nsh_p.reshape(bp, SHIFT_W))
        small["sh_s"].append(proj[np_rows:, COL_C:COL_C + SHIFT_W])
        small["wkv_p"].append(nwkv_p)
        small["d_p"].append(nd_p)
    gf = (g_final.reshape(1, d), pl.BlockSpec((1, d), lambda *_: (0, 0)))
    y_p = _rmsnorm(x, gf, F32, 0, np_rows)
    y_s = _rmsnorm(x, gf, F32, np_rows, bs)
    st = lambda k: jnp.stack(small[k])
    return (y_p.reshape(bp, seq, d), y_s.reshape(bs, 1, d), st("a_p"), na_s, st("ret_p"), nret_s,
            st("sh_p"), st("sh_s"), st("wkv_p"), jnp.transpose(nwkv_s, (0, 4, 1, 2, 3)),
            st("d_p"), jnp.transpose(nd_s, (0, 2, 1, 3)), mk_all, mv_all)
```
